```python
import math
import jax, jax.numpy as jnp
from jax import lax
import numpy as np

D_MODEL = 2048
BATCH = 8
SEQ = 4096
DEPTH = 4

ATTN_HEADS = 8
ATTN_KV_HEADS = 2
HEAD_DIM = 128
WINDOW = 128
ROPE_THETA = 10000.0
DN_HEADS = 4
DN_HEAD_DIM = 128
DN_CONV = 4
DN_CHUNK = 64
S5_GROUPS = 32
S5_GROUP_CH = 16
S5_STATE = 64
ATTN_WIDTH = ATTN_HEADS * HEAD_DIM
ATTN_KV_WIDTH = ATTN_KV_HEADS * HEAD_DIM
DN_WIDTH = DN_HEADS * DN_HEAD_DIM
S5_WIDTH = S5_GROUPS * S5_GROUP_CH
MIX_WIDTH = ATTN_WIDTH + DN_WIDTH + S5_WIDTH
IN_SPLITS = (ATTN_WIDTH, ATTN_KV_WIDTH, ATTN_KV_WIDTH, 3 * DN_WIDTH, DN_WIDTH, DN_HEADS, DN_HEADS, S5_WIDTH)
IN_WIDTH = sum(IN_SPLITS)
D_FF = 5632
FFN_RES_WEIGHT = 0.5
NORM_EPS = 1e-6

kernel_name = "hymba_style_swa_deltanet_s5_macaron"


def rms_norm(x, gain):
    xf = x.astype(jnp.float32)
    y = xf * lax.rsqrt(jnp.mean(xf * xf, axis=-1, keepdims=True) + NORM_EPS)
    return (y * gain.astype(jnp.float32)).astype(x.dtype)


def l2_norm(x):
    return x * lax.rsqrt(jnp.sum(x * x, axis=-1, keepdims=True) + NORM_EPS)


def swiglu(h, w_gate, w_up, w_down):
    return (jax.nn.silu(h @ w_gate) * (h @ w_up)) @ w_down


def rope_tables(seq):
    half = HEAD_DIM // 2
    inv_freq = ROPE_THETA ** (-jnp.arange(half, dtype=jnp.float32) / half)
    ang = jnp.arange(seq, dtype=jnp.float32)[:, None] * inv_freq[None, :]
    return jnp.cos(ang), jnp.sin(ang)


def apply_rope(x, cos, sin):
    half = HEAD_DIM // 2
    xf = x.astype(jnp.float32)
    x1, x2 = xf[..., :half], xf[..., half:]
    c = cos[None, :, None, :]
    s = sin[None, :, None, :]
    return jnp.concatenate([x1 * c - x2 * s, x2 * c + x1 * s], axis=-1).astype(x.dtype)


def sliding_window_attention(q, k, v, sinks):
    b, s, hq, d = q.shape
    hkv = k.shape[2]
    grp = hq // hkv
    nb = s // WINDOW
    qb = q.reshape(b, nb, WINDOW, hkv, grp, d)
    kb = k.reshape(b, nb, WINDOW, hkv, d)
    vb = v.reshape(b, nb, WINDOW, hkv, d)
    pad = ((0, 0), (1, 0), (0, 0), (0, 0), (0, 0))
    kk = jnp.concatenate([jnp.pad(kb, pad)[:, :-1], kb], axis=2)
    vv = jnp.concatenate([jnp.pad(vb, pad)[:, :-1], vb], axis=2)
    scores = jnp.einsum('bnqhgd,bnkhd->bnhgqk', qb, kk, preferred_element_type=jnp.float32) * (d ** -0.5)
    qi = jnp.arange(WINDOW)[:, None] + WINDOW
    kj = jnp.arange(2 * WINDOW)[None, :]
    rel = qi - kj
    band = (rel >= 0) & (rel < WINDOW)
    first = (jnp.arange(nb) == 0)[:, None, None] & (kj < WINDOW)[None]
    mask = band[None] & jnp.logical_not(first)
    scores = jnp.where(mask[None, :, None, None], scores, -jnp.inf)
    sink = sinks.astype(jnp.float32).reshape(hkv, grp)[None, None, :, :, None, None]
    m = jnp.maximum(jnp.max(scores, axis=-1, keepdims=True), sink)
    p = jnp.exp(scores - m)
    p = p / (jnp.sum(p, axis=-1, keepdims=True) + jnp.exp(sink - m))
    out = jnp.einsum('bnhgqk,bnkhd->bnqhgd', p.astype(vv.dtype), vv)
    return out.reshape(b, s, hq * d)


def causal_depthwise_conv(u, w):
    taps = w.shape[0]
    return lax.conv_general_dilated(u, w[:, None, :], window_strides=(1,), padding=[(taps - 1, 0)],
                                    dimension_numbers=('NWC', 'WIO', 'NWC'), feature_group_count=u.shape[-1])


def gated_delta_rule(q, k, v, g, beta):
    b, s, h, dk = q.shape
    dv = v.shape[-1]
    c = DN_CHUNK
    n = s // c

    def chunks(t):
        return t.reshape(b, n, c, h, -1).transpose(0, 1, 3, 2, 4)

    q = chunks(q) * (dk ** -0.5)
    k = chunks(k)
    v = chunks(v)
    beta = chunks(beta[..., None])[..., 0]
    g = jnp.cumsum(chunks(g[..., None])[..., 0], axis=-1)
    causal = jnp.tril(jnp.ones((c, c), bool))
    strict = jnp.tril(jnp.ones((c, c), bool), -1)
    decay = jnp.exp(jnp.where(causal, g[..., :, None] - g[..., None, :], -jnp.inf))
    k_beta = k * beta[..., None]
    lower = jnp.where(strict, jnp.einsum('bnhid,bnhjd->bnhij', k_beta, k) * decay, 0.0) + jnp.eye(c, dtype=jnp.float32)
    rhs = jnp.concatenate([v * beta[..., None], k_beta * jnp.exp(g)[..., None]], axis=-1)
    uw = lax.linalg.triangular_solve(lower, rhs, left_side=True, lower=True, unit_diagonal=True)
    u, w = uw[..., :dv], uw[..., dv:]
    attn = jnp.where(causal, jnp.einsum('bnhid,bnhjd->bnhij', q, k) * decay, 0.0)
    q_dec = q * jnp.exp(g)[..., None]
    g_last = g[..., -1]
    k_dec = k * jnp.exp(g_last[..., None] - g)[..., None]

    def step(state, inp):
        q_c, k_c, u_c, w_c, a_c, gl = inp
        v_new = u_c - jnp.einsum('bhck,bhkv->bhcv', w_c, state)
        o = jnp.einsum('bhck,bhkv->bhcv', q_c, state) + jnp.einsum('bhij,bhjv->bhiv', a_c, v_new)
        state = state * jnp.exp(gl)[..., None, None] + jnp.einsum('bhck,bhcv->bhkv', k_c, v_new)
        return state, o

    mv = lambda t: jnp.moveaxis(t, 1, 0)
    s0 = jnp.zeros((b, h, dk, dv), jnp.float32)
    _, o = lax.scan(step, s0, (mv(q_dec), mv(k_dec), mv(u), mv(w), mv(attn), mv(g_last)))
    return o.transpose(1, 0, 3, 2, 4).reshape(b, s, h, dv)


def gated_deltanet(qkv_raw, z_gate, b_raw, a_raw, conv_w, a_log, dt_bias, norm_w):
    b, s, _ = qkv_raw.shape
    qkv = jax.nn.silu(causal_depthwise_conv(qkv_raw, conv_w)).astype(jnp.float32)
    q, k, v = jnp.split(qkv, 3, axis=-1)
    shp = (b, s, DN_HEADS, DN_HEAD_DIM)
    q = l2_norm(q.reshape(shp))
    k = l2_norm(k.reshape(shp))
    v = v.reshape(shp)
    beta = jax.nn.sigmoid(b_raw.astype(jnp.float32))
    g = -jnp.exp(a_log.astype(jnp.float32)) * jax.nn.softplus(a_raw.astype(jnp.float32) + dt_bias.astype(jnp.float32))
    o = gated_delta_rule(q, k, v, g, beta)
    o = o * lax.rsqrt(jnp.mean(o * o, axis=-1, keepdims=True) + NORM_EPS) * norm_w.astype(jnp.float32)
    o = o * jax.nn.silu(z_gate.astype(jnp.float32).reshape(shp))
    return o.reshape(b, s, DN_WIDTH).astype(qkv_raw.dtype)


def s5_mixer(u, a_re, a_im, log_dt, b_re, b_im, c_re, c_im, d_skip, glu_w, glu_b):
    b, s, _ = u.shape
    uf = u.astype(jnp.float32).reshape(b, s, S5_GROUPS, S5_GROUP_CH)
    lam = lax.complex(a_re.astype(jnp.float32), a_im.astype(jnp.float32))
    dt = jnp.exp(log_dt.astype(jnp.float32))[:, None]
    a_bar = jnp.exp(lam * dt)
    b_c = lax.complex(b_re.astype(jnp.float32), b_im.astype(jnp.float32))
    b_bar = ((a_bar - 1.0) / lam)[..., None] * b_c
    bu = jnp.einsum('bsgh,gph->bsgp', uf.astype(jnp.complex64), b_bar)
    a_seq = jnp.broadcast_to(a_bar, bu.shape)

    def combine(e1, e2):
        a1, x1 = e1
        a2, x2 = e2
        return a1 * a2, a2 * x1 + x2

    _, states = lax.associative_scan(combine, (a_seq, bu), axis=1)
    c_c = lax.complex(c_re.astype(jnp.float32), c_im.astype(jnp.float32))
    y = jnp.real(jnp.einsum('bsgp,ghp->bsgh', states, c_c))
    y = y + d_skip.astype(jnp.float32).reshape(S5_GROUPS, S5_GROUP_CH) * uf
    y = jax.nn.gelu(y.reshape(b, s, S5_WIDTH)).astype(u.dtype)
    return y * jax.nn.sigmoid(y @ glu_w + glu_b)


def _fwd_setup_inputs(seed: int = 0) -> dict:
    key = jax.random.key(seed)
    ks = iter(jax.random.split(key, 40))
    f32 = jnp.float32

    def nrm(shape, scale):
        return scale * jax.random.normal(next(ks), shape, f32)

    def gain(width=D_MODEL):
        return 1.0 + nrm((DEPTH, width), 0.02)

    x = nrm((BATCH, SEQ, D_MODEL), 1.0)
    ff1_norm_pre = gain()
    ff1_w_gate = nrm((DEPTH, D_MODEL, D_FF), D_MODEL ** -0.5)
    ff1_w_up = nrm((DEPTH, D_MODEL, D_FF), D_MODEL ** -0.5)
    ff1_w_down = nrm((DEPTH, D_FF, D_MODEL), D_FF ** -0.5)
    ff1_norm_post = gain()
    mix_norm_pre = gain()
    w_in = nrm((DEPTH, D_MODEL, IN_WIDTH), D_MODEL ** -0.5)
    attn_sinks = nrm((DEPTH, ATTN_HEADS), 0.5)
    dn_conv_w = nrm((DEPTH, DN_CONV, 3 * DN_WIDTH), DN_CONV ** -0.5)
    dn_a_log = jnp.log(jax.random.uniform(next(ks), (DEPTH, DN_HEADS), f32, 1.0, 16.0))
    dn_dt = jnp.exp(jax.random.uniform(next(ks), (DEPTH, DN_HEADS), f32, math.log(1e-3), math.log(1e-1)))
    dn_dt_bias = dn_dt + jnp.log(-jnp.expm1(-dn_dt))
    dn_norm_w = gain(DN_HEAD_DIM)
    s5_a_re = -0.5 + nrm((DEPTH, S5_GROUPS, S5_STATE), 0.01)
    s5_a_im = math.pi * jnp.arange(S5_STATE, dtype=f32)[None, None, :] + nrm((DEPTH, S5_GROUPS, S5_STATE), 0.01)
    s5_log_dt = jax.random.uniform(next(ks), (DEPTH, S5_GROUPS), f32, math.log(1e-3), math.log(1e-1))
    s5_b_re = nrm((DEPTH, S5_GROUPS, S5_STATE, S5_GROUP_CH), (2 * S5_GROUP_CH) ** -0.5)
    s5_b_im = nrm((DEPTH, S5_GROUPS, S5_STATE, S5_GROUP_CH), (2 * S5_GROUP_CH) ** -0.5)
    s5_c_re = nrm((DEPTH, S5_GROUPS, S5_GROUP_CH, S5_STATE), (2 * S5_STATE) ** -0.5)
    s5_c_im = nrm((DEPTH, S5_GROUPS, S5_GROUP_CH, S5_STATE), (2 * S5_STATE) ** -0.5)
    s5_d = nrm((DEPTH, S5_WIDTH), 1.0)
    s5_glu_w = nrm((DEPTH, S5_WIDTH, S5_WIDTH), S5_WIDTH ** -0.5)
    s5_glu_b = nrm((DEPTH, S5_WIDTH), 0.01)
    w_out = nrm((DEPTH, MIX_WIDTH, D_MODEL), MIX_WIDTH ** -0.5)
    mix_norm_post = gain()
    ff2_norm_pre = gain()
    ff2_w_gate = nrm((DEPTH, D_MODEL, D_FF), D_MODEL ** -0.5)
    ff2_w_up = nrm((DEPTH, D_MODEL, D_FF), D_MODEL ** -0.5)
    ff2_w_down = nrm((DEPTH, D_FF, D_MODEL), D_FF ** -0.5)
    ff2_norm_post = gain()
    return {"x": x, "ff1_norm_pre": ff1_norm_pre, "ff1_w_gate": ff1_w_gate, "ff1_w_up": ff1_w_up,
            "ff1_w_down": ff1_w_down, "ff1_norm_post": ff1_norm_post, "mix_norm_pre": mix_norm_pre,
            "w_in": w_in, "attn_sinks": attn_sinks, "dn_conv_w": dn_conv_w, "dn_a_log": dn_a_log,
            "dn_dt_bias": dn_dt_bias, "dn_norm_w": dn_norm_w, "s5_a_re": s5_a_re, "s5_a_im": s5_a_im,
            "s5_log_dt": s5_log_dt, "s5_b_re": s5_b_re, "s5_b_im": s5_b_im, "s5_c_re": s5_c_re,
            "s5_c_im": s5_c_im, "s5_d": s5_d, "s5_glu_w": s5_glu_w, "s5_glu_b": s5_glu_b, "w_out": w_out,
            "mix_norm_post": mix_norm_post, "ff2_norm_pre": ff2_norm_pre, "ff2_w_gate": ff2_w_gate,
            "ff2_w_up": ff2_w_up, "ff2_w_down": ff2_w_down, "ff2_norm_post": ff2_norm_post}


def _fwd_reference(x, ff1_norm_pre, ff1_w_gate, ff1_w_up, ff1_w_down, ff1_norm_post, mix_norm_pre, w_in,
              attn_sinks, dn_conv_w, dn_a_log, dn_dt_bias, dn_norm_w, s5_a_re, s5_a_im, s5_log_dt,
              s5_b_re, s5_b_im, s5_c_re, s5_c_im, s5_d, s5_glu_w, s5_glu_b, w_out, mix_norm_post,
              ff2_norm_pre, ff2_w_gate, ff2_w_up, ff2_w_down, ff2_norm_post):
    b, s, _ = x.shape
    cos, sin = rope_tables(s)
    offsets = np.cumsum(IN_SPLITS)[:-1].tolist()
    for l in range(DEPTH):
        h = rms_norm(x, ff1_norm_pre[l])
        x = x + FFN_RES_WEIGHT * rms_norm(swiglu(h, ff1_w_gate[l], ff1_w_up[l], ff1_w_down[l]), ff1_norm_post[l])
        h = rms_norm(x, mix_norm_pre[l])
        z = h @ w_in[l]
        aq, ak, av, dn_qkv, dn_z, dn_b, dn_a, s5_u = jnp.split(z, offsets, axis=-1)
        aq = apply_rope(aq.reshape(b, s, ATTN_HEADS, HEAD_DIM), cos, sin)
        ak = apply_rope(ak.reshape(b, s, ATTN_KV_HEADS, HEAD_DIM), cos, sin)
        av = av.reshape(b, s, ATTN_KV_HEADS, HEAD_DIM)
        y_attn = sliding_window_attention(aq, ak, av, attn_sinks[l])
        y_dn = gated_deltanet(dn_qkv, dn_z, dn_b, dn_a, dn_conv_w[l], dn_a_log[l],
                              dn_dt_bias[l], dn_norm_w[l])
        y_s5 = s5_mixer(s5_u, s5_a_re[l], s5_a_im[l], s5_log_dt[l], s5_b_re[l], s5_b_im[l],
                        s5_c_re[l], s5_c_im[l], s5_d[l], s5_glu_w[l], s5_glu_b[l])
        mixed = jnp.concatenate([y_attn, y_dn, y_s5], axis=-1) @ w_out[l]
        x = x + rms_norm(mixed, mix_norm_post[l])
        h = rms_norm(x, ff2_norm_pre[l])
        x = x + FFN_RES_WEIGHT * rms_norm(swiglu(h, ff2_w_gate[l], ff2_w_up[l], ff2_w_down[l]), ff2_norm_post[l])
    return x


import jax as _jax
import jax.numpy as _jnp

TWIN_FORMAT = 'train_step'
FWD_PARAMS = ['x', 'ff1_norm_pre', 'ff1_w_gate', 'ff1_w_up', 'ff1_w_down', 'ff1_norm_post', 'mix_norm_pre', 'w_in', 'attn_sinks', 'dn_conv_w', 'dn_a_log', 'dn_dt_bias', 'dn_norm_w', 's5_a_re', 's5_a_im', 's5_log_dt', 's5_b_re', 's5_b_im', 's5_c_re', 's5_c_im', 's5_d', 's5_glu_w', 's5_glu_b', 'w_out', 'mix_norm_post', 'ff2_norm_pre', 'ff2_w_gate', 'ff2_w_up', 'ff2_w_down', 'ff2_norm_post']
TWIN_WEIGHTS = ['ff1_norm_pre', 'ff1_w_gate', 'ff1_w_up', 'ff1_w_down', 'ff1_norm_post', 'mix_norm_pre', 'w_in', 'attn_sinks', 'dn_conv_w', 'dn_a_log', 'dn_dt_bias', 'dn_norm_w', 's5_a_re', 's5_a_im', 's5_log_dt', 's5_b_re', 's5_b_im', 's5_c_re', 's5_c_im', 's5_d', 's5_glu_w', 's5_glu_b', 'w_out', 'mix_norm_post', 'ff2_norm_pre', 'ff2_w_gate', 'ff2_w_up', 'ff2_w_down', 'ff2_norm_post']
TWIN_DIFF_INPUT = 'x'
TWIN_INPUTS = ['x', 'ff1_norm_pre', 'ff1_w_gate', 'ff1_w_up', 'ff1_w_down', 'ff1_norm_post', 'mix_norm_pre', 'w_in', 'attn_sinks', 'dn_conv_w', 'dn_a_log', 'dn_dt_bias', 'dn_norm_w', 's5_a_re', 's5_a_im', 's5_log_dt', 's5_b_re', 's5_b_im', 's5_c_re', 's5_c_im', 's5_d', 's5_glu_w', 's5_glu_b', 'w_out', 'mix_norm_post', 'ff2_norm_pre', 'ff2_w_gate', 'ff2_w_up', 'ff2_w_down', 'ff2_norm_post', 'loss_target', 'm_ff1_norm_pre', 'm_ff1_w_gate', 'm_ff1_w_up', 'm_ff1_w_down', 'm_ff1_norm_post', 'm_mix_norm_pre', 'm_w_in', 'm_attn_sinks', 'm_dn_conv_w', 'm_dn_a_log', 'm_dn_dt_bias', 'm_dn_norm_w', 'm_s5_a_re', 'm_s5_a_im', 'm_s5_log_dt', 'm_s5_b_re', 'm_s5_b_im', 'm_s5_c_re', 'm_s5_c_im', 'm_s5_d', 'm_s5_glu_w', 'm_s5_glu_b', 'm_w_out', 'm_mix_norm_post', 'm_ff2_norm_pre', 'm_ff2_w_gate', 'm_ff2_w_up', 'm_ff2_w_down', 'm_ff2_norm_post', 'v_ff1_norm_pre', 'v_ff1_w_gate', 'v_ff1_w_up', 'v_ff1_w_down', 'v_ff1_norm_post', 'v_mix_norm_pre', 'v_w_in', 'v_attn_sinks', 'v_dn_conv_w', 'v_dn_a_log', 'v_dn_dt_bias', 'v_dn_norm_w', 'v_s5_a_re', 'v_s5_a_im', 'v_s5_log_dt', 'v_s5_b_re', 'v_s5_b_im', 'v_s5_c_re', 'v_s5_c_im', 'v_s5_d', 'v_s5_glu_w', 'v_s5_glu_b', 'v_w_out', 'v_mix_norm_post', 'v_ff2_norm_pre', 'v_ff2_w_gate', 'v_ff2_w_up', 'v_ff2_w_down', 'v_ff2_norm_post']
TWIN_OUTPUTS = ['loss', 'grad_x', 'grad_ff1_norm_pre', 'grad_ff1_w_gate', 'grad_ff1_w_up', 'grad_ff1_w_down', 'grad_ff1_norm_post', 'grad_mix_norm_pre', 'grad_w_in', 'grad_attn_sinks', 'grad_dn_conv_w', 'grad_dn_a_log', 'grad_dn_dt_bias', 'grad_dn_norm_w', 'grad_s5_a_re', 'grad_s5_a_im', 'grad_s5_log_dt', 'grad_s5_b_re', 'grad_s5_b_im', 'grad_s5_c_re', 'grad_s5_c_im', 'grad_s5_d', 'grad_s5_glu_w', 'grad_s5_glu_b', 'grad_w_out', 'grad_mix_norm_post', 'grad_ff2_norm_pre', 'grad_ff2_w_gate', 'grad_ff2_w_up', 'grad_ff2_w_down', 'grad_ff2_norm_post', 'delta_ff1_norm_pre', 'delta_ff1_w_gate', 'delta_ff1_w_up', 'delta_ff1_w_down', 'delta_ff1_norm_post', 'delta_mix_norm_pre', 'delta_w_in', 'delta_attn_sinks', 'delta_dn_conv_w', 'delta_dn_a_log', 'delta_dn_dt_bias', 'delta_dn_norm_w', 'delta_s5_a_re', 'delta_s5_a_im', 'delta_s5_log_dt', 'delta_s5_b_re', 'delta_s5_b_im', 'delta_s5_c_re', 'delta_s5_c_im', 'delta_s5_d', 'delta_s5_glu_w', 'delta_s5_glu_b', 'delta_w_out', 'delta_mix_norm_post', 'delta_ff2_norm_pre', 'delta_ff2_w_gate', 'delta_ff2_w_up', 'delta_ff2_w_down', 'delta_ff2_norm_post', 'new_m_ff1_norm_pre', 'new_m_ff1_w_gate', 'new_m_ff1_w_up', 'new_m_ff1_w_down', 'new_m_ff1_norm_post', 'new_m_mix_norm_pre', 'new_m_w_in', 'new_m_attn_sinks', 'new_m_dn_conv_w', 'new_m_dn_a_log', 'new_m_dn_dt_bias', 'new_m_dn_norm_w', 'new_m_s5_a_re', 'new_m_s5_a_im', 'new_m_s5_log_dt', 'new_m_s5_b_re', 'new_m_s5_b_im', 'new_m_s5_c_re', 'new_m_s5_c_im', 'new_m_s5_d', 'new_m_s5_glu_w', 'new_m_s5_glu_b', 'new_m_w_out', 'new_m_mix_norm_post', 'new_m_ff2_norm_pre', 'new_m_ff2_w_gate', 'new_m_ff2_w_up', 'new_m_ff2_w_down', 'new_m_ff2_norm_post', 'new_v_ff1_norm_pre', 'new_v_ff1_w_gate', 'new_v_ff1_w_up', 'new_v_ff1_w_down', 'new_v_ff1_norm_post', 'new_v_mix_norm_pre', 'new_v_w_in', 'new_v_attn_sinks', 'new_v_dn_conv_w', 'new_v_dn_a_log', 'new_v_dn_dt_bias', 'new_v_dn_norm_w', 'new_v_s5_a_re', 'new_v_s5_a_im', 'new_v_s5_log_dt', 'new_v_s5_b_re', 'new_v_s5_b_im', 'new_v_s5_c_re', 'new_v_s5_c_im', 'new_v_s5_d', 'new_v_s5_glu_w', 'new_v_s5_glu_b', 'new_v_w_out', 'new_v_mix_norm_post', 'new_v_ff2_norm_pre', 'new_v_ff2_w_gate', 'new_v_ff2_w_up', 'new_v_ff2_w_down', 'new_v_ff2_norm_post']
TWIN_LEAF_KINDS = {'loss': 'loss', 'grad_x': 'grad_x', 'grad_ff1_norm_pre': 'grad_w', 'grad_ff1_w_gate': 'grad_w', 'grad_ff1_w_up': 'grad_w', 'grad_ff1_w_down': 'grad_w', 'grad_ff1_norm_post': 'grad_w', 'grad_mix_norm_pre': 'grad_w', 'grad_w_in': 'grad_w', 'grad_attn_sinks': 'grad_w', 'grad_dn_conv_w': 'grad_w', 'grad_dn_a_log': 'grad_w', 'grad_dn_dt_bias': 'grad_w', 'grad_dn_norm_w': 'grad_w', 'grad_s5_a_re': 'grad_w', 'grad_s5_a_im': 'grad_w', 'grad_s5_log_dt': 'grad_w', 'grad_s5_b_re': 'grad_w', 'grad_s5_b_im': 'grad_w', 'grad_s5_c_re': 'grad_w', 'grad_s5_c_im': 'grad_w', 'grad_s5_d': 'grad_w', 'grad_s5_glu_w': 'grad_w', 'grad_s5_glu_b': 'grad_w', 'grad_w_out': 'grad_w', 'grad_mix_norm_post': 'grad_w', 'grad_ff2_norm_pre': 'grad_w', 'grad_ff2_w_gate': 'grad_w', 'grad_ff2_w_up': 'grad_w', 'grad_ff2_w_down': 'grad_w', 'grad_ff2_norm_post': 'grad_w', 'delta_ff1_norm_pre': 'delta_w', 'delta_ff1_w_gate': 'delta_w', 'delta_ff1_w_up': 'delta_w', 'delta_ff1_w_down': 'delta_w', 'delta_ff1_norm_post': 'delta_w', 'delta_mix_norm_pre': 'delta_w', 'delta_w_in': 'delta_w', 'delta_attn_sinks': 'delta_w', 'delta_dn_conv_w': 'delta_w', 'delta_dn_a_log': 'delta_w', 'delta_dn_dt_bias': 'delta_w', 'delta_dn_norm_w': 'delta_w', 'delta_s5_a_re': 'delta_w', 'delta_s5_a_im': 'delta_w', 'delta_s5_log_dt': 'delta_w', 'delta_s5_b_re': 'delta_w', 'delta_s5_b_im': 'delta_w', 'delta_s5_c_re': 'delta_w', 'delta_s5_c_im': 'delta_w', 'delta_s5_d': 'delta_w', 'delta_s5_glu_w': 'delta_w', 'delta_s5_glu_b': 'delta_w', 'delta_w_out': 'delta_w', 'delta_mix_norm_post': 'delta_w', 'delta_ff2_norm_pre': 'delta_w', 'delta_ff2_w_gate': 'delta_w', 'delta_ff2_w_up': 'delta_w', 'delta_ff2_w_down': 'delta_w', 'delta_ff2_norm_post': 'delta_w', 'new_m_ff1_norm_pre': 'new_m', 'new_m_ff1_w_gate': 'new_m', 'new_m_ff1_w_up': 'new_m', 'new_m_ff1_w_down': 'new_m', 'new_m_ff1_norm_post': 'new_m', 'new_m_mix_norm_pre': 'new_m', 'new_m_w_in': 'new_m', 'new_m_attn_sinks': 'new_m', 'new_m_dn_conv_w': 'new_m', 'new_m_dn_a_log': 'new_m', 'new_m_dn_dt_bias': 'new_m', 'new_m_dn_norm_w': 'new_m', 'new_m_s5_a_re': 'new_m', 'new_m_s5_a_im': 'new_m', 'new_m_s5_log_dt': 'new_m', 'new_m_s5_b_re': 'new_m', 'new_m_s5_b_im': 'new_m', 'new_m_s5_c_re': 'new_m', 'new_m_s5_c_im': 'new_m', 'new_m_s5_d': 'new_m', 'new_m_s5_glu_w': 'new_m', 'new_m_s5_glu_b': 'new_m', 'new_m_w_out': 'new_m', 'new_m_mix_norm_post': 'new_m', 'new_m_ff2_norm_pre': 'new_m', 'new_m_ff2_w_gate': 'new_m', 'new_m_ff2_w_up': 'new_m', 'new_m_ff2_w_down': 'new_m', 'new_m_ff2_norm_post': 'new_m', 'new_v_ff1_norm_pre': 'new_v', 'new_v_ff1_w_gate': 'new_v', 'new_v_ff1_w_up': 'new_v', 'new_v_ff1_w_down': 'new_v', 'new_v_ff1_norm_post': 'new_v', 'new_v_mix_norm_pre': 'new_v', 'new_v_w_in': 'new_v', 'new_v_attn_sinks': 'new_v', 'new_v_dn_conv_w': 'new_v', 'new_v_dn_a_log': 'new_v', 'new_v_dn_dt_bias': 'new_v', 'new_v_dn_norm_w': 'new_v', 'new_v_s5_a_re': 'new_v', 'new_v_s5_a_im': 'new_v', 'new_v_s5_log_dt': 'new_v', 'new_v_s5_b_re': 'new_v', 'new_v_s5_b_im': 'new_v', 'new_v_s5_c_re': 'new_v', 'new_v_s5_c_im': 'new_v', 'new_v_s5_d': 'new_v', 'new_v_s5_glu_w': 'new_v', 'new_v_s5_glu_b': 'new_v', 'new_v_w_out': 'new_v', 'new_v_mix_norm_post': 'new_v', 'new_v_ff2_norm_pre': 'new_v', 'new_v_ff2_w_gate': 'new_v', 'new_v_ff2_w_up': 'new_v', 'new_v_ff2_w_down': 'new_v', 'new_v_ff2_norm_post': 'new_v'}


def _forward(args):
    return _fwd_reference(*[args[k] for k in FWD_PARAMS])


def _output_shape():
    def fwd():
        inp = _fwd_setup_inputs(0)
        return _fwd_reference(*[inp[k] for k in FWD_PARAMS])
    out = _jax.eval_shape(fwd)
    return out.shape, out.dtype

N_MICROBATCH = 1
ADAM_LR = 0.001
ADAM_B1 = 0.9
ADAM_B2 = 0.999
ADAM_EPS = 1e-08
ADAM_WD = 0.01
ADAM_STEP = 10
PER_EXAMPLE_BATCH_AXIS = {'x': 0, 'loss_target': 0}
SHARED_INPUTS = []
_WEIGHT_DTYPES = {'ff1_norm_pre': _jnp.float32, 'ff1_w_gate': _jnp.float32, 'ff1_w_up': _jnp.float32, 'ff1_w_down': _jnp.float32, 'ff1_norm_post': _jnp.float32, 'mix_norm_pre': _jnp.float32, 'w_in': _jnp.float32, 'attn_sinks': _jnp.float32, 'dn_conv_w': _jnp.float32, 'dn_a_log': _jnp.float32, 'dn_dt_bias': _jnp.float32, 'dn_norm_w': _jnp.float32, 's5_a_re': _jnp.float32, 's5_a_im': _jnp.float32, 's5_log_dt': _jnp.float32, 's5_b_re': _jnp.float32, 's5_b_im': _jnp.float32, 's5_c_re': _jnp.float32, 's5_c_im': _jnp.float32, 's5_d': _jnp.float32, 's5_glu_w': _jnp.float32, 's5_glu_b': _jnp.float32, 'w_out': _jnp.float32, 'mix_norm_post': _jnp.float32, 'ff2_norm_pre': _jnp.float32, 'ff2_w_gate': _jnp.float32, 'ff2_w_up': _jnp.float32, 'ff2_w_down': _jnp.float32, 'ff2_norm_post': _jnp.float32}
MOMENT_SCALE = {'ff1_norm_pre': 1.309117e+00, 'ff1_w_gate': 3.937817e-01, 'ff1_w_up': 5.803654e-01, 'ff1_w_down': 9.818129e-01, 'ff1_norm_post': 4.022779e+00, 'mix_norm_pre': 5.756091e+00, 'w_in': 4.114149e+00, 'attn_sinks': 8.639081e-01, 'dn_conv_w': 3.517806e+00, 'dn_a_log': 1.077530e+01, 'dn_dt_bias': 1.038256e+01, 'dn_norm_w': 1.316144e+01, 's5_a_re': 1.415195e-01, 's5_a_im': 1.295786e-01, 's5_log_dt': 2.496505e+01, 's5_b_re': 1.293342e-01, 's5_b_im': 1.058337e-01, 's5_c_re': 2.080836e-01, 's5_c_im': 2.071146e-01, 's5_d': 8.790583e+00, 's5_glu_w': 1.225795e+00, 's5_glu_b': 3.403175e+00, 'w_out': 7.099536e+00, 'mix_norm_post': 1.785060e+01, 'ff2_norm_pre': 1.479018e+00, 'ff2_w_gate': 4.181961e-01, 'ff2_w_up': 7.532689e-01, 'ff2_w_down': 1.249123e+00, 'ff2_norm_post': 4.172884e+00}


def _to_microbatches(a, axis):
    t = _jnp.moveaxis(a, axis, 0)
    t = t.reshape((N_MICROBATCH, t.shape[0] // N_MICROBATCH) + t.shape[1:])
    return _jnp.moveaxis(t, 1, axis + 1)


def setup_inputs(seed: int = 0) -> dict:
    inp = _fwd_setup_inputs(seed)
    key = _jax.random.fold_in(_jax.random.key(seed), 7919)
    shape, _ = _output_shape()
    out = dict(inp)
    out["loss_target"] = _jax.random.normal(_jax.random.fold_in(key, 0), shape, _jnp.float32)
    for i, name in enumerate(TWIN_WEIGHTS):
        w = inp[name].astype(_jnp.float32)
        if MOMENT_SCALE is None:
            s = _jnp.sqrt(_jnp.mean(_jnp.square(w)) + 1e-30)
        else:
            s = MOMENT_SCALE[name]
        km, kv = _jax.random.split(_jax.random.fold_in(key, i + 1))
        out[name] = w
        out["m_" + name] = s * _jax.random.normal(km, w.shape, _jnp.float32)
        out["v_" + name] = (s * s) * _jax.random.uniform(kv, w.shape, _jnp.float32, 0.5, 1.5)
    if N_MICROBATCH > 1:
        for name, axis in PER_EXAMPLE_BATCH_AXIS.items():
            out[name] = _to_microbatches(out[name], axis)
    return {'x': out['x'], 'ff1_norm_pre': out['ff1_norm_pre'], 'ff1_w_gate': out['ff1_w_gate'], 'ff1_w_up': out['ff1_w_up'], 'ff1_w_down': out['ff1_w_down'], 'ff1_norm_post': out['ff1_norm_post'], 'mix_norm_pre': out['mix_norm_pre'], 'w_in': out['w_in'], 'attn_sinks': out['attn_sinks'], 'dn_conv_w': out['dn_conv_w'], 'dn_a_log': out['dn_a_log'], 'dn_dt_bias': out['dn_dt_bias'], 'dn_norm_w': out['dn_norm_w'], 's5_a_re': out['s5_a_re'], 's5_a_im': out['s5_a_im'], 's5_log_dt': out['s5_log_dt'], 's5_b_re': out['s5_b_re'], 's5_b_im': out['s5_b_im'], 's5_c_re': out['s5_c_re'], 's5_c_im': out['s5_c_im'], 's5_d': out['s5_d'], 's5_glu_w': out['s5_glu_w'], 's5_glu_b': out['s5_glu_b'], 'w_out': out['w_out'], 'mix_norm_post': out['mix_norm_post'], 'ff2_norm_pre': out['ff2_norm_pre'], 'ff2_w_gate': out['ff2_w_gate'], 'ff2_w_up': out['ff2_w_up'], 'ff2_w_down': out['ff2_w_down'], 'ff2_norm_post': out['ff2_norm_post'], 'loss_target': out['loss_target'], 'm_ff1_norm_pre': out['m_ff1_norm_pre'], 'm_ff1_w_gate': out['m_ff1_w_gate'], 'm_ff1_w_up': out['m_ff1_w_up'], 'm_ff1_w_down': out['m_ff1_w_down'], 'm_ff1_norm_post': out['m_ff1_norm_post'], 'm_mix_norm_pre': out['m_mix_norm_pre'], 'm_w_in': out['m_w_in'], 'm_attn_sinks': out['m_attn_sinks'], 'm_dn_conv_w': out['m_dn_conv_w'], 'm_dn_a_log': out['m_dn_a_log'], 'm_dn_dt_bias': out['m_dn_dt_bias'], 'm_dn_norm_w': out['m_dn_norm_w'], 'm_s5_a_re': out['m_s5_a_re'], 'm_s5_a_im': out['m_s5_a_im'], 'm_s5_log_dt': out['m_s5_log_dt'], 'm_s5_b_re': out['m_s5_b_re'], 'm_s5_b_im': out['m_s5_b_im'], 'm_s5_c_re': out['m_s5_c_re'], 'm_s5_c_im': out['m_s5_c_im'], 'm_s5_d': out['m_s5_d'], 'm_s5_glu_w': out['m_s5_glu_w'], 'm_s5_glu_b': out['m_s5_glu_b'], 'm_w_out': out['m_w_out'], 'm_mix_norm_post': out['m_mix_norm_post'], 'm_ff2_norm_pre': out['m_ff2_norm_pre'], 'm_ff2_w_gate': out['m_ff2_w_gate'], 'm_ff2_w_up': out['m_ff2_w_up'], 'm_ff2_w_down': out['m_ff2_w_down'], 'm_ff2_norm_post': out['m_ff2_norm_post'], 'v_ff1_norm_pre': out['v_ff1_norm_pre'], 'v_ff1_w_gate': out['v_ff1_w_gate'], 'v_ff1_w_up': out['v_ff1_w_up'], 'v_ff1_w_down': out['v_ff1_w_down'], 'v_ff1_norm_post': out['v_ff1_norm_post'], 'v_mix_norm_pre': out['v_mix_norm_pre'], 'v_w_in': out['v_w_in'], 'v_attn_sinks': out['v_attn_sinks'], 'v_dn_conv_w': out['v_dn_conv_w'], 'v_dn_a_log': out['v_dn_a_log'], 'v_dn_dt_bias': out['v_dn_dt_bias'], 'v_dn_norm_w': out['v_dn_norm_w'], 'v_s5_a_re': out['v_s5_a_re'], 'v_s5_a_im': out['v_s5_a_im'], 'v_s5_log_dt': out['v_s5_log_dt'], 'v_s5_b_re': out['v_s5_b_re'], 'v_s5_b_im': out['v_s5_b_im'], 'v_s5_c_re': out['v_s5_c_re'], 'v_s5_c_im': out['v_s5_c_im'], 'v_s5_d': out['v_s5_d'], 'v_s5_glu_w': out['v_s5_glu_w'], 'v_s5_glu_b': out['v_s5_glu_b'], 'v_w_out': out['v_w_out'], 'v_mix_norm_post': out['v_mix_norm_post'], 'v_ff2_norm_pre': out['v_ff2_norm_pre'], 'v_ff2_w_gate': out['v_ff2_w_gate'], 'v_ff2_w_up': out['v_ff2_w_up'], 'v_ff2_w_down': out['v_ff2_w_down'], 'v_ff2_norm_post': out['v_ff2_norm_post']}


def _loss(weights, diff, rest, loss_target):
    with _jax.named_scope("forward"):
        args = {**rest, TWIN_DIFF_INPUT: diff, **{k: w.astype(_WEIGHT_DTYPES[k]) for k, w in weights.items()}}
        y = _forward(args)
    with _jax.named_scope("loss_head"):
        err = _jnp.square(y.astype(_jnp.float32) - loss_target)
        return 0.5 * _jnp.sum(_jnp.mean(err, axis=-1)) if err.ndim else 0.5 * err


def _adamw(w, g, m, v):
    m = ADAM_B1 * m + (1.0 - ADAM_B1) * g
    v = ADAM_B2 * v + (1.0 - ADAM_B2) * _jnp.square(g)
    m_hat = m / (1.0 - ADAM_B1 ** ADAM_STEP)
    v_hat = v / (1.0 - ADAM_B2 ** ADAM_STEP)
    delta = -ADAM_LR * (m_hat / (_jnp.sqrt(v_hat) + ADAM_EPS) + ADAM_WD * w)
    return delta, m, v


def reference(x, ff1_norm_pre, ff1_w_gate, ff1_w_up, ff1_w_down, ff1_norm_post, mix_norm_pre, w_in, attn_sinks, dn_conv_w, dn_a_log, dn_dt_bias, dn_norm_w, s5_a_re, s5_a_im, s5_log_dt, s5_b_re, s5_b_im, s5_c_re, s5_c_im, s5_d, s5_glu_w, s5_glu_b, w_out, mix_norm_post, ff2_norm_pre, ff2_w_gate, ff2_w_up, ff2_w_down, ff2_norm_post, loss_target, m_ff1_norm_pre, m_ff1_w_gate, m_ff1_w_up, m_ff1_w_down, m_ff1_norm_post, m_mix_norm_pre, m_w_in, m_attn_sinks, m_dn_conv_w, m_dn_a_log, m_dn_dt_bias, m_dn_norm_w, m_s5_a_re, m_s5_a_im, m_s5_log_dt, m_s5_b_re, m_s5_b_im, m_s5_c_re, m_s5_c_im, m_s5_d, m_s5_glu_w, m_s5_glu_b, m_w_out, m_mix_norm_post, m_ff2_norm_pre, m_ff2_w_gate, m_ff2_w_up, m_ff2_w_down, m_ff2_norm_post, v_ff1_norm_pre, v_ff1_w_gate, v_ff1_w_up, v_ff1_w_down, v_ff1_norm_post, v_mix_norm_pre, v_w_in, v_attn_sinks, v_dn_conv_w, v_dn_a_log, v_dn_dt_bias, v_dn_norm_w, v_s5_a_re, v_s5_a_im, v_s5_log_dt, v_s5_b_re, v_s5_b_im, v_s5_c_re, v_s5_c_im, v_s5_d, v_s5_glu_w, v_s5_glu_b, v_w_out, v_mix_norm_post, v_ff2_norm_pre, v_ff2_w_gate, v_ff2_w_up, v_ff2_w_down, v_ff2_norm_post):
    given = dict(x=x, ff1_norm_pre=ff1_norm_pre, ff1_w_gate=ff1_w_gate, ff1_w_up=ff1_w_up, ff1_w_down=ff1_w_down, ff1_norm_post=ff1_norm_post, mix_norm_pre=mix_norm_pre, w_in=w_in, attn_sinks=attn_sinks, dn_conv_w=dn_conv_w, dn_a_log=dn_a_log, dn_dt_bias=dn_dt_bias, dn_norm_w=dn_norm_w, s5_a_re=s5_a_re, s5_a_im=s5_a_im, s5_log_dt=s5_log_dt, s5_b_re=s5_b_re, s5_b_im=s5_b_im, s5_c_re=s5_c_re, s5_c_im=s5_c_im, s5_d=s5_d, s5_glu_w=s5_glu_w, s5_glu_b=s5_glu_b, w_out=w_out, mix_norm_post=mix_norm_post, ff2_norm_pre=ff2_norm_pre, ff2_w_gate=ff2_w_gate, ff2_w_up=ff2_w_up, ff2_w_down=ff2_w_down, ff2_norm_post=ff2_norm_post, loss_target=loss_target, m_ff1_norm_pre=m_ff1_norm_pre, m_ff1_w_gate=m_ff1_w_gate, m_ff1_w_up=m_ff1_w_up, m_ff1_w_down=m_ff1_w_down, m_ff1_norm_post=m_ff1_norm_post, m_mix_norm_pre=m_mix_norm_pre, m_w_in=m_w_in, m_attn_sinks=m_attn_sinks, m_dn_conv_w=m_dn_conv_w, m_dn_a_log=m_dn_a_log, m_dn_dt_bias=m_dn_dt_bias, m_dn_norm_w=m_dn_norm_w, m_s5_a_re=m_s5_a_re, m_s5_a_im=m_s5_a_im, m_s5_log_dt=m_s5_log_dt, m_s5_b_re=m_s5_b_re, m_s5_b_im=m_s5_b_im, m_s5_c_re=m_s5_c_re, m_s5_c_im=m_s5_c_im, m_s5_d=m_s5_d, m_s5_glu_w=m_s5_glu_w, m_s5_glu_b=m_s5_glu_b, m_w_out=m_w_out, m_mix_norm_post=m_mix_norm_post, m_ff2_norm_pre=m_ff2_norm_pre, m_ff2_w_gate=m_ff2_w_gate, m_ff2_w_up=m_ff2_w_up, m_ff2_w_down=m_ff2_w_down, m_ff2_norm_post=m_ff2_norm_post, v_ff1_norm_pre=v_ff1_norm_pre, v_ff1_w_gate=v_ff1_w_gate, v_ff1_w_up=v_ff1_w_up, v_ff1_w_down=v_ff1_w_down, v_ff1_norm_post=v_ff1_norm_post, v_mix_norm_pre=v_mix_norm_pre, v_w_in=v_w_in, v_attn_sinks=v_attn_sinks, v_dn_conv_w=v_dn_conv_w, v_dn_a_log=v_dn_a_log, v_dn_dt_bias=v_dn_dt_bias, v_dn_norm_w=v_dn_norm_w, v_s5_a_re=v_s5_a_re, v_s5_a_im=v_s5_a_im, v_s5_log_dt=v_s5_log_dt, v_s5_b_re=v_s5_b_re, v_s5_b_im=v_s5_b_im, v_s5_c_re=v_s5_c_re, v_s5_c_im=v_s5_c_im, v_s5_d=v_s5_d, v_s5_glu_w=v_s5_glu_w, v_s5_glu_b=v_s5_glu_b, v_w_out=v_w_out, v_mix_norm_post=v_mix_norm_post, v_ff2_norm_pre=v_ff2_norm_pre, v_ff2_w_gate=v_ff2_w_gate, v_ff2_w_up=v_ff2_w_up, v_ff2_w_down=v_ff2_w_down, v_ff2_norm_post=v_ff2_norm_post)
    weights = {n: given[n] for n in TWIN_WEIGHTS}
    shared = {n: given[n] for n in SHARED_INPUTS}
    per_example = {n: given[n] for n in ['x']}
    grad_fn = _jax.value_and_grad(_loss, argnums=(0, 1))

    def one_microbatch(ex, loss_target):
        ex = dict(ex)
        diff = ex.pop(TWIN_DIFF_INPUT)
        return grad_fn(weights, diff, {**shared, **ex}, loss_target)

    if N_MICROBATCH == 1:
        loss, (grad_w, grad_x) = one_microbatch(per_example, given["loss_target"])
    else:
        def body(carry, xs):
            loss_sum, grad_sum = carry
            l_k, (gw_k, gx_k) = one_microbatch(xs[0], xs[1])
            with _jax.named_scope("update"):
                return (loss_sum + l_k, _jax.tree.map(_jnp.add, grad_sum, gw_k)), gx_k

        init = (_jnp.zeros((), _jnp.float32), _jax.tree.map(_jnp.zeros_like, weights))
        (loss, grad_w), grad_x = _jax.lax.scan(body, init, (per_example, given["loss_target"]))
    with _jax.named_scope("update"):
        delta_w, new_m, new_v = {}, {}, {}
        for n in TWIN_WEIGHTS:
            delta_w[n], new_m[n], new_v[n] = _adamw(weights[n], grad_w[n], given["m_" + n], given["v_" + n])
    return (loss, grad_x, *[grad_w[n] for n in TWIN_WEIGHTS], *[delta_w[n] for n in TWIN_WEIGHTS],
            *[new_m[n] for n in TWIN_WEIGHTS], *[new_v[n] for n in TWIN_WEIGHTS])
```

```python
import functools
import math

import numpy as np
import jax
import jax.numpy as jnp
from jax import lax
from jax.experimental import pallas as pl
from jax.experimental.pallas import tpu as pltpu

F32 = jnp.float32
BF16 = jnp.bfloat16
HI = lax.Precision.HIGHEST
MESH_ID = pl.DeviceIdType.MESH

NORM_EPS = 1e-6
FFN_RES_WEIGHT = 0.5
ATTN_HEADS, ATTN_KV_HEADS, HEAD_DIM, WINDOW = 8, 2, 128, 128
ROPE_THETA = 10000.0
DN_HEADS, DN_HEAD_DIM, DN_CONV, DN_CHUNK = 4, 128, 4, 64
S5_GROUPS, S5_GROUP_CH, S5_STATE = 32, 16, 64
ATTN_WIDTH = ATTN_HEADS * HEAD_DIM
ATTN_KV_WIDTH = ATTN_KV_HEADS * HEAD_DIM
DN_WIDTH = DN_HEADS * DN_HEAD_DIM
S5_WIDTH = S5_GROUPS * S5_GROUP_CH
S5_LANES = S5_GROUPS * S5_STATE
MIX_WIDTH = ATTN_WIDTH + DN_WIDTH + S5_WIDTH
IN_SPLITS = (ATTN_WIDTH, ATTN_KV_WIDTH, ATTN_KV_WIDTH, 3 * DN_WIDTH, DN_WIDTH, DN_HEADS, DN_HEADS, S5_WIDTH)
IN_WIDTH = sum(IN_SPLITS)
Z_AQ, Z_AK, Z_AV = 0, ATTN_WIDTH, ATTN_WIDTH + ATTN_KV_WIDTH
Z_DQKV = ATTN_WIDTH + 2 * ATTN_KV_WIDTH
Z_DZ = Z_DQKV + 3 * DN_WIDTH
Z_SU = Z_DZ + DN_WIDTH
Z_DBA = Z_SU + S5_WIDTH
Z_WIDTH = Z_DBA + 128

ADAM_LR, ADAM_B1, ADAM_B2, ADAM_EPS, ADAM_WD, ADAM_STEP = 0.001, 0.9, 0.999, 1e-08, 0.01, 10

N_SHARD = 4
FLAT_LANES = 512
VMEM_LIMIT = 56 * 1024 * 1024

WEIGHTS = ['ff1_norm_pre', 'ff1_w_gate', 'ff1_w_up', 'ff1_w_down', 'ff1_norm_post', 'mix_norm_pre', 'w_in',
           'attn_sinks', 'dn_conv_w', 'dn_a_log', 'dn_dt_bias', 'dn_norm_w', 's5_a_re', 's5_a_im', 's5_log_dt',
           's5_b_re', 's5_b_im', 's5_c_re', 's5_c_im', 's5_d', 's5_glu_w', 's5_glu_b', 'w_out', 'mix_norm_post',
           'ff2_norm_pre', 'ff2_w_gate', 'ff2_w_up', 'ff2_w_down', 'ff2_norm_post']
BIG = ['ff1_w_gate', 'ff1_w_up', 'ff1_w_down', 'w_in', 's5_glu_w', 'w_out', 'ff2_w_gate', 'ff2_w_up', 'ff2_w_down']
SMALL = [n for n in WEIGHTS if n not in BIG]


def _pick(dim, cands):
    for c in cands:
        if dim % c == 0:
            return c
    return dim


def _params(sem=None):
    return pltpu.CompilerParams(dimension_semantics=sem, vmem_limit_bytes=VMEM_LIMIT)


def _mm(a, b, *, ta=False, tb=False, out_dtype=F32, name, b_groups=None, bk_groups=None, out_groups=None):
    K, M = a.shape if ta else a.shape[::-1]
    Ng = Kg = None
    if b_groups:
        assert not tb
        _, Kb, Ng = b.shape
        N = b_groups * Ng
    elif bk_groups:
        assert tb
        _, N, Kg = b.shape
        Kb = bk_groups * Kg
    else:
        N, Kb = (b.shape if tb else b.shape[::-1])
    assert K == Kb, (a.shape, b.shape, ta, tb)
    tiles = (1408, 1024, 512, 384, 256, 128)
    tm = _pick(M, (512, 256, 128))
    tk = _pick(Kg if bk_groups else K, tiles)
    tn = _pick(N // out_groups if out_groups else (Ng if b_groups else N), tiles)
    nk = K // tk
    dims = (((0,) if ta else (1,), (1,) if tb else (0,)), ((), ()))

    def body(a_ref, b_ref, o_ref, acc_ref):
        k = pl.program_id(2)

        @pl.when(k == 0)
        def _():
            acc_ref[...] = jnp.zeros_like(acc_ref)

        acc_ref[...] += lax.dot_general(a_ref[...].astype(BF16), b_ref[...].astype(BF16), dims,
                                        preferred_element_type=F32)

        @pl.when(k == nk - 1)
        def _():
            o_ref[...] = acc_ref[...].astype(o_ref.dtype)

    a_spec = pl.BlockSpec((tk, tm), lambda i, j, k: (k, i)) if ta else pl.BlockSpec((tm, tk), lambda i, j, k: (i, k))
    if b_groups:
        per = Ng // tn
        b_spec = pl.BlockSpec((None, tk, tn), lambda i, j, k: (j // per, k, j % per))
    elif bk_groups:
        per = Kg // tk
        b_spec = pl.BlockSpec((None, tn, tk), lambda i, j, k: (k // per, j, k % per))
    else:
        b_spec = pl.BlockSpec((tn, tk), lambda i, j, k: (j, k)) if tb else pl.BlockSpec((tk, tn), lambda i, j, k: (k, j))
    if out_groups:
        pero = (N // out_groups) // tn
        o_spec = pl.BlockSpec((None, tm, tn), lambda i, j, k: (j // pero, i, j % pero))
        o_shape = jax.ShapeDtypeStruct((out_groups, M, N // out_groups), out_dtype)
    else:
        o_spec = pl.BlockSpec((tm, tn), lambda i, j, k: (i, j))
        o_shape = jax.ShapeDtypeStruct((M, N), out_dtype)
    return pl.pallas_call(
        body, name=name, grid=(M // tm, N // tn, nk), in_specs=[a_spec, b_spec], out_specs=o_spec, out_shape=o_shape,
        scratch_shapes=[pltpu.VMEM((tm, tn), F32)],
        compiler_params=_params(("parallel", "parallel", "arbitrary")))(a, b)


def _row_spec(tm, width, off):
    return pl.BlockSpec((tm, width), lambda i, j: (i, off + j))


def _const_spec(shape):
    return pl.BlockSpec(shape, lambda i, j: (0,) * len(shape))


def _rows_fwd(fn, rows, consts, outs, *, name, tm=256, ncol=1):
    S = rows[0][0].shape[0]
    tm = _pick(S, (tm, 128, 64))
    nr, nc = len(rows), len(consts)

    def body(*refs):
        vals = [r[...].astype(F32) for r in refs[:nr + nc]]
        res = fn(*vals)
        for o_ref, o in zip(refs[nr + nc:], res):
            o_ref[...] = o.astype(o_ref.dtype)

    return pl.pallas_call(
        body, name=name, grid=(S // tm, ncol),
        in_specs=[_row_spec(tm, w, off) for _, w, off in rows] + [_const_spec(c.shape) for c in consts],
        out_specs=[_row_spec(tm, bw, 0) for _, bw, _ in outs],
        out_shape=[jax.ShapeDtypeStruct((S, tw), dt) for tw, _, dt in outs],
        compiler_params=_params(("parallel", "parallel")))(*[r[0] for r in rows], *consts)


def _rows_bwd(fn, rows, consts, cts, row_grads, *, name, tm=256, ncol=1, add_to_first=None, lane_sum_consts=()):
    S = rows[0][0].shape[0]
    tm = _pick(S, (tm, 128, 64))
    nr, nc, nt = len(rows), len(consts), len(cts)
    n_in = nr + nc + nt + (1 if add_to_first is not None else 0)

    def body(*refs):
        vals = [r[...].astype(F32) for r in refs[:nr + nc]]
        ct = tuple(r[...].astype(F32) for r in refs[nr + nc:nr + nc + nt])
        _, vjp = jax.vjp(fn, *vals)
        grads = vjp(ct)
        outs = refs[n_in:]
        for n, (idx, _, _) in enumerate(row_grads):
            g = grads[idx]
            if n == 0 and add_to_first is not None:
                g = g + refs[n_in - 1][...].astype(F32)
            outs[n][...] = g.astype(outs[n].dtype)
        first = jnp.logical_and(pl.program_id(0) == 0, pl.program_id(1) == 0)
        for c in range(nc):
            o_ref = outs[len(row_grads) + c]
            g = grads[nr + c]
            if c in lane_sum_consts:
                g = jnp.broadcast_to(jnp.sum(g, axis=-1, keepdims=True), g.shape)

            @pl.when(first)
            def _():
                o_ref[...] = jnp.zeros_like(o_ref)

            o_ref[...] += g

    in_specs = ([_row_spec(tm, w, off) for _, w, off in rows] + [_const_spec(c.shape) for c in consts]
                + [_row_spec(tm, w, off) for _, w, off in cts])
    args = [r[0] for r in rows] + list(consts) + [c[0] for c in cts]
    if add_to_first is not None:
        in_specs.append(_row_spec(tm, rows[row_grads[0][0]][1], 0))
        args.append(add_to_first)
    out_specs = [_row_spec(tm, rows[idx][1], 0) for idx, _, _ in row_grads] + [_const_spec(c.shape) for c in consts]
    out_shape = ([jax.ShapeDtypeStruct((S, tw), dt) for _, tw, dt in row_grads]
                 + [jax.ShapeDtypeStruct(c.shape, F32) for c in consts])
    return pl.pallas_call(
        body, name=name, grid=(S // tm, ncol), in_specs=in_specs, out_specs=out_specs, out_shape=out_shape,
        compiler_params=_params(("arbitrary", "arbitrary")))(*args)


def _rms(x, gain):
    return x * lax.rsqrt(jnp.mean(x * x, axis=-1, keepdims=True) + NORM_EPS) * gain


def _prenorm_fn(x, gain):
    return (_rms(x, gain),)


def _postnorm_fn(weight):
    def fn(y, gain):
        return (weight * _rms(y, gain),)
    return fn


def _residual_fn(weight):
    def fn(x, y, gain):
        return (x + weight * _rms(y, gain),)
    return fn


def _swiglu_fn(blk):
    tf = blk.shape[1] // 2
    gate, up = blk[:, :tf], blk[:, tf:]
    return (gate * jax.nn.sigmoid(gate) * up,)


def _ffn_fwd(x, g_pre, g_post, wgu, wd, tag):
    D = x.shape[1]
    Fs = wgu.shape[2] // 2
    (h,) = _rows_fwd(_prenorm_fn, [(x, D, 0)], [g_pre], [(D, D, BF16)], name=f"{tag}_prenorm")
    gu = _mm(h, wgu, b_groups=N_SHARD, out_dtype=BF16, name=f"{tag}_gate_up")
    (act,) = _rows_fwd(_swiglu_fn, [(gu, 2 * Fs, 0)], [], [(N_SHARD * Fs, Fs, BF16)], name=f"{tag}_swiglu",
                       ncol=N_SHARD)
    y = _mm(act, wd, out_dtype=F32, name=f"{tag}_down")
    (x_new,) = _rows_fwd(_residual_fn(FFN_RES_WEIGHT), [(x, D, 0), (y, D, 0)], [g_post], [(D, D, F32)],
                         name=f"{tag}_residual")
    return x_new, (h, gu, act, y)


def _ffn_bwd(dx_new, x, g_pre, g_post, wgu, wd, saved, tag):
    h, gu, act, y = saved
    D = x.shape[1]
    Fs = wgu.shape[2] // 2
    d_y, d_g_post = _rows_bwd(_postnorm_fn(FFN_RES_WEIGHT), [(y, D, 0)], [g_post], [(dx_new, D, 0)],
                              [(0, D, BF16)], name=f"{tag}_postnorm_bwd")
    d_act = _mm(d_y, wd, tb=True, out_dtype=BF16, name=f"{tag}_down_dx")
    d_wd = _mm(act, d_y, ta=True, out_dtype=BF16, name=f"{tag}_down_dw")
    (d_gu,) = _rows_bwd(_swiglu_fn, [(gu, 2 * Fs, 0)], [], [(d_act, Fs, 0)], [(0, 2 * N_SHARD * Fs, BF16)],
                        name=f"{tag}_swiglu_bwd", ncol=N_SHARD)
    d_wgu = _mm(h, d_gu, ta=True, out_dtype=BF16, out_groups=N_SHARD, name=f"{tag}_gate_up_dw")
    d_h = _mm(d_gu, wgu, tb=True, bk_groups=N_SHARD, out_dtype=BF16, name=f"{tag}_gate_up_dx")
    dx, d_g_pre = _rows_bwd(_prenorm_fn, [(x, D, 0)], [g_pre], [(d_h, D, 0)], [(0, D, F32)],
                            name=f"{tag}_prenorm_bwd", add_to_first=dx_new)
    return dx, d_g_pre, d_g_post, d_wgu, d_wd


@jax.custom_vjp
def _swap_halves(x):
    return pltpu.roll(x, HEAD_DIM // 2, 1)


def _swap_fwd(x):
    return _swap_halves(x), None


def _swap_bwd(_, g):
    return (_swap_halves(g),)


_swap_halves.defvjp(_swap_fwd, _swap_bwd)


def _rope(x, cc, ss):
    return x * cc + _swap_halves(x) * ss


def _attn_block_fn(has_prev):
    grp = ATTN_HEADS // ATTN_KV_HEADS
    scale = HEAD_DIM ** -0.5
    nt = (((1,), (1,)), ((), ()))
    nn = (((1,), (0,)), ((), ()))

    def fn(*a):
        q = a[:8]
        kp, kc, vp, vc = a[8:10], a[10:12], a[12:14], a[14:16]
        cc, ss, ccp, ssp, sinks = a[16:21]
        row = lax.broadcasted_iota(jnp.int32, (WINDOW, WINDOW), 0)
        col = lax.broadcasted_iota(jnp.int32, (WINDOW, WINDOW), 1)
        m_cur = col <= row
        m_prev = jnp.logical_and(col > row, has_prev)
        outs = []
        for h in range(ATTN_HEADS):
            g = h // grp
            qr = _rope(q[h], cc, ss).astype(BF16)
            kcr = _rope(kc[g], cc, ss).astype(BF16)
            kpr = _rope(kp[g], ccp, ssp).astype(BF16)
            s_c = jnp.where(m_cur, lax.dot_general(qr, kcr, nt, preferred_element_type=F32) * scale, -jnp.inf)
            s_p = jnp.where(m_prev, lax.dot_general(qr, kpr, nt, preferred_element_type=F32) * scale, -jnp.inf)
            sink = sinks[h:h + 1, :]
            m = jnp.maximum(jnp.maximum(jnp.max(s_c, axis=-1, keepdims=True), jnp.max(s_p, axis=-1, keepdims=True)),
                            sink)
            p_c = jnp.exp(s_c - m)
            p_p = jnp.exp(s_p - m)
            den = (jnp.sum(p_c, axis=-1, keepdims=True) + jnp.sum(p_p, axis=-1, keepdims=True) + jnp.exp(sink - m))
            inv = 1.0 / den
            o = (lax.dot_general((p_c * inv).astype(BF16), vc[g].astype(BF16), nn, preferred_element_type=F32)
                 + lax.dot_general((p_p * inv).astype(BF16), vp[g].astype(BF16), nn, preferred_element_type=F32))
            outs.append(o)
        return tuple(outs)

    return fn


def _attn_specs(order):
    kvb = ATTN_WIDTH // (2 * ATTN_KV_WIDTH)
    return [
        pl.BlockSpec((WINDOW, ATTN_WIDTH), lambda i: (order(i), 0)),
        pl.BlockSpec((WINDOW, 2 * ATTN_KV_WIDTH), lambda i: (jnp.maximum(order(i) - 1, 0), kvb)),
        pl.BlockSpec((WINDOW, 2 * ATTN_KV_WIDTH), lambda i: (order(i), kvb)),
        pl.BlockSpec((WINDOW, HEAD_DIM), lambda i: (order(i), 0)),
        pl.BlockSpec((WINDOW, HEAD_DIM), lambda i: (order(i), 0)),
        pl.BlockSpec((WINDOW, HEAD_DIM), lambda i: (jnp.maximum(order(i) - 1, 0), 0)),
        pl.BlockSpec((WINDOW, HEAD_DIM), lambda i: (jnp.maximum(order(i) - 1, 0), 0)),
        pl.BlockSpec((ATTN_HEADS, HEAD_DIM), lambda i: (0, 0)),
    ]


def _attn_args(q_ref, kvp_ref, kvc_ref, cc, ss, ccp, ssp, sinks):
    d = HEAD_DIM
    q = [q_ref[:, h * d:(h + 1) * d].astype(F32) for h in range(ATTN_HEADS)]
    kp = [kvp_ref[:, g * d:(g + 1) * d].astype(F32) for g in range(ATTN_KV_HEADS)]
    vp = [kvp_ref[:, ATTN_KV_WIDTH + g * d:ATTN_KV_WIDTH + (g + 1) * d].astype(F32) for g in range(ATTN_KV_HEADS)]
    kc = [kvc_ref[:, g * d:(g + 1) * d].astype(F32) for g in range(ATTN_KV_HEADS)]
    vc = [kvc_ref[:, ATTN_KV_WIDTH + g * d:ATTN_KV_WIDTH + (g + 1) * d].astype(F32) for g in range(ATTN_KV_HEADS)]
    return q + kp + kc + vp + vc + [cc[...], ss[...], ccp[...], ssp[...], sinks[...]]


def _attn_fwd(z, cc, ss, sinks_b, tag):
    S = z.shape[0]
    nb = S // WINDOW

    def body(q_ref, kvp_ref, kvc_ref, cc_r, ss_r, ccp_r, ssp_r, sink_r, o_ref):
        n = pl.program_id(0)
        outs = _attn_block_fn(n > 0)(*_attn_args(q_ref, kvp_ref, kvc_ref, cc_r, ss_r, ccp_r, ssp_r, sink_r))
        for h in range(ATTN_HEADS):
            o_ref[:, h * HEAD_DIM:(h + 1) * HEAD_DIM] = outs[h].astype(o_ref.dtype)

    return pl.pallas_call(
        body, name=f"{tag}_attn", grid=(nb,), in_specs=_attn_specs(lambda i: i),
        out_specs=pl.BlockSpec((WINDOW, ATTN_WIDTH), lambda i: (i, 0)),
        out_shape=jax.ShapeDtypeStruct((S, ATTN_WIDTH), BF16),
        compiler_params=_params(("parallel",)))(z, z, z, cc, ss, cc, ss, sinks_b)


def _attn_bwd(z, cc, ss, sinks_b, d_out, tag):
    S = z.shape[0]
    nb = S // WINDOW
    d = HEAD_DIM
    rev = lambda i: nb - 1 - i

    def body(q_ref, kvp_ref, kvc_ref, cc_r, ss_r, ccp_r, ssp_r, sink_r, do_ref, dq_ref, dkv_ref, dsink_ref, carry):
        i = pl.program_id(0)
        n = nb - 1 - i

        @pl.when(i == 0)
        def _():
            carry[...] = jnp.zeros_like(carry)
            dsink_ref[...] = jnp.zeros_like(dsink_ref)

        args = _attn_args(q_ref, kvp_ref, kvc_ref, cc_r, ss_r, ccp_r, ssp_r, sink_r)
        _, vjp = jax.vjp(_attn_block_fn(n > 0), *args)
        g = vjp(tuple(do_ref[:, h * d:(h + 1) * d].astype(F32) for h in range(ATTN_HEADS)))
        for h in range(ATTN_HEADS):
            dq_ref[:, h * d:(h + 1) * d] = g[h].astype(dq_ref.dtype)
        for gi in range(ATTN_KV_HEADS):
            ks = slice(gi * d, (gi + 1) * d)
            vs = slice(ATTN_KV_WIDTH + gi * d, ATTN_KV_WIDTH + (gi + 1) * d)
            dkv_ref[:, ks] = (g[10 + gi] + carry[:, ks]).astype(dkv_ref.dtype)
            dkv_ref[:, vs] = (g[14 + gi] + carry[:, vs]).astype(dkv_ref.dtype)
            carry[:, ks] = g[8 + gi]
            carry[:, vs] = g[12 + gi]
        ds = g[20]
        dsink_ref[...] += jnp.broadcast_to(jnp.sum(ds, axis=-1, keepdims=True), ds.shape)

    return pl.pallas_call(
        body, name=f"{tag}_attn_bwd", grid=(nb,),
        in_specs=_attn_specs(rev) + [pl.BlockSpec((WINDOW, ATTN_WIDTH), lambda i: (rev(i), 0))],
        out_specs=[pl.BlockSpec((WINDOW, ATTN_WIDTH), lambda i: (rev(i), 0)),
                   pl.BlockSpec((WINDOW, 2 * ATTN_KV_WIDTH), lambda i: (rev(i), 0)),
                   pl.BlockSpec((ATTN_HEADS, HEAD_DIM), lambda i: (0, 0))],
        out_shape=[jax.ShapeDtypeStruct((S, ATTN_WIDTH), BF16), jax.ShapeDtypeStruct((S, 2 * ATTN_KV_WIDTH), BF16),
                   jax.ShapeDtypeStruct((ATTN_HEADS, HEAD_DIM), F32)],
        scratch_shapes=[pltpu.VMEM((WINDOW, 2 * ATTN_KV_WIDTH), F32)],
        compiler_params=_params(("arbitrary",)))(z, z, z, cc, ss, cc, ss, sinks_b, d_out)


def _rope_tables(seq):
    half = HEAD_DIM // 2
    inv_freq = ROPE_THETA ** (-jnp.arange(half, dtype=F32) / half)
    ang = jnp.arange(seq, dtype=F32)[:, None] * inv_freq[None, :]
    cos, sin = jnp.cos(ang), jnp.sin(ang)
    return jnp.concatenate([cos, cos], axis=1), jnp.concatenate([-sin, sin], axis=1)


CONV_COLS = 128


def _conv_pre(u, w_ref, S):
    row = lax.broadcasted_iota(jnp.int32, u.shape, 0)
    shifted = [u] + [jnp.where(row >= s, pltpu.roll(u, s, 0), 0.0) for s in range(1, DN_CONV)]
    y = shifted[0] * w_ref[DN_CONV - 1:DN_CONV, :]
    for s in range(1, DN_CONV):
        y = y + shifted[s] * w_ref[DN_CONV - 1 - s:DN_CONV - s, :]
    return y, shifted, row


def _conv_fwd(z, conv_w, tag):
    S = z.shape[0]
    ncol = 3 * DN_WIDTH // CONV_COLS

    def body(u_ref, w_ref, o_ref):
        y, _, _ = _conv_pre(u_ref[...], w_ref, S)
        o_ref[...] = y * jax.nn.sigmoid(y)

    return pl.pallas_call(
        body, name=f"{tag}_conv", grid=(ncol,),
        in_specs=[pl.BlockSpec((S, CONV_COLS), lambda j: (0, Z_DQKV // CONV_COLS + j)),
                  pl.BlockSpec((DN_CONV, CONV_COLS), lambda j: (0, j))],
        out_specs=pl.BlockSpec((S, CONV_COLS), lambda j: (0, j)),
        out_shape=jax.ShapeDtypeStruct((S, 3 * DN_WIDTH), F32),
        compiler_params=_params(("parallel",)))(z, conv_w)


def _conv_bwd(z, conv_w, d_out, tag):
    S = z.shape[0]
    ncol = 3 * DN_WIDTH // CONV_COLS

    def body(u_ref, w_ref, do_ref, du_ref, dw_ref):
        y, shifted, row = _conv_pre(u_ref[...], w_ref, S)
        sg = jax.nn.sigmoid(y)
        d_y = do_ref[...] * (sg * (1.0 + y * (1.0 - sg)))
        d_u = d_y * w_ref[DN_CONV - 1:DN_CONV, :]
        dw_ref[DN_CONV - 1:DN_CONV, :] = jnp.sum(d_y * shifted[0], axis=0, keepdims=True)
        for s in range(1, DN_CONV):
            back = jnp.where(row < S - s, pltpu.roll(d_y, S - s, 0), 0.0)
            d_u = d_u + back * w_ref[DN_CONV - 1 - s:DN_CONV - s, :]
            dw_ref[DN_CONV - 1 - s:DN_CONV - s, :] = jnp.sum(d_y * shifted[s], axis=0, keepdims=True)
        du_ref[...] = d_u.astype(du_ref.dtype)

    return pl.pallas_call(
        body, name=f"{tag}_conv_bwd", grid=(ncol,),
        in_specs=[pl.BlockSpec((S, CONV_COLS), lambda j: (0, Z_DQKV // CONV_COLS + j)),
                  pl.BlockSpec((DN_CONV, CONV_COLS), lambda j: (0, j)),
                  pl.BlockSpec((S, CONV_COLS), lambda j: (0, j))],
        out_specs=[pl.BlockSpec((S, CONV_COLS), lambda j: (0, j)), pl.BlockSpec((DN_CONV, CONV_COLS), lambda j: (0, j))],
        out_shape=[jax.ShapeDtypeStruct((S, 3 * DN_WIDTH), BF16), jax.ShapeDtypeStruct((DN_CONV, 3 * DN_WIDTH), F32)],
        compiler_params=_params(("parallel",)))(z, conv_w, d_out)


def _dot(a, b, dims=(((1,), (0,)), ((), ()))):
    return lax.dot_general(a, b, dims, precision=HI, preferred_element_type=F32)


_NT = (((1,), (1,)), ((), ()))
_TN = (((0,), (0,)), ((), ()))


def _dn_head_fn(h):
    C, dk = DN_CHUNK, DN_HEAD_DIM

    def fn(state, q, k, v, zg, ba, prm, norm_w):
        lane = lax.broadcasted_iota(jnp.int32, (C, dk), 1)
        rowl = lax.broadcasted_iota(jnp.int32, (C, dk), 0)
        row = lax.broadcasted_iota(jnp.int32, (C, C), 0)
        col = lax.broadcasted_iota(jnp.int32, (C, C), 1)
        bcol = jnp.sum(jnp.where(lane == h, ba, 0.0), axis=-1, keepdims=True)
        acol = jnp.sum(jnp.where(lane == DN_HEADS + h, ba, 0.0), axis=-1, keepdims=True)
        a_log, dtb = prm[h:h + 1, :], prm[DN_HEADS + h:DN_HEADS + h + 1, :]
        qn = q * lax.rsqrt(jnp.sum(q * q, axis=-1, keepdims=True) + NORM_EPS) * (dk ** -0.5)
        kn = k * lax.rsqrt(jnp.sum(k * k, axis=-1, keepdims=True) + NORM_EPS)
        beta = jax.nn.sigmoid(bcol)
        sp_in = acol + dtb
        softplus = jnp.maximum(sp_in, 0.0) + jnp.log(1.0 + jnp.exp(-jnp.abs(sp_in)))
        gt = -jnp.exp(a_log) * softplus
        gc = _dot((row >= col).astype(F32), gt)
        gcol = jnp.mean(gc, axis=-1, keepdims=True)
        grow = _dot(jnp.full((C, dk), 1.0 / dk, F32), gc, _NT)
        decay = jnp.exp(jnp.where(row >= col, gcol - grow, -jnp.inf))
        kb = kn * beta
        m = -jnp.where(row > col, _dot(kb, kn, _NT) * decay, 0.0)
        u = v * beta
        w = kb * jnp.exp(gc)
        for step in range(6):
            u = u + _dot(m, u)
            w = w + _dot(m, w)
            if step < 5:
                m = _dot(m, m)
        attn = jnp.where(row >= col, _dot(qn, kn, _NT) * decay, 0.0)
        q_dec = qn * jnp.exp(gc)
        gl = jnp.sum(jnp.where(rowl == C - 1, gc, 0.0), axis=0, keepdims=True)
        k_dec = kn * jnp.exp(gl - gc)
        v_new = u - _dot(w, state)
        o = _dot(q_dec, state) + _dot(attn, v_new)
        state_new = state * jnp.exp(gl) + _dot(k_dec, v_new, _TN)
        y = o * lax.rsqrt(jnp.mean(o * o, axis=-1, keepdims=True) + NORM_EPS) * norm_w
        y = y * (zg * jax.nn.sigmoid(zg))
        return state_new, y

    return fn


def _dn_specs(order):
    C = DN_CHUNK
    return [pl.BlockSpec((C, 3 * DN_WIDTH), lambda i: (order(i), 0)),
            pl.BlockSpec((C, DN_WIDTH), lambda i: (order(i), Z_DZ // DN_WIDTH)),
            pl.BlockSpec((C, 128), lambda i: (order(i), Z_DBA // 128)),
            pl.BlockSpec((8, 128), lambda i: (0, 0)),
            pl.BlockSpec((1, DN_HEAD_DIM), lambda i: (0, 0))]


def _dn_head_args(h, qkv_ref, zg_ref, ba_ref, prm_ref, nw_ref):
    d = DN_HEAD_DIM
    sl = lambda base: slice(base + h * d, base + (h + 1) * d)
    return [qkv_ref[:, sl(0)], qkv_ref[:, sl(DN_WIDTH)], qkv_ref[:, sl(2 * DN_WIDTH)], zg_ref[:, sl(0)].astype(F32),
            ba_ref[...].astype(F32), prm_ref[...], nw_ref[...]]


def _dn_fwd(qkv, z, prm, norm_w, tag):
    S = qkv.shape[0]
    nchunk = S // DN_CHUNK
    d = DN_HEAD_DIM

    def body(qkv_ref, zg_ref, ba_ref, prm_ref, nw_ref, y_ref, st_ref, state):
        @pl.when(pl.program_id(0) == 0)
        def _():
            state[...] = jnp.zeros_like(state)

        for h in range(DN_HEADS):
            st_ref[h] = state[h]
            new, y = _dn_head_fn(h)(state[h], *_dn_head_args(h, qkv_ref, zg_ref, ba_ref, prm_ref, nw_ref))
            state[h] = new
            y_ref[:, h * d:(h + 1) * d] = y.astype(y_ref.dtype)

    return pl.pallas_call(
        body, name=f"{tag}_deltanet", grid=(nchunk,), in_specs=_dn_specs(lambda i: i),
        out_specs=[pl.BlockSpec((DN_CHUNK, DN_WIDTH), lambda i: (i, 0)),
                   pl.BlockSpec((None, DN_HEADS, d, d), lambda i: (i, 0, 0, 0))],
        out_shape=[jax.ShapeDtypeStruct((S, DN_WIDTH), BF16), jax.ShapeDtypeStruct((nchunk, DN_HEADS, d, d), F32)],
        scratch_shapes=[pltpu.VMEM((DN_HEADS, d, d), F32)],
        compiler_params=_params(("arbitrary",)))(qkv, z, z, prm, norm_w)


def _dn_bwd(qkv, z, prm, norm_w, states, d_y, tag):
    S = qkv.shape[0]
    nchunk = S // DN_CHUNK
    d = DN_HEAD_DIM
    rev = lambda i: nchunk - 1 - i

    def body(qkv_ref, zg_ref, ba_ref, prm_ref, nw_ref, st_ref, dy_ref,
             dqkv_ref, dzg_ref, dba_ref, dprm_ref, dnw_ref, d_state):
        @pl.when(pl.program_id(0) == 0)
        def _():
            d_state[...] = jnp.zeros_like(d_state)
            dprm_ref[...] = jnp.zeros_like(dprm_ref)
            dnw_ref[...] = jnp.zeros_like(dnw_ref)

        d_ba = jnp.zeros((DN_CHUNK, 128), F32)
        d_prm = jnp.zeros((8, 128), F32)
        d_nw = jnp.zeros((1, d), F32)
        for h in range(DN_HEADS):
            args = [st_ref[h]] + _dn_head_args(h, qkv_ref, zg_ref, ba_ref, prm_ref, nw_ref)
            _, vjp = jax.vjp(_dn_head_fn(h), *args)
            g = vjp((d_state[h], dy_ref[:, h * d:(h + 1) * d].astype(F32)))
            d_state[h] = g[0]
            for n, base in enumerate((0, DN_WIDTH, 2 * DN_WIDTH)):
                dqkv_ref[:, base + h * d:base + (h + 1) * d] = g[1 + n]
            dzg_ref[:, h * d:(h + 1) * d] = g[4].astype(dzg_ref.dtype)
            d_ba = d_ba + g[5]
            d_prm = d_prm + g[6]
            d_nw = d_nw + g[7]
        dba_ref[...] = d_ba.astype(dba_ref.dtype)
        dprm_ref[...] += jnp.broadcast_to(jnp.sum(d_prm, axis=-1, keepdims=True), d_prm.shape)
        dnw_ref[...] += d_nw

    return pl.pallas_call(
        body, name=f"{tag}_deltanet_bwd", grid=(nchunk,),
        in_specs=_dn_specs(rev) + [pl.BlockSpec((None, DN_HEADS, d, d), lambda i: (rev(i), 0, 0, 0)),
                                   pl.BlockSpec((DN_CHUNK, DN_WIDTH), lambda i: (rev(i), 0))],
        out_specs=[pl.BlockSpec((DN_CHUNK, 3 * DN_WIDTH), lambda i: (rev(i), 0)),
                   pl.BlockSpec((DN_CHUNK, DN_WIDTH), lambda i: (rev(i), 0)),
                   pl.BlockSpec((DN_CHUNK, 128), lambda i: (rev(i), 0)),
                   pl.BlockSpec((8, 128), lambda i: (0, 0)),
                   pl.BlockSpec((1, d), lambda i: (0, 0))],
        out_shape=[jax.ShapeDtypeStruct((S, 3 * DN_WIDTH), F32), jax.ShapeDtypeStruct((S, DN_WIDTH), BF16),
                   jax.ShapeDtypeStruct((S, 128), BF16), jax.ShapeDtypeStruct((8, 128), F32),
                   jax.ShapeDtypeStruct((1, d), F32)],
        scratch_shapes=[pltpu.VMEM((DN_HEADS, d, d), F32)],
        compiler_params=_params(("arbitrary",)))(qkv, z, z, prm, norm_w, states, d_y)


def _whole_fwd(fn, ins, outs, *, name):
    n = len(ins)

    def body(*refs):
        res = fn(*[r[...] for r in refs[:n]])
        for o_ref, o in zip(refs[n:], res):
            o_ref[...] = o

    return pl.pallas_call(body, name=name, out_shape=[jax.ShapeDtypeStruct(s, F32) for s in outs],
                          compiler_params=_params())(*ins)


def _whole_bwd(fn, ins, cts, n_grads, *, name, lane_sum=()):
    n, nt = len(ins), len(cts)

    def body(*refs):
        _, vjp = jax.vjp(fn, *[r[...] for r in refs[:n]])
        grads = vjp(tuple(r[...] for r in refs[n:n + nt]))
        for k in range(n_grads):
            g = grads[k]
            if k in lane_sum:
                g = jnp.broadcast_to(jnp.sum(g, axis=-1, keepdims=True), g.shape)
            refs[n + nt + k][...] = g

    return pl.pallas_call(body, name=name, out_shape=[jax.ShapeDtypeStruct(a.shape, F32) for a in ins[:n_grads]],
                          compiler_params=_params())(*ins, *cts)


S5_CHUNK = 256


def _s5_param_fn(a_re, a_im, ldt, bt_re, bt_im, expand):
    dt = jnp.exp(ldt)
    er = jnp.exp(a_re * dt)
    ab_re, ab_im = er * jnp.cos(a_im * dt), er * jnp.sin(a_im * dt)
    den = a_re * a_re + a_im * a_im
    co_re = ((ab_re - 1.0) * a_re + ab_im * a_im) / den
    co_im = (ab_im * a_re - (ab_re - 1.0) * a_im) / den
    cr, ci = _dot(expand, co_re), _dot(expand, co_im)
    return ab_re, ab_im, cr * bt_re - ci * bt_im, cr * bt_im + ci * bt_re


def _s5_scan(b_re, b_im, a_re, a_im, row, T, reverse):
    x_re, x_im, p_re, p_im = b_re, b_im, a_re, a_im
    d = 1
    while d < T:
        if reverse:
            s_re = jnp.where(row < T - d, pltpu.roll(x_re, T - d, 0), 0.0)
            s_im = jnp.where(row < T - d, pltpu.roll(x_im, T - d, 0), 0.0)
        else:
            s_re = jnp.where(row >= d, pltpu.roll(x_re, d, 0), 0.0)
            s_im = jnp.where(row >= d, pltpu.roll(x_im, d, 0), 0.0)
        x_re, x_im = x_re + p_re * s_re - p_im * s_im, x_im + p_re * s_im + p_im * s_re
        p_re, p_im = p_re * p_re - p_im * p_im, 2.0 * p_re * p_im
        d *= 2
    return x_re, x_im


def _s5_states(u, bre_ref, bim_ref, a_re, a_im, c_re, c_im, row, T):
    bu_re = _dot(u, bre_ref[...]) + jnp.where(row == 0, a_re * c_re - a_im * c_im, 0.0)
    bu_im = _dot(u, bim_ref[...]) + jnp.where(row == 0, a_re * c_im + a_im * c_re, 0.0)
    return _s5_scan(bu_re, bu_im, a_re, a_im, row, T, False)


def _s5_in_specs(order, T):
    full = lambda shape: pl.BlockSpec(shape, lambda i: (0,) * len(shape))
    return [pl.BlockSpec((T, S5_WIDTH), lambda i: (order(i), Z_SU // S5_WIDTH)),
            full((S5_WIDTH, S5_LANES)), full((S5_WIDTH, S5_LANES)), full((S5_WIDTH, S5_LANES)),
            full((S5_WIDTH, S5_LANES)), full((1, S5_LANES)), full((1, S5_LANES)), full((1, S5_WIDTH))]


def _s5_fwd(z, bre, bim, cre, cim, ab_re, ab_im, dskip, tag):
    S = z.shape[0]
    T = _pick(S, (S5_CHUNK, 128))
    nch = S // T

    def body(u_ref, bre_ref, bim_ref, cre_ref, cim_ref, are_ref, aim_ref, d_ref, y_ref, kre_ref, kim_ref, c_re, c_im):
        @pl.when(pl.program_id(0) == 0)
        def _():
            c_re[...] = jnp.zeros_like(c_re)
            c_im[...] = jnp.zeros_like(c_im)

        kre_ref[...] = c_re[...]
        kim_ref[...] = c_im[...]
        u = u_ref[...]
        row = lax.broadcasted_iota(jnp.int32, (T, S5_LANES), 0)
        x_re, x_im = _s5_states(u, bre_ref, bim_ref, are_ref[...], aim_ref[...], c_re[...], c_im[...], row, T)
        c_re[...] = jnp.sum(jnp.where(row == T - 1, x_re, 0.0), axis=0, keepdims=True)
        c_im[...] = jnp.sum(jnp.where(row == T - 1, x_im, 0.0), axis=0, keepdims=True)
        y_ref[...] = _dot(x_re, cre_ref[...], _NT) - _dot(x_im, cim_ref[...], _NT) + d_ref[...] * u

    return pl.pallas_call(
        body, name=f"{tag}_s5", grid=(nch,), in_specs=_s5_in_specs(lambda i: i, T),
        out_specs=[pl.BlockSpec((T, S5_WIDTH), lambda i: (i, 0)),
                   pl.BlockSpec((None, 1, S5_LANES), lambda i: (i, 0, 0)),
                   pl.BlockSpec((None, 1, S5_LANES), lambda i: (i, 0, 0))],
        out_shape=[jax.ShapeDtypeStruct((S, S5_WIDTH), F32), jax.ShapeDtypeStruct((nch, 1, S5_LANES), F32),
                   jax.ShapeDtypeStruct((nch, 1, S5_LANES), F32)],
        scratch_shapes=[pltpu.VMEM((1, S5_LANES), F32), pltpu.VMEM((1, S5_LANES), F32)],
        compiler_params=_params(("arbitrary",)))(z, bre, bim, cre, cim, ab_re, ab_im, dskip)


def _s5_bwd(z, bre, bim, cre, cim, ab_re, ab_im, dskip, kre, kim, d_y, tag):
    S = z.shape[0]
    T = _pick(S, (S5_CHUNK, 128))
    nch = S // T
    rev = lambda i: nch - 1 - i
    full = lambda shape: pl.BlockSpec(shape, lambda i: (0,) * len(shape))

    def body(u_ref, bre_ref, bim_ref, cre_ref, cim_ref, are_ref, aim_ref, d_ref, kre_ref, kim_ref, dy_ref,
             du_ref, dbre_ref, dbim_ref, dcre_ref, dcim_ref, dare_ref, daim_ref, dd_ref, g_re, g_im):
        @pl.when(pl.program_id(0) == 0)
        def _():
            g_re[...] = jnp.zeros_like(g_re)
            g_im[...] = jnp.zeros_like(g_im)
            for r in (dbre_ref, dbim_ref, dcre_ref, dcim_ref, dare_ref, daim_ref, dd_ref):
                r[...] = jnp.zeros_like(r)

        u = u_ref[...]
        dy = dy_ref[...].astype(F32)
        a_re, a_im = are_ref[...], aim_ref[...]
        row = lax.broadcasted_iota(jnp.int32, (T, S5_LANES), 0)
        x_re, x_im = _s5_states(u, bre_ref, bim_ref, a_re, a_im, kre_ref[...], kim_ref[...], row, T)
        dcre_ref[...] += _dot(dy, x_re, _TN)
        dcim_ref[...] -= _dot(dy, x_im, _TN)
        xp_re = jnp.where(row >= 1, pltpu.roll(x_re, 1, 0), 0.0) + jnp.where(row == 0, kre_ref[...], 0.0)
        xp_im = jnp.where(row >= 1, pltpu.roll(x_im, 1, 0), 0.0) + jnp.where(row == 0, kim_ref[...], 0.0)
        last = row == T - 1
        gd_re = _dot(dy, cre_ref[...]) + jnp.where(last, a_re * g_re[...] + a_im * g_im[...], 0.0)
        gd_im = -_dot(dy, cim_ref[...]) + jnp.where(last, a_re * g_im[...] - a_im * g_re[...], 0.0)
        t_re, t_im = _s5_scan(gd_re, gd_im, a_re, -a_im, row, T, True)
        g_re[...] = jnp.sum(jnp.where(row == 0, t_re, 0.0), axis=0, keepdims=True)
        g_im[...] = jnp.sum(jnp.where(row == 0, t_im, 0.0), axis=0, keepdims=True)
        du_ref[...] = (_dot(t_re, bre_ref[...], _NT) + _dot(t_im, bim_ref[...], _NT) + dy * d_ref[...]).astype(du_ref.dtype)
        dbre_ref[...] += _dot(u, t_re, _TN)
        dbim_ref[...] += _dot(u, t_im, _TN)
        dare_ref[...] += jnp.sum(t_re * xp_re + t_im * xp_im, axis=0, keepdims=True)
        daim_ref[...] += jnp.sum(t_im * xp_re - t_re * xp_im, axis=0, keepdims=True)
        dd_ref[...] += jnp.sum(dy * u, axis=0, keepdims=True)

    return pl.pallas_call(
        body, name=f"{tag}_s5_bwd", grid=(nch,),
        in_specs=_s5_in_specs(rev, T) + [pl.BlockSpec((None, 1, S5_LANES), lambda i: (rev(i), 0, 0)),
                                         pl.BlockSpec((None, 1, S5_LANES), lambda i: (rev(i), 0, 0)),
                                         pl.BlockSpec((T, S5_WIDTH), lambda i: (rev(i), 0))],
        out_specs=[pl.BlockSpec((T, S5_WIDTH), lambda i: (rev(i), 0))] + [full((S5_WIDTH, S5_LANES))] * 4
        + [full((1, S5_LANES))] * 2 + [full((1, S5_WIDTH))],
        out_shape=[jax.ShapeDtypeStruct((S, S5_WIDTH), BF16)] + [jax.ShapeDtypeStruct((S5_WIDTH, S5_LANES), F32)] * 4
        + [jax.ShapeDtypeStruct((1, S5_LANES), F32)] * 2 + [jax.ShapeDtypeStruct((1, S5_WIDTH), F32)],
        scratch_shapes=[pltpu.VMEM((1, S5_LANES), F32), pltpu.VMEM((1, S5_LANES), F32)],
        compiler_params=_params(("arbitrary",)))(z, bre, bim, cre, cim, ab_re, ab_im, dskip, kre, kim, d_y)


def _s5_glu_fn(y, glu_w, glu_b):
    g = 0.5 * y * (1.0 + jnp.tanh(math.sqrt(2.0 / math.pi) * (y + 0.044715 * (y * y * y))))
    lin = lax.dot_general(g.astype(BF16), glu_w.astype(BF16), (((1,), (0,)), ((), ())), preferred_element_type=F32)
    return (g * jax.nn.sigmoid(lin + glu_b),)


def _block_diag(m):
    G, H, P = S5_GROUPS, S5_GROUP_CH, S5_STATE
    eye = jnp.eye(G, dtype=m.dtype)
    return (m.reshape(G, H, 1, P) * eye[:, None, :, None]).reshape(G * H, G * P)


def _block_diag_take(m):
    G, H, P = S5_GROUPS, S5_GROUP_CH, S5_STATE
    idx = jnp.arange(G)
    return m.reshape(G, H, G, P)[idx, :, idx, :].reshape(G * H, P)


def _loss_head(y, target, tag):
    S, D = y.shape
    tm = _pick(S, (256, 128, 64))

    def body(y_ref, t_ref, dy_ref, loss_ref):
        @pl.when(pl.program_id(0) == 0)
        def _():
            loss_ref[...] = jnp.zeros_like(loss_ref)

        err = y_ref[...] - t_ref[...]
        dy_ref[...] = err * (1.0 / D)
        part = 0.5 * jnp.sum(jnp.mean(err * err, axis=-1, keepdims=True), axis=0, keepdims=True)
        loss_ref[...] += jnp.broadcast_to(part, loss_ref.shape)

    return pl.pallas_call(
        body, name=f"{tag}_loss", grid=(S // tm,),
        in_specs=[pl.BlockSpec((tm, D), lambda i: (i, 0)), pl.BlockSpec((tm, D), lambda i: (i, 0))],
        out_specs=[pl.BlockSpec((tm, D), lambda i: (i, 0)), pl.BlockSpec((8, 128), lambda i: (0, 0))],
        out_shape=[jax.ShapeDtypeStruct((S, D), F32), jax.ShapeDtypeStruct((8, 128), F32)],
        compiler_params=_params(("arbitrary",)))(y, target)


def _adamw(w, g, m, v, name):
    shape = w.shape
    cols = shape[-1]
    rows = int(np.prod(shape[:-1]))
    tr = _pick(rows, [t for t in (512, 256, 128, 64, 32, 16, 8) if t * cols <= 256 * 1024] or [8])
    c1 = 1.0 - ADAM_B1 ** ADAM_STEP
    c2 = 1.0 - ADAM_B2 ** ADAM_STEP

    def body(w_ref, g_ref, m_ref, v_ref, d_ref, mo_ref, vo_ref):
        gg = g_ref[...]
        mn = ADAM_B1 * m_ref[...] + (1.0 - ADAM_B1) * gg
        vn = ADAM_B2 * v_ref[...] + (1.0 - ADAM_B2) * (gg * gg)
        d_ref[...] = -ADAM_LR * ((mn / c1) / (jnp.sqrt(vn / c2) + ADAM_EPS) + ADAM_WD * w_ref[...])
        mo_ref[...] = mn
        vo_ref[...] = vn

    spec = pl.BlockSpec((tr, cols), lambda i: (i, 0))
    outs = pl.pallas_call(
        body, name=name, grid=(rows // tr,), in_specs=[spec] * 4, out_specs=[spec] * 3,
        out_shape=[jax.ShapeDtypeStruct((rows, cols), F32)] * 3,
        compiler_params=_params(("parallel",)))(*[a.reshape(rows, cols) for a in (w, g, m, v)])
    return [o.reshape(shape) for o in outs]


def _sum_slots(slots, name):
    n, R, C = slots.shape
    tr = _pick(R, (1024, 512, 256, 128, 64, 32, 16, 8))

    def body(s_ref, o_ref):
        acc = s_ref[0].astype(F32)
        for k in range(1, n):
            acc = acc + s_ref[k].astype(F32)
        o_ref[...] = acc

    return pl.pallas_call(
        body, name=name, grid=(R // tr,), in_specs=[pl.BlockSpec((n, tr, C), lambda i: (0, i, 0))],
        out_specs=pl.BlockSpec((tr, C), lambda i: (i, 0)),
        out_shape=jax.ShapeDtypeStruct((R, C), F32), compiler_params=_params(("parallel",)))(slots)


def _add_pairs(a, b, name):
    n, R, C = a.shape
    tr = _pick(R, (1024, 512, 256, 128, 64, 32, 16))

    def body(a_ref, b_ref, o_ref):
        o_ref[...] = (a_ref[...].astype(F32) + b_ref[...].astype(F32)).astype(o_ref.dtype)

    spec = pl.BlockSpec((None, tr, C), lambda k, i: (k, i, 0))
    return pl.pallas_call(
        body, name=name, grid=(n, R // tr), in_specs=[spec, spec], out_specs=spec,
        out_shape=jax.ShapeDtypeStruct((n, R, C), BF16), compiler_params=_params(("parallel", "parallel")))(a, b)


_ANY = pl.BlockSpec(memory_space=pl.ANY)


def _place():
    return lax.axis_index("x"), lax.axis_index("y"), lax.axis_index("c")


def _other_chips(x, y):
    return [(1 - x, y), (x, 1 - y), (1 - x, 1 - y)]


def _all_gather_flat(buf, name):
    R, C = buf.shape
    R2 = R // 2

    def body(in_ref, out_ref, send_sems, recv_sems, local_sem):
        x, y, c = _place()
        k = 2 * x + y
        sibling = (x, y, 1 - c)
        chips = _other_chips(x, y)
        mine = pl.ds(c * R2, R2)
        theirs = pl.ds((1 - c) * R2, R2)

        def copy(sem, src, dst, to):
            return pltpu.make_async_remote_copy(src_ref=src, dst_ref=dst, send_sem=send_sems.at[sem],
                                                recv_sem=recv_sems.at[sem], device_id=to, device_id_type=MESH_ID)

        own = pltpu.make_async_copy(in_ref, out_ref.at[k], local_sem)
        own.start()
        first = [copy(j, in_ref.at[mine], out_ref.at[k, mine], (cx, cy, c)) for j, (cx, cy) in enumerate(chips)]
        for cp in first:
            cp.start()
        passed = []
        for j, (cx, cy) in enumerate(chips):
            landed = out_ref.at[2 * cx + cy, mine]
            copy(j, landed, landed, sibling).wait_recv()
            passed.append(copy(3 + j, landed, landed, sibling))
            passed[-1].start()
        for j, (cx, cy) in enumerate(chips):
            from_sibling = out_ref.at[2 * cx + cy, theirs]
            copy(3 + j, from_sibling, from_sibling, sibling).wait_recv()
        for cp in first + passed:
            cp.wait_send()
        own.wait()

    return pl.pallas_call(
        body, name=name, in_specs=[_ANY], out_specs=_ANY, out_shape=jax.ShapeDtypeStruct((N_SHARD, R, C), buf.dtype),
        scratch_shapes=[pltpu.SemaphoreType.DMA((6,)), pltpu.SemaphoreType.DMA((6,)), pltpu.SemaphoreType.DMA(())],
        )(buf)


def _swap_with_sibling(buf, name):
    def body(in_ref, out_ref, send_sem, recv_sem):
        x, y, c = _place()
        cp = pltpu.make_async_remote_copy(src_ref=in_ref, dst_ref=out_ref, send_sem=send_sem, recv_sem=recv_sem,
                                          device_id=(x, y, 1 - c), device_id_type=MESH_ID)
        cp.start()
        cp.wait()

    return pl.pallas_call(
        body, name=name, in_specs=[_ANY], out_specs=_ANY, out_shape=jax.ShapeDtypeStruct(buf.shape, buf.dtype),
        scratch_shapes=[pltpu.SemaphoreType.DMA(()), pltpu.SemaphoreType.DMA(())],
        )(buf)


def _scatter_to_chips(parts, name):
    def body(in_ref, out_ref, send_sems, recv_sems, local_sem):
        x, y, c = _place()
        k = 2 * x + y
        chips = _other_chips(x, y)
        own = pltpu.make_async_copy(in_ref.at[k], out_ref.at[k], local_sem)
        own.start()
        sends = []
        for j, (cx, cy) in enumerate(chips):
            sends.append(pltpu.make_async_remote_copy(
                src_ref=in_ref.at[2 * cx + cy], dst_ref=out_ref.at[k], send_sem=send_sems.at[j],
                recv_sem=recv_sems.at[j], device_id=(cx, cy, c), device_id_type=MESH_ID))
            sends[-1].start()
        for j, (cx, cy) in enumerate(chips):
            slot = out_ref.at[2 * cx + cy]
            pltpu.make_async_remote_copy(src_ref=slot, dst_ref=slot, send_sem=send_sems.at[j], recv_sem=recv_sems.at[j],
                                         device_id=(cx, cy, c), device_id_type=MESH_ID).wait_recv()
        for cp in sends:
            cp.wait_send()
        own.wait()

    return pl.pallas_call(
        body, name=name, in_specs=[_ANY], out_specs=_ANY, out_shape=jax.ShapeDtypeStruct(parts.shape, parts.dtype),
        scratch_shapes=[pltpu.SemaphoreType.DMA((3,)), pltpu.SemaphoreType.DMA((3,)), pltpu.SemaphoreType.DMA(())],
        )(parts)


def _join_halves(half, name):
    R2, C = half.shape

    def body(in_ref, out_ref, send_sem, recv_sem, local_sem):
        x, y, c = _place()
        mine = pl.ds(c * R2, R2)
        theirs = pl.ds((1 - c) * R2, R2)
        own = pltpu.make_async_copy(in_ref, out_ref.at[mine], local_sem)
        own.start()
        cp = pltpu.make_async_remote_copy(src_ref=in_ref, dst_ref=out_ref.at[mine], send_sem=send_sem,
                                          recv_sem=recv_sem, device_id=(x, y, 1 - c), device_id_type=MESH_ID)
        cp.start()
        arrives = out_ref.at[theirs]
        pltpu.make_async_remote_copy(src_ref=arrives, dst_ref=arrives, send_sem=send_sem, recv_sem=recv_sem,
                                     device_id=(x, y, 1 - c), device_id_type=MESH_ID).wait_recv()
        cp.wait_send()
        own.wait()

    return pl.pallas_call(
        body, name=name, in_specs=[_ANY], out_specs=_ANY, out_shape=jax.ShapeDtypeStruct((2 * R2, C), half.dtype),
        scratch_shapes=[pltpu.SemaphoreType.DMA(()), pltpu.SemaphoreType.DMA(()), pltpu.SemaphoreType.DMA(())],
        )(half)


def _gather_all_devices(vec, name):
    def body(in_ref, out_ref, send_sems, recv_sems, local_sem):
        x, y, c = _place()
        me = 4 * x + 2 * y + c
        own = pltpu.make_async_copy(in_ref, out_ref.at[me], local_sem)
        own.start()
        sends = []
        for r in range(1, 8):
            fx, fy, fc = (r >> 2) & 1, (r >> 1) & 1, r & 1
            to = (x ^ fx, y ^ fy, c ^ fc)
            sends.append(pltpu.make_async_remote_copy(
                src_ref=in_ref, dst_ref=out_ref.at[me], send_sem=send_sems.at[r - 1], recv_sem=recv_sems.at[r - 1],
                device_id=to, device_id_type=MESH_ID))
            sends[-1].start()
        for r in range(1, 8):
            fx, fy, fc = (r >> 2) & 1, (r >> 1) & 1, r & 1
            slot = out_ref.at[4 * (x ^ fx) + 2 * (y ^ fy) + (c ^ fc)]
            pltpu.make_async_remote_copy(src_ref=slot, dst_ref=slot, send_sem=send_sems.at[r - 1],
                                         recv_sem=recv_sems.at[r - 1], device_id=(x, y, c),
                                         device_id_type=MESH_ID).wait_recv()
        for cp in sends:
            cp.wait_send()
        own.wait()

    return pl.pallas_call(
        body, name=name, in_specs=[_ANY], out_specs=_ANY, out_shape=jax.ShapeDtypeStruct((8,) + vec.shape, vec.dtype),
        scratch_shapes=[pltpu.SemaphoreType.DMA((7,)), pltpu.SemaphoreType.DMA((7,)), pltpu.SemaphoreType.DMA(())],
        )(vec)


def _permute_w_in(w):
    cut = Z_SU
    return jnp.concatenate([w[:, :cut], w[:, cut + 2 * DN_HEADS:], w[:, cut:cut + 2 * DN_HEADS],
                            jnp.zeros((w.shape[0], Z_WIDTH - IN_WIDTH), w.dtype)], axis=1)


def _unpermute_w_in(wp):
    return jnp.concatenate([wp[:, :Z_SU], wp[:, Z_DBA:Z_DBA + 2 * DN_HEADS], wp[:, Z_SU:Z_DBA]], axis=1)


def _s5_inputs(sp):
    ab_re, ab_im, bb_re, bb_im = _whole_fwd(
        _s5_param_fn, sp["s5_pins"], [(S5_GROUPS, S5_STATE)] * 2 + [(S5_WIDTH, S5_STATE)] * 2, name="s5_params")
    return (_block_diag(bb_re), _block_diag(bb_im), _block_diag(sp["c_re"]), _block_diag(sp["c_im"]),
            ab_re.reshape(1, S5_LANES), ab_im.reshape(1, S5_LANES), sp["dskip"])


def _mixer_fwd(x, sp, w_in, w_out, glu_w, rope):
    D = x.shape[1]
    (h,) = _rows_fwd(_prenorm_fn, [(x, D, 0)], [sp["mix_norm_pre"]], [(D, D, BF16)], name="mix_prenorm")
    z = _mm(h, w_in, out_dtype=F32, name="mix_in")
    y_attn = _attn_fwd(z, rope[0], rope[1], sp["sinks"], "mix")
    qkv = _conv_fwd(z, sp["conv_w"], "mix")
    y_dn, states = _dn_fwd(qkv, z, sp["dn_prm"], sp["dn_norm_w"], "mix")
    s5_in = _s5_inputs(sp)
    y_lin, kre, kim = _s5_fwd(z, *s5_in, "mix")
    (y_s5,) = _rows_fwd(_s5_glu_fn, [(y_lin, S5_WIDTH, 0)], [glu_w, sp["glu_b"]], [(S5_WIDTH, S5_WIDTH, BF16)],
                        name="mix_s5_glu")
    cat = jnp.concatenate([y_attn, y_dn, y_s5], axis=1)
    mixed = _mm(cat, w_out, out_dtype=F32, name="mix_out")
    (x_new,) = _rows_fwd(_residual_fn(1.0), [(x, D, 0), (mixed, D, 0)], [sp["mix_norm_post"]], [(D, D, F32)],
                         name="mix_residual")
    return x_new, (h, z, qkv, states, s5_in, y_lin, kre, kim, cat, mixed)


def _mixer_bwd(dx_new, x, sp, w_in, w_out, glu_w, rope, saved):
    h, z, qkv, states, s5_in, y_lin, kre, kim, cat, mixed = saved
    D = x.shape[1]
    G, H, P = S5_GROUPS, S5_GROUP_CH, S5_STATE
    d_mixed, d_g_post = _rows_bwd(_postnorm_fn(1.0), [(mixed, D, 0)], [sp["mix_norm_post"]], [(dx_new, D, 0)],
                                  [(0, D, BF16)], name="mix_postnorm_bwd")
    d_cat = _mm(d_mixed, w_out, tb=True, out_dtype=BF16, name="mix_out_dx")
    d_w_out = _mm(cat, d_mixed, ta=True, out_dtype=BF16, name="mix_out_dw")
    d_attn, d_dn, d_s5 = d_cat[:, :ATTN_WIDTH], d_cat[:, ATTN_WIDTH:ATTN_WIDTH + DN_WIDTH], d_cat[:, ATTN_WIDTH + DN_WIDTH:]
    d_ylin, d_glu_w, d_glu_b = _rows_bwd(_s5_glu_fn, [(y_lin, S5_WIDTH, 0)], [glu_w, sp["glu_b"]],
                                         [(d_s5, S5_WIDTH, 0)], [(0, S5_WIDTH, F32)], name="mix_s5_glu_bwd")
    d_us5, d_bre, d_bim, d_cre, d_cim, d_are, d_aim, d_dskip = _s5_bwd(z, *s5_in, kre, kim, d_ylin, "mix")
    cts = [d_are.reshape(G, P), d_aim.reshape(G, P), _block_diag_take(d_bre), _block_diag_take(d_bim)]
    d_a_re, d_a_im, d_ldt, d_bt_re, d_bt_im = _whole_bwd(_s5_param_fn, sp["s5_pins"], cts, 5, name="s5_params_bwd",
                                                         lane_sum=(2,))
    from_t = lambda m: m.reshape(G, H, P).transpose(0, 2, 1)
    d_qkv, d_zg, d_ba, d_prm, d_nw = _dn_bwd(qkv, z, sp["dn_prm"], sp["dn_norm_w"], states, d_dn, "mix")
    d_uconv, d_conv_w = _conv_bwd(z, sp["conv_w"], d_qkv, "mix")
    d_q, d_kv, d_sinks = _attn_bwd(z, rope[0], rope[1], sp["sinks"], d_attn, "mix")
    d_z = jnp.concatenate([d_q, d_kv, d_uconv, d_zg, d_us5, d_ba], axis=1)
    d_w_in = _mm(h, d_z, ta=True, out_dtype=BF16, name="mix_in_dw")
    d_h = _mm(d_z, w_in, tb=True, out_dtype=BF16, name="mix_in_dx")
    dx, d_g_pre = _rows_bwd(_prenorm_fn, [(x, D, 0)], [sp["mix_norm_pre"]], [(d_h, D, 0)], [(0, D, F32)],
                            name="mix_prenorm_bwd", add_to_first=dx_new)
    small = {
        "mix_norm_pre": d_g_pre[0], "mix_norm_post": d_g_post[0], "attn_sinks": d_sinks[:, 0], "dn_conv_w": d_conv_w,
        "dn_a_log": d_prm[:DN_HEADS, 0], "dn_dt_bias": d_prm[DN_HEADS:2 * DN_HEADS, 0], "dn_norm_w": d_nw[0],
        "s5_a_re": d_a_re, "s5_a_im": d_a_im, "s5_log_dt": d_ldt[:, 0], "s5_b_re": from_t(d_bt_re),
        "s5_b_im": from_t(d_bt_im), "s5_c_re": _block_diag_take(d_cre).reshape(G, H, P),
        "s5_c_im": _block_diag_take(d_cim).reshape(G, H, P), "s5_d": d_dskip[0], "s5_glu_b": d_glu_b[0],
    }
    return dx, {"win": d_w_in, "wout": d_w_out, "glu": d_glu_w.astype(BF16)}, small


BIG_PIECES = ("gu1", "wd1", "win", "wout", "glu", "gu2", "wd2")
FLAT_ROW_MULTIPLE = 2048


def _flat_layout(L, D, Fs):
    sizes = {"gu1": D * 2 * Fs, "wd1": Fs * D, "win": D * (IN_WIDTH // N_SHARD), "wout": (MIX_WIDTH // N_SHARD) * D,
             "glu": (S5_WIDTH // N_SHARD) * S5_WIDTH, "gu2": D * 2 * Fs, "wd2": Fs * D}
    offs, o = {}, 0
    for l in range(L):
        for k in BIG_PIECES:
            offs[(l, k)] = (o, sizes[k])
            o += sizes[k]
    quantum = FLAT_LANES * FLAT_ROW_MULTIPLE
    total = -(-o // quantum) * quantum
    return offs, o, total // FLAT_LANES


def _to_flat(pieces, used, rows, lead=()):
    flat = jnp.concatenate(pieces + [jnp.zeros(lead + (rows * FLAT_LANES - used,), pieces[0].dtype)], axis=-1)
    return flat.reshape(lead + (rows, FLAT_LANES))


def _pack_small(arrs, extra=()):
    flat = jnp.concatenate([a.reshape(-1) for a in arrs] + list(extra))
    n = flat.shape[0]
    padded = -(-n // 1024) * 1024
    return jnp.concatenate([flat, jnp.zeros((padded - n,), flat.dtype)]).reshape(padded // 128, 128)


def _unpack_small(flat2d, shapes):
    flat = flat2d.reshape(-1)
    out, o = [], 0
    for s in shapes:
        n = int(np.prod(s))
        out.append(flat[o:o + n].reshape(s))
        o += n
    return out, flat[o:]


def _step(a):
    x, target = a["x"][0], a["loss_target"][0]
    S, D = x.shape
    L, _, Fs = a["ff1_w_gate"].shape
    px, py, pc = _place()
    chip = 2 * px + py
    offs, used, rows = _flat_layout(L, D, Fs)
    rows2 = rows // 2

    local = []
    for l in range(L):
        per = {"gu1": jnp.concatenate([a["ff1_w_gate"][l], a["ff1_w_up"][l]], axis=1), "wd1": a["ff1_w_down"][l],
               "win": a["w_in"][l], "wout": a["w_out"][l], "glu": a["s5_glu_w"][l],
               "gu2": jnp.concatenate([a["ff2_w_gate"][l], a["ff2_w_up"][l]], axis=1), "wd2": a["ff2_w_down"][l]}
        local += [per[k].reshape(-1).astype(BF16) for k in BIG_PIECES]
    gathered = _all_gather_flat(_to_flat(local, used, rows), "gather_weights").reshape(N_SHARD, rows * FLAT_LANES)

    def full(l, k):
        o, n = offs[(l, k)]
        p = gathered[:, o:o + n]
        if k in ("gu1", "gu2"):
            return p.reshape(N_SHARD, D, 2 * Fs)
        if k in ("wd1", "wd2"):
            return p.reshape(N_SHARD * Fs, D)
        if k == "win":
            return _permute_w_in(p.reshape(N_SHARD, D, IN_WIDTH // N_SHARD).transpose(1, 0, 2).reshape(D, IN_WIDTH))
        if k == "wout":
            return p.reshape(MIX_WIDTH, D)
        return p.reshape(S5_WIDTH, S5_WIDTH)

    conv_local = a["dn_conv_w"].reshape(-1)
    conv_rows = -(-conv_local.shape[0] // (16 * FLAT_LANES)) * 16
    conv_all = _all_gather_flat(_to_flat([conv_local], conv_local.shape[0], conv_rows), "gather_conv")
    conv_all = conv_all.reshape(N_SHARD, -1)[:, :conv_local.shape[0]].reshape(N_SHARD, L, DN_CONV, -1)
    conv_full = conv_all.transpose(1, 2, 0, 3).reshape(L, DN_CONV, 3 * DN_WIDTH)

    expand = jnp.repeat(jnp.eye(S5_GROUPS, dtype=F32), S5_GROUP_CH, axis=0)
    lanes = lambda v, n=128: jnp.broadcast_to(v[:, None], (v.shape[0], n))
    to_t = lambda m: m.transpose(0, 2, 1).reshape(S5_WIDTH, S5_STATE)

    def small_params(l):
        sp = {k: a[k][l][None] for k in ("ff1_norm_pre", "ff1_norm_post", "mix_norm_pre", "mix_norm_post",
                                         "ff2_norm_pre", "ff2_norm_post")}
        sp["sinks"] = lanes(a["attn_sinks"][l])
        sp["conv_w"] = conv_full[l]
        sp["dn_prm"] = jnp.concatenate([lanes(a["dn_a_log"][l]), lanes(a["dn_dt_bias"][l])], axis=0)
        sp["dn_norm_w"] = a["dn_norm_w"][l][None]
        sp["s5_pins"] = [a["s5_a_re"][l], a["s5_a_im"][l], lanes(a["s5_log_dt"][l], S5_STATE),
                         to_t(a["s5_b_re"][l]), to_t(a["s5_b_im"][l]), expand]
        sp["c_re"] = a["s5_c_re"][l].reshape(S5_WIDTH, S5_STATE)
        sp["c_im"] = a["s5_c_im"][l].reshape(S5_WIDTH, S5_STATE)
        sp["dskip"] = a["s5_d"][l][None]
        sp["glu_b"] = a["s5_glu_b"][l][None]
        return sp

    rope = _rope_tables(S)
    sps = [small_params(l) for l in range(L)]
    big = [{k: full(l, k) for k in BIG_PIECES} for l in range(L)]

    saved = []
    for l in range(L):
        sp, w = sps[l], big[l]
        x1, s1 = _ffn_fwd(x, sp["ff1_norm_pre"], sp["ff1_norm_post"], w["gu1"], w["wd1"], "ff1")
        x2, s2 = _mixer_fwd(x1, sp, w["win"], w["wout"], w["glu"], rope)
        x3, s3 = _ffn_fwd(x2, sp["ff2_norm_pre"], sp["ff2_norm_post"], w["gu2"], w["wd2"], "ff2")
        saved.append((x, s1, x1, s2, x2, s3))
        x = x3
    dx, loss_part = _loss_head(x, target, "head")

    big_grads, small_grads = [None] * L, [None] * L
    for l in reversed(range(L)):
        sp, w = sps[l], big[l]
        x0, s1, x1, s2, x2, s3 = saved[l]
        dx, g_pre2, g_post2, d_gu2, d_wd2 = _ffn_bwd(dx, x2, sp["ff2_norm_pre"], sp["ff2_norm_post"], w["gu2"], w["wd2"],
                                                     s3, "ff2")
        dx, bg, sg = _mixer_bwd(dx, x1, sp, w["win"], w["wout"], w["glu"], rope, s2)
        dx, g_pre1, g_post1, d_gu1, d_wd1 = _ffn_bwd(dx, x0, sp["ff1_norm_pre"], sp["ff1_norm_post"], w["gu1"], w["wd1"],
                                                     s1, "ff1")
        d_win = _unpermute_w_in(bg["win"]).reshape(D, N_SHARD, IN_WIDTH // N_SHARD).transpose(1, 0, 2)
        bgl = {"gu1": d_gu1, "wd1": d_wd1, "win": d_win, "wout": bg["wout"], "glu": bg["glu"], "gu2": d_gu2, "wd2": d_wd2}
        big_grads[l] = [bgl[k].reshape(N_SHARD, -1) for k in BIG_PIECES]
        sg.update({"ff1_norm_pre": g_pre1[0], "ff1_norm_post": g_post1[0], "ff2_norm_pre": g_pre2[0],
                   "ff2_norm_post": g_post2[0]})
        small_grads[l] = sg
    grad_x = dx[None]

    part = _to_flat([p for l in range(L) for p in big_grads[l]], used, rows, lead=(N_SHARD,))
    keep = lax.dynamic_slice_in_dim(part, pc * rows2, rows2, axis=1)
    give = lax.dynamic_slice_in_dim(part, (1 - pc) * rows2, rows2, axis=1)
    chip_part = _add_pairs(keep, _swap_with_sibling(give, "reduce_siblings"), "reduce_siblings_add")
    reduced_half = _sum_slots(_scatter_to_chips(chip_part, "reduce_chips"), "reduce_chips_sum")
    reduced = _join_halves(reduced_half, "reduce_join").reshape(rows * FLAT_LANES)

    def mine(l, k):
        o, n = offs[(l, k)]
        return reduced[o:o + n]

    grads = {}
    for f, (gu, wd) in (("ff1", ("gu1", "wd1")), ("ff2", ("gu2", "wd2"))):
        gus = jnp.stack([mine(l, gu).reshape(D, 2 * Fs) for l in range(L)])
        grads[f + "_w_gate"], grads[f + "_w_up"] = gus[:, :, :Fs], gus[:, :, Fs:]
        grads[f + "_w_down"] = jnp.stack([mine(l, wd).reshape(Fs, D) for l in range(L)])
    grads["w_in"] = jnp.stack([mine(l, "win").reshape(D, IN_WIDTH // N_SHARD) for l in range(L)])
    grads["w_out"] = jnp.stack([mine(l, "wout").reshape(MIX_WIDTH // N_SHARD, D) for l in range(L)])
    grads["s5_glu_w"] = jnp.stack([mine(l, "glu").reshape(S5_WIDTH // N_SHARD, S5_WIDTH) for l in range(L)])

    small_local = [jnp.stack([small_grads[l][n] for l in range(L)]) for n in SMALL]
    vec = _pack_small(small_local, extra=(loss_part[0, :1],))
    total = _sum_slots(_gather_all_devices(vec, "gather_small"), "sum_small")
    small_total, rest = _unpack_small(total, [g.shape for g in small_local])
    loss = rest[0]
    for n, g in zip(SMALL, small_total):
        grads[n] = g
    cw = 3 * DN_WIDTH // N_SHARD
    grads["dn_conv_w"] = lax.dynamic_slice_in_dim(grads["dn_conv_w"], chip * cw, cw, axis=2)

    delta, new_m, new_v = {}, {}, {}
    for n in BIG:
        delta[n], new_m[n], new_v[n] = _adamw(a[n], grads[n], a["m_" + n], a["v_" + n], "adamw_" + n)
    shapes = [a[n].shape for n in SMALL]
    packed = [_pack_small([src[n] for n in SMALL]) for src in
              (a, grads, {n: a["m_" + n] for n in SMALL}, {n: a["v_" + n] for n in SMALL})]
    for dst, res in zip((delta, new_m, new_v), _adamw(*packed, "adamw_small")):
        for n, val in zip(SMALL, _unpack_small(res, shapes)[0]):
            dst[n] = val
    return (loss, grad_x, *[grads[n] for n in WEIGHTS], *[delta[n] for n in WEIGHTS], *[new_m[n] for n in WEIGHTS],
            *[new_v[n] for n in WEIGHTS])


def kernel(x, ff1_norm_pre, ff1_w_gate, ff1_w_up, ff1_w_down, ff1_norm_post, mix_norm_pre, w_in, attn_sinks, dn_conv_w, dn_a_log, dn_dt_bias, dn_norm_w, s5_a_re, s5_a_im, s5_log_dt, s5_b_re, s5_b_im, s5_c_re, s5_c_im, s5_d, s5_glu_w, s5_glu_b, w_out, mix_norm_post, ff2_norm_pre, ff2_w_gate, ff2_w_up, ff2_w_down, ff2_norm_post, loss_target, m_ff1_norm_pre, m_ff1_w_gate, m_ff1_w_up, m_ff1_w_down, m_ff1_norm_post, m_mix_norm_pre, m_w_in, m_attn_sinks, m_dn_conv_w, m_dn_a_log, m_dn_dt_bias, m_dn_norm_w, m_s5_a_re, m_s5_a_im, m_s5_log_dt, m_s5_b_re, m_s5_b_im, m_s5_c_re, m_s5_c_im, m_s5_d, m_s5_glu_w, m_s5_glu_b, m_w_out, m_mix_norm_post, m_ff2_norm_pre, m_ff2_w_gate, m_ff2_w_up, m_ff2_w_down, m_ff2_norm_post, v_ff1_norm_pre, v_ff1_w_gate, v_ff1_w_up, v_ff1_w_down, v_ff1_norm_post, v_mix_norm_pre, v_w_in, v_attn_sinks, v_dn_conv_w, v_dn_a_log, v_dn_dt_bias, v_dn_norm_w, v_s5_a_re, v_s5_a_im, v_s5_log_dt, v_s5_b_re, v_s5_b_im, v_s5_c_re, v_s5_c_im, v_s5_d, v_s5_glu_w, v_s5_glu_b, v_w_out, v_mix_norm_post, v_ff2_norm_pre, v_ff2_w_gate, v_ff2_w_up, v_ff2_w_down, v_ff2_norm_post):
    return _step(dict(locals()))
```

```python
import functools
import math

import numpy as np
import jax
import jax.numpy as jnp
from jax import lax
from jax.experimental import pallas as pl
from jax.experimental.pallas import tpu as pltpu

F32 = jnp.float32
BF16 = jnp.bfloat16
HI = lax.Precision.HIGHEST
MESH_ID = pl.DeviceIdType.MESH

NORM_EPS = 1e-6
FFN_RES_WEIGHT = 0.5
ATTN_HEADS, ATTN_KV_HEADS, HEAD_DIM, WINDOW = 8, 2, 128, 128
ROPE_THETA = 10000.0
DN_HEADS, DN_HEAD_DIM, DN_CONV, DN_CHUNK = 4, 128, 4, 64
S5_GROUPS, S5_GROUP_CH, S5_STATE = 32, 16, 64
ATTN_WIDTH = ATTN_HEADS * HEAD_DIM
ATTN_KV_WIDTH = ATTN_KV_HEADS * HEAD_DIM
DN_WIDTH = DN_HEADS * DN_HEAD_DIM
S5_WIDTH = S5_GROUPS * S5_GROUP_CH
S5_LANES = S5_GROUPS * S5_STATE
MIX_WIDTH = ATTN_WIDTH + DN_WIDTH + S5_WIDTH
IN_SPLITS = (ATTN_WIDTH, ATTN_KV_WIDTH, ATTN_KV_WIDTH, 3 * DN_WIDTH, DN_WIDTH, DN_HEADS, DN_HEADS, S5_WIDTH)
IN_WIDTH = sum(IN_SPLITS)
Z_AQ, Z_AK, Z_AV = 0, ATTN_WIDTH, ATTN_WIDTH + ATTN_KV_WIDTH
Z_DQKV = ATTN_WIDTH + 2 * ATTN_KV_WIDTH
Z_DZ = Z_DQKV + 3 * DN_WIDTH
Z_SU = Z_DZ + DN_WIDTH
Z_DBA = Z_SU + S5_WIDTH
Z_WIDTH = Z_DBA + 128

ADAM_LR, ADAM_B1, ADAM_B2, ADAM_EPS, ADAM_WD, ADAM_STEP = 0.001, 0.9, 0.999, 1e-08, 0.01, 10

N_SHARD = 4
FLAT_LANES = 512
VMEM_LIMIT = 56 * 1024 * 1024

WEIGHTS = ['ff1_norm_pre', 'ff1_w_gate', 'ff1_w_up', 'ff1_w_down', 'ff1_norm_post', 'mix_norm_pre', 'w_in',
           'attn_sinks', 'dn_conv_w', 'dn_a_log', 'dn_dt_bias', 'dn_norm_w', 's5_a_re', 's5_a_im', 's5_log_dt',
           's5_b_re', 's5_b_im', 's5_c_re', 's5_c_im', 's5_d', 's5_glu_w', 's5_glu_b', 'w_out', 'mix_norm_post',
           'ff2_norm_pre', 'ff2_w_gate', 'ff2_w_up', 'ff2_w_down', 'ff2_norm_post']
BIG = ['ff1_w_gate', 'ff1_w_up', 'ff1_w_down', 'w_in', 's5_glu_w', 'w_out', 'ff2_w_gate', 'ff2_w_up', 'ff2_w_down']
SMALL = [n for n in WEIGHTS if n not in BIG]


def _pick(dim, cands):
    for c in cands:
        if dim % c == 0:
            return c
    return dim


def _params(sem=None):
    return pltpu.CompilerParams(dimension_semantics=sem, vmem_limit_bytes=VMEM_LIMIT)


def _mm(a, b, *, ta=False, tb=False, out_dtype=F32, name, b_groups=None, bk_groups=None, out_groups=None):
    K, M = a.shape if ta else a.shape[::-1]
    Ng = Kg = None
    if b_groups:
        assert not tb
        _, Kb, Ng = b.shape
        N = b_groups * Ng
    elif bk_groups:
        assert tb
        _, N, Kg = b.shape
        Kb = bk_groups * Kg
    else:
        N, Kb = (b.shape if tb else b.shape[::-1])
    assert K == Kb, (a.shape, b.shape, ta, tb)
    tiles = (1408, 1024, 512, 384, 256, 128)
    tm = _pick(M, (1024, 512, 256, 128))
    tk = _pick(Kg if bk_groups else K, (2048,) + tiles)
    tn = _pick(N // out_groups if out_groups else (Ng if b_groups else N), tiles)
    nk = K // tk
    dims = (((0,) if ta else (1,), (1,) if tb else (0,)), ((), ()))

    def body(a_ref, b_ref, o_ref, *scratch):
        part = lax.dot_general(a_ref[...].astype(BF16), b_ref[...].astype(BF16), dims, preferred_element_type=F32)
        if nk == 1:
            o_ref[...] = part.astype(o_ref.dtype)
            return
        acc_ref, = scratch
        k = pl.program_id(2)

        @pl.when(k == 0)
        def _():
            acc_ref[...] = part

        @pl.when(k > 0)
        def _():
            acc_ref[...] += part

        @pl.when(k == nk - 1)
        def _():
            o_ref[...] = acc_ref[...].astype(o_ref.dtype)

    a_spec = pl.BlockSpec((tk, tm), lambda i, j, k: (k, i)) if ta else pl.BlockSpec((tm, tk), lambda i, j, k: (i, k))
    if b_groups:
        per = Ng // tn
        b_spec = pl.BlockSpec((None, tk, tn), lambda i, j, k: (j // per, k, j % per))
    elif bk_groups:
        per = Kg // tk
        b_spec = pl.BlockSpec((None, tn, tk), lambda i, j, k: (k // per, j, k % per))
    else:
        b_spec = pl.BlockSpec((tn, tk), lambda i, j, k: (j, k)) if tb else pl.BlockSpec((tk, tn), lambda i, j, k: (k, j))
    if out_groups:
        pero = (N // out_groups) // tn
        o_spec = pl.BlockSpec((None, tm, tn), lambda i, j, k: (j // pero, i, j % pero))
        o_shape = jax.ShapeDtypeStruct((out_groups, M, N // out_groups), out_dtype)
    else:
        o_spec = pl.BlockSpec((tm, tn), lambda i, j, k: (i, j))
        o_shape = jax.ShapeDtypeStruct((M, N), out_dtype)
    return pl.pallas_call(
        body, name=name, grid=(M // tm, N // tn, nk), in_specs=[a_spec, b_spec], out_specs=o_spec, out_shape=o_shape,
        scratch_shapes=[pltpu.VMEM((tm, tn), F32)] if nk > 1 else [],
        compiler_params=_params(("parallel", "parallel", "arbitrary")))(a, b)


def _row_spec(tm, width, off):
    return pl.BlockSpec((tm, width), lambda i, j: (i, off + j))


def _const_spec(shape):
    return pl.BlockSpec(shape, lambda i, j: (0,) * len(shape))


def _rows_fwd(fn, rows, consts, outs, *, name, tm=256, ncol=1):
    S = rows[0][0].shape[0]
    tm = _pick(S, (tm, 128, 64))
    nr, nc = len(rows), len(consts)

    def body(*refs):
        vals = [r[...].astype(F32) for r in refs[:nr + nc]]
        res = fn(*vals)
        for o_ref, o in zip(refs[nr + nc:], res):
            o_ref[...] = o.astype(o_ref.dtype)

    return pl.pallas_call(
        body, name=name, grid=(S // tm, ncol),
        in_specs=[_row_spec(tm, w, off) for _, w, off in rows] + [_const_spec(c.shape) for c in consts],
        out_specs=[_row_spec(tm, bw, 0) for _, bw, _ in outs],
        out_shape=[jax.ShapeDtypeStruct((S, tw), dt) for tw, _, dt in outs],
        compiler_params=_params(("parallel", "parallel")))(*[r[0] for r in rows], *consts)


def _rows_bwd(fn, rows, consts, cts, row_grads, *, name, tm=256, ncol=1, add_to_first=None, lane_sum_consts=()):
    S = rows[0][0].shape[0]
    tm = _pick(S, (tm, 128, 64))
    nr, nc, nt = len(rows), len(consts), len(cts)
    n_in = nr + nc + nt + (1 if add_to_first is not None else 0)

    def body(*refs):
        vals = [r[...].astype(F32) for r in refs[:nr + nc]]
        ct = tuple(r[...].astype(F32) for r in refs[nr + nc:nr + nc + nt])
        _, vjp = jax.vjp(fn, *vals)
        grads = vjp(ct)
        outs = refs[n_in:]
        for n, (idx, _, _) in enumerate(row_grads):
            g = grads[idx]
            if n == 0 and add_to_first is not None:
                g = g + refs[n_in - 1][...].astype(F32)
            outs[n][...] = g.astype(outs[n].dtype)
        first = jnp.logical_and(pl.program_id(0) == 0, pl.program_id(1) == 0)
        for c in range(nc):
            o_ref = outs[len(row_grads) + c]
            g = grads[nr + c]
            if c in lane_sum_consts:
                g = jnp.broadcast_to(jnp.sum(g, axis=-1, keepdims=True), g.shape)

            @pl.when(first)
            def _():
                o_ref[...] = jnp.zeros_like(o_ref)

            o_ref[...] += g

    in_specs = ([_row_spec(tm, w, off) for _, w, off in rows] + [_const_spec(c.shape) for c in consts]
                + [_row_spec(tm, w, off) for _, w, off in cts])
    args = [r[0] for r in rows] + list(consts) + [c[0] for c in cts]
    if add_to_first is not None:
        in_specs.append(_row_spec(tm, rows[row_grads[0][0]][1], 0))
        args.append(add_to_first)
    out_specs = [_row_spec(tm, rows[idx][1], 0) for idx, _, _ in row_grads] + [_const_spec(c.shape) for c in consts]
    out_shape = ([jax.ShapeDtypeStruct((S, tw), dt) for _, tw, dt in row_grads]
                 + [jax.ShapeDtypeStruct(c.shape, F32) for c in consts])
    return pl.pallas_call(
        body, name=name, grid=(S // tm, ncol), in_specs=in_specs, out_specs=out_specs, out_shape=out_shape,
        compiler_params=_params(("arbitrary", "arbitrary")))(*args)


def _rms(x, gain):
    return x * lax.rsqrt(jnp.mean(x * x, axis=-1, keepdims=True) + NORM_EPS) * gain


def _prenorm_fn(x, gain):
    return (_rms(x, gain),)


def _postnorm_fn(weight):
    def fn(y, gain):
        return (weight * _rms(y, gain),)
    return fn


def _residual_fn(weight):
    def fn(x, y, gain):
        return (x + weight * _rms(y, gain),)
    return fn


def _swiglu_fn(blk):
    tf = blk.shape[1] // 2
    gate, up = blk[:, :tf], blk[:, tf:]
    return (gate * jax.nn.sigmoid(gate) * up,)


def _ffn_fwd(x, g_pre, g_post, wgu, wd, tag):
    D = x.shape[1]
    Fs = wgu.shape[2] // 2
    (h,) = _rows_fwd(_prenorm_fn, [(x, D, 0)], [g_pre], [(D, D, BF16)], name=f"{tag}_prenorm")
    gu = _mm(h, wgu, b_groups=N_SHARD, out_dtype=BF16, name=f"{tag}_gate_up")
    (act,) = _rows_fwd(_swiglu_fn, [(gu, 2 * Fs, 0)], [], [(N_SHARD * Fs, Fs, BF16)], name=f"{tag}_swiglu",
                       ncol=N_SHARD)
    y = _mm(act, wd, out_dtype=F32, name=f"{tag}_down")
    (x_new,) = _rows_fwd(_residual_fn(FFN_RES_WEIGHT), [(x, D, 0), (y, D, 0)], [g_post], [(D, D, F32)],
                         name=f"{tag}_residual")
    return x_new, (h, gu, act, y)


def _ffn_bwd(dx_new, x, g_pre, g_post, wgu, wd, saved, tag):
    h, gu, act, y = saved
    D = x.shape[1]
    Fs = wgu.shape[2] // 2
    d_y, d_g_post = _rows_bwd(_postnorm_fn(FFN_RES_WEIGHT), [(y, D, 0)], [g_post], [(dx_new, D, 0)],
                              [(0, D, BF16)], name=f"{tag}_postnorm_bwd")
    d_act = _mm(d_y, wd, tb=True, out_dtype=BF16, name=f"{tag}_down_dx")
    d_wd = _mm(act, d_y, ta=True, out_dtype=BF16, name=f"{tag}_down_dw")
    (d_gu,) = _rows_bwd(_swiglu_fn, [(gu, 2 * Fs, 0)], [], [(d_act, Fs, 0)], [(0, 2 * N_SHARD * Fs, BF16)],
                        name=f"{tag}_swiglu_bwd", ncol=N_SHARD)
    d_wgu = _mm(h, d_gu, ta=True, out_dtype=BF16, out_groups=N_SHARD, name=f"{tag}_gate_up_dw")
    d_h = _mm(d_gu, wgu, tb=True, bk_groups=N_SHARD, out_dtype=BF16, name=f"{tag}_gate_up_dx")
    dx, d_g_pre = _rows_bwd(_prenorm_fn, [(x, D, 0)], [g_pre], [(d_h, D, 0)], [(0, D, F32)],
                            name=f"{tag}_prenorm_bwd", add_to_first=dx_new)
    return dx, d_g_pre, d_g_post, d_wgu, d_wd


@jax.custom_vjp
def _swap_halves(x):
    return pltpu.roll(x, HEAD_DIM // 2, 1)


def _swap_fwd(x):
    return _swap_halves(x), None


def _swap_bwd(_, g):
    return (_swap_halves(g),)


_swap_halves.defvjp(_swap_fwd, _swap_bwd)


def _rope(x, cc, ss):
    return x * cc + _swap_halves(x) * ss


def _attn_block_fn(has_prev):
    grp = ATTN_HEADS // ATTN_KV_HEADS
    scale = HEAD_DIM ** -0.5
    nt = (((1,), (1,)), ((), ()))
    nn = (((1,), (0,)), ((), ()))

    def fn(*a):
        q = a[:8]
        kp, kc, vp, vc = a[8:10], a[10:12], a[12:14], a[14:16]
        cc, ss, ccp, ssp, sinks = a[16:21]
        row = lax.broadcasted_iota(jnp.int32, (WINDOW, WINDOW), 0)
        col = lax.broadcasted_iota(jnp.int32, (WINDOW, WINDOW), 1)
        m_cur = col <= row
        m_prev = jnp.logical_and(col > row, has_prev)
        outs = []
        for h in range(ATTN_HEADS):
            g = h // grp
            qr = _rope(q[h], cc, ss).astype(BF16)
            kcr = _rope(kc[g], cc, ss).astype(BF16)
            kpr = _rope(kp[g], ccp, ssp).astype(BF16)
            s_c = jnp.where(m_cur, lax.dot_general(qr, kcr, nt, preferred_element_type=F32) * scale, -jnp.inf)
            s_p = jnp.where(m_prev, lax.dot_general(qr, kpr, nt, preferred_element_type=F32) * scale, -jnp.inf)
            sink = sinks[h:h + 1, :]
            m = jnp.maximum(jnp.maximum(jnp.max(s_c, axis=-1, keepdims=True), jnp.max(s_p, axis=-1, keepdims=True)),
                            sink)
            p_c = jnp.exp(s_c - m)
            p_p = jnp.exp(s_p - m)
            den = (jnp.sum(p_c, axis=-1, keepdims=True) + jnp.sum(p_p, axis=-1, keepdims=True) + jnp.exp(sink - m))
            inv = 1.0 / den
            o = (lax.dot_general((p_c * inv).astype(BF16), vc[g].astype(BF16), nn, preferred_element_type=F32)
                 + lax.dot_general((p_p * inv).astype(BF16), vp[g].astype(BF16), nn, preferred_element_type=F32))
            outs.append(o)
        return tuple(outs)

    return fn


def _attn_specs(order):
    kvb = ATTN_WIDTH // (2 * ATTN_KV_WIDTH)
    return [
        pl.BlockSpec((WINDOW, ATTN_WIDTH), lambda i: (order(i), 0)),
        pl.BlockSpec((WINDOW, 2 * ATTN_KV_WIDTH), lambda i: (jnp.maximum(order(i) - 1, 0), kvb)),
        pl.BlockSpec((WINDOW, 2 * ATTN_KV_WIDTH), lambda i: (order(i), kvb)),
        pl.BlockSpec((WINDOW, HEAD_DIM), lambda i: (order(i), 0)),
        pl.BlockSpec((WINDOW, HEAD_DIM), lambda i: (order(i), 0)),
        pl.BlockSpec((WINDOW, HEAD_DIM), lambda i: (jnp.maximum(order(i) - 1, 0), 0)),
        pl.BlockSpec((WINDOW, HEAD_DIM), lambda i: (jnp.maximum(order(i) - 1, 0), 0)),
        pl.BlockSpec((ATTN_HEADS, HEAD_DIM), lambda i: (0, 0)),
    ]


def _attn_args(q_ref, kvp_ref, kvc_ref, cc, ss, ccp, ssp, sinks):
    d = HEAD_DIM
    q = [q_ref[:, h * d:(h + 1) * d].astype(F32) for h in range(ATTN_HEADS)]
    kp = [kvp_ref[:, g * d:(g + 1) * d].astype(F32) for g in range(ATTN_KV_HEADS)]
    vp = [kvp_ref[:, ATTN_KV_WIDTH + g * d:ATTN_KV_WIDTH + (g + 1) * d].astype(F32) for g in range(ATTN_KV_HEADS)]
    kc = [kvc_ref[:, g * d:(g + 1) * d].astype(F32) for g in range(ATTN_KV_HEADS)]
    vc = [kvc_ref[:, ATTN_KV_WIDTH + g * d:ATTN_KV_WIDTH + (g + 1) * d].astype(F32) for g in range(ATTN_KV_HEADS)]
    return q + kp + kc + vp + vc + [cc[...], ss[...], ccp[...], ssp[...], sinks[...]]


def _attn_fwd(z, cc, ss, sinks_b, tag):
    S = z.shape[0]
    nb = S // WINDOW

    def body(q_ref, kvp_ref, kvc_ref, cc_r, ss_r, ccp_r, ssp_r, sink_r, o_ref):
        n = pl.program_id(0)
        outs = _attn_block_fn(n > 0)(*_attn_args(q_ref, kvp_ref, kvc_ref, cc_r, ss_r, ccp_r, ssp_r, sink_r))
        for h in range(ATTN_HEADS):
            o_ref[:, h * HEAD_DIM:(h + 1) * HEAD_DIM] = outs[h].astype(o_ref.dtype)

    return pl.pallas_call(
        body, name=f"{tag}_attn", grid=(nb,), in_specs=_attn_specs(lambda i: i),
        out_specs=pl.BlockSpec((WINDOW, ATTN_WIDTH), lambda i: (i, 0)),
        out_shape=jax.ShapeDtypeStruct((S, ATTN_WIDTH), BF16),
        compiler_params=_params(("parallel",)))(z, z, z, cc, ss, cc, ss, sinks_b)


def _attn_bwd(z, cc, ss, sinks_b, d_out, tag):
    S = z.shape[0]
    nb = S // WINDOW
    d = HEAD_DIM
    rev = lambda i: nb - 1 - i

    def body(q_ref, kvp_ref, kvc_ref, cc_r, ss_r, ccp_r, ssp_r, sink_r, do_ref, dq_ref, dkv_ref, dsink_ref, carry):
        i = pl.program_id(0)
        n = nb - 1 - i

        @pl.when(i == 0)
        def _():
            carry[...] = jnp.zeros_like(carry)
            dsink_ref[...] = jnp.zeros_like(dsink_ref)

        args = _attn_args(q_ref, kvp_ref, kvc_ref, cc_r, ss_r, ccp_r, ssp_r, sink_r)
        _, vjp = jax.vjp(_attn_block_fn(n > 0), *args)
        g = vjp(tuple(do_ref[:, h * d:(h + 1) * d].astype(F32) for h in range(ATTN_HEADS)))
        for h in range(ATTN_HEADS):
            dq_ref[:, h * d:(h + 1) * d] = g[h].astype(dq_ref.dtype)
        for gi in range(ATTN_KV_HEADS):
            ks = slice(gi * d, (gi + 1) * d)
            vs = slice(ATTN_KV_WIDTH + gi * d, ATTN_KV_WIDTH + (gi + 1) * d)
            dkv_ref[:, ks] = (g[10 + gi] + carry[:, ks]).astype(dkv_ref.dtype)
            dkv_ref[:, vs] = (g[14 + gi] + carry[:, vs]).astype(dkv_ref.dtype)
            carry[:, ks] = g[8 + gi]
            carry[:, vs] = g[12 + gi]
        ds = g[20]
        dsink_ref[...] += jnp.broadcast_to(jnp.sum(ds, axis=-1, keepdims=True), ds.shape)

    return pl.pallas_call(
        body, name=f"{tag}_attn_bwd", grid=(nb,),
        in_specs=_attn_specs(rev) + [pl.BlockSpec((WINDOW, ATTN_WIDTH), lambda i: (rev(i), 0))],
        out_specs=[pl.BlockSpec((WINDOW, ATTN_WIDTH), lambda i: (rev(i), 0)),
                   pl.BlockSpec((WINDOW, 2 * ATTN_KV_WIDTH), lambda i: (rev(i), 0)),
                   pl.BlockSpec((ATTN_HEADS, HEAD_DIM), lambda i: (0, 0))],
        out_shape=[jax.ShapeDtypeStruct((S, ATTN_WIDTH), BF16), jax.ShapeDtypeStruct((S, 2 * ATTN_KV_WIDTH), BF16),
                   jax.ShapeDtypeStruct((ATTN_HEADS, HEAD_DIM), F32)],
        scratch_shapes=[pltpu.VMEM((WINDOW, 2 * ATTN_KV_WIDTH), F32)],
        compiler_params=_params(("arbitrary",)))(z, z, z, cc, ss, cc, ss, sinks_b, d_out)


def _rope_tables(seq):
    half = HEAD_DIM // 2
    inv_freq = ROPE_THETA ** (-jnp.arange(half, dtype=F32) / half)
    ang = jnp.arange(seq, dtype=F32)[:, None] * inv_freq[None, :]
    cos, sin = jnp.cos(ang), jnp.sin(ang)
    return jnp.concatenate([cos, cos], axis=1), jnp.concatenate([-sin, sin], axis=1)


CONV_COLS = 128


def _conv_pre(u, w_ref, S):
    row = lax.broadcasted_iota(jnp.int32, u.shape, 0)
    shifted = [u] + [jnp.where(row >= s, pltpu.roll(u, s, 0), 0.0) for s in range(1, DN_CONV)]
    y = shifted[0] * w_ref[DN_CONV - 1:DN_CONV, :]
    for s in range(1, DN_CONV):
        y = y + shifted[s] * w_ref[DN_CONV - 1 - s:DN_CONV - s, :]
    return y, shifted, row


def _conv_fwd(z, conv_w, tag):
    S = z.shape[0]
    ncol = 3 * DN_WIDTH // CONV_COLS

    def body(u_ref, w_ref, o_ref):
        y, _, _ = _conv_pre(u_ref[...], w_ref, S)
        o_ref[...] = y * jax.nn.sigmoid(y)

    return pl.pallas_call(
        body, name=f"{tag}_conv", grid=(ncol,),
        in_specs=[pl.BlockSpec((S, CONV_COLS), lambda j: (0, Z_DQKV // CONV_COLS + j)),
                  pl.BlockSpec((DN_CONV, CONV_COLS), lambda j: (0, j))],
        out_specs=pl.BlockSpec((S, CONV_COLS), lambda j: (0, j)),
        out_shape=jax.ShapeDtypeStruct((S, 3 * DN_WIDTH), F32),
        compiler_params=_params(("parallel",)))(z, conv_w)


def _conv_bwd(z, conv_w, d_out, tag):
    S = z.shape[0]
    ncol = 3 * DN_WIDTH // CONV_COLS

    def body(u_ref, w_ref, do_ref, du_ref, dw_ref):
        y, shifted, row = _conv_pre(u_ref[...], w_ref, S)
        sg = jax.nn.sigmoid(y)
        d_y = do_ref[...] * (sg * (1.0 + y * (1.0 - sg)))
        d_u = d_y * w_ref[DN_CONV - 1:DN_CONV, :]
        dw_ref[DN_CONV - 1:DN_CONV, :] = jnp.sum(d_y * shifted[0], axis=0, keepdims=True)
        for s in range(1, DN_CONV):
            back = jnp.where(row < S - s, pltpu.roll(d_y, S - s, 0), 0.0)
            d_u = d_u + back * w_ref[DN_CONV - 1 - s:DN_CONV - s, :]
            dw_ref[DN_CONV - 1 - s:DN_CONV - s, :] = jnp.sum(d_y * shifted[s], axis=0, keepdims=True)
        du_ref[...] = d_u.astype(du_ref.dtype)

    return pl.pallas_call(
        body, name=f"{tag}_conv_bwd", grid=(ncol,),
        in_specs=[pl.BlockSpec((S, CONV_COLS), lambda j: (0, Z_DQKV // CONV_COLS + j)),
                  pl.BlockSpec((DN_CONV, CONV_COLS), lambda j: (0, j)),
                  pl.BlockSpec((S, CONV_COLS), lambda j: (0, j))],
        out_specs=[pl.BlockSpec((S, CONV_COLS), lambda j: (0, j)), pl.BlockSpec((DN_CONV, CONV_COLS), lambda j: (0, j))],
        out_shape=[jax.ShapeDtypeStruct((S, 3 * DN_WIDTH), BF16), jax.ShapeDtypeStruct((DN_CONV, 3 * DN_WIDTH), F32)],
        compiler_params=_params(("parallel",)))(z, conv_w, d_out)


_NN = (((1,), (0,)), ((), ()))
_NT = (((1,), (1,)), ((), ()))
_TN = (((0,), (0,)), ((), ()))


def _dot3(a, b, dims):
    a_hi, b_hi = a.astype(BF16), b.astype(BF16)
    a_lo, b_lo = (a - a_hi.astype(F32)).astype(BF16), (b - b_hi.astype(F32)).astype(BF16)
    mm = lambda p, q: lax.dot_general(p, q, dims, preferred_element_type=F32)
    return mm(a_hi, b_hi) + (mm(a_hi, b_lo) + mm(a_lo, b_hi))


@functools.partial(jax.custom_vjp, nondiff_argnums=(2,))
def _dot_vjp(a, b, dims):
    return _dot3(a, b, dims)


def _dot_vjp_fwd(a, b, dims):
    return _dot3(a, b, dims), (a, b)


def _dot_vjp_bwd(dims, res, g):
    a, b = res
    if dims == _NN:
        return _dot3(g, b, _NT), _dot3(a, g, _TN)
    if dims == _NT:
        return _dot3(g, b, _NN), _dot3(g, a, _TN)
    return _dot3(b, g, _NT), _dot3(a, g, _NN)


_dot_vjp.defvjp(_dot_vjp_fwd, _dot_vjp_bwd)


def _dot(a, b, dims=_NN):
    return _dot_vjp(a, b, dims)


def _dn_head_fn(h):
    C, dk = DN_CHUNK, DN_HEAD_DIM

    def fn(state, q, k, v, zg, ba, prm, norm_w):
        lane = lax.broadcasted_iota(jnp.int32, (C, dk), 1)
        rowl = lax.broadcasted_iota(jnp.int32, (C, dk), 0)
        row = lax.broadcasted_iota(jnp.int32, (C, C), 0)
        col = lax.broadcasted_iota(jnp.int32, (C, C), 1)
        bcol = jnp.sum(jnp.where(lane == h, ba, 0.0), axis=-1, keepdims=True)
        acol = jnp.sum(jnp.where(lane == DN_HEADS + h, ba, 0.0), axis=-1, keepdims=True)
        a_log, dtb = prm[h:h + 1, :], prm[DN_HEADS + h:DN_HEADS + h + 1, :]
        qn = q * lax.rsqrt(jnp.sum(q * q, axis=-1, keepdims=True) + NORM_EPS) * (dk ** -0.5)
        kn = k * lax.rsqrt(jnp.sum(k * k, axis=-1, keepdims=True) + NORM_EPS)
        beta = jax.nn.sigmoid(bcol)
        sp_in = acol + dtb
        softplus = jnp.maximum(sp_in, 0.0) + jnp.log(1.0 + jnp.exp(-jnp.abs(sp_in)))
        gt = -jnp.exp(a_log) * softplus
        gc = _dot((row >= col).astype(F32), gt)
        gcol = jnp.mean(gc, axis=-1, keepdims=True)
        grow = _dot(jnp.full((C, dk), 1.0 / dk, F32), gc, _NT)
        decay = jnp.exp(jnp.where(row >= col, gcol - grow, -jnp.inf))
        kb = kn * beta
        m = -jnp.where(row > col, _dot(kb, kn, _NT) * decay, 0.0)
        u = v * beta
        w = kb * jnp.exp(gc)
        for step in range(6):
            u = u + _dot(m, u)
            w = w + _dot(m, w)
            if step < 5:
                m = _dot(m, m)
        attn = jnp.where(row >= col, _dot(qn, kn, _NT) * decay, 0.0)
        q_dec = qn * jnp.exp(gc)
        gl = jnp.sum(jnp.where(rowl == C - 1, gc, 0.0), axis=0, keepdims=True)
        k_dec = kn * jnp.exp(gl - gc)
        v_new = u - _dot(w, state)
        o = _dot(q_dec, state) + _dot(attn, v_new)
        state_new = state * jnp.exp(gl) + _dot(k_dec, v_new, _TN)
        y = o * lax.rsqrt(jnp.mean(o * o, axis=-1, keepdims=True) + NORM_EPS) * norm_w
        y = y * (zg * jax.nn.sigmoid(zg))
        return state_new, y

    return fn


def _dn_specs(order):
    C = DN_CHUNK
    return [pl.BlockSpec((C, 3 * DN_WIDTH), lambda i: (order(i), 0)),
            pl.BlockSpec((C, DN_WIDTH), lambda i: (order(i), Z_DZ // DN_WIDTH)),
            pl.BlockSpec((C, 128), lambda i: (order(i), Z_DBA // 128)),
            pl.BlockSpec((8, 128), lambda i: (0, 0)),
            pl.BlockSpec((1, DN_HEAD_DIM), lambda i: (0, 0))]


def _dn_head_args(h, qkv_ref, zg_ref, ba_ref, prm_ref, nw_ref):
    d = DN_HEAD_DIM
    sl = lambda base: slice(base + h * d, base + (h + 1) * d)
    return [qkv_ref[:, sl(0)], qkv_ref[:, sl(DN_WIDTH)], qkv_ref[:, sl(2 * DN_WIDTH)], zg_ref[:, sl(0)].astype(F32),
            ba_ref[...].astype(F32), prm_ref[...], nw_ref[...]]


def _dn_fwd(qkv, z, prm, norm_w, tag):
    S = qkv.shape[0]
    nchunk = S // DN_CHUNK
    d = DN_HEAD_DIM

    def body(qkv_ref, zg_ref, ba_ref, prm_ref, nw_ref, y_ref, st_ref, state):
        @pl.when(pl.program_id(0) == 0)
        def _():
            state[...] = jnp.zeros_like(state)

        for h in range(DN_HEADS):
            st_ref[h] = state[h]
            new, y = _dn_head_fn(h)(state[h], *_dn_head_args(h, qkv_ref, zg_ref, ba_ref, prm_ref, nw_ref))
            state[h] = new
            y_ref[:, h * d:(h + 1) * d] = y.astype(y_ref.dtype)

    return pl.pallas_call(
        body, name=f"{tag}_deltanet", grid=(nchunk,), in_specs=_dn_specs(lambda i: i),
        out_specs=[pl.BlockSpec((DN_CHUNK, DN_WIDTH), lambda i: (i, 0)),
                   pl.BlockSpec((None, DN_HEADS, d, d), lambda i: (i, 0, 0, 0))],
        out_shape=[jax.ShapeDtypeStruct((S, DN_WIDTH), BF16), jax.ShapeDtypeStruct((nchunk, DN_HEADS, d, d), F32)],
        scratch_shapes=[pltpu.VMEM((DN_HEADS, d, d), F32)],
        compiler_params=_params(("arbitrary",)))(qkv, z, z, prm, norm_w)


def _dn_bwd(qkv, z, prm, norm_w, states, d_y, tag):
    S = qkv.shape[0]
    nchunk = S // DN_CHUNK
    d = DN_HEAD_DIM
    rev = lambda i: nchunk - 1 - i

    def body(qkv_ref, zg_ref, ba_ref, prm_ref, nw_ref, st_ref, dy_ref,
             dqkv_ref, dzg_ref, dba_ref, dprm_ref, dnw_ref, d_state):
        @pl.when(pl.program_id(0) == 0)
        def _():
            d_state[...] = jnp.zeros_like(d_state)
            dprm_ref[...] = jnp.zeros_like(dprm_ref)
            dnw_ref[...] = jnp.zeros_like(dnw_ref)

        d_ba = jnp.zeros((DN_CHUNK, 128), F32)
        d_prm = jnp.zeros((8, 128), F32)
        d_nw = jnp.zeros((1, d), F32)
        for h in range(DN_HEADS):
            args = [st_ref[h]] + _dn_head_args(h, qkv_ref, zg_ref, ba_ref, prm_ref, nw_ref)
            _, vjp = jax.vjp(_dn_head_fn(h), *args)
            g = vjp((d_state[h], dy_ref[:, h * d:(h + 1) * d].astype(F32)))
            d_state[h] = g[0]
            for n, base in enumerate((0, DN_WIDTH, 2 * DN_WIDTH)):
                dqkv_ref[:, base + h * d:base + (h + 1) * d] = g[1 + n]
            dzg_ref[:, h * d:(h + 1) * d] = g[4].astype(dzg_ref.dtype)
            d_ba = d_ba + g[5]
            d_prm = d_prm + g[6]
            d_nw = d_nw + g[7]
        dba_ref[...] = d_ba.astype(dba_ref.dtype)
        dprm_ref[...] += jnp.broadcast_to(jnp.sum(d_prm, axis=-1, keepdims=True), d_prm.shape)
        dnw_ref[...] += d_nw

    return pl.pallas_call(
        body, name=f"{tag}_deltanet_bwd", grid=(nchunk,),
        in_specs=_dn_specs(rev) + [pl.BlockSpec((None, DN_HEADS, d, d), lambda i: (rev(i), 0, 0, 0)),
                                   pl.BlockSpec((DN_CHUNK, DN_WIDTH), lambda i: (rev(i), 0))],
        out_specs=[pl.BlockSpec((DN_CHUNK, 3 * DN_WIDTH), lambda i: (rev(i), 0)),
                   pl.BlockSpec((DN_CHUNK, DN_WIDTH), lambda i: (rev(i), 0)),
                   pl.BlockSpec((DN_CHUNK, 128), lambda i: (rev(i), 0)),
                   pl.BlockSpec((8, 128), lambda i: (0, 0)),
                   pl.BlockSpec((1, d), lambda i: (0, 0))],
        out_shape=[jax.ShapeDtypeStruct((S, 3 * DN_WIDTH), F32), jax.ShapeDtypeStruct((S, DN_WIDTH), BF16),
                   jax.ShapeDtypeStruct((S, 128), BF16), jax.ShapeDtypeStruct((8, 128), F32),
                   jax.ShapeDtypeStruct((1, d), F32)],
        scratch_shapes=[pltpu.VMEM((DN_HEADS, d, d), F32)],
        compiler_params=_params(("arbitrary",)))(qkv, z, z, prm, norm_w, states, d_y)


def _whole_fwd(fn, ins, outs, *, name):
    n = len(ins)

    def body(*refs):
        res = fn(*[r[...] for r in refs[:n]])
        for o_ref, o in zip(refs[n:], res):
            o_ref[...] = o

    return pl.pallas_call(body, name=name, out_shape=[jax.ShapeDtypeStruct(s, F32) for s in outs],
                          compiler_params=_params())(*ins)


def _whole_bwd(fn, ins, cts, n_grads, *, name, lane_sum=()):
    n, nt = len(ins), len(cts)

    def body(*refs):
        _, vjp = jax.vjp(fn, *[r[...] for r in refs[:n]])
        grads = vjp(tuple(r[...] for r in refs[n:n + nt]))
        for k in range(n_grads):
            g = grads[k]
            if k in lane_sum:
                g = jnp.broadcast_to(jnp.sum(g, axis=-1, keepdims=True), g.shape)
            refs[n + nt + k][...] = g

    return pl.pallas_call(body, name=name, out_shape=[jax.ShapeDtypeStruct(a.shape, F32) for a in ins[:n_grads]],
                          compiler_params=_params())(*ins, *cts)


S5_CHUNK = 256


def _s5_param_fn(a_re, a_im, ldt, bt_re, bt_im, expand):
    dt = jnp.exp(ldt)
    er = jnp.exp(a_re * dt)
    ab_re, ab_im = er * jnp.cos(a_im * dt), er * jnp.sin(a_im * dt)
    den = a_re * a_re + a_im * a_im
    co_re = ((ab_re - 1.0) * a_re + ab_im * a_im) / den
    co_im = (ab_im * a_re - (ab_re - 1.0) * a_im) / den
    cr, ci = _dot(expand, co_re), _dot(expand, co_im)
    return ab_re, ab_im, cr * bt_re - ci * bt_im, cr * bt_im + ci * bt_re


def _s5_scan(b_re, b_im, a_re, a_im, row, T, reverse):
    x_re, x_im, p_re, p_im = b_re, b_im, a_re, a_im
    d = 1
    while d < T:
        if reverse:
            s_re = jnp.where(row < T - d, pltpu.roll(x_re, T - d, 0), 0.0)
            s_im = jnp.where(row < T - d, pltpu.roll(x_im, T - d, 0), 0.0)
        else:
            s_re = jnp.where(row >= d, pltpu.roll(x_re, d, 0), 0.0)
            s_im = jnp.where(row >= d, pltpu.roll(x_im, d, 0), 0.0)
        x_re, x_im = x_re + p_re * s_re - p_im * s_im, x_im + p_re * s_im + p_im * s_re
        p_re, p_im = p_re * p_re - p_im * p_im, 2.0 * p_re * p_im
        d *= 2
    return x_re, x_im


def _s5_states(u, bre_ref, bim_ref, a_re, a_im, c_re, c_im, row, T):
    bu_re = _dot(u, bre_ref[...]) + jnp.where(row == 0, a_re * c_re - a_im * c_im, 0.0)
    bu_im = _dot(u, bim_ref[...]) + jnp.where(row == 0, a_re * c_im + a_im * c_re, 0.0)
    return _s5_scan(bu_re, bu_im, a_re, a_im, row, T, False)


def _s5_in_specs(order, T):
    full = lambda shape: pl.BlockSpec(shape, lambda i: (0,) * len(shape))
    return [pl.BlockSpec((T, S5_WIDTH), lambda i: (order(i), Z_SU // S5_WIDTH)),
            full((S5_WIDTH, S5_LANES)), full((S5_WIDTH, S5_LANES)), full((S5_WIDTH, S5_LANES)),
            full((S5_WIDTH, S5_LANES)), full((1, S5_LANES)), full((1, S5_LANES)), full((1, S5_WIDTH))]


def _s5_fwd(z, bre, bim, cre, cim, ab_re, ab_im, dskip, tag):
    S = z.shape[0]
    T = _pick(S, (S5_CHUNK, 128))
    nch = S // T

    def body(u_ref, bre_ref, bim_ref, cre_ref, cim_ref, are_ref, aim_ref, d_ref, y_ref, kre_ref, kim_ref, c_re, c_im):
        @pl.when(pl.program_id(0) == 0)
        def _():
            c_re[...] = jnp.zeros_like(c_re)
            c_im[...] = jnp.zeros_like(c_im)

        kre_ref[...] = c_re[...]
        kim_ref[...] = c_im[...]
        u = u_ref[...]
        row = lax.broadcasted_iota(jnp.int32, (T, S5_LANES), 0)
        x_re, x_im = _s5_states(u, bre_ref, bim_ref, are_ref[...], aim_ref[...], c_re[...], c_im[...], row, T)
        c_re[...] = jnp.sum(jnp.where(row == T - 1, x_re, 0.0), axis=0, keepdims=True)
        c_im[...] = jnp.sum(jnp.where(row == T - 1, x_im, 0.0), axis=0, keepdims=True)
        y_ref[...] = _dot(x_re, cre_ref[...], _NT) - _dot(x_im, cim_ref[...], _NT) + d_ref[...] * u

    return pl.pallas_call(
        body, name=f"{tag}_s5", grid=(nch,), in_specs=_s5_in_specs(lambda i: i, T),
        out_specs=[pl.BlockSpec((T, S5_WIDTH), lambda i: (i, 0)),
                   pl.BlockSpec((None, 1, S5_LANES), lambda i: (i, 0, 0)),
                   pl.BlockSpec((None, 1, S5_LANES), lambda i: (i, 0, 0))],
        out_shape=[jax.ShapeDtypeStruct((S, S5_WIDTH), F32), jax.ShapeDtypeStruct((nch, 1, S5_LANES), F32),
                   jax.ShapeDtypeStruct((nch, 1, S5_LANES), F32)],
        scratch_shapes=[pltpu.VMEM((1, S5_LANES), F32), pltpu.VMEM((1, S5_LANES), F32)],
        compiler_params=_params(("arbitrary",)))(z, bre, bim, cre, cim, ab_re, ab_im, dskip)


def _s5_bwd(z, bre, bim, cre, cim, ab_re, ab_im, dskip, kre, kim, d_y, tag):
    S = z.shape[0]
    T = _pick(S, (S5_CHUNK, 128))
    nch = S // T
    rev = lambda i: nch - 1 - i
    full = lambda shape: pl.BlockSpec(shape, lambda i: (0,) * len(shape))

    def body(u_ref, bre_ref, bim_ref, cre_ref, cim_ref, are_ref, aim_ref, d_ref, kre_ref, kim_ref, dy_ref,
             du_ref, dbre_ref, dbim_ref, dcre_ref, dcim_ref, dare_ref, daim_ref, dd_ref, g_re, g_im):
        @pl.when(pl.program_id(0) == 0)
        def _():
            g_re[...] = jnp.zeros_like(g_re)
            g_im[...] = jnp.zeros_like(g_im)
            for r in (dbre_ref, dbim_ref, dcre_ref, dcim_ref, dare_ref, daim_ref, dd_ref):
                r[...] = jnp.zeros_like(r)

        u = u_ref[...]
        dy = dy_ref[...].astype(F32)
        a_re, a_im = are_ref[...], aim_ref[...]
        row = lax.broadcasted_iota(jnp.int32, (T, S5_LANES), 0)
        x_re, x_im = _s5_states(u, bre_ref, bim_ref, a_re, a_im, kre_ref[...], kim_ref[...], row, T)
        dcre_ref[...] += _dot(dy, x_re, _TN)
        dcim_ref[...] -= _dot(dy, x_im, _TN)
        xp_re = jnp.where(row >= 1, pltpu.roll(x_re, 1, 0), 0.0) + jnp.where(row == 0, kre_ref[...], 0.0)
        xp_im = jnp.where(row >= 1, pltpu.roll(x_im, 1, 0), 0.0) + jnp.where(row == 0, kim_ref[...], 0.0)
        last = row == T - 1
        gd_re = _dot(dy, cre_ref[...]) + jnp.where(last, a_re * g_re[...] + a_im * g_im[...], 0.0)
        gd_im = -_dot(dy, cim_ref[...]) + jnp.where(last, a_re * g_im[...] - a_im * g_re[...], 0.0)
        t_re, t_im = _s5_scan(gd_re, gd_im, a_re, -a_im, row, T, True)
        g_re[...] = jnp.sum(jnp.where(row == 0, t_re, 0.0), axis=0, keepdims=True)
        g_im[...] = jnp.sum(jnp.where(row == 0, t_im, 0.0), axis=0, keepdims=True)
        du_ref[...] = (_dot(t_re, bre_ref[...], _NT) + _dot(t_im, bim_ref[...], _NT) + dy * d_ref[...]).astype(du_ref.dtype)
        dbre_ref[...] += _dot(u, t_re, _TN)
        dbim_ref[...] += _dot(u, t_im, _TN)
        dare_ref[...] += jnp.sum(t_re * xp_re + t_im * xp_im, axis=0, keepdims=True)
        daim_ref[...] += jnp.sum(t_im * xp_re - t_re * xp_im, axis=0, keepdims=True)
        dd_ref[...] += jnp.sum(dy * u, axis=0, keepdims=True)

    return pl.pallas_call(
        body, name=f"{tag}_s5_bwd", grid=(nch,),
        in_specs=_s5_in_specs(rev, T) + [pl.BlockSpec((None, 1, S5_LANES), lambda i: (rev(i), 0, 0)),
                                         pl.BlockSpec((None, 1, S5_LANES), lambda i: (rev(i), 0, 0)),
                                         pl.BlockSpec((T, S5_WIDTH), lambda i: (rev(i), 0))],
        out_specs=[pl.BlockSpec((T, S5_WIDTH), lambda i: (rev(i), 0))] + [full((S5_WIDTH, S5_LANES))] * 4
        + [full((1, S5_LANES))] * 2 + [full((1, S5_WIDTH))],
        out_shape=[jax.ShapeDtypeStruct((S, S5_WIDTH), BF16)] + [jax.ShapeDtypeStruct((S5_WIDTH, S5_LANES), F32)] * 4
        + [jax.ShapeDtypeStruct((1, S5_LANES), F32)] * 2 + [jax.ShapeDtypeStruct((1, S5_WIDTH), F32)],
        scratch_shapes=[pltpu.VMEM((1, S5_LANES), F32), pltpu.VMEM((1, S5_LANES), F32)],
        compiler_params=_params(("arbitrary",)))(z, bre, bim, cre, cim, ab_re, ab_im, dskip, kre, kim, d_y)


def _s5_glu_fn(y, glu_w, glu_b):
    g = 0.5 * y * (1.0 + jnp.tanh(math.sqrt(2.0 / math.pi) * (y + 0.044715 * (y * y * y))))
    lin = lax.dot_general(g.astype(BF16), glu_w.astype(BF16), (((1,), (0,)), ((), ())), preferred_element_type=F32)
    return (g * jax.nn.sigmoid(lin + glu_b),)


def _block_diag(m):
    G, H, P = S5_GROUPS, S5_GROUP_CH, S5_STATE
    eye = jnp.eye(G, dtype=m.dtype)
    return (m.reshape(G, H, 1, P) * eye[:, None, :, None]).reshape(G * H, G * P)


def _block_diag_take(m):
    G, H, P = S5_GROUPS, S5_GROUP_CH, S5_STATE
    eye = jnp.eye(G, dtype=m.dtype)
    return jnp.sum(m.reshape(G, H, G, P) * eye[:, None, :, None], axis=2).reshape(G * H, P)


def _loss_head(y, target, tag):
    S, D = y.shape
    tm = _pick(S, (256, 128, 64))

    def body(y_ref, t_ref, dy_ref, loss_ref):
        @pl.when(pl.program_id(0) == 0)
        def _():
            loss_ref[...] = jnp.zeros_like(loss_ref)

        err = y_ref[...] - t_ref[...]
        dy_ref[...] = err * (1.0 / D)
        part = 0.5 * jnp.sum(jnp.mean(err * err, axis=-1, keepdims=True), axis=0, keepdims=True)
        loss_ref[...] += jnp.broadcast_to(part, loss_ref.shape)

    return pl.pallas_call(
        body, name=f"{tag}_loss", grid=(S // tm,),
        in_specs=[pl.BlockSpec((tm, D), lambda i: (i, 0)), pl.BlockSpec((tm, D), lambda i: (i, 0))],
        out_specs=[pl.BlockSpec((tm, D), lambda i: (i, 0)), pl.BlockSpec((8, 128), lambda i: (0, 0))],
        out_shape=[jax.ShapeDtypeStruct((S, D), F32), jax.ShapeDtypeStruct((8, 128), F32)],
        compiler_params=_params(("arbitrary",)))(y, target)


def _adamw(w, g, m, v, name):
    shape = w.shape
    cols = shape[-1]
    rows = int(np.prod(shape[:-1]))
    tr = _pick(rows, [t for t in (512, 256, 128, 64, 32, 16, 8) if t * cols <= 256 * 1024] or [8])
    c1 = 1.0 - ADAM_B1 ** ADAM_STEP
    c2 = 1.0 - ADAM_B2 ** ADAM_STEP

    def body(w_ref, g_ref, m_ref, v_ref, d_ref, mo_ref, vo_ref):
        gg = g_ref[...]
        mn = ADAM_B1 * m_ref[...] + (1.0 - ADAM_B1) * gg
        vn = ADAM_B2 * v_ref[...] + (1.0 - ADAM_B2) * (gg * gg)
        d_ref[...] = -ADAM_LR * ((mn / c1) / (jnp.sqrt(vn / c2) + ADAM_EPS) + ADAM_WD * w_ref[...])
        mo_ref[...] = mn
        vo_ref[...] = vn

    spec = pl.BlockSpec((tr, cols), lambda i: (i, 0))
    outs = pl.pallas_call(
        body, name=name, grid=(rows // tr,), in_specs=[spec] * 4, out_specs=[spec] * 3,
        out_shape=[jax.ShapeDtypeStruct((rows, cols), F32)] * 3,
        compiler_params=_params(("parallel",)))(*[a.reshape(rows, cols) for a in (w, g, m, v)])
    return [o.reshape(shape) for o in outs]


def _sum_slots(slots, name):
    n, R, C = slots.shape
    tr = _pick(R, (1024, 512, 256, 128, 64, 32, 16, 8))

    def body(s_ref, o_ref):
        acc = s_ref[0].astype(F32)
        for k in range(1, n):
            acc = acc + s_ref[k].astype(F32)
        o_ref[...] = acc

    return pl.pallas_call(
        body, name=name, grid=(R // tr,), in_specs=[pl.BlockSpec((n, tr, C), lambda i: (0, i, 0))],
        out_specs=pl.BlockSpec((tr, C), lambda i: (i, 0)),
        out_shape=jax.ShapeDtypeStruct((R, C), F32), compiler_params=_params(("parallel",)))(slots)


def _row_tile(R, C, itemsize, target=1 << 20):
    return _pick(R, [t for t in (2048, 1024, 512, 256, 128, 64, 32, 16) if t * C * itemsize <= target] or [16])


def _add_sibling(part, got, name):
    n, _, R2, C = part.shape
    tr = _row_tile(R2, C, 2)

    def body(p_ref, q_ref, o_ref):
        o_ref[...] = (p_ref[...].astype(F32) + q_ref[...].astype(F32)).astype(o_ref.dtype)

    spec = pl.BlockSpec((None, tr, C), lambda k, i: (k, i, 0))
    return pl.pallas_call(
        body, name=name, grid=(n, R2 // tr),
        in_specs=[pl.BlockSpec((None, None, tr, C), lambda k, i: (k, lax.axis_index("c"), i, 0)), spec], out_specs=spec,
        out_shape=jax.ShapeDtypeStruct((n, R2, C), BF16), compiler_params=_params(("parallel", "parallel")))(part, got)


def _sum_chips(chip_part, others, name):
    _, R2, C = chip_part.shape
    tr = _row_tile(R2, C, 4)

    def body(a_ref, b_ref, o_ref):
        acc = a_ref[...].astype(F32)
        for j in range(N_SHARD - 1):
            acc = acc + b_ref[j].astype(F32)
        o_ref[...] = acc

    return pl.pallas_call(
        body, name=name, grid=(R2 // tr,),
        in_specs=[pl.BlockSpec((None, tr, C), lambda i: (2 * lax.axis_index("x") + lax.axis_index("y"), i, 0)),
                  pl.BlockSpec((N_SHARD - 1, tr, C), lambda i: (0, i, 0))],
        out_specs=pl.BlockSpec((None, tr, C), lambda i: (lax.axis_index("c"), i, 0)),
        out_shape=jax.ShapeDtypeStruct((2, R2, C), F32), compiler_params=_params(("parallel",)))(chip_part, others)


_ANY = pl.BlockSpec(memory_space=pl.ANY)


def _place():
    return lax.axis_index("x"), lax.axis_index("y"), lax.axis_index("c")


def _other_chips(x, y):
    return [(1 - x, y), (x, 1 - y), (1 - x, 1 - y)]


def _remote(src, dst, send_sems, recv_sems, n, to):
    return pltpu.make_async_remote_copy(src_ref=src, dst_ref=dst, send_sem=send_sems.at[n], recv_sem=recv_sems.at[n],
                                        device_id=to, device_id_type=MESH_ID)


def _gather_shards(shards, name):
    n = len(shards)
    per = 7

    def body(*refs):
        ins, outs = refs[:n], refs[n:2 * n]
        send_sems, recv_sems = refs[2 * n:]
        x, y, c = _place()
        k = 2 * x + y
        sibling = (x, y, 1 - c)
        chips = _other_chips(x, y)
        started = []
        for a in range(n):
            cp = _remote(ins[a], outs[a].at[k], send_sems, recv_sems, per * a, sibling)
            cp.start()
            started.append(cp)
            for j, (cx, cy) in enumerate(chips):
                cp = _remote(ins[a].at[c], outs[a].at[k, c], send_sems, recv_sems, per * a + 1 + j, (cx, cy, c))
                cp.start()
                started.append(cp)
        for a in range(n):
            for j, (cx, cy) in enumerate(chips):
                landed = outs[a].at[2 * cx + cy, c]
                _remote(landed, landed, send_sems, recv_sems, per * a + 1 + j, sibling).wait_recv()
                cp = _remote(landed, landed, send_sems, recv_sems, per * a + 4 + j, sibling)
                cp.start()
                started.append(cp)
        for a in range(n):
            own = outs[a].at[k]
            _remote(own, own, send_sems, recv_sems, per * a, sibling).wait_recv()
            for j, (cx, cy) in enumerate(chips):
                passed = outs[a].at[2 * cx + cy, 1 - c]
                _remote(passed, passed, send_sems, recv_sems, per * a + 4 + j, sibling).wait_recv()
        for cp in started:
            cp.wait_send()

    return pl.pallas_call(
        body, name=name, in_specs=[_ANY] * n, out_specs=[_ANY] * n,
        out_shape=[jax.ShapeDtypeStruct((N_SHARD,) + s.shape, s.dtype) for s in shards],
        scratch_shapes=[pltpu.SemaphoreType.DMA((per * n,)), pltpu.SemaphoreType.DMA((per * n,))],
        )(*shards)


def _swap_other_half(parts, name):
    n = len(parts)

    def body(*refs):
        ins, outs = refs[:n], refs[n:2 * n]
        send_sems, recv_sems = refs[2 * n:]
        x, y, c = _place()
        started = []
        for a in range(n):
            for k in range(N_SHARD):
                cp = _remote(ins[a].at[k, 1 - c], outs[a].at[k], send_sems, recv_sems, N_SHARD * a + k, (x, y, 1 - c))
                cp.start()
                started.append(cp)
        for cp in started:
            cp.wait()

    return pl.pallas_call(
        body, name=name, in_specs=[_ANY] * n, out_specs=[_ANY] * n,
        out_shape=[jax.ShapeDtypeStruct((N_SHARD,) + p.shape[2:], p.dtype) for p in parts],
        scratch_shapes=[pltpu.SemaphoreType.DMA((N_SHARD * n,)), pltpu.SemaphoreType.DMA((N_SHARD * n,))],
        )(*parts)


def _scatter_to_chips(parts, name):
    n = len(parts)
    per = N_SHARD - 1

    def body(*refs):
        ins, outs = refs[:n], refs[n:2 * n]
        send_sems, recv_sems = refs[2 * n:]
        x, y, c = _place()
        chips = _other_chips(x, y)
        started = []
        for a in range(n):
            for j, (cx, cy) in enumerate(chips):
                cp = _remote(ins[a].at[2 * cx + cy], outs[a].at[j], send_sems, recv_sems, per * a + j, (cx, cy, c))
                cp.start()
                started.append(cp)
        for cp in started:
            cp.wait()

    return pl.pallas_call(
        body, name=name, in_specs=[_ANY] * n, out_specs=[_ANY] * n,
        out_shape=[jax.ShapeDtypeStruct((per,) + p.shape[1:], p.dtype) for p in parts],
        scratch_shapes=[pltpu.SemaphoreType.DMA((per * n,)), pltpu.SemaphoreType.DMA((per * n,))],
        )(*parts)


def _join_halves(bufs, name):
    n = len(bufs)

    def body(*refs):
        outs = refs[n:2 * n]
        send_sems, recv_sems = refs[2 * n:]
        x, y, c = _place()
        started = []
        for a in range(n):
            cp = _remote(outs[a].at[c], outs[a].at[c], send_sems, recv_sems, a, (x, y, 1 - c))
            cp.start()
            started.append(cp)
        for a in range(n):
            arrives = outs[a].at[1 - c]
            _remote(arrives, arrives, send_sems, recv_sems, a, (x, y, 1 - c)).wait_recv()
        for cp in started:
            cp.wait_send()

    return pl.pallas_call(
        body, name=name, in_specs=[_ANY] * n, out_specs=[_ANY] * n,
        out_shape=[jax.ShapeDtypeStruct(b.shape, b.dtype) for b in bufs],
        input_output_aliases={a: a for a in range(n)},
        scratch_shapes=[pltpu.SemaphoreType.DMA((n,)), pltpu.SemaphoreType.DMA((n,))],
        )(*bufs)


def _gather_all_devices(vec, name):
    def body(in_ref, out_ref, send_sems, recv_sems, local_sem):
        x, y, c = _place()
        me = 4 * x + 2 * y + c
        own = pltpu.make_async_copy(in_ref, out_ref.at[me], local_sem)
        own.start()
        sends = []
        for r in range(1, 8):
            fx, fy, fc = (r >> 2) & 1, (r >> 1) & 1, r & 1
            to = (x ^ fx, y ^ fy, c ^ fc)
            sends.append(pltpu.make_async_remote_copy(
                src_ref=in_ref, dst_ref=out_ref.at[me], send_sem=send_sems.at[r - 1], recv_sem=recv_sems.at[r - 1],
                device_id=to, device_id_type=MESH_ID))
            sends[-1].start()
        for r in range(1, 8):
            fx, fy, fc = (r >> 2) & 1, (r >> 1) & 1, r & 1
            slot = out_ref.at[4 * (x ^ fx) + 2 * (y ^ fy) + (c ^ fc)]
            pltpu.make_async_remote_copy(src_ref=slot, dst_ref=slot, send_sem=send_sems.at[r - 1],
                                         recv_sem=recv_sems.at[r - 1], device_id=(x, y, c),
                                         device_id_type=MESH_ID).wait_recv()
        for cp in sends:
            cp.wait_send()
        own.wait()

    return pl.pallas_call(
        body, name=name, in_specs=[_ANY], out_specs=_ANY, out_shape=jax.ShapeDtypeStruct((8,) + vec.shape, vec.dtype),
        scratch_shapes=[pltpu.SemaphoreType.DMA((7,)), pltpu.SemaphoreType.DMA((7,)), pltpu.SemaphoreType.DMA(())],
        )(vec)


def _permute_w_in(w):
    cut = Z_SU
    return jnp.concatenate([w[:, :cut], w[:, cut + 2 * DN_HEADS:], w[:, cut:cut + 2 * DN_HEADS],
                            jnp.zeros((w.shape[0], Z_WIDTH - IN_WIDTH), w.dtype)], axis=1)


def _unpermute_w_in(wp):
    return jnp.concatenate([wp[:, :Z_SU], wp[:, Z_DBA:Z_DBA + 2 * DN_HEADS], wp[:, Z_SU:Z_DBA]], axis=1)


def _s5_inputs(sp):
    ab_re, ab_im, bb_re, bb_im = _whole_fwd(
        _s5_param_fn, sp["s5_pins"], [(S5_GROUPS, S5_STATE)] * 2 + [(S5_WIDTH, S5_STATE)] * 2, name="s5_params")
    return (_block_diag(bb_re), _block_diag(bb_im), _block_diag(sp["c_re"]), _block_diag(sp["c_im"]),
            ab_re.reshape(1, S5_LANES), ab_im.reshape(1, S5_LANES), sp["dskip"])


def _mixer_fwd(x, sp, w_in, w_out, glu_w, rope):
    D = x.shape[1]
    (h,) = _rows_fwd(_prenorm_fn, [(x, D, 0)], [sp["mix_norm_pre"]], [(D, D, BF16)], name="mix_prenorm")
    z = _mm(h, w_in, out_dtype=F32, name="mix_in")
    y_attn = _attn_fwd(z, rope[0], rope[1], sp["sinks"], "mix")
    qkv = _conv_fwd(z, sp["conv_w"], "mix")
    y_dn, states = _dn_fwd(qkv, z, sp["dn_prm"], sp["dn_norm_w"], "mix")
    s5_in = _s5_inputs(sp)
    y_lin, kre, kim = _s5_fwd(z, *s5_in, "mix")
    (y_s5,) = _rows_fwd(_s5_glu_fn, [(y_lin, S5_WIDTH, 0)], [glu_w, sp["glu_b"]], [(S5_WIDTH, S5_WIDTH, BF16)],
                        name="mix_s5_glu")
    cat = jnp.concatenate([y_attn, y_dn, y_s5], axis=1)
    mixed = _mm(cat, w_out, out_dtype=F32, name="mix_out")
    (x_new,) = _rows_fwd(_residual_fn(1.0), [(x, D, 0), (mixed, D, 0)], [sp["mix_norm_post"]], [(D, D, F32)],
                         name="mix_residual")
    return x_new, (h, z, qkv, states, s5_in, y_lin, kre, kim, cat, mixed)


def _mixer_bwd(dx_new, x, sp, w_in, w_out, glu_w, rope, saved):
    h, z, qkv, states, s5_in, y_lin, kre, kim, cat, mixed = saved
    D = x.shape[1]
    G, H, P = S5_GROUPS, S5_GROUP_CH, S5_STATE
    d_mixed, d_g_post = _rows_bwd(_postnorm_fn(1.0), [(mixed, D, 0)], [sp["mix_norm_post"]], [(dx_new, D, 0)],
                                  [(0, D, BF16)], name="mix_postnorm_bwd")
    d_cat = _mm(d_mixed, w_out, tb=True, out_dtype=BF16, name="mix_out_dx")
    d_w_out = _mm(cat, d_mixed, ta=True, out_dtype=BF16, name="mix_out_dw")
    d_attn, d_dn, d_s5 = d_cat[:, :ATTN_WIDTH], d_cat[:, ATTN_WIDTH:ATTN_WIDTH + DN_WIDTH], d_cat[:, ATTN_WIDTH + DN_WIDTH:]
    d_ylin, d_glu_w, d_glu_b = _rows_bwd(_s5_glu_fn, [(y_lin, S5_WIDTH, 0)], [glu_w, sp["glu_b"]],
                                         [(d_s5, S5_WIDTH, 0)], [(0, S5_WIDTH, F32)], name="mix_s5_glu_bwd")
    d_us5, d_bre, d_bim, d_cre, d_cim, d_are, d_aim, d_dskip = _s5_bwd(z, *s5_in, kre, kim, d_ylin, "mix")
    cts = [d_are.reshape(G, P), d_aim.reshape(G, P), _block_diag_take(d_bre), _block_diag_take(d_bim)]
    d_a_re, d_a_im, d_ldt, d_bt_re, d_bt_im = _whole_bwd(_s5_param_fn, sp["s5_pins"], cts, 5, name="s5_params_bwd",
                                                         lane_sum=(2,))
    from_t = lambda m: m.reshape(G, H, P).transpose(0, 2, 1)
    d_qkv, d_zg, d_ba, d_prm, d_nw = _dn_bwd(qkv, z, sp["dn_prm"], sp["dn_norm_w"], states, d_dn, "mix")
    d_uconv, d_conv_w = _conv_bwd(z, sp["conv_w"], d_qkv, "mix")
    d_q, d_kv, d_sinks = _attn_bwd(z, rope[0], rope[1], sp["sinks"], d_attn, "mix")
    d_z = jnp.concatenate([d_q, d_kv, d_uconv, d_zg, d_us5, d_ba], axis=1)
    d_w_in = _mm(h, d_z, ta=True, out_dtype=BF16, name="mix_in_dw")
    d_h = _mm(d_z, w_in, tb=True, out_dtype=BF16, name="mix_in_dx")
    dx, d_g_pre = _rows_bwd(_prenorm_fn, [(x, D, 0)], [sp["mix_norm_pre"]], [(d_h, D, 0)], [(0, D, F32)],
                            name="mix_prenorm_bwd", add_to_first=dx_new)
    small = {
        "mix_norm_pre": d_g_pre[0], "mix_norm_post": d_g_post[0], "attn_sinks": d_sinks[:, 0], "dn_conv_w": d_conv_w,
        "dn_a_log": d_prm[:DN_HEADS, 0], "dn_dt_bias": d_prm[DN_HEADS:2 * DN_HEADS, 0], "dn_norm_w": d_nw[0],
        "s5_a_re": d_a_re, "s5_a_im": d_a_im, "s5_log_dt": d_ldt[:, 0], "s5_b_re": from_t(d_bt_re),
        "s5_b_im": from_t(d_bt_im), "s5_c_re": _block_diag_take(d_cre).reshape(G, H, P),
        "s5_c_im": _block_diag_take(d_cim).reshape(G, H, P), "s5_d": d_dskip[0], "s5_glu_b": d_glu_b[0],
    }
    return dx, {"win": d_w_in, "wout": d_w_out, "glu": d_glu_w.astype(BF16)}, small


BIG_PIECES = ("gu1", "wd1", "win", "wout", "glu", "gu2", "wd2")


def _halves(m):
    return m.reshape(m.shape[:-2] + (2, m.shape[-2] // 2, m.shape[-1]))


def _whole(m):
    return m.reshape(m.shape[:-3] + (2 * m.shape[-2], m.shape[-1]))


def _pack_small(arrs, extra=()):
    flat = jnp.concatenate([a.reshape(-1) for a in arrs] + list(extra))
    n = flat.shape[0]
    padded = -(-n // 1024) * 1024
    return jnp.concatenate([flat, jnp.zeros((padded - n,), flat.dtype)]).reshape(padded // 128, 128)


def _unpack_small(flat2d, shapes):
    flat = flat2d.reshape(-1)
    out, o = [], 0
    for s in shapes:
        n = int(np.prod(s))
        out.append(flat[o:o + n].reshape(s))
        o += n
    return out, flat[o:]


def _step(a):
    x, target = a["x"][0], a["loss_target"][0]
    S, D = x.shape
    L, _, Fs = a["ff1_w_gate"].shape
    px, py, pc = _place()
    chip = 2 * px + py

    def full(k, g):
        g = _whole(g)
        if k in ("gu1", "gu2"):
            return g
        if k == "win":
            return _permute_w_in(g.transpose(1, 0, 2).reshape(D, IN_WIDTH))
        return g.reshape(N_SHARD * g.shape[1], g.shape[2])

    big = []
    for l in range(L):
        per = {"gu1": jnp.concatenate([a["ff1_w_gate"][l], a["ff1_w_up"][l]], axis=1), "wd1": a["ff1_w_down"][l],
               "win": a["w_in"][l], "wout": a["w_out"][l], "glu": a["s5_glu_w"][l],
               "gu2": jnp.concatenate([a["ff2_w_gate"][l], a["ff2_w_up"][l]], axis=1), "wd2": a["ff2_w_down"][l]}
        got = _gather_shards([_halves(per[k].astype(BF16)) for k in BIG_PIECES], "gather_weights")
        big.append({k: full(k, g) for k, g in zip(BIG_PIECES, got)})

    conv_local = a["dn_conv_w"].reshape(-1)
    conv_rows = -(-conv_local.shape[0] // (16 * FLAT_LANES)) * 16
    conv_pad = jnp.concatenate([conv_local, jnp.zeros((conv_rows * FLAT_LANES - conv_local.shape[0],), F32)])
    (conv_all,) = _gather_shards([conv_pad.reshape(2, conv_rows // 2, FLAT_LANES)], "gather_conv")
    conv_all = conv_all.reshape(N_SHARD, -1)[:, :conv_local.shape[0]].reshape(N_SHARD, L, DN_CONV, -1)
    conv_full = conv_all.transpose(1, 2, 0, 3).reshape(L, DN_CONV, 3 * DN_WIDTH)

    expand = jnp.repeat(jnp.eye(S5_GROUPS, dtype=F32), S5_GROUP_CH, axis=0)
    lanes = lambda v, n=128: jnp.broadcast_to(v[:, None], (v.shape[0], n))
    to_t = lambda m: m.transpose(0, 2, 1).reshape(S5_WIDTH, S5_STATE)

    def small_params(l):
        sp = {k: a[k][l][None] for k in ("ff1_norm_pre", "ff1_norm_post", "mix_norm_pre", "mix_norm_post",
                                         "ff2_norm_pre", "ff2_norm_post")}
        sp["sinks"] = lanes(a["attn_sinks"][l])
        sp["conv_w"] = conv_full[l]
        sp["dn_prm"] = jnp.concatenate([lanes(a["dn_a_log"][l]), lanes(a["dn_dt_bias"][l])], axis=0)
        sp["dn_norm_w"] = a["dn_norm_w"][l][None]
        sp["s5_pins"] = [a["s5_a_re"][l], a["s5_a_im"][l], lanes(a["s5_log_dt"][l], S5_STATE),
                         to_t(a["s5_b_re"][l]), to_t(a["s5_b_im"][l]), expand]
        sp["c_re"] = a["s5_c_re"][l].reshape(S5_WIDTH, S5_STATE)
        sp["c_im"] = a["s5_c_im"][l].reshape(S5_WIDTH, S5_STATE)
        sp["dskip"] = a["s5_d"][l][None]
        sp["glu_b"] = a["s5_glu_b"][l][None]
        return sp

    rope = _rope_tables(S)
    sps = [small_params(l) for l in range(L)]

    saved = []
    for l in range(L):
        sp, w = sps[l], big[l]
        x1, s1 = _ffn_fwd(x, sp["ff1_norm_pre"], sp["ff1_norm_post"], w["gu1"], w["wd1"], "ff1")
        x2, s2 = _mixer_fwd(x1, sp, w["win"], w["wout"], w["glu"], rope)
        x3, s3 = _ffn_fwd(x2, sp["ff2_norm_pre"], sp["ff2_norm_post"], w["gu2"], w["wd2"], "ff2")
        saved.append((x, s1, x1, s2, x2, s3))
        x = x3
    dx, loss_part = _loss_head(x, target, "head")

    big_grads, small_grads = [None] * L, [None] * L
    for l in reversed(range(L)):
        sp, w = sps[l], big[l]
        x0, s1, x1, s2, x2, s3 = saved[l]
        dx, g_pre2, g_post2, d_gu2, d_wd2 = _ffn_bwd(dx, x2, sp["ff2_norm_pre"], sp["ff2_norm_post"], w["gu2"], w["wd2"],
                                                     s3, "ff2")
        dx, bg, sg = _mixer_bwd(dx, x1, sp, w["win"], w["wout"], w["glu"], rope, s2)
        dx, g_pre1, g_post1, d_gu1, d_wd1 = _ffn_bwd(dx, x0, sp["ff1_norm_pre"], sp["ff1_norm_post"], w["gu1"], w["wd1"],
                                                     s1, "ff1")
        d_win = _unpermute_w_in(bg["win"]).reshape(D, N_SHARD, IN_WIDTH // N_SHARD).transpose(1, 0, 2)
        shard_major = lambda m: m.reshape(N_SHARD, m.shape[0] // N_SHARD, m.shape[1])
        bgl = {"gu1": d_gu1, "wd1": shard_major(d_wd1), "win": d_win, "wout": shard_major(bg["wout"]),
               "glu": shard_major(bg["glu"]), "gu2": d_gu2, "wd2": shard_major(d_wd2)}
        parts = [_halves(bgl[k]) for k in BIG_PIECES]
        got = _swap_other_half(parts, "reduce_siblings")
        chip_parts = [_add_sibling(p, q, "reduce_siblings_add") for p, q in zip(parts, got)]
        others = _scatter_to_chips(chip_parts, "reduce_chips")
        mine = _join_halves([_sum_chips(p, o, "reduce_chips_sum") for p, o in zip(chip_parts, others)], "reduce_join")
        big_grads[l] = {k: _whole(m) for k, m in zip(BIG_PIECES, mine)}
        sg.update({"ff1_norm_pre": g_pre1[0], "ff1_norm_post": g_post1[0], "ff2_norm_pre": g_pre2[0],
                   "ff2_norm_post": g_post2[0]})
        small_grads[l] = sg
    grad_x = dx[None]

    grads = {}
    layers = lambda k: jnp.stack([big_grads[l][k] for l in range(L)])
    for f, (gu, wd) in (("ff1", ("gu1", "wd1")), ("ff2", ("gu2", "wd2"))):
        gus = layers(gu)
        grads[f + "_w_gate"], grads[f + "_w_up"] = gus[:, :, :Fs], gus[:, :, Fs:]
        grads[f + "_w_down"] = layers(wd)
    grads["w_in"], grads["w_out"], grads["s5_glu_w"] = layers("win"), layers("wout"), layers("glu")

    small_local = [jnp.stack([small_grads[l][n] for l in range(L)]) for n in SMALL]
    vec = _pack_small(small_local, extra=(loss_part[0, :1],))
    total = _sum_slots(_gather_all_devices(vec, "gather_small"), "sum_small")
    small_total, rest = _unpack_small(total, [g.shape for g in small_local])
    loss = rest[0]
    for n, g in zip(SMALL, small_total):
        grads[n] = g
    cw = 3 * DN_WIDTH // N_SHARD
    grads["dn_conv_w"] = lax.dynamic_slice_in_dim(grads["dn_conv_w"], chip * cw, cw, axis=2)

    delta, new_m, new_v = {}, {}, {}
    for n in BIG:
        delta[n], new_m[n], new_v[n] = _adamw(a[n], grads[n], a["m_" + n], a["v_" + n], "adamw_" + n)
    shapes = [a[n].shape for n in SMALL]
    packed = [_pack_small([src[n] for n in SMALL]) for src in
              (a, grads, {n: a["m_" + n] for n in SMALL}, {n: a["v_" + n] for n in SMALL})]
    for dst, res in zip((delta, new_m, new_v), _adamw(*packed, "adamw_small")):
        for n, val in zip(SMALL, _unpack_small(res, shapes)[0]):
            dst[n] = val
    return (loss, grad_x, *[grads[n] for n in WEIGHTS], *[delta[n] for n in WEIGHTS], *[new_m[n] for n in WEIGHTS],
            *[new_v[n] for n in WEIGHTS])


def kernel(x, ff1_norm_pre, ff1_w_gate, ff1_w_up, ff1_w_down, ff1_norm_post, mix_norm_pre, w_in, attn_sinks, dn_conv_w, dn_a_log, dn_dt_bias, dn_norm_w, s5_a_re, s5_a_im, s5_log_dt, s5_b_re, s5_b_im, s5_c_re, s5_c_im, s5_d, s5_glu_w, s5_glu_b, w_out, mix_norm_post, ff2_norm_pre, ff2_w_gate, ff2_w_up, ff2_w_down, ff2_norm_post, loss_target, m_ff1_norm_pre, m_ff1_w_gate, m_ff1_w_up, m_ff1_w_down, m_ff1_norm_post, m_mix_norm_pre, m_w_in, m_attn_sinks, m_dn_conv_w, m_dn_a_log, m_dn_dt_bias, m_dn_norm_w, m_s5_a_re, m_s5_a_im, m_s5_log_dt, m_s5_b_re, m_s5_b_im, m_s5_c_re, m_s5_c_im, m_s5_d, m_s5_glu_w, m_s5_glu_b, m_w_out, m_mix_norm_post, m_ff2_norm_pre, m_ff2_w_gate, m_ff2_w_up, m_ff2_w_down, m_ff2_norm_post, v_ff1_norm_pre, v_ff1_w_gate, v_ff1_w_up, v_ff1_w_down, v_ff1_norm_post, v_mix_norm_pre, v_w_in, v_attn_sinks, v_dn_conv_w, v_dn_a_log, v_dn_dt_bias, v_dn_norm_w, v_s5_a_re, v_s5_a_im, v_s5_log_dt, v_s5_b_re, v_s5_b_im, v_s5_c_re, v_s5_c_im, v_s5_d, v_s5_glu_w, v_s5_glu_b, v_w_out, v_mix_norm_post, v_ff2_norm_pre, v_ff2_w_gate, v_ff2_w_up, v_ff2_w_down, v_ff2_norm_post):
    return _step(dict(locals()))
```

```python
import functools
import math

import numpy as np
import jax
import jax.numpy as jnp
from jax import lax
from jax.experimental import pallas as pl
from jax.experimental.pallas import tpu as pltpu

F32 = jnp.float32
BF16 = jnp.bfloat16
HI = lax.Precision.HIGHEST
MESH_ID = pl.DeviceIdType.MESH

NORM_EPS = 1e-6
FFN_RES_WEIGHT = 0.5
ATTN_HEADS, ATTN_KV_HEADS, HEAD_DIM, WINDOW = 8, 2, 128, 128
ROPE_THETA = 10000.0
DN_HEADS, DN_HEAD_DIM, DN_CONV, DN_CHUNK = 4, 128, 4, 64
S5_GROUPS, S5_GROUP_CH, S5_STATE = 32, 16, 64
ATTN_WIDTH = ATTN_HEADS * HEAD_DIM
ATTN_KV_WIDTH = ATTN_KV_HEADS * HEAD_DIM
DN_WIDTH = DN_HEADS * DN_HEAD_DIM
S5_WIDTH = S5_GROUPS * S5_GROUP_CH
S5_LANES = S5_GROUPS * S5_STATE
MIX_WIDTH = ATTN_WIDTH + DN_WIDTH + S5_WIDTH
IN_SPLITS = (ATTN_WIDTH, ATTN_KV_WIDTH, ATTN_KV_WIDTH, 3 * DN_WIDTH, DN_WIDTH, DN_HEADS, DN_HEADS, S5_WIDTH)
IN_WIDTH = sum(IN_SPLITS)
Z_AQ, Z_AK, Z_AV = 0, ATTN_WIDTH, ATTN_WIDTH + ATTN_KV_WIDTH
Z_DQKV = ATTN_WIDTH + 2 * ATTN_KV_WIDTH
Z_DZ = Z_DQKV + 3 * DN_WIDTH
Z_SU = Z_DZ + DN_WIDTH
Z_DBA = Z_SU + S5_WIDTH
Z_WIDTH = Z_DBA + 128

ADAM_LR, ADAM_B1, ADAM_B2, ADAM_EPS, ADAM_WD, ADAM_STEP = 0.001, 0.9, 0.999, 1e-08, 0.01, 10

N_SHARD = 4
FLAT_LANES = 512
VMEM_LIMIT = 56 * 1024 * 1024

WEIGHTS = ['ff1_norm_pre', 'ff1_w_gate', 'ff1_w_up', 'ff1_w_down', 'ff1_norm_post', 'mix_norm_pre', 'w_in',
           'attn_sinks', 'dn_conv_w', 'dn_a_log', 'dn_dt_bias', 'dn_norm_w', 's5_a_re', 's5_a_im', 's5_log_dt',
           's5_b_re', 's5_b_im', 's5_c_re', 's5_c_im', 's5_d', 's5_glu_w', 's5_glu_b', 'w_out', 'mix_norm_post',
           'ff2_norm_pre', 'ff2_w_gate', 'ff2_w_up', 'ff2_w_down', 'ff2_norm_post']
BIG = ['ff1_w_gate', 'ff1_w_up', 'ff1_w_down', 'w_in', 's5_glu_w', 'w_out', 'ff2_w_gate', 'ff2_w_up', 'ff2_w_down']
SMALL = [n for n in WEIGHTS if n not in BIG]


def _pick(dim, cands):
    for c in cands:
        if dim % c == 0:
            return c
    return dim


def _params(sem=None):
    return pltpu.CompilerParams(dimension_semantics=sem, vmem_limit_bytes=VMEM_LIMIT)


def _mm(a, b, *, ta=False, tb=False, out_dtype=F32, name, b_groups=None, bk_groups=None, out_groups=None, jobs=()):
    K, M = a.shape if ta else a.shape[::-1]
    Ng = Kg = None
    if b_groups:
        assert not tb
        _, Kb, Ng = b.shape
        N = b_groups * Ng
    elif bk_groups:
        assert tb
        _, N, Kg = b.shape
        Kb = bk_groups * Kg
    else:
        N, Kb = (b.shape if tb else b.shape[::-1])
    assert K == Kb, (a.shape, b.shape, ta, tb)
    tiles = (1408, 1024, 512, 384, 256, 128)
    tm = _pick(M, (1024, 512, 256, 128))
    tk = _pick(Kg if bk_groups else K, (2048,) + tiles)
    tn = _pick(N // out_groups if out_groups else (Ng if b_groups else N), tiles)
    nk = K // tk
    dims = (((0,) if ta else (1,), (1,) if tb else (0,)), ((), ()))

    grid = (M // tm, N // tn, nk)
    n_job_in = sum(len(j["ins"]) for j in jobs)
    n_sems = sum(_JOB_SEMS[j["kind"]] for j in jobs)

    def body(*refs):
        a_ref, b_ref = refs[:2]
        job_ins = refs[2:2 + n_job_in]
        o_ref = refs[2 + n_job_in]
        job_outs = refs[3 + n_job_in:3 + n_job_in + len(jobs)]
        scratch = refs[3 + n_job_in + len(jobs):]
        if jobs:
            send_sems, recv_sems = scratch[-2:]
            scratch = scratch[:-2]
            ids = [pl.program_id(d) for d in range(3)]
            first = functools.reduce(jnp.logical_and, [i == 0 for i in ids])
            last = functools.reduce(jnp.logical_and, [i == g - 1 for i, g in zip(ids, grid)])

            def copies():
                sends, recvs, at, sem = [], [], 0, 0
                for j, out in zip(jobs, job_outs):
                    s, r = _job_copies(j["kind"], job_ins[at:at + len(j["ins"])], out, send_sems, recv_sems, sem)
                    sends, recvs = sends + s, recvs + r
                    at, sem = at + len(j["ins"]), sem + _JOB_SEMS[j["kind"]]
                return sends, recvs

            @pl.when(first)
            def _():
                for cp in copies()[0]:
                    cp.start()

        part = lax.dot_general(a_ref[...].astype(BF16), b_ref[...].astype(BF16), dims, preferred_element_type=F32)
        if nk == 1:
            o_ref[...] = part.astype(o_ref.dtype)
        else:
            acc_ref, = scratch
            k = pl.program_id(2)

            @pl.when(k == 0)
            def _():
                acc_ref[...] = part

            @pl.when(k > 0)
            def _():
                acc_ref[...] += part

            @pl.when(k == nk - 1)
            def _():
                o_ref[...] = acc_ref[...].astype(o_ref.dtype)

        if jobs:
            @pl.when(last)
            def _():
                sends, recvs = copies()
                for cp in recvs:
                    cp.wait_recv()
                for cp in sends:
                    cp.wait_send()

    a_spec = pl.BlockSpec((tk, tm), lambda i, j, k: (k, i)) if ta else pl.BlockSpec((tm, tk), lambda i, j, k: (i, k))
    if b_groups:
        per = Ng // tn
        b_spec = pl.BlockSpec((None, tk, tn), lambda i, j, k: (j // per, k, j % per))
    elif bk_groups:
        per = Kg // tk
        b_spec = pl.BlockSpec((None, tn, tk), lambda i, j, k: (k // per, j, k % per))
    else:
        b_spec = pl.BlockSpec((tn, tk), lambda i, j, k: (j, k)) if tb else pl.BlockSpec((tk, tn), lambda i, j, k: (k, j))
    if out_groups:
        pero = (N // out_groups) // tn
        o_spec = pl.BlockSpec((None, tm, tn), lambda i, j, k: (j // pero, i, j % pero))
        o_shape = jax.ShapeDtypeStruct((out_groups, M, N // out_groups), out_dtype)
    else:
        o_spec = pl.BlockSpec((tm, tn), lambda i, j, k: (i, j))
        o_shape = jax.ShapeDtypeStruct((M, N), out_dtype)
    scratch = [pltpu.VMEM((tm, tn), F32)] if nk > 1 else []
    if not jobs:
        return pl.pallas_call(
            body, name=name, grid=grid, in_specs=[a_spec, b_spec], out_specs=o_spec, out_shape=o_shape,
            scratch_shapes=scratch, compiler_params=_params(("parallel", "parallel", "arbitrary")))(a, b)
    job_args = [x for j in jobs for x in j["ins"]]
    aliases, at = {}, 2
    for n, j in enumerate(jobs):
        if j["kind"] == "gather_d2d":
            aliases[at] = 1 + n
        at += len(j["ins"])
    res = pl.pallas_call(
        body, name=name, grid=grid, in_specs=[a_spec, b_spec] + [_ANY] * n_job_in,
        out_specs=[o_spec] + [_ANY] * len(jobs), out_shape=[o_shape] + [_job_out_shape(j) for j in jobs],
        input_output_aliases=aliases,
        scratch_shapes=scratch + [pltpu.SemaphoreType.DMA((n_sems,)), pltpu.SemaphoreType.DMA((n_sems,))],
        compiler_params=_params(("arbitrary", "arbitrary", "arbitrary")))(a, b, *job_args)
    return res[0], list(res[1:])


def _row_spec(tm, width, off):
    return pl.BlockSpec((tm, width), lambda i, j: (i, off + j))


def _const_spec(shape):
    return pl.BlockSpec(shape, lambda i, j: (0,) * len(shape))


def _rows_fwd(fn, rows, consts, outs, *, name, tm=256, ncol=1):
    S = rows[0][0].shape[0]
    tm = _pick(S, (tm, 128, 64))
    nr, nc = len(rows), len(consts)

    def body(*refs):
        vals = [r[...].astype(F32) for r in refs[:nr + nc]]
        res = fn(*vals)
        for o_ref, o in zip(refs[nr + nc:], res):
            o_ref[...] = o.astype(o_ref.dtype)

    return pl.pallas_call(
        body, name=name, grid=(S // tm, ncol),
        in_specs=[_row_spec(tm, w, off) for _, w, off in rows] + [_const_spec(c.shape) for c in consts],
        out_specs=[_row_spec(tm, bw, 0) for _, bw, _ in outs],
        out_shape=[jax.ShapeDtypeStruct((S, tw), dt) for tw, _, dt in outs],
        compiler_params=_params(("parallel", "parallel")))(*[r[0] for r in rows], *consts)


def _rows_bwd(fn, rows, consts, cts, row_grads, *, name, tm=256, ncol=1, add_to_first=None, lane_sum_consts=()):
    S = rows[0][0].shape[0]
    tm = _pick(S, (tm, 128, 64))
    nr, nc, nt = len(rows), len(consts), len(cts)
    n_in = nr + nc + nt + (1 if add_to_first is not None else 0)

    def body(*refs):
        vals = [r[...].astype(F32) for r in refs[:nr + nc]]
        ct = tuple(r[...].astype(F32) for r in refs[nr + nc:nr + nc + nt])
        _, vjp = jax.vjp(fn, *vals)
        grads = vjp(ct)
        outs = refs[n_in:]
        for n, (idx, _, _) in enumerate(row_grads):
            g = grads[idx]
            if n == 0 and add_to_first is not None:
                g = g + refs[n_in - 1][...].astype(F32)
            outs[n][...] = g.astype(outs[n].dtype)
        first = jnp.logical_and(pl.program_id(0) == 0, pl.program_id(1) == 0)
        for c in range(nc):
            o_ref = outs[len(row_grads) + c]
            g = grads[nr + c]
            if c in lane_sum_consts:
                g = jnp.broadcast_to(jnp.sum(g, axis=-1, keepdims=True), g.shape)

            @pl.when(first)
            def _():
                o_ref[...] = jnp.zeros_like(o_ref)

            o_ref[...] += g

    in_specs = ([_row_spec(tm, w, off) for _, w, off in rows] + [_const_spec(c.shape) for c in consts]
                + [_row_spec(tm, w, off) for _, w, off in cts])
    args = [r[0] for r in rows] + list(consts) + [c[0] for c in cts]
    if add_to_first is not None:
        in_specs.append(_row_spec(tm, rows[row_grads[0][0]][1], 0))
        args.append(add_to_first)
    out_specs = [_row_spec(tm, rows[idx][1], 0) for idx, _, _ in row_grads] + [_const_spec(c.shape) for c in consts]
    out_shape = ([jax.ShapeDtypeStruct((S, tw), dt) for _, tw, dt in row_grads]
                 + [jax.ShapeDtypeStruct(c.shape, F32) for c in consts])
    return pl.pallas_call(
        body, name=name, grid=(S // tm, ncol), in_specs=in_specs, out_specs=out_specs, out_shape=out_shape,
        compiler_params=_params(("arbitrary", "arbitrary")))(*args)


def _rms(x, gain):
    return x * lax.rsqrt(jnp.mean(x * x, axis=-1, keepdims=True) + NORM_EPS) * gain


def _prenorm_fn(x, gain):
    return (_rms(x, gain),)


def _postnorm_fn(weight):
    def fn(y, gain):
        return (weight * _rms(y, gain),)
    return fn


def _residual_fn(weight):
    def fn(x, y, gain):
        return (x + weight * _rms(y, gain),)
    return fn


def _swiglu_fn(blk):
    tf = blk.shape[1] // 2
    gate, up = blk[:, :tf], blk[:, tf:]
    return (gate * jax.nn.sigmoid(gate) * up,)


def _ffn_fwd(x, g_pre, g_post, weights, tag, mm):
    D = x.shape[1]
    (h,) = _rows_fwd(_prenorm_fn, [(x, D, 0)], [g_pre], [(D, D, BF16)], name=f"{tag}_prenorm")
    wgu = weights("gu")
    Fs = wgu.shape[2] // 2
    gu = mm("gu", h, wgu, b_groups=N_SHARD, out_dtype=BF16, name=f"{tag}_gate_up")
    (act,) = _rows_fwd(_swiglu_fn, [(gu, 2 * Fs, 0)], [], [(N_SHARD * Fs, Fs, BF16)], name=f"{tag}_swiglu",
                       ncol=N_SHARD)
    wd = weights("wd")
    y = mm("wd", act, wd, out_dtype=F32, name=f"{tag}_down")
    (x_new,) = _rows_fwd(_residual_fn(FFN_RES_WEIGHT), [(x, D, 0), (y, D, 0)], [g_post], [(D, D, F32)],
                         name=f"{tag}_residual")
    return x_new, (h, gu, act, y, wgu, wd)


def _ffn_bwd(dx_new, x, g_pre, g_post, saved, tag, mm, sink):
    h, gu, act, y, wgu, wd = saved
    D = x.shape[1]
    Fs = wgu.shape[2] // 2
    d_y, d_g_post = _rows_bwd(_postnorm_fn(FFN_RES_WEIGHT), [(y, D, 0)], [g_post], [(dx_new, D, 0)],
                              [(0, D, BF16)], name=f"{tag}_postnorm_bwd")
    d_act = mm(d_y, wd, tb=True, out_dtype=BF16, name=f"{tag}_down_dx")
    sink("wd", mm(act, d_y, ta=True, out_dtype=BF16, name=f"{tag}_down_dw"))
    (d_gu,) = _rows_bwd(_swiglu_fn, [(gu, 2 * Fs, 0)], [], [(d_act, Fs, 0)], [(0, 2 * N_SHARD * Fs, BF16)],
                        name=f"{tag}_swiglu_bwd", ncol=N_SHARD)
    sink("gu", mm(h, d_gu, ta=True, out_dtype=BF16, out_groups=N_SHARD, name=f"{tag}_gate_up_dw"))
    d_h = mm(d_gu, wgu, tb=True, bk_groups=N_SHARD, out_dtype=BF16, name=f"{tag}_gate_up_dx")
    dx, d_g_pre = _rows_bwd(_prenorm_fn, [(x, D, 0)], [g_pre], [(d_h, D, 0)], [(0, D, F32)],
                            name=f"{tag}_prenorm_bwd", add_to_first=dx_new)
    return dx, d_g_pre, d_g_post


@jax.custom_vjp
def _swap_halves(x):
    return pltpu.roll(x, HEAD_DIM // 2, 1)


def _swap_fwd(x):
    return _swap_halves(x), None


def _swap_bwd(_, g):
    return (_swap_halves(g),)


_swap_halves.defvjp(_swap_fwd, _swap_bwd)


def _rope(x, cc, ss):
    return x * cc + _swap_halves(x) * ss


def _attn_block_fn(has_prev):
    grp = ATTN_HEADS // ATTN_KV_HEADS
    scale = HEAD_DIM ** -0.5
    nt = (((1,), (1,)), ((), ()))
    nn = (((1,), (0,)), ((), ()))

    def fn(*a):
        q = a[:8]
        kp, kc, vp, vc = a[8:10], a[10:12], a[12:14], a[14:16]
        cc, ss, ccp, ssp, sinks = a[16:21]
        row = lax.broadcasted_iota(jnp.int32, (WINDOW, WINDOW), 0)
        col = lax.broadcasted_iota(jnp.int32, (WINDOW, WINDOW), 1)
        m_cur = col <= row
        m_prev = jnp.logical_and(col > row, has_prev)
        outs = []
        for h in range(ATTN_HEADS):
            g = h // grp
            qr = _rope(q[h], cc, ss).astype(BF16)
            kcr = _rope(kc[g], cc, ss).astype(BF16)
            kpr = _rope(kp[g], ccp, ssp).astype(BF16)
            s_c = jnp.where(m_cur, lax.dot_general(qr, kcr, nt, preferred_element_type=F32) * scale, -jnp.inf)
            s_p = jnp.where(m_prev, lax.dot_general(qr, kpr, nt, preferred_element_type=F32) * scale, -jnp.inf)
            sink = sinks[h:h + 1, :]
            m = jnp.maximum(jnp.maximum(jnp.max(s_c, axis=-1, keepdims=True), jnp.max(s_p, axis=-1, keepdims=True)),
                            sink)
            p_c = jnp.exp(s_c - m)
            p_p = jnp.exp(s_p - m)
            den = (jnp.sum(p_c, axis=-1, keepdims=True) + jnp.sum(p_p, axis=-1, keepdims=True) + jnp.exp(sink - m))
            inv = 1.0 / den
            o = (lax.dot_general((p_c * inv).astype(BF16), vc[g].astype(BF16), nn, preferred_element_type=F32)
                 + lax.dot_general((p_p * inv).astype(BF16), vp[g].astype(BF16), nn, preferred_element_type=F32))
            outs.append(o)
        return tuple(outs)

    return fn


def _attn_specs(order):
    kvb = ATTN_WIDTH // (2 * ATTN_KV_WIDTH)
    return [
        pl.BlockSpec((WINDOW, ATTN_WIDTH), lambda i: (order(i), 0)),
        pl.BlockSpec((WINDOW, 2 * ATTN_KV_WIDTH), lambda i: (jnp.maximum(order(i) - 1, 0), kvb)),
        pl.BlockSpec((WINDOW, 2 * ATTN_KV_WIDTH), lambda i: (order(i), kvb)),
        pl.BlockSpec((WINDOW, HEAD_DIM), lambda i: (order(i), 0)),
        pl.BlockSpec((WINDOW, HEAD_DIM), lambda i: (order(i), 0)),
        pl.BlockSpec((WINDOW, HEAD_DIM), lambda i: (jnp.maximum(order(i) - 1, 0), 0)),
        pl.BlockSpec((WINDOW, HEAD_DIM), lambda i: (jnp.maximum(order(i) - 1, 0), 0)),
        pl.BlockSpec((ATTN_HEADS, HEAD_DIM), lambda i: (0, 0)),
    ]


def _attn_args(q_ref, kvp_ref, kvc_ref, cc, ss, ccp, ssp, sinks):
    d = HEAD_DIM
    q = [q_ref[:, h * d:(h + 1) * d].astype(F32) for h in range(ATTN_HEADS)]
    kp = [kvp_ref[:, g * d:(g + 1) * d].astype(F32) for g in range(ATTN_KV_HEADS)]
    vp = [kvp_ref[:, ATTN_KV_WIDTH + g * d:ATTN_KV_WIDTH + (g + 1) * d].astype(F32) for g in range(ATTN_KV_HEADS)]
    kc = [kvc_ref[:, g * d:(g + 1) * d].astype(F32) for g in range(ATTN_KV_HEADS)]
    vc = [kvc_ref[:, ATTN_KV_WIDTH + g * d:ATTN_KV_WIDTH + (g + 1) * d].astype(F32) for g in range(ATTN_KV_HEADS)]
    return q + kp + kc + vp + vc + [cc[...], ss[...], ccp[...], ssp[...], sinks[...]]


def _attn_fwd(z, cc, ss, sinks_b, tag):
    S = z.shape[0]
    nb = S // WINDOW

    def body(q_ref, kvp_ref, kvc_ref, cc_r, ss_r, ccp_r, ssp_r, sink_r, o_ref):
        n = pl.program_id(0)
        outs = _attn_block_fn(n > 0)(*_attn_args(q_ref, kvp_ref, kvc_ref, cc_r, ss_r, ccp_r, ssp_r, sink_r))
        for h in range(ATTN_HEADS):
            o_ref[:, h * HEAD_DIM:(h + 1) * HEAD_DIM] = outs[h].astype(o_ref.dtype)

    return pl.pallas_call(
        body, name=f"{tag}_attn", grid=(nb,), in_specs=_attn_specs(lambda i: i),
        out_specs=pl.BlockSpec((WINDOW, ATTN_WIDTH), lambda i: (i, 0)),
        out_shape=jax.ShapeDtypeStruct((S, ATTN_WIDTH), BF16),
        compiler_params=_params(("parallel",)))(z, z, z, cc, ss, cc, ss, sinks_b)


def _attn_bwd(z, cc, ss, sinks_b, d_out, tag):
    S = z.shape[0]
    nb = S // WINDOW
    d = HEAD_DIM
    rev = lambda i: nb - 1 - i

    def body(q_ref, kvp_ref, kvc_ref, cc_r, ss_r, ccp_r, ssp_r, sink_r, do_ref, dq_ref, dkv_ref, dsink_ref, carry):
        i = pl.program_id(0)
        n = nb - 1 - i

        @pl.when(i == 0)
        def _():
            carry[...] = jnp.zeros_like(carry)
            dsink_ref[...] = jnp.zeros_like(dsink_ref)

        args = _attn_args(q_ref, kvp_ref, kvc_ref, cc_r, ss_r, ccp_r, ssp_r, sink_r)
        _, vjp = jax.vjp(_attn_block_fn(n > 0), *args)
        g = vjp(tuple(do_ref[:, h * d:(h + 1) * d].astype(F32) for h in range(ATTN_HEADS)))
        for h in range(ATTN_HEADS):
            dq_ref[:, h * d:(h + 1) * d] = g[h].astype(dq_ref.dtype)
        for gi in range(ATTN_KV_HEADS):
            ks = slice(gi * d, (gi + 1) * d)
            vs = slice(ATTN_KV_WIDTH + gi * d, ATTN_KV_WIDTH + (gi + 1) * d)
            dkv_ref[:, ks] = (g[10 + gi] + carry[:, ks]).astype(dkv_ref.dtype)
            dkv_ref[:, vs] = (g[14 + gi] + carry[:, vs]).astype(dkv_ref.dtype)
            carry[:, ks] = g[8 + gi]
            carry[:, vs] = g[12 + gi]
        ds = g[20]
        dsink_ref[...] += jnp.broadcast_to(jnp.sum(ds, axis=-1, keepdims=True), ds.shape)

    return pl.pallas_call(
        body, name=f"{tag}_attn_bwd", grid=(nb,),
        in_specs=_attn_specs(rev) + [pl.BlockSpec((WINDOW, ATTN_WIDTH), lambda i: (rev(i), 0))],
        out_specs=[pl.BlockSpec((WINDOW, ATTN_WIDTH), lambda i: (rev(i), 0)),
                   pl.BlockSpec((WINDOW, 2 * ATTN_KV_WIDTH), lambda i: (rev(i), 0)),
                   pl.BlockSpec((ATTN_HEADS, HEAD_DIM), lambda i: (0, 0))],
        out_shape=[jax.ShapeDtypeStruct((S, ATTN_WIDTH), BF16), jax.ShapeDtypeStruct((S, 2 * ATTN_KV_WIDTH), BF16),
                   jax.ShapeDtypeStruct((ATTN_HEADS, HEAD_DIM), F32)],
        scratch_shapes=[pltpu.VMEM((WINDOW, 2 * ATTN_KV_WIDTH), F32)],
        compiler_params=_params(("arbitrary",)))(z, z, z, cc, ss, cc, ss, sinks_b, d_out)


def _rope_tables(seq):
    half = HEAD_DIM // 2
    inv_freq = ROPE_THETA ** (-jnp.arange(half, dtype=F32) / half)
    ang = jnp.arange(seq, dtype=F32)[:, None] * inv_freq[None, :]
    cos, sin = jnp.cos(ang), jnp.sin(ang)
    return jnp.concatenate([cos, cos], axis=1), jnp.concatenate([-sin, sin], axis=1)


CONV_COLS = 128


def _conv_pre(u, w_ref, S):
    row = lax.broadcasted_iota(jnp.int32, u.shape, 0)
    shifted = [u] + [jnp.where(row >= s, pltpu.roll(u, s, 0), 0.0) for s in range(1, DN_CONV)]
    y = shifted[0] * w_ref[DN_CONV - 1:DN_CONV, :]
    for s in range(1, DN_CONV):
        y = y + shifted[s] * w_ref[DN_CONV - 1 - s:DN_CONV - s, :]
    return y, shifted, row


def _conv_fwd(z, conv_w, tag):
    S = z.shape[0]
    ncol = 3 * DN_WIDTH // CONV_COLS

    def body(u_ref, w_ref, o_ref):
        y, _, _ = _conv_pre(u_ref[...], w_ref, S)
        o_ref[...] = y * jax.nn.sigmoid(y)

    return pl.pallas_call(
        body, name=f"{tag}_conv", grid=(ncol,),
        in_specs=[pl.BlockSpec((S, CONV_COLS), lambda j: (0, Z_DQKV // CONV_COLS + j)),
                  pl.BlockSpec((DN_CONV, CONV_COLS), lambda j: (0, j))],
        out_specs=pl.BlockSpec((S, CONV_COLS), lambda j: (0, j)),
        out_shape=jax.ShapeDtypeStruct((S, 3 * DN_WIDTH), F32),
        compiler_params=_params(("parallel",)))(z, conv_w)


def _conv_bwd(z, conv_w, d_out, tag):
    S = z.shape[0]
    ncol = 3 * DN_WIDTH // CONV_COLS

    def body(u_ref, w_ref, do_ref, du_ref, dw_ref):
        y, shifted, row = _conv_pre(u_ref[...], w_ref, S)
        sg = jax.nn.sigmoid(y)
        d_y = do_ref[...] * (sg * (1.0 + y * (1.0 - sg)))
        d_u = d_y * w_ref[DN_CONV - 1:DN_CONV, :]
        dw_ref[DN_CONV - 1:DN_CONV, :] = jnp.sum(d_y * shifted[0], axis=0, keepdims=True)
        for s in range(1, DN_CONV):
            back = jnp.where(row < S - s, pltpu.roll(d_y, S - s, 0), 0.0)
            d_u = d_u + back * w_ref[DN_CONV - 1 - s:DN_CONV - s, :]
            dw_ref[DN_CONV - 1 - s:DN_CONV - s, :] = jnp.sum(d_y * shifted[s], axis=0, keepdims=True)
        du_ref[...] = d_u.astype(du_ref.dtype)

    return pl.pallas_call(
        body, name=f"{tag}_conv_bwd", grid=(ncol,),
        in_specs=[pl.BlockSpec((S, CONV_COLS), lambda j: (0, Z_DQKV // CONV_COLS + j)),
                  pl.BlockSpec((DN_CONV, CONV_COLS), lambda j: (0, j)),
                  pl.BlockSpec((S, CONV_COLS), lambda j: (0, j))],
        out_specs=[pl.BlockSpec((S, CONV_COLS), lambda j: (0, j)), pl.BlockSpec((DN_CONV, CONV_COLS), lambda j: (0, j))],
        out_shape=[jax.ShapeDtypeStruct((S, 3 * DN_WIDTH), BF16), jax.ShapeDtypeStruct((DN_CONV, 3 * DN_WIDTH), F32)],
        compiler_params=_params(("parallel",)))(z, conv_w, d_out)


_NN = (((1,), (0,)), ((), ()))
_NT = (((1,), (1,)), ((), ()))
_TN = (((0,), (0,)), ((), ()))


def _dot3(a, b, dims):
    a_hi, b_hi = a.astype(BF16), b.astype(BF16)
    a_lo, b_lo = (a - a_hi.astype(F32)).astype(BF16), (b - b_hi.astype(F32)).astype(BF16)
    mm = lambda p, q: lax.dot_general(p, q, dims, preferred_element_type=F32)
    return mm(a_hi, b_hi) + (mm(a_hi, b_lo) + mm(a_lo, b_hi))


@functools.partial(jax.custom_vjp, nondiff_argnums=(2,))
def _dot_vjp(a, b, dims):
    return _dot3(a, b, dims)


def _dot_vjp_fwd(a, b, dims):
    return _dot3(a, b, dims), (a, b)


def _dot_vjp_bwd(dims, res, g):
    a, b = res
    if dims == _NN:
        return _dot3(g, b, _NT), _dot3(a, g, _TN)
    if dims == _NT:
        return _dot3(g, b, _NN), _dot3(g, a, _TN)
    return _dot3(b, g, _NT), _dot3(a, g, _NN)


_dot_vjp.defvjp(_dot_vjp_fwd, _dot_vjp_bwd)


def _dot(a, b, dims=_NN):
    return _dot_vjp(a, b, dims)


def _dn_head_fn(h):
    C, dk = DN_CHUNK, DN_HEAD_DIM

    def fn(state, q, k, v, zg, ba, prm, norm_w):
        lane = lax.broadcasted_iota(jnp.int32, (C, dk), 1)
        rowl = lax.broadcasted_iota(jnp.int32, (C, dk), 0)
        row = lax.broadcasted_iota(jnp.int32, (C, C), 0)
        col = lax.broadcasted_iota(jnp.int32, (C, C), 1)
        bcol = jnp.sum(jnp.where(lane == h, ba, 0.0), axis=-1, keepdims=True)
        acol = jnp.sum(jnp.where(lane == DN_HEADS + h, ba, 0.0), axis=-1, keepdims=True)
        a_log, dtb = prm[h:h + 1, :], prm[DN_HEADS + h:DN_HEADS + h + 1, :]
        qn = q * lax.rsqrt(jnp.sum(q * q, axis=-1, keepdims=True) + NORM_EPS) * (dk ** -0.5)
        kn = k * lax.rsqrt(jnp.sum(k * k, axis=-1, keepdims=True) + NORM_EPS)
        beta = jax.nn.sigmoid(bcol)
        sp_in = acol + dtb
        softplus = jnp.maximum(sp_in, 0.0) + jnp.log(1.0 + jnp.exp(-jnp.abs(sp_in)))
        gt = -jnp.exp(a_log) * softplus
        gc = _dot((row >= col).astype(F32), gt)
        gcol = jnp.mean(gc, axis=-1, keepdims=True)
        grow = _dot(jnp.full((C, dk), 1.0 / dk, F32), gc, _NT)
        decay = jnp.exp(jnp.where(row >= col, gcol - grow, -jnp.inf))
        kb = kn * beta
        m = -jnp.where(row > col, _dot(kb, kn, _NT) * decay, 0.0)
        u = v * beta
        w = kb * jnp.exp(gc)
        for step in range(6):
            u = u + _dot(m, u)
            w = w + _dot(m, w)
            if step < 5:
                m = _dot(m, m)
        attn = jnp.where(row >= col, _dot(qn, kn, _NT) * decay, 0.0)
        q_dec = qn * jnp.exp(gc)
        gl = jnp.sum(jnp.where(rowl == C - 1, gc, 0.0), axis=0, keepdims=True)
        k_dec = kn * jnp.exp(gl - gc)
        v_new = u - _dot(w, state)
        o = _dot(q_dec, state) + _dot(attn, v_new)
        state_new = state * jnp.exp(gl) + _dot(k_dec, v_new, _TN)
        y = o * lax.rsqrt(jnp.mean(o * o, axis=-1, keepdims=True) + NORM_EPS) * norm_w
        y = y * (zg * jax.nn.sigmoid(zg))
        return state_new, y

    return fn


def _dn_specs(order):
    C = DN_CHUNK
    return [pl.BlockSpec((C, 3 * DN_WIDTH), lambda i: (order(i), 0)),
            pl.BlockSpec((C, DN_WIDTH), lambda i: (order(i), Z_DZ // DN_WIDTH)),
            pl.BlockSpec((C, 128), lambda i: (order(i), Z_DBA // 128)),
            pl.BlockSpec((8, 128), lambda i: (0, 0)),
            pl.BlockSpec((1, DN_HEAD_DIM), lambda i: (0, 0))]


def _dn_head_args(h, qkv_ref, zg_ref, ba_ref, prm_ref, nw_ref):
    d = DN_HEAD_DIM
    sl = lambda base: slice(base + h * d, base + (h + 1) * d)
    return [qkv_ref[:, sl(0)], qkv_ref[:, sl(DN_WIDTH)], qkv_ref[:, sl(2 * DN_WIDTH)], zg_ref[:, sl(0)].astype(F32),
            ba_ref[...].astype(F32), prm_ref[...], nw_ref[...]]


def _dn_fwd(qkv, z, prm, norm_w, tag):
    S = qkv.shape[0]
    nchunk = S // DN_CHUNK
    d = DN_HEAD_DIM

    def body(qkv_ref, zg_ref, ba_ref, prm_ref, nw_ref, y_ref, st_ref, state):
        @pl.when(pl.program_id(0) == 0)
        def _():
            state[...] = jnp.zeros_like(state)

        for h in range(DN_HEADS):
            st_ref[h] = state[h]
            new, y = _dn_head_fn(h)(state[h], *_dn_head_args(h, qkv_ref, zg_ref, ba_ref, prm_ref, nw_ref))
            state[h] = new
            y_ref[:, h * d:(h + 1) * d] = y.astype(y_ref.dtype)

    return pl.pallas_call(
        body, name=f"{tag}_deltanet", grid=(nchunk,), in_specs=_dn_specs(lambda i: i),
        out_specs=[pl.BlockSpec((DN_CHUNK, DN_WIDTH), lambda i: (i, 0)),
                   pl.BlockSpec((None, DN_HEADS, d, d), lambda i: (i, 0, 0, 0))],
        out_shape=[jax.ShapeDtypeStruct((S, DN_WIDTH), BF16), jax.ShapeDtypeStruct((nchunk, DN_HEADS, d, d), F32)],
        scratch_shapes=[pltpu.VMEM((DN_HEADS, d, d), F32)],
        compiler_params=_params(("arbitrary",)))(qkv, z, z, prm, norm_w)


def _dn_bwd(qkv, z, prm, norm_w, states, d_y, tag):
    S = qkv.shape[0]
    nchunk = S // DN_CHUNK
    d = DN_HEAD_DIM
    rev = lambda i: nchunk - 1 - i

    def body(qkv_ref, zg_ref, ba_ref, prm_ref, nw_ref, st_ref, dy_ref,
             dqkv_ref, dzg_ref, dba_ref, dprm_ref, dnw_ref, d_state):
        @pl.when(pl.program_id(0) == 0)
        def _():
            d_state[...] = jnp.zeros_like(d_state)
            dprm_ref[...] = jnp.zeros_like(dprm_ref)
            dnw_ref[...] = jnp.zeros_like(dnw_ref)

        d_ba = jnp.zeros((DN_CHUNK, 128), F32)
        d_prm = jnp.zeros((8, 128), F32)
        d_nw = jnp.zeros((1, d), F32)
        for h in range(DN_HEADS):
            args = [st_ref[h]] + _dn_head_args(h, qkv_ref, zg_ref, ba_ref, prm_ref, nw_ref)
            _, vjp = jax.vjp(_dn_head_fn(h), *args)
            g = vjp((d_state[h], dy_ref[:, h * d:(h + 1) * d].astype(F32)))
            d_state[h] = g[0]
            for n, base in enumerate((0, DN_WIDTH, 2 * DN_WIDTH)):
                dqkv_ref[:, base + h * d:base + (h + 1) * d] = g[1 + n]
            dzg_ref[:, h * d:(h + 1) * d] = g[4].astype(dzg_ref.dtype)
            d_ba = d_ba + g[5]
            d_prm = d_prm + g[6]
            d_nw = d_nw + g[7]
        dba_ref[...] = d_ba.astype(dba_ref.dtype)
        dprm_ref[...] += jnp.broadcast_to(jnp.sum(d_prm, axis=-1, keepdims=True), d_prm.shape)
        dnw_ref[...] += d_nw

    return pl.pallas_call(
        body, name=f"{tag}_deltanet_bwd", grid=(nchunk,),
        in_specs=_dn_specs(rev) + [pl.BlockSpec((None, DN_HEADS, d, d), lambda i: (rev(i), 0, 0, 0)),
                                   pl.BlockSpec((DN_CHUNK, DN_WIDTH), lambda i: (rev(i), 0))],
        out_specs=[pl.BlockSpec((DN_CHUNK, 3 * DN_WIDTH), lambda i: (rev(i), 0)),
                   pl.BlockSpec((DN_CHUNK, DN_WIDTH), lambda i: (rev(i), 0)),
                   pl.BlockSpec((DN_CHUNK, 128), lambda i: (rev(i), 0)),
                   pl.BlockSpec((8, 128), lambda i: (0, 0)),
                   pl.BlockSpec((1, d), lambda i: (0, 0))],
        out_shape=[jax.ShapeDtypeStruct((S, 3 * DN_WIDTH), F32), jax.ShapeDtypeStruct((S, DN_WIDTH), BF16),
                   jax.ShapeDtypeStruct((S, 128), BF16), jax.ShapeDtypeStruct((8, 128), F32),
                   jax.ShapeDtypeStruct((1, d), F32)],
        scratch_shapes=[pltpu.VMEM((DN_HEADS, d, d), F32)],
        compiler_params=_params(("arbitrary",)))(qkv, z, z, prm, norm_w, states, d_y)


def _whole_fwd(fn, ins, outs, *, name):
    n = len(ins)

    def body(*refs):
        res = fn(*[r[...] for r in refs[:n]])
        for o_ref, o in zip(refs[n:], res):
            o_ref[...] = o

    return pl.pallas_call(body, name=name, out_shape=[jax.ShapeDtypeStruct(s, F32) for s in outs],
                          compiler_params=_params())(*ins)


def _whole_bwd(fn, ins, cts, n_grads, *, name, lane_sum=()):
    n, nt = len(ins), len(cts)

    def body(*refs):
        _, vjp = jax.vjp(fn, *[r[...] for r in refs[:n]])
        grads = vjp(tuple(r[...] for r in refs[n:n + nt]))
        for k in range(n_grads):
            g = grads[k]
            if k in lane_sum:
                g = jnp.broadcast_to(jnp.sum(g, axis=-1, keepdims=True), g.shape)
            refs[n + nt + k][...] = g

    return pl.pallas_call(body, name=name, out_shape=[jax.ShapeDtypeStruct(a.shape, F32) for a in ins[:n_grads]],
                          compiler_params=_params())(*ins, *cts)


S5_CHUNK = 256


def _s5_param_fn(a_re, a_im, ldt, bt_re, bt_im, expand):
    dt = jnp.exp(ldt)
    er = jnp.exp(a_re * dt)
    ab_re, ab_im = er * jnp.cos(a_im * dt), er * jnp.sin(a_im * dt)
    den = a_re * a_re + a_im * a_im
    co_re = ((ab_re - 1.0) * a_re + ab_im * a_im) / den
    co_im = (ab_im * a_re - (ab_re - 1.0) * a_im) / den
    cr, ci = _dot(expand, co_re), _dot(expand, co_im)
    return ab_re, ab_im, cr * bt_re - ci * bt_im, cr * bt_im + ci * bt_re


def _s5_scan(b_re, b_im, a_re, a_im, row, T, reverse):
    x_re, x_im, p_re, p_im = b_re, b_im, a_re, a_im
    d = 1
    while d < T:
        if reverse:
            s_re = jnp.where(row < T - d, pltpu.roll(x_re, T - d, 0), 0.0)
            s_im = jnp.where(row < T - d, pltpu.roll(x_im, T - d, 0), 0.0)
        else:
            s_re = jnp.where(row >= d, pltpu.roll(x_re, d, 0), 0.0)
            s_im = jnp.where(row >= d, pltpu.roll(x_im, d, 0), 0.0)
        x_re, x_im = x_re + p_re * s_re - p_im * s_im, x_im + p_re * s_im + p_im * s_re
        p_re, p_im = p_re * p_re - p_im * p_im, 2.0 * p_re * p_im
        d *= 2
    return x_re, x_im


def _s5_states(u, bre_ref, bim_ref, a_re, a_im, c_re, c_im, row, T):
    bu_re = _dot(u, bre_ref[...]) + jnp.where(row == 0, a_re * c_re - a_im * c_im, 0.0)
    bu_im = _dot(u, bim_ref[...]) + jnp.where(row == 0, a_re * c_im + a_im * c_re, 0.0)
    return _s5_scan(bu_re, bu_im, a_re, a_im, row, T, False)


def _s5_in_specs(order, T):
    full = lambda shape: pl.BlockSpec(shape, lambda i: (0,) * len(shape))
    return [pl.BlockSpec((T, S5_WIDTH), lambda i: (order(i), Z_SU // S5_WIDTH)),
            full((S5_WIDTH, S5_LANES)), full((S5_WIDTH, S5_LANES)), full((S5_WIDTH, S5_LANES)),
            full((S5_WIDTH, S5_LANES)), full((1, S5_LANES)), full((1, S5_LANES)), full((1, S5_WIDTH))]


def _s5_fwd(z, bre, bim, cre, cim, ab_re, ab_im, dskip, tag):
    S = z.shape[0]
    T = _pick(S, (S5_CHUNK, 128))
    nch = S // T

    def body(u_ref, bre_ref, bim_ref, cre_ref, cim_ref, are_ref, aim_ref, d_ref, y_ref, kre_ref, kim_ref, c_re, c_im):
        @pl.when(pl.program_id(0) == 0)
        def _():
            c_re[...] = jnp.zeros_like(c_re)
            c_im[...] = jnp.zeros_like(c_im)

        kre_ref[...] = c_re[...]
        kim_ref[...] = c_im[...]
        u = u_ref[...]
        row = lax.broadcasted_iota(jnp.int32, (T, S5_LANES), 0)
        x_re, x_im = _s5_states(u, bre_ref, bim_ref, are_ref[...], aim_ref[...], c_re[...], c_im[...], row, T)
        c_re[...] = jnp.sum(jnp.where(row == T - 1, x_re, 0.0), axis=0, keepdims=True)
        c_im[...] = jnp.sum(jnp.where(row == T - 1, x_im, 0.0), axis=0, keepdims=True)
        y_ref[...] = _dot(x_re, cre_ref[...], _NT) - _dot(x_im, cim_ref[...], _NT) + d_ref[...] * u

    return pl.pallas_call(
        body, name=f"{tag}_s5", grid=(nch,), in_specs=_s5_in_specs(lambda i: i, T),
        out_specs=[pl.BlockSpec((T, S5_WIDTH), lambda i: (i, 0)),
                   pl.BlockSpec((None, 1, S5_LANES), lambda i: (i, 0, 0)),
                   pl.BlockSpec((None, 1, S5_LANES), lambda i: (i, 0, 0))],
        out_shape=[jax.ShapeDtypeStruct((S, S5_WIDTH), F32), jax.ShapeDtypeStruct((nch, 1, S5_LANES), F32),
                   jax.ShapeDtypeStruct((nch, 1, S5_LANES), F32)],
        scratch_shapes=[pltpu.VMEM((1, S5_LANES), F32), pltpu.VMEM((1, S5_LANES), F32)],
        compiler_params=_params(("arbitrary",)))(z, bre, bim, cre, cim, ab_re, ab_im, dskip)


def _s5_bwd(z, bre, bim, cre, cim, ab_re, ab_im, dskip, kre, kim, d_y, tag):
    S = z.shape[0]
    T = _pick(S, (S5_CHUNK, 128))
    nch = S // T
    rev = lambda i: nch - 1 - i
    full = lambda shape: pl.BlockSpec(shape, lambda i: (0,) * len(shape))

    def body(u_ref, bre_ref, bim_ref, cre_ref, cim_ref, are_ref, aim_ref, d_ref, kre_ref, kim_ref, dy_ref,
             du_ref, dbre_ref, dbim_ref, dcre_ref, dcim_ref, dare_ref, daim_ref, dd_ref, g_re, g_im):
        @pl.when(pl.program_id(0) == 0)
        def _():
            g_re[...] = jnp.zeros_like(g_re)
            g_im[...] = jnp.zeros_like(g_im)
            for r in (dbre_ref, dbim_ref, dcre_ref, dcim_ref, dare_ref, daim_ref, dd_ref):
                r[...] = jnp.zeros_like(r)

        u = u_ref[...]
        dy = dy_ref[...].astype(F32)
        a_re, a_im = are_ref[...], aim_ref[...]
        row = lax.broadcasted_iota(jnp.int32, (T, S5_LANES), 0)
        x_re, x_im = _s5_states(u, bre_ref, bim_ref, a_re, a_im, kre_ref[...], kim_ref[...], row, T)
        dcre_ref[...] += _dot(dy, x_re, _TN)
        dcim_ref[...] -= _dot(dy, x_im, _TN)
        xp_re = jnp.where(row >= 1, pltpu.roll(x_re, 1, 0), 0.0) + jnp.where(row == 0, kre_ref[...], 0.0)
        xp_im = jnp.where(row >= 1, pltpu.roll(x_im, 1, 0), 0.0) + jnp.where(row == 0, kim_ref[...], 0.0)
        last = row == T - 1
        gd_re = _dot(dy, cre_ref[...]) + jnp.where(last, a_re * g_re[...] + a_im * g_im[...], 0.0)
        gd_im = -_dot(dy, cim_ref[...]) + jnp.where(last, a_re * g_im[...] - a_im * g_re[...], 0.0)
        t_re, t_im = _s5_scan(gd_re, gd_im, a_re, -a_im, row, T, True)
        g_re[...] = jnp.sum(jnp.where(row == 0, t_re, 0.0), axis=0, keepdims=True)
        g_im[...] = jnp.sum(jnp.where(row == 0, t_im, 0.0), axis=0, keepdims=True)
        du_ref[...] = (_dot(t_re, bre_ref[...], _NT) + _dot(t_im, bim_ref[...], _NT) + dy * d_ref[...]).astype(du_ref.dtype)
        dbre_ref[...] += _dot(u, t_re, _TN)
        dbim_ref[...] += _dot(u, t_im, _TN)
        dare_ref[...] += jnp.sum(t_re * xp_re + t_im * xp_im, axis=0, keepdims=True)
        daim_ref[...] += jnp.sum(t_im * xp_re - t_re * xp_im, axis=0, keepdims=True)
        dd_ref[...] += jnp.sum(dy * u, axis=0, keepdims=True)

    return pl.pallas_call(
        body, name=f"{tag}_s5_bwd", grid=(nch,),
        in_specs=_s5_in_specs(rev, T) + [pl.BlockSpec((None, 1, S5_LANES), lambda i: (rev(i), 0, 0)),
                                         pl.BlockSpec((None, 1, S5_LANES), lambda i: (rev(i), 0, 0)),
                                         pl.BlockSpec((T, S5_WIDTH), lambda i: (rev(i), 0))],
        out_specs=[pl.BlockSpec((T, S5_WIDTH), lambda i: (rev(i), 0))] + [full((S5_WIDTH, S5_LANES))] * 4
        + [full((1, S5_LANES))] * 2 + [full((1, S5_WIDTH))],
        out_shape=[jax.ShapeDtypeStruct((S, S5_WIDTH), BF16)] + [jax.ShapeDtypeStruct((S5_WIDTH, S5_LANES), F32)] * 4
        + [jax.ShapeDtypeStruct((1, S5_LANES), F32)] * 2 + [jax.ShapeDtypeStruct((1, S5_WIDTH), F32)],
        scratch_shapes=[pltpu.VMEM((1, S5_LANES), F32), pltpu.VMEM((1, S5_LANES), F32)],
        compiler_params=_params(("arbitrary",)))(z, bre, bim, cre, cim, ab_re, ab_im, dskip, kre, kim, d_y)


def _s5_glu_fn(y, glu_w, glu_b):
    g = 0.5 * y * (1.0 + jnp.tanh(math.sqrt(2.0 / math.pi) * (y + 0.044715 * (y * y * y))))
    lin = lax.dot_general(g.astype(BF16), glu_w.astype(BF16), (((1,), (0,)), ((), ())), preferred_element_type=F32)
    return (g * jax.nn.sigmoid(lin + glu_b),)


def _block_diag(m):
    G, H, P = S5_GROUPS, S5_GROUP_CH, S5_STATE
    eye = jnp.eye(G, dtype=m.dtype)
    return (m.reshape(G, H, 1, P) * eye[:, None, :, None]).reshape(G * H, G * P)


def _block_diag_take(m):
    G, H, P = S5_GROUPS, S5_GROUP_CH, S5_STATE
    eye = jnp.eye(G, dtype=m.dtype)
    return jnp.sum(m.reshape(G, H, G, P) * eye[:, None, :, None], axis=2).reshape(G * H, P)


def _loss_head(y, target, tag):
    S, D = y.shape
    tm = _pick(S, (256, 128, 64))

    def body(y_ref, t_ref, dy_ref, loss_ref):
        @pl.when(pl.program_id(0) == 0)
        def _():
            loss_ref[...] = jnp.zeros_like(loss_ref)

        err = y_ref[...] - t_ref[...]
        dy_ref[...] = err * (1.0 / D)
        part = 0.5 * jnp.sum(jnp.mean(err * err, axis=-1, keepdims=True), axis=0, keepdims=True)
        loss_ref[...] += jnp.broadcast_to(part, loss_ref.shape)

    return pl.pallas_call(
        body, name=f"{tag}_loss", grid=(S // tm,),
        in_specs=[pl.BlockSpec((tm, D), lambda i: (i, 0)), pl.BlockSpec((tm, D), lambda i: (i, 0))],
        out_specs=[pl.BlockSpec((tm, D), lambda i: (i, 0)), pl.BlockSpec((8, 128), lambda i: (0, 0))],
        out_shape=[jax.ShapeDtypeStruct((S, D), F32), jax.ShapeDtypeStruct((8, 128), F32)],
        compiler_params=_params(("arbitrary",)))(y, target)


def _adamw(w, g, m, v, name):
    shape = w.shape
    cols = shape[-1]
    rows = int(np.prod(shape[:-1]))
    tr = _pick(rows, [t for t in (512, 256, 128, 64, 32, 16, 8) if t * cols <= 256 * 1024] or [8])
    c1 = 1.0 - ADAM_B1 ** ADAM_STEP
    c2 = 1.0 - ADAM_B2 ** ADAM_STEP

    def body(w_ref, g_ref, m_ref, v_ref, d_ref, mo_ref, vo_ref):
        gg = g_ref[...]
        mn = ADAM_B1 * m_ref[...] + (1.0 - ADAM_B1) * gg
        vn = ADAM_B2 * v_ref[...] + (1.0 - ADAM_B2) * (gg * gg)
        d_ref[...] = -ADAM_LR * ((mn / c1) / (jnp.sqrt(vn / c2) + ADAM_EPS) + ADAM_WD * w_ref[...])
        mo_ref[...] = mn
        vo_ref[...] = vn

    spec = pl.BlockSpec((tr, cols), lambda i: (i, 0))
    outs = pl.pallas_call(
        body, name=name, grid=(rows // tr,), in_specs=[spec] * 4, out_specs=[spec] * 3,
        out_shape=[jax.ShapeDtypeStruct((rows, cols), F32)] * 3,
        compiler_params=_params(("parallel",)))(*[a.reshape(rows, cols) for a in (w, g, m, v)])
    return [o.reshape(shape) for o in outs]


def _sum_slots(slots, name):
    n, R, C = slots.shape
    tr = _pick(R, (1024, 512, 256, 128, 64, 32, 16, 8))

    def body(s_ref, o_ref):
        acc = s_ref[0].astype(F32)
        for k in range(1, n):
            acc = acc + s_ref[k].astype(F32)
        o_ref[...] = acc

    return pl.pallas_call(
        body, name=name, grid=(R // tr,), in_specs=[pl.BlockSpec((n, tr, C), lambda i: (0, i, 0))],
        out_specs=pl.BlockSpec((tr, C), lambda i: (i, 0)),
        out_shape=jax.ShapeDtypeStruct((R, C), F32), compiler_params=_params(("parallel",)))(slots)


def _row_tile(R, C, itemsize, target=1 << 20):
    return _pick(R, [t for t in (2048, 1024, 512, 256, 128, 64, 32, 16) if t * C * itemsize <= target] or [16])


def _add_sibling(part, got, name):
    n, _, R2, C = part.shape
    tr = _row_tile(R2, C, 2)

    def body(p_ref, q_ref, o_ref):
        o_ref[...] = (p_ref[...].astype(F32) + q_ref[...].astype(F32)).astype(o_ref.dtype)

    spec = pl.BlockSpec((None, tr, C), lambda k, i: (k, i, 0))
    return pl.pallas_call(
        body, name=name, grid=(n, R2 // tr),
        in_specs=[pl.BlockSpec((None, None, tr, C), lambda k, i: (k, lax.axis_index("c"), i, 0)), spec], out_specs=spec,
        out_shape=jax.ShapeDtypeStruct((n, R2, C), BF16), compiler_params=_params(("parallel", "parallel")))(part, got)


def _sum_chips(chip_part, others, name):
    _, R2, C = chip_part.shape
    tr = _row_tile(R2, C, 4)

    def body(a_ref, b_ref, o_ref):
        acc = a_ref[...].astype(F32)
        for j in range(N_SHARD - 1):
            acc = acc + b_ref[j].astype(F32)
        o_ref[...] = acc

    return pl.pallas_call(
        body, name=name, grid=(R2 // tr,),
        in_specs=[pl.BlockSpec((None, tr, C), lambda i: (2 * lax.axis_index("x") + lax.axis_index("y"), i, 0)),
                  pl.BlockSpec((N_SHARD - 1, tr, C), lambda i: (0, i, 0))],
        out_specs=pl.BlockSpec((None, tr, C), lambda i: (lax.axis_index("c"), i, 0)),
        out_shape=jax.ShapeDtypeStruct((2, R2, C), F32), compiler_params=_params(("parallel",)))(chip_part, others)


_ANY = pl.BlockSpec(memory_space=pl.ANY)


def _place():
    return lax.axis_index("x"), lax.axis_index("y"), lax.axis_index("c")


def _other_chips(x, y):
    return [(1 - x, y), (x, 1 - y), (1 - x, 1 - y)]


def _remote(src, dst, send_sems, recv_sems, n, to):
    return pltpu.make_async_remote_copy(src_ref=src, dst_ref=dst, send_sem=send_sems.at[n], recv_sem=recv_sems.at[n],
                                        device_id=to, device_id_type=MESH_ID)


_JOB_SEMS = {"gather_ici": 3, "gather_d2d": 4, "scatter": 3}


def _job_out_shape(job):
    src = job["ins"][0]
    if job["kind"] == "gather_ici":
        return jax.ShapeDtypeStruct((N_SHARD,) + src.shape, src.dtype)
    if job["kind"] == "gather_d2d":
        return jax.ShapeDtypeStruct(src.shape, src.dtype)
    return jax.ShapeDtypeStruct((N_SHARD - 1,) + src.shape[1:], src.dtype)


def _job_copies(kind, ins, out, send_sems, recv_sems, base):
    x, y, c = _place()
    k = 2 * x + y
    sibling = (x, y, 1 - c)
    sends, recvs = [], []
    for j, (cx, cy) in enumerate(_other_chips(x, y)):
        kj = 2 * cx + cy
        if kind == "gather_ici":
            sends.append(_remote(ins[0].at[c], out.at[k, c], send_sems, recv_sems, base + j, (cx, cy, c)))
            recvs.append(_remote(out.at[kj, c], out.at[kj, c], send_sems, recv_sems, base + j, sibling))
        elif kind == "gather_d2d":
            sends.append(_remote(out.at[kj, c], out.at[kj, c], send_sems, recv_sems, base + 1 + j, sibling))
            recvs.append(_remote(out.at[kj, 1 - c], out.at[kj, 1 - c], send_sems, recv_sems, base + 1 + j, sibling))
        else:
            sends.append(_remote(ins[0].at[kj], out.at[j], send_sems, recv_sems, base + j, (cx, cy, c)))
            recvs.append(_remote(out.at[j], out.at[j], send_sems, recv_sems, base + j, sibling))
    if kind == "gather_d2d":
        sends.append(_remote(ins[1], out.at[k], send_sems, recv_sems, base, sibling))
        recvs.append(_remote(out.at[k], out.at[k], send_sems, recv_sems, base, sibling))
    return sends, recvs


def _gather_shards(shards, name):
    n = len(shards)
    per = 7

    def body(*refs):
        ins, outs = refs[:n], refs[n:2 * n]
        send_sems, recv_sems = refs[2 * n:]
        x, y, c = _place()
        k = 2 * x + y
        sibling = (x, y, 1 - c)
        chips = _other_chips(x, y)
        started = []
        for a in range(n):
            cp = _remote(ins[a], outs[a].at[k], send_sems, recv_sems, per * a, sibling)
            cp.start()
            started.append(cp)
            for j, (cx, cy) in enumerate(chips):
                cp = _remote(ins[a].at[c], outs[a].at[k, c], send_sems, recv_sems, per * a + 1 + j, (cx, cy, c))
                cp.start()
                started.append(cp)
        for a in range(n):
            for j, (cx, cy) in enumerate(chips):
                landed = outs[a].at[2 * cx + cy, c]
                _remote(landed, landed, send_sems, recv_sems, per * a + 1 + j, sibling).wait_recv()
                cp = _remote(landed, landed, send_sems, recv_sems, per * a + 4 + j, sibling)
                cp.start()
                started.append(cp)
        for a in range(n):
            own = outs[a].at[k]
            _remote(own, own, send_sems, recv_sems, per * a, sibling).wait_recv()
            for j, (cx, cy) in enumerate(chips):
                passed = outs[a].at[2 * cx + cy, 1 - c]
                _remote(passed, passed, send_sems, recv_sems, per * a + 4 + j, sibling).wait_recv()
        for cp in started:
            cp.wait_send()

    return pl.pallas_call(
        body, name=name, in_specs=[_ANY] * n, out_specs=[_ANY] * n,
        out_shape=[jax.ShapeDtypeStruct((N_SHARD,) + s.shape, s.dtype) for s in shards],
        scratch_shapes=[pltpu.SemaphoreType.DMA((per * n,)), pltpu.SemaphoreType.DMA((per * n,))],
        )(*shards)


def _swap_other_half(parts, name):
    n = len(parts)

    def body(*refs):
        ins, outs = refs[:n], refs[n:2 * n]
        send_sems, recv_sems = refs[2 * n:]
        x, y, c = _place()
        started = []
        for a in range(n):
            for k in range(N_SHARD):
                cp = _remote(ins[a].at[k, 1 - c], outs[a].at[k], send_sems, recv_sems, N_SHARD * a + k, (x, y, 1 - c))
                cp.start()
                started.append(cp)
        for cp in started:
            cp.wait()

    return pl.pallas_call(
        body, name=name, in_specs=[_ANY] * n, out_specs=[_ANY] * n,
        out_shape=[jax.ShapeDtypeStruct((N_SHARD,) + p.shape[2:], p.dtype) for p in parts],
        scratch_shapes=[pltpu.SemaphoreType.DMA((N_SHARD * n,)), pltpu.SemaphoreType.DMA((N_SHARD * n,))],
        )(*parts)


def _scatter_to_chips(parts, name):
    n = len(parts)
    per = N_SHARD - 1

    def body(*refs):
        ins, outs = refs[:n], refs[n:2 * n]
        send_sems, recv_sems = refs[2 * n:]
        x, y, c = _place()
        chips = _other_chips(x, y)
        started = []
        for a in range(n):
            for j, (cx, cy) in enumerate(chips):
                cp = _remote(ins[a].at[2 * cx + cy], outs[a].at[j], send_sems, recv_sems, per * a + j, (cx, cy, c))
                cp.start()
                started.append(cp)
        for cp in started:
            cp.wait()

    return pl.pallas_call(
        body, name=name, in_specs=[_ANY] * n, out_specs=[_ANY] * n,
        out_shape=[jax.ShapeDtypeStruct((per,) + p.shape[1:], p.dtype) for p in parts],
        scratch_shapes=[pltpu.SemaphoreType.DMA((per * n,)), pltpu.SemaphoreType.DMA((per * n,))],
        )(*parts)


def _join_halves(bufs, name):
    n = len(bufs)

    def body(*refs):
        outs = refs[n:2 * n]
        send_sems, recv_sems = refs[2 * n:]
        x, y, c = _place()
        started = []
        for a in range(n):
            cp = _remote(outs[a].at[c], outs[a].at[c], send_sems, recv_sems, a, (x, y, 1 - c))
            cp.start()
            started.append(cp)
        for a in range(n):
            arrives = outs[a].at[1 - c]
            _remote(arrives, arrives, send_sems, recv_sems, a, (x, y, 1 - c)).wait_recv()
        for cp in started:
            cp.wait_send()

    return pl.pallas_call(
        body, name=name, in_specs=[_ANY] * n, out_specs=[_ANY] * n,
        out_shape=[jax.ShapeDtypeStruct(b.shape, b.dtype) for b in bufs],
        input_output_aliases={a: a for a in range(n)},
        scratch_shapes=[pltpu.SemaphoreType.DMA((n,)), pltpu.SemaphoreType.DMA((n,))],
        )(*bufs)


def _gather_all_devices(vec, name):
    def body(in_ref, out_ref, send_sems, recv_sems, local_sem):
        x, y, c = _place()
        me = 4 * x + 2 * y + c
        own = pltpu.make_async_copy(in_ref, out_ref.at[me], local_sem)
        own.start()
        sends = []
        for r in range(1, 8):
            fx, fy, fc = (r >> 2) & 1, (r >> 1) & 1, r & 1
            to = (x ^ fx, y ^ fy, c ^ fc)
            sends.append(pltpu.make_async_remote_copy(
                src_ref=in_ref, dst_ref=out_ref.at[me], send_sem=send_sems.at[r - 1], recv_sem=recv_sems.at[r - 1],
                device_id=to, device_id_type=MESH_ID))
            sends[-1].start()
        for r in range(1, 8):
            fx, fy, fc = (r >> 2) & 1, (r >> 1) & 1, r & 1
            slot = out_ref.at[4 * (x ^ fx) + 2 * (y ^ fy) + (c ^ fc)]
            pltpu.make_async_remote_copy(src_ref=slot, dst_ref=slot, send_sem=send_sems.at[r - 1],
                                         recv_sem=recv_sems.at[r - 1], device_id=(x, y, c),
                                         device_id_type=MESH_ID).wait_recv()
        for cp in sends:
            cp.wait_send()
        own.wait()

    return pl.pallas_call(
        body, name=name, in_specs=[_ANY], out_specs=_ANY, out_shape=jax.ShapeDtypeStruct((8,) + vec.shape, vec.dtype),
        scratch_shapes=[pltpu.SemaphoreType.DMA((7,)), pltpu.SemaphoreType.DMA((7,)), pltpu.SemaphoreType.DMA(())],
        )(vec)


def _permute_w_in(w):
    cut = Z_SU
    return jnp.concatenate([w[:, :cut], w[:, cut + 2 * DN_HEADS:], w[:, cut:cut + 2 * DN_HEADS],
                            jnp.zeros((w.shape[0], Z_WIDTH - IN_WIDTH), w.dtype)], axis=1)


def _unpermute_w_in(wp):
    return jnp.concatenate([wp[:, :Z_SU], wp[:, Z_DBA:Z_DBA + 2 * DN_HEADS], wp[:, Z_SU:Z_DBA]], axis=1)


def _s5_inputs(sp):
    ab_re, ab_im, bb_re, bb_im = _whole_fwd(
        _s5_param_fn, sp["s5_pins"], [(S5_GROUPS, S5_STATE)] * 2 + [(S5_WIDTH, S5_STATE)] * 2, name="s5_params")
    return (_block_diag(bb_re), _block_diag(bb_im), _block_diag(sp["c_re"]), _block_diag(sp["c_im"]),
            ab_re.reshape(1, S5_LANES), ab_im.reshape(1, S5_LANES), sp["dskip"])


def _mixer_fwd(x, sp, weights, rope, mm):
    D = x.shape[1]
    (h,) = _rows_fwd(_prenorm_fn, [(x, D, 0)], [sp["mix_norm_pre"]], [(D, D, BF16)], name="mix_prenorm")
    w_in = weights("win")
    z = mm("win", h, w_in, out_dtype=F32, name="mix_in")
    w_out, glu_w = weights("wout")
    y_attn = _attn_fwd(z, rope[0], rope[1], sp["sinks"], "mix")
    qkv = _conv_fwd(z, sp["conv_w"], "mix")
    y_dn, states = _dn_fwd(qkv, z, sp["dn_prm"], sp["dn_norm_w"], "mix")
    s5_in = _s5_inputs(sp)
    y_lin, kre, kim = _s5_fwd(z, *s5_in, "mix")
    (y_s5,) = _rows_fwd(_s5_glu_fn, [(y_lin, S5_WIDTH, 0)], [glu_w, sp["glu_b"]], [(S5_WIDTH, S5_WIDTH, BF16)],
                        name="mix_s5_glu")
    cat = jnp.concatenate([y_attn, y_dn, y_s5], axis=1)
    mixed = mm("wout", cat, w_out, out_dtype=F32, name="mix_out")
    (x_new,) = _rows_fwd(_residual_fn(1.0), [(x, D, 0), (mixed, D, 0)], [sp["mix_norm_post"]], [(D, D, F32)],
                         name="mix_residual")
    return x_new, (h, z, qkv, states, s5_in, y_lin, kre, kim, cat, mixed, w_in, w_out, glu_w)


def _mixer_bwd(dx_new, x, sp, rope, saved, mm, sink):
    h, z, qkv, states, s5_in, y_lin, kre, kim, cat, mixed, w_in, w_out, glu_w = saved
    D = x.shape[1]
    G, H, P = S5_GROUPS, S5_GROUP_CH, S5_STATE
    d_mixed, d_g_post = _rows_bwd(_postnorm_fn(1.0), [(mixed, D, 0)], [sp["mix_norm_post"]], [(dx_new, D, 0)],
                                  [(0, D, BF16)], name="mix_postnorm_bwd")
    d_cat = mm(d_mixed, w_out, tb=True, out_dtype=BF16, name="mix_out_dx")
    sink("wout", mm(cat, d_mixed, ta=True, out_dtype=BF16, name="mix_out_dw"))
    d_attn, d_dn, d_s5 = d_cat[:, :ATTN_WIDTH], d_cat[:, ATTN_WIDTH:ATTN_WIDTH + DN_WIDTH], d_cat[:, ATTN_WIDTH + DN_WIDTH:]
    d_ylin, d_glu_w, d_glu_b = _rows_bwd(_s5_glu_fn, [(y_lin, S5_WIDTH, 0)], [glu_w, sp["glu_b"]],
                                         [(d_s5, S5_WIDTH, 0)], [(0, S5_WIDTH, F32)], name="mix_s5_glu_bwd")
    sink("glu", d_glu_w.astype(BF16))
    d_us5, d_bre, d_bim, d_cre, d_cim, d_are, d_aim, d_dskip = _s5_bwd(z, *s5_in, kre, kim, d_ylin, "mix")
    cts = [d_are.reshape(G, P), d_aim.reshape(G, P), _block_diag_take(d_bre), _block_diag_take(d_bim)]
    d_a_re, d_a_im, d_ldt, d_bt_re, d_bt_im = _whole_bwd(_s5_param_fn, sp["s5_pins"], cts, 5, name="s5_params_bwd",
                                                         lane_sum=(2,))
    from_t = lambda m: m.reshape(G, H, P).transpose(0, 2, 1)
    d_qkv, d_zg, d_ba, d_prm, d_nw = _dn_bwd(qkv, z, sp["dn_prm"], sp["dn_norm_w"], states, d_dn, "mix")
    d_uconv, d_conv_w = _conv_bwd(z, sp["conv_w"], d_qkv, "mix")
    d_q, d_kv, d_sinks = _attn_bwd(z, rope[0], rope[1], sp["sinks"], d_attn, "mix")
    d_z = jnp.concatenate([d_q, d_kv, d_uconv, d_zg, d_us5, d_ba], axis=1)
    sink("win", mm(h, d_z, ta=True, out_dtype=BF16, name="mix_in_dw"))
    d_h = mm(d_z, w_in, tb=True, out_dtype=BF16, name="mix_in_dx")
    dx, d_g_pre = _rows_bwd(_prenorm_fn, [(x, D, 0)], [sp["mix_norm_pre"]], [(d_h, D, 0)], [(0, D, F32)],
                            name="mix_prenorm_bwd", add_to_first=dx_new)
    small = {
        "mix_norm_pre": d_g_pre[0], "mix_norm_post": d_g_post[0], "attn_sinks": d_sinks[:, 0], "dn_conv_w": d_conv_w,
        "dn_a_log": d_prm[:DN_HEADS, 0], "dn_dt_bias": d_prm[DN_HEADS:2 * DN_HEADS, 0], "dn_norm_w": d_nw[0],
        "s5_a_re": d_a_re, "s5_a_im": d_a_im, "s5_log_dt": d_ldt[:, 0], "s5_b_re": from_t(d_bt_re),
        "s5_b_im": from_t(d_bt_im), "s5_c_re": _block_diag_take(d_cre).reshape(G, H, P),
        "s5_c_im": _block_diag_take(d_cim).reshape(G, H, P), "s5_d": d_dskip[0], "s5_glu_b": d_glu_b[0],
    }
    return dx, small


BIG_PIECES = ("gu1", "wd1", "win", "wout", "glu", "gu2", "wd2")


def _halves(m):
    return m.reshape(m.shape[:-2] + (2, m.shape[-2] // 2, m.shape[-1]))


def _whole(m):
    return m.reshape(m.shape[:-3] + (2 * m.shape[-2], m.shape[-1]))


class _WeightGather:
    def __init__(self, shards):
        self.shards = shards
        pieces = list(shards[0])
        flat = [s for p in pieces for s in shards[0][p]]
        got = iter(_gather_shards(flat, "gather_weights"))
        self.ready = {(0, p): [next(got) for _ in shards[0][p]] for p in pieces}
        self.landing = None

    def weights(self, l, piece):
        return self.ready[(l, piece)]

    def mm(self, l, piece, a, b, **kw):
        jobs, done = [], None
        if self.landing is not None:
            done = self.landing
            jobs += [{"kind": "gather_d2d", "ins": [buf, s]} for buf, s in zip(done[2], self.shards[done[0]][done[1]])]
        n_done = len(jobs)
        if l + 1 < len(self.shards):
            jobs += [{"kind": "gather_ici", "ins": [s]} for s in self.shards[l + 1][piece]]
        if not jobs:
            return _mm(a, b, **kw)
        out, bufs = _mm(a, b, jobs=jobs, **kw)
        if done is not None:
            self.ready[done[:2]] = bufs[:n_done]
        self.landing = (l + 1, piece, bufs[n_done:]) if l + 1 < len(self.shards) else None
        return out


class _GradReduce:
    def __init__(self):
        self.waiting, self.landed = [], {}

    def add(self, piece, grad):
        part = _halves(grad)
        (got,) = _swap_other_half([part], "reduce_siblings")
        self.waiting.append((piece, _add_sibling(part, got, "reduce_siblings_add")))

    def mm(self, a, b, **kw):
        if not self.waiting:
            return _mm(a, b, **kw)
        out, others = _mm(a, b, jobs=[{"kind": "scatter", "ins": [cp]} for _, cp in self.waiting], **kw)
        for (piece, cp), o in zip(self.waiting, others):
            self.landed[piece] = (cp, o)
        self.waiting = []
        return out

    def finish(self):
        if self.waiting:
            others = _scatter_to_chips([cp for _, cp in self.waiting], "reduce_chips")
            for (piece, cp), o in zip(self.waiting, others):
                self.landed[piece] = (cp, o)
        pieces = list(self.landed)
        joined = _join_halves([_sum_chips(*self.landed[p], "reduce_chips_sum") for p in pieces], "reduce_join")
        return {p: _whole(m) for p, m in zip(pieces, joined)}


def _pack_small(arrs, extra=()):
    flat = jnp.concatenate([a.reshape(-1) for a in arrs] + list(extra))
    n = flat.shape[0]
    padded = -(-n // 1024) * 1024
    return jnp.concatenate([flat, jnp.zeros((padded - n,), flat.dtype)]).reshape(padded // 128, 128)


def _unpack_small(flat2d, shapes):
    flat = flat2d.reshape(-1)
    out, o = [], 0
    for s in shapes:
        n = int(np.prod(s))
        out.append(flat[o:o + n].reshape(s))
        o += n
    return out, flat[o:]


def _step(a):
    x, target = a["x"][0], a["loss_target"][0]
    S, D = x.shape
    L, _, Fs = a["ff1_w_gate"].shape
    px, py, pc = _place()
    chip = 2 * px + py

    def rows_of_chips(g):
        g = _whole(g)
        return g.reshape(N_SHARD * g.shape[1], g.shape[2])

    shards = []
    for l in range(L):
        half = lambda m: _halves(m.astype(BF16))
        shards.append({
            "gu1": [half(jnp.concatenate([a["ff1_w_gate"][l], a["ff1_w_up"][l]], axis=1))], "wd1": [half(a["ff1_w_down"][l])],
            "win": [half(a["w_in"][l])], "wout": [half(a["w_out"][l]), half(a["s5_glu_w"][l])],
            "gu2": [half(jnp.concatenate([a["ff2_w_gate"][l], a["ff2_w_up"][l]], axis=1))], "wd2": [half(a["ff2_w_down"][l])]})
    gather = _WeightGather(shards)

    def ffn_weights(l, f):
        def get(piece):
            (g,) = gather.weights(l, piece + f)
            return _whole(g) if piece == "gu" else rows_of_chips(g)
        return get

    def mixer_weights(l):
        def get(piece):
            got = gather.weights(l, piece)
            if piece == "win":
                return _permute_w_in(_whole(got[0]).transpose(1, 0, 2).reshape(D, IN_WIDTH))
            return rows_of_chips(got[0]), rows_of_chips(got[1])
        return get

    conv_local = a["dn_conv_w"].reshape(-1)
    conv_rows = -(-conv_local.shape[0] // (16 * FLAT_LANES)) * 16
    conv_pad = jnp.concatenate([conv_local, jnp.zeros((conv_rows * FLAT_LANES - conv_local.shape[0],), F32)])
    (conv_all,) = _gather_shards([conv_pad.reshape(2, conv_rows // 2, FLAT_LANES)], "gather_conv")
    conv_all = conv_all.reshape(N_SHARD, -1)[:, :conv_local.shape[0]].reshape(N_SHARD, L, DN_CONV, -1)
    conv_full = conv_all.transpose(1, 2, 0, 3).reshape(L, DN_CONV, 3 * DN_WIDTH)

    expand = jnp.repeat(jnp.eye(S5_GROUPS, dtype=F32), S5_GROUP_CH, axis=0)
    lanes = lambda v, n=128: jnp.broadcast_to(v[:, None], (v.shape[0], n))
    to_t = lambda m: m.transpose(0, 2, 1).reshape(S5_WIDTH, S5_STATE)

    def small_params(l):
        sp = {k: a[k][l][None] for k in ("ff1_norm_pre", "ff1_norm_post", "mix_norm_pre", "mix_norm_post",
                                         "ff2_norm_pre", "ff2_norm_post")}
        sp["sinks"] = lanes(a["attn_sinks"][l])
        sp["conv_w"] = conv_full[l]
        sp["dn_prm"] = jnp.concatenate([lanes(a["dn_a_log"][l]), lanes(a["dn_dt_bias"][l])], axis=0)
        sp["dn_norm_w"] = a["dn_norm_w"][l][None]
        sp["s5_pins"] = [a["s5_a_re"][l], a["s5_a_im"][l], lanes(a["s5_log_dt"][l], S5_STATE),
                         to_t(a["s5_b_re"][l]), to_t(a["s5_b_im"][l]), expand]
        sp["c_re"] = a["s5_c_re"][l].reshape(S5_WIDTH, S5_STATE)
        sp["c_im"] = a["s5_c_im"][l].reshape(S5_WIDTH, S5_STATE)
        sp["dskip"] = a["s5_d"][l][None]
        sp["glu_b"] = a["s5_glu_b"][l][None]
        return sp

    rope = _rope_tables(S)
    sps = [small_params(l) for l in range(L)]

    saved = []
    for l in range(L):
        sp = sps[l]
        mm_of = lambda f: (lambda piece, p, q, **kw: gather.mm(l, piece + f, p, q, **kw))
        x1, s1 = _ffn_fwd(x, sp["ff1_norm_pre"], sp["ff1_norm_post"], ffn_weights(l, "1"), "ff1", mm_of("1"))
        x2, s2 = _mixer_fwd(x1, sp, mixer_weights(l), rope, mm_of(""))
        x3, s3 = _ffn_fwd(x2, sp["ff2_norm_pre"], sp["ff2_norm_post"], ffn_weights(l, "2"), "ff2", mm_of("2"))
        saved.append((x, s1, x1, s2, x2, s3))
        x = x3
    dx, loss_part = _loss_head(x, target, "head")

    big_grads, small_grads = [None] * L, [None] * L
    shard_major = lambda m: m.reshape(N_SHARD, m.shape[0] // N_SHARD, m.shape[1])
    for l in reversed(range(L)):
        sp = sps[l]
        x0, s1, x1, s2, x2, s3 = saved[l]
        red = _GradReduce()

        def ffn_sink(f):
            return lambda piece, g: red.add(piece + f, g if piece == "gu" else shard_major(g))

        def mixer_sink(piece, g):
            if piece == "win":
                g = _unpermute_w_in(g).reshape(D, N_SHARD, IN_WIDTH // N_SHARD).transpose(1, 0, 2)
            red.add(piece, g if piece == "win" else shard_major(g))

        dx, g_pre2, g_post2 = _ffn_bwd(dx, x2, sp["ff2_norm_pre"], sp["ff2_norm_post"], s3, "ff2", red.mm, ffn_sink("2"))
        dx, sg = _mixer_bwd(dx, x1, sp, rope, s2, red.mm, mixer_sink)
        dx, g_pre1, g_post1 = _ffn_bwd(dx, x0, sp["ff1_norm_pre"], sp["ff1_norm_post"], s1, "ff1", red.mm, ffn_sink("1"))
        big_grads[l] = red.finish()
        sg.update({"ff1_norm_pre": g_pre1[0], "ff1_norm_post": g_post1[0], "ff2_norm_pre": g_pre2[0],
                   "ff2_norm_post": g_post2[0]})
        small_grads[l] = sg
    grad_x = dx[None]

    grads = {}
    layers = lambda k: jnp.stack([big_grads[l][k] for l in range(L)])
    for f, (gu, wd) in (("ff1", ("gu1", "wd1")), ("ff2", ("gu2", "wd2"))):
        gus = layers(gu)
        grads[f + "_w_gate"], grads[f + "_w_up"] = gus[:, :, :Fs], gus[:, :, Fs:]
        grads[f + "_w_down"] = layers(wd)
    grads["w_in"], grads["w_out"], grads["s5_glu_w"] = layers("win"), layers("wout"), layers("glu")

    small_local = [jnp.stack([small_grads[l][n] for l in range(L)]) for n in SMALL]
    vec = _pack_small(small_local, extra=(loss_part[0, :1],))
    total = _sum_slots(_gather_all_devices(vec, "gather_small"), "sum_small")
    small_total, rest = _unpack_small(total, [g.shape for g in small_local])
    loss = rest[0]
    for n, g in zip(SMALL, small_total):
        grads[n] = g
    cw = 3 * DN_WIDTH // N_SHARD
    grads["dn_conv_w"] = lax.dynamic_slice_in_dim(grads["dn_conv_w"], chip * cw, cw, axis=2)

    delta, new_m, new_v = {}, {}, {}
    for n in BIG:
        delta[n], new_m[n], new_v[n] = _adamw(a[n], grads[n], a["m_" + n], a["v_" + n], "adamw_" + n)
    shapes = [a[n].shape for n in SMALL]
    packed = [_pack_small([src[n] for n in SMALL]) for src in
              (a, grads, {n: a["m_" + n] for n in SMALL}, {n: a["v_" + n] for n in SMALL})]
    for dst, res in zip((delta, new_m, new_v), _adamw(*packed, "adamw_small")):
        for n, val in zip(SMALL, _unpack_small(res, shapes)[0]):
            dst[n] = val
    return (loss, grad_x, *[grads[n] for n in WEIGHTS], *[delta[n] for n in WEIGHTS], *[new_m[n] for n in WEIGHTS],
            *[new_v[n] for n in WEIGHTS])


def kernel(x, ff1_norm_pre, ff1_w_gate, ff1_w_up, ff1_w_down, ff1_norm_post, mix_norm_pre, w_in, attn_sinks, dn_conv_w, dn_a_log, dn_dt_bias, dn_norm_w, s5_a_re, s5_a_im, s5_log_dt, s5_b_re, s5_b_im, s5_c_re, s5_c_im, s5_d, s5_glu_w, s5_glu_b, w_out, mix_norm_post, ff2_norm_pre, ff2_w_gate, ff2_w_up, ff2_w_down, ff2_norm_post, loss_target, m_ff1_norm_pre, m_ff1_w_gate, m_ff1_w_up, m_ff1_w_down, m_ff1_norm_post, m_mix_norm_pre, m_w_in, m_attn_sinks, m_dn_conv_w, m_dn_a_log, m_dn_dt_bias, m_dn_norm_w, m_s5_a_re, m_s5_a_im, m_s5_log_dt, m_s5_b_re, m_s5_b_im, m_s5_c_re, m_s5_c_im, m_s5_d, m_s5_glu_w, m_s5_glu_b, m_w_out, m_mix_norm_post, m_ff2_norm_pre, m_ff2_w_gate, m_ff2_w_up, m_ff2_w_down, m_ff2_norm_post, v_ff1_norm_pre, v_ff1_w_gate, v_ff1_w_up, v_ff1_w_down, v_ff1_norm_post, v_mix_norm_pre, v_w_in, v_attn_sinks, v_dn_conv_w, v_dn_a_log, v_dn_dt_bias, v_dn_norm_w, v_s5_a_re, v_s5_a_im, v_s5_log_dt, v_s5_b_re, v_s5_b_im, v_s5_c_re, v_s5_c_im, v_s5_d, v_s5_glu_w, v_s5_glu_b, v_w_out, v_mix_norm_post, v_ff2_norm_pre, v_ff2_w_gate, v_ff2_w_up, v_ff2_w_down, v_ff2_norm_post):
    return _step(dict(locals()))
```

```python
import functools
import math

import numpy as np
import jax
import jax.numpy as jnp
from jax import lax
from jax.experimental import pallas as pl
from jax.experimental.pallas import tpu as pltpu

F32 = jnp.float32
BF16 = jnp.bfloat16
HI = lax.Precision.HIGHEST
MESH_ID = pl.DeviceIdType.MESH

NORM_EPS = 1e-6
FFN_RES_WEIGHT = 0.5
ATTN_HEADS, ATTN_KV_HEADS, HEAD_DIM, WINDOW = 8, 2, 128, 128
ROPE_THETA = 10000.0
DN_HEADS, DN_HEAD_DIM, DN_CONV, DN_CHUNK = 4, 128, 4, 64
S5_GROUPS, S5_GROUP_CH, S5_STATE = 32, 16, 64
ATTN_WIDTH = ATTN_HEADS * HEAD_DIM
ATTN_KV_WIDTH = ATTN_KV_HEADS * HEAD_DIM
DN_WIDTH = DN_HEADS * DN_HEAD_DIM
S5_WIDTH = S5_GROUPS * S5_GROUP_CH
S5_LANES = S5_GROUPS * S5_STATE
MIX_WIDTH = ATTN_WIDTH + DN_WIDTH + S5_WIDTH
IN_SPLITS = (ATTN_WIDTH, ATTN_KV_WIDTH, ATTN_KV_WIDTH, 3 * DN_WIDTH, DN_WIDTH, DN_HEADS, DN_HEADS, S5_WIDTH)
IN_WIDTH = sum(IN_SPLITS)
Z_AQ, Z_AK, Z_AV = 0, ATTN_WIDTH, ATTN_WIDTH + ATTN_KV_WIDTH
Z_DQKV = ATTN_WIDTH + 2 * ATTN_KV_WIDTH
Z_DZ = Z_DQKV + 3 * DN_WIDTH
Z_SU = Z_DZ + DN_WIDTH
Z_DBA = Z_SU + S5_WIDTH
Z_WIDTH = Z_DBA + 128

ADAM_LR, ADAM_B1, ADAM_B2, ADAM_EPS, ADAM_WD, ADAM_STEP = 0.001, 0.9, 0.999, 1e-08, 0.01, 10

N_SHARD = 4
FLAT_LANES = 512
SMALL_LANES = 1024
VMEM_LIMIT = 56 * 1024 * 1024

WEIGHTS = ['ff1_norm_pre', 'ff1_w_gate', 'ff1_w_up', 'ff1_w_down', 'ff1_norm_post', 'mix_norm_pre', 'w_in',
           'attn_sinks', 'dn_conv_w', 'dn_a_log', 'dn_dt_bias', 'dn_norm_w', 's5_a_re', 's5_a_im', 's5_log_dt',
           's5_b_re', 's5_b_im', 's5_c_re', 's5_c_im', 's5_d', 's5_glu_w', 's5_glu_b', 'w_out', 'mix_norm_post',
           'ff2_norm_pre', 'ff2_w_gate', 'ff2_w_up', 'ff2_w_down', 'ff2_norm_post']
BIG = ['ff1_w_gate', 'ff1_w_up', 'ff1_w_down', 'w_in', 's5_glu_w', 'w_out', 'ff2_w_gate', 'ff2_w_up', 'ff2_w_down']
SMALL = [n for n in WEIGHTS if n not in BIG]


def _pick(dim, cands):
    for c in cands:
        if dim % c == 0:
            return c
    return dim


def _params(sem=None):
    return pltpu.CompilerParams(dimension_semantics=sem, vmem_limit_bytes=VMEM_LIMIT)


def _mm(a, b, *, ta=False, tb=False, out_dtype=F32, name, b_groups=None, bk_groups=None, out_groups=None, jobs=()):
    K, M = a.shape if ta else a.shape[::-1]
    Ng = Kg = None
    if b_groups:
        assert not tb
        _, Kb, Ng = b.shape
        N = b_groups * Ng
    elif bk_groups:
        assert tb
        _, N, Kg = b.shape
        Kb = bk_groups * Kg
    else:
        N, Kb = (b.shape if tb else b.shape[::-1])
    assert K == Kb, (a.shape, b.shape, ta, tb)
    tiles = (1408, 1024, 512, 384, 256, 128)
    tm = _pick(M, (1024, 512, 256, 128))
    tk = _pick(Kg if bk_groups else K, (2048,) + tiles)
    tn = _pick(N // out_groups if out_groups else (Ng if b_groups else N), tiles)
    nk = K // tk
    dims = (((0,) if ta else (1,), (1,) if tb else (0,)), ((), ()))

    grid = (M // tm, N // tn, nk)
    n_job_in = sum(len(j["ins"]) for j in jobs)
    n_sems = sum(_JOB_SEMS[j["kind"]] for j in jobs)

    def body(*refs):
        a_ref, b_ref = refs[:2]
        job_ins = refs[2:2 + n_job_in]
        o_ref = refs[2 + n_job_in]
        job_outs = refs[3 + n_job_in:3 + n_job_in + len(jobs)]
        scratch = refs[3 + n_job_in + len(jobs):]
        if jobs:
            send_sems, recv_sems = scratch[-2:]
            scratch = scratch[:-2]
            ids = [pl.program_id(d) for d in range(3)]
            first = functools.reduce(jnp.logical_and, [i == 0 for i in ids])
            last = functools.reduce(jnp.logical_and, [i == g - 1 for i, g in zip(ids, grid)])

            def copies():
                sends, recvs, at, sem = [], [], 0, 0
                for j, out in zip(jobs, job_outs):
                    s, r = _job_copies(j["kind"], job_ins[at:at + len(j["ins"])], out, send_sems, recv_sems, sem)
                    sends, recvs = sends + s, recvs + r
                    at, sem = at + len(j["ins"]), sem + _JOB_SEMS[j["kind"]]
                return sends, recvs

            @pl.when(first)
            def _():
                for cp in copies()[0]:
                    cp.start()

        part = lax.dot_general(a_ref[...].astype(BF16), b_ref[...].astype(BF16), dims, preferred_element_type=F32)
        if nk == 1:
            o_ref[...] = part.astype(o_ref.dtype)
        else:
            acc_ref, = scratch
            k = pl.program_id(2)

            @pl.when(k == 0)
            def _():
                acc_ref[...] = part

            @pl.when(k > 0)
            def _():
                acc_ref[...] += part

            @pl.when(k == nk - 1)
            def _():
                o_ref[...] = acc_ref[...].astype(o_ref.dtype)

        if jobs:
            @pl.when(last)
            def _():
                sends, recvs = copies()
                for cp in recvs:
                    cp.wait_recv()
                for cp in sends:
                    cp.wait_send()

    a_spec = pl.BlockSpec((tk, tm), lambda i, j, k: (k, i)) if ta else pl.BlockSpec((tm, tk), lambda i, j, k: (i, k))
    if b_groups:
        per = Ng // tn
        b_spec = pl.BlockSpec((None, tk, tn), lambda i, j, k: (j // per, k, j % per))
    elif bk_groups:
        per = Kg // tk
        b_spec = pl.BlockSpec((None, tn, tk), lambda i, j, k: (k // per, j, k % per))
    else:
        b_spec = pl.BlockSpec((tn, tk), lambda i, j, k: (j, k)) if tb else pl.BlockSpec((tk, tn), lambda i, j, k: (k, j))
    if out_groups:
        pero = (N // out_groups) // tn
        o_spec = pl.BlockSpec((None, tm, tn), lambda i, j, k: (j // pero, i, j % pero))
        o_shape = jax.ShapeDtypeStruct((out_groups, M, N // out_groups), out_dtype)
    else:
        o_spec = pl.BlockSpec((tm, tn), lambda i, j, k: (i, j))
        o_shape = jax.ShapeDtypeStruct((M, N), out_dtype)
    scratch = [pltpu.VMEM((tm, tn), F32)] if nk > 1 else []
    if not jobs:
        return pl.pallas_call(
            body, name=name, grid=grid, in_specs=[a_spec, b_spec], out_specs=o_spec, out_shape=o_shape,
            scratch_shapes=scratch, compiler_params=_params(("parallel", "parallel", "arbitrary")))(a, b)
    job_args = [x for j in jobs for x in j["ins"]]
    aliases, at = {}, 2
    for n, j in enumerate(jobs):
        if j["kind"] == "gather_d2d":
            aliases[at] = 1 + n
        at += len(j["ins"])
    res = pl.pallas_call(
        body, name=name, grid=grid, in_specs=[a_spec, b_spec] + [_ANY] * n_job_in,
        out_specs=[o_spec] + [_ANY] * len(jobs), out_shape=[o_shape] + [_job_out_shape(j) for j in jobs],
        input_output_aliases=aliases,
        scratch_shapes=scratch + [pltpu.SemaphoreType.DMA((n_sems,)), pltpu.SemaphoreType.DMA((n_sems,))],
        compiler_params=_params(("arbitrary", "arbitrary", "arbitrary")))(a, b, *job_args)
    return res[0], list(res[1:])


def _row_spec(tm, width, off):
    return pl.BlockSpec((tm, width), lambda i, j: (i, off + j))


def _const_spec(shape):
    return pl.BlockSpec(shape, lambda i, j: (0,) * len(shape))


def _rows_fwd(fn, rows, consts, outs, *, name, tm=256, ncol=1):
    S = rows[0][0].shape[0]
    tm = _pick(S, (tm, 128, 64))
    nr, nc = len(rows), len(consts)

    def body(*refs):
        vals = [r[...].astype(F32) for r in refs[:nr + nc]]
        res = fn(*vals)
        for o_ref, o in zip(refs[nr + nc:], res):
            o_ref[...] = o.astype(o_ref.dtype)

    return pl.pallas_call(
        body, name=name, grid=(S // tm, ncol),
        in_specs=[_row_spec(tm, w, off) for _, w, off in rows] + [_const_spec(c.shape) for c in consts],
        out_specs=[_row_spec(tm, bw, 0) for _, bw, _ in outs],
        out_shape=[jax.ShapeDtypeStruct((S, tw), dt) for tw, _, dt in outs],
        compiler_params=_params(("parallel", "parallel")))(*[r[0] for r in rows], *consts)


def _rows_bwd(fn, rows, consts, cts, row_grads, *, name, tm=256, ncol=1, add_to_first=None, lane_sum_consts=()):
    S = rows[0][0].shape[0]
    tm = _pick(S, (tm, 128, 64))
    nr, nc, nt = len(rows), len(consts), len(cts)
    n_in = nr + nc + nt + (1 if add_to_first is not None else 0)

    def body(*refs):
        vals = [r[...].astype(F32) for r in refs[:nr + nc]]
        ct = tuple(r[...].astype(F32) for r in refs[nr + nc:nr + nc + nt])
        _, vjp = jax.vjp(fn, *vals)
        grads = vjp(ct)
        outs = refs[n_in:]
        for n, (idx, _, _) in enumerate(row_grads):
            g = grads[idx]
            if n == 0 and add_to_first is not None:
                g = g + refs[n_in - 1][...].astype(F32)
            outs[n][...] = g.astype(outs[n].dtype)
        first = jnp.logical_and(pl.program_id(0) == 0, pl.program_id(1) == 0)
        for c in range(nc):
            o_ref = outs[len(row_grads) + c]
            g = grads[nr + c]
            if c in lane_sum_consts:
                g = jnp.broadcast_to(jnp.sum(g, axis=-1, keepdims=True), g.shape)

            @pl.when(first)
            def _():
                o_ref[...] = jnp.zeros_like(o_ref)

            o_ref[...] += g

    in_specs = ([_row_spec(tm, w, off) for _, w, off in rows] + [_const_spec(c.shape) for c in consts]
                + [_row_spec(tm, w, off) for _, w, off in cts])
    args = [r[0] for r in rows] + list(consts) + [c[0] for c in cts]
    if add_to_first is not None:
        in_specs.append(_row_spec(tm, rows[row_grads[0][0]][1], 0))
        args.append(add_to_first)
    out_specs = [_row_spec(tm, rows[idx][1], 0) for idx, _, _ in row_grads] + [_const_spec(c.shape) for c in consts]
    out_shape = ([jax.ShapeDtypeStruct((S, tw), dt) for _, tw, dt in row_grads]
                 + [jax.ShapeDtypeStruct(c.shape, F32) for c in consts])
    return pl.pallas_call(
        body, name=name, grid=(S // tm, ncol), in_specs=in_specs, out_specs=out_specs, out_shape=out_shape,
        compiler_params=_params(("arbitrary", "arbitrary")))(*args)


def _rms(x, gain):
    return x * lax.rsqrt(jnp.mean(x * x, axis=-1, keepdims=True) + NORM_EPS) * gain


def _prenorm_fn(x, gain):
    return (_rms(x, gain),)


def _postnorm_fn(weight):
    def fn(y, gain):
        return (weight * _rms(y, gain),)
    return fn


def _residual_fn(weight):
    def fn(x, y, gain):
        return (x + weight * _rms(y, gain),)
    return fn


def _swiglu_fn(blk):
    tf = blk.shape[1] // 2
    gate, up = blk[:, :tf], blk[:, tf:]
    return (gate * jax.nn.sigmoid(gate) * up,)


def _ffn_fwd(x, g_pre, g_post, weights, tag, mm):
    D = x.shape[1]
    (h,) = _rows_fwd(_prenorm_fn, [(x, D, 0)], [g_pre], [(D, D, BF16)], name=f"{tag}_prenorm")
    wgu = weights("gu")
    Fs = wgu.shape[2] // 2
    gu = mm("gu", h, wgu, b_groups=N_SHARD, out_dtype=BF16, name=f"{tag}_gate_up")
    (act,) = _rows_fwd(_swiglu_fn, [(gu, 2 * Fs, 0)], [], [(N_SHARD * Fs, Fs, BF16)], name=f"{tag}_swiglu",
                       ncol=N_SHARD)
    wd = weights("wd")
    y = mm("wd", act, wd, out_dtype=F32, name=f"{tag}_down")
    (x_new,) = _rows_fwd(_residual_fn(FFN_RES_WEIGHT), [(x, D, 0), (y, D, 0)], [g_post], [(D, D, F32)],
                         name=f"{tag}_residual")
    return x_new, (h, gu, act, y, wgu, wd)


def _ffn_bwd(dx_new, x, g_pre, g_post, saved, tag, mm, sink):
    h, gu, act, y, wgu, wd = saved
    D = x.shape[1]
    Fs = wgu.shape[2] // 2
    d_y, d_g_post = _rows_bwd(_postnorm_fn(FFN_RES_WEIGHT), [(y, D, 0)], [g_post], [(dx_new, D, 0)],
                              [(0, D, BF16)], name=f"{tag}_postnorm_bwd")
    d_act = mm(d_y, wd, tb=True, out_dtype=BF16, name=f"{tag}_down_dx")
    sink("wd", mm(act, d_y, ta=True, out_dtype=BF16, name=f"{tag}_down_dw"))
    (d_gu,) = _rows_bwd(_swiglu_fn, [(gu, 2 * Fs, 0)], [], [(d_act, Fs, 0)], [(0, 2 * N_SHARD * Fs, BF16)],
                        name=f"{tag}_swiglu_bwd", ncol=N_SHARD)
    sink("gu", mm(h, d_gu, ta=True, out_dtype=BF16, out_groups=N_SHARD, name=f"{tag}_gate_up_dw"))
    d_h = mm(d_gu, wgu, tb=True, bk_groups=N_SHARD, out_dtype=BF16, name=f"{tag}_gate_up_dx")
    dx, d_g_pre = _rows_bwd(_prenorm_fn, [(x, D, 0)], [g_pre], [(d_h, D, 0)], [(0, D, F32)],
                            name=f"{tag}_prenorm_bwd", add_to_first=dx_new)
    return dx, d_g_pre, d_g_post


@jax.custom_vjp
def _swap_halves(x):
    return pltpu.roll(x, HEAD_DIM // 2, 1)


def _swap_fwd(x):
    return _swap_halves(x), None


def _swap_bwd(_, g):
    return (_swap_halves(g),)


_swap_halves.defvjp(_swap_fwd, _swap_bwd)


def _rope(x, cc, ss):
    return x * cc + _swap_halves(x) * ss


def _attn_block_fn(has_prev):
    grp = ATTN_HEADS // ATTN_KV_HEADS
    scale = HEAD_DIM ** -0.5
    nt = (((1,), (1,)), ((), ()))
    nn = (((1,), (0,)), ((), ()))

    def fn(*a):
        q = a[:8]
        kp, kc, vp, vc = a[8:10], a[10:12], a[12:14], a[14:16]
        cc, ss, ccp, ssp, sinks = a[16:21]
        row = lax.broadcasted_iota(jnp.int32, (WINDOW, WINDOW), 0)
        col = lax.broadcasted_iota(jnp.int32, (WINDOW, WINDOW), 1)
        m_cur = col <= row
        m_prev = jnp.logical_and(col > row, has_prev)
        outs = []
        for h in range(ATTN_HEADS):
            g = h // grp
            qr = _rope(q[h], cc, ss).astype(BF16)
            kcr = _rope(kc[g], cc, ss).astype(BF16)
            kpr = _rope(kp[g], ccp, ssp).astype(BF16)
            s_c = jnp.where(m_cur, lax.dot_general(qr, kcr, nt, preferred_element_type=F32) * scale, -jnp.inf)
            s_p = jnp.where(m_prev, lax.dot_general(qr, kpr, nt, preferred_element_type=F32) * scale, -jnp.inf)
            sink = sinks[h:h + 1, :]
            m = jnp.maximum(jnp.maximum(jnp.max(s_c, axis=-1, keepdims=True), jnp.max(s_p, axis=-1, keepdims=True)),
                            sink)
            p_c = jnp.exp(s_c - m)
            p_p = jnp.exp(s_p - m)
            den = (jnp.sum(p_c, axis=-1, keepdims=True) + jnp.sum(p_p, axis=-1, keepdims=True) + jnp.exp(sink - m))
            inv = 1.0 / den
            o = (lax.dot_general((p_c * inv).astype(BF16), vc[g].astype(BF16), nn, preferred_element_type=F32)
                 + lax.dot_general((p_p * inv).astype(BF16), vp[g].astype(BF16), nn, preferred_element_type=F32))
            outs.append(o)
        return tuple(outs)

    return fn


def _attn_specs(order):
    kvb = ATTN_WIDTH // (2 * ATTN_KV_WIDTH)
    return [
        pl.BlockSpec((WINDOW, ATTN_WIDTH), lambda i: (order(i), 0)),
        pl.BlockSpec((WINDOW, 2 * ATTN_KV_WIDTH), lambda i: (jnp.maximum(order(i) - 1, 0), kvb)),
        pl.BlockSpec((WINDOW, 2 * ATTN_KV_WIDTH), lambda i: (order(i), kvb)),
        pl.BlockSpec((WINDOW, HEAD_DIM), lambda i: (order(i), 0)),
        pl.BlockSpec((WINDOW, HEAD_DIM), lambda i: (order(i), 0)),
        pl.BlockSpec((WINDOW, HEAD_DIM), lambda i: (jnp.maximum(order(i) - 1, 0), 0)),
        pl.BlockSpec((WINDOW, HEAD_DIM), lambda i: (jnp.maximum(order(i) - 1, 0), 0)),
        pl.BlockSpec((ATTN_HEADS, HEAD_DIM), lambda i: (0, 0)),
    ]


def _attn_args(q_ref, kvp_ref, kvc_ref, cc, ss, ccp, ssp, sinks):
    d = HEAD_DIM
    q = [q_ref[:, h * d:(h + 1) * d].astype(F32) for h in range(ATTN_HEADS)]
    kp = [kvp_ref[:, g * d:(g + 1) * d].astype(F32) for g in range(ATTN_KV_HEADS)]
    vp = [kvp_ref[:, ATTN_KV_WIDTH + g * d:ATTN_KV_WIDTH + (g + 1) * d].astype(F32) for g in range(ATTN_KV_HEADS)]
    kc = [kvc_ref[:, g * d:(g + 1) * d].astype(F32) for g in range(ATTN_KV_HEADS)]
    vc = [kvc_ref[:, ATTN_KV_WIDTH + g * d:ATTN_KV_WIDTH + (g + 1) * d].astype(F32) for g in range(ATTN_KV_HEADS)]
    return q + kp + kc + vp + vc + [cc[...], ss[...], ccp[...], ssp[...], sinks[...]]


def _attn_fwd(z, cc, ss, sinks_b, tag):
    S = z.shape[0]
    nb = S // WINDOW

    def body(q_ref, kvp_ref, kvc_ref, cc_r, ss_r, ccp_r, ssp_r, sink_r, o_ref):
        n = pl.program_id(0)
        outs = _attn_block_fn(n > 0)(*_attn_args(q_ref, kvp_ref, kvc_ref, cc_r, ss_r, ccp_r, ssp_r, sink_r))
        for h in range(ATTN_HEADS):
            o_ref[:, h * HEAD_DIM:(h + 1) * HEAD_DIM] = outs[h].astype(o_ref.dtype)

    return pl.pallas_call(
        body, name=f"{tag}_attn", grid=(nb,), in_specs=_attn_specs(lambda i: i),
        out_specs=pl.BlockSpec((WINDOW, ATTN_WIDTH), lambda i: (i, 0)),
        out_shape=jax.ShapeDtypeStruct((S, ATTN_WIDTH), BF16),
        compiler_params=_params(("parallel",)))(z, z, z, cc, ss, cc, ss, sinks_b)


def _attn_bwd(z, cc, ss, sinks_b, d_out, tag):
    S = z.shape[0]
    nb = S // WINDOW
    d = HEAD_DIM
    rev = lambda i: nb - 1 - i

    def body(q_ref, kvp_ref, kvc_ref, cc_r, ss_r, ccp_r, ssp_r, sink_r, do_ref, dq_ref, dkv_ref, dsink_ref, carry):
        i = pl.program_id(0)
        n = nb - 1 - i

        @pl.when(i == 0)
        def _():
            carry[...] = jnp.zeros_like(carry)
            dsink_ref[...] = jnp.zeros_like(dsink_ref)

        args = _attn_args(q_ref, kvp_ref, kvc_ref, cc_r, ss_r, ccp_r, ssp_r, sink_r)
        _, vjp = jax.vjp(_attn_block_fn(n > 0), *args)
        g = vjp(tuple(do_ref[:, h * d:(h + 1) * d].astype(F32) for h in range(ATTN_HEADS)))
        for h in range(ATTN_HEADS):
            dq_ref[:, h * d:(h + 1) * d] = g[h].astype(dq_ref.dtype)
        for gi in range(ATTN_KV_HEADS):
            ks = slice(gi * d, (gi + 1) * d)
            vs = slice(ATTN_KV_WIDTH + gi * d, ATTN_KV_WIDTH + (gi + 1) * d)
            dkv_ref[:, ks] = (g[10 + gi] + carry[:, ks]).astype(dkv_ref.dtype)
            dkv_ref[:, vs] = (g[14 + gi] + carry[:, vs]).astype(dkv_ref.dtype)
            carry[:, ks] = g[8 + gi]
            carry[:, vs] = g[12 + gi]
        ds = g[20]
        dsink_ref[...] += jnp.broadcast_to(jnp.sum(ds, axis=-1, keepdims=True), ds.shape)

    return pl.pallas_call(
        body, name=f"{tag}_attn_bwd", grid=(nb,),
        in_specs=_attn_specs(rev) + [pl.BlockSpec((WINDOW, ATTN_WIDTH), lambda i: (rev(i), 0))],
        out_specs=[pl.BlockSpec((WINDOW, ATTN_WIDTH), lambda i: (rev(i), 0)),
                   pl.BlockSpec((WINDOW, 2 * ATTN_KV_WIDTH), lambda i: (rev(i), 0)),
                   pl.BlockSpec((ATTN_HEADS, HEAD_DIM), lambda i: (0, 0))],
        out_shape=[jax.ShapeDtypeStruct((S, ATTN_WIDTH), BF16), jax.ShapeDtypeStruct((S, 2 * ATTN_KV_WIDTH), BF16),
                   jax.ShapeDtypeStruct((ATTN_HEADS, HEAD_DIM), F32)],
        scratch_shapes=[pltpu.VMEM((WINDOW, 2 * ATTN_KV_WIDTH), F32)],
        compiler_params=_params(("arbitrary",)))(z, z, z, cc, ss, cc, ss, sinks_b, d_out)


def _rope_tables(seq):
    half = HEAD_DIM // 2
    inv_freq = ROPE_THETA ** (-jnp.arange(half, dtype=F32) / half)
    ang = jnp.arange(seq, dtype=F32)[:, None] * inv_freq[None, :]
    cos, sin = jnp.cos(ang), jnp.sin(ang)
    return jnp.concatenate([cos, cos], axis=1), jnp.concatenate([-sin, sin], axis=1)


CONV_COLS = 128


def _conv_pre(u, w_ref, S):
    row = lax.broadcasted_iota(jnp.int32, u.shape, 0)
    shifted = [u] + [jnp.where(row >= s, pltpu.roll(u, s, 0), 0.0) for s in range(1, DN_CONV)]
    y = shifted[0] * w_ref[DN_CONV - 1:DN_CONV, :]
    for s in range(1, DN_CONV):
        y = y + shifted[s] * w_ref[DN_CONV - 1 - s:DN_CONV - s, :]
    return y, shifted, row


def _conv_fwd(z, conv_w, tag):
    S = z.shape[0]
    ncol = 3 * DN_WIDTH // CONV_COLS

    def body(u_ref, w_ref, o_ref):
        y, _, _ = _conv_pre(u_ref[...], w_ref, S)
        o_ref[...] = y * jax.nn.sigmoid(y)

    return pl.pallas_call(
        body, name=f"{tag}_conv", grid=(ncol,),
        in_specs=[pl.BlockSpec((S, CONV_COLS), lambda j: (0, Z_DQKV // CONV_COLS + j)),
                  pl.BlockSpec((DN_CONV, CONV_COLS), lambda j: (0, j))],
        out_specs=pl.BlockSpec((S, CONV_COLS), lambda j: (0, j)),
        out_shape=jax.ShapeDtypeStruct((S, 3 * DN_WIDTH), F32),
        compiler_params=_params(("parallel",)))(z, conv_w)


def _conv_bwd(z, conv_w, d_out, tag):
    S = z.shape[0]
    ncol = 3 * DN_WIDTH // CONV_COLS

    def body(u_ref, w_ref, do_ref, du_ref, dw_ref):
        y, shifted, row = _conv_pre(u_ref[...], w_ref, S)
        sg = jax.nn.sigmoid(y)
        d_y = do_ref[...] * (sg * (1.0 + y * (1.0 - sg)))
        d_u = d_y * w_ref[DN_CONV - 1:DN_CONV, :]
        dw_ref[DN_CONV - 1:DN_CONV, :] = jnp.sum(d_y * shifted[0], axis=0, keepdims=True)
        for s in range(1, DN_CONV):
            back = jnp.where(row < S - s, pltpu.roll(d_y, S - s, 0), 0.0)
            d_u = d_u + back * w_ref[DN_CONV - 1 - s:DN_CONV - s, :]
            dw_ref[DN_CONV - 1 - s:DN_CONV - s, :] = jnp.sum(d_y * shifted[s], axis=0, keepdims=True)
        du_ref[...] = d_u.astype(du_ref.dtype)

    return pl.pallas_call(
        body, name=f"{tag}_conv_bwd", grid=(ncol,),
        in_specs=[pl.BlockSpec((S, CONV_COLS), lambda j: (0, Z_DQKV // CONV_COLS + j)),
                  pl.BlockSpec((DN_CONV, CONV_COLS), lambda j: (0, j)),
                  pl.BlockSpec((S, CONV_COLS), lambda j: (0, j))],
        out_specs=[pl.BlockSpec((S, CONV_COLS), lambda j: (0, j)), pl.BlockSpec((DN_CONV, CONV_COLS), lambda j: (0, j))],
        out_shape=[jax.ShapeDtypeStruct((S, 3 * DN_WIDTH), BF16), jax.ShapeDtypeStruct((DN_CONV, 3 * DN_WIDTH), F32)],
        compiler_params=_params(("parallel",)))(z, conv_w, d_out)


_NN = (((1,), (0,)), ((), ()))
_NT = (((1,), (1,)), ((), ()))
_TN = (((0,), (0,)), ((), ()))


def _dot3(a, b, dims):
    a_hi, b_hi = a.astype(BF16), b.astype(BF16)
    a_lo, b_lo = (a - a_hi.astype(F32)).astype(BF16), (b - b_hi.astype(F32)).astype(BF16)
    mm = lambda p, q: lax.dot_general(p, q, dims, preferred_element_type=F32)
    return mm(a_hi, b_hi) + (mm(a_hi, b_lo) + mm(a_lo, b_hi))


@functools.partial(jax.custom_vjp, nondiff_argnums=(2,))
def _dot_vjp(a, b, dims):
    return _dot3(a, b, dims)


def _dot_vjp_fwd(a, b, dims):
    return _dot3(a, b, dims), (a, b)


_BATCH = ((0,), (0,))
_BNN, _BNT, _BTN = (((2,), (1,)), _BATCH), (((2,), (2,)), _BATCH), (((1,), (1,)), _BATCH)


def _dot_vjp_bwd(dims, res, g):
    a, b = res
    nn, nt, tn = (_NN, _NT, _TN) if dims in (_NN, _NT, _TN) else (_BNN, _BNT, _BTN)
    if dims == nn:
        return _dot3(g, b, nt), _dot3(a, g, tn)
    if dims == nt:
        return _dot3(g, b, nn), _dot3(g, a, tn)
    return _dot3(b, g, nt), _dot3(a, g, nn)


_dot_vjp.defvjp(_dot_vjp_fwd, _dot_vjp_bwd)


def _dot(a, b, dims=_NN):
    return _dot_vjp(a, b, dims)


def _dot1(a, b, dims=_NN):
    return lax.dot_general(a.astype(BF16), b.astype(BF16), dims, preferred_element_type=F32)


def _dn_chunk_fn(state, q, k, v, zg, ba, a_log, dtb, norm_w):
    H, C, dk = DN_HEADS, DN_CHUNK, DN_HEAD_DIM
    head = lax.broadcasted_iota(jnp.int32, (H, C, dk), 0)
    rowl = lax.broadcasted_iota(jnp.int32, (H, C, dk), 1)
    lane = lax.broadcasted_iota(jnp.int32, (H, C, dk), 2)
    row = lax.broadcasted_iota(jnp.int32, (H, C, C), 1)
    col = lax.broadcasted_iota(jnp.int32, (H, C, C), 2)
    ba3 = jnp.broadcast_to(ba[None], (H, C, dk))
    bcol = jnp.sum(jnp.where(lane == head, ba3, 0.0), axis=-1, keepdims=True)
    acol = jnp.sum(jnp.where(lane == H + head, ba3, 0.0), axis=-1, keepdims=True)
    qn = q * lax.rsqrt(jnp.sum(q * q, axis=-1, keepdims=True) + NORM_EPS) * (dk ** -0.5)
    kn = k * lax.rsqrt(jnp.sum(k * k, axis=-1, keepdims=True) + NORM_EPS)
    beta = jax.nn.sigmoid(bcol)
    sp_in = acol + dtb
    softplus = jnp.maximum(sp_in, 0.0) + jnp.log(1.0 + jnp.exp(-jnp.abs(sp_in)))
    gt = -jnp.exp(a_log) * softplus
    gc = _dot((row >= col).astype(F32), gt, _BNN)
    gcol = jnp.mean(gc, axis=-1, keepdims=True)
    grow = _dot(jnp.full((H, C, dk), 1.0 / dk, F32), gc, _BNT)
    decay = jnp.exp(jnp.where(row >= col, gcol - grow, -jnp.inf))
    kb = kn * beta
    m = -jnp.where(row > col, _dot(kb, kn, _BNT) * decay, 0.0)
    u = v * beta
    w = kb * jnp.exp(gc)
    for step in range(6):
        u = u + _dot(m, u, _BNN)
        w = w + _dot(m, w, _BNN)
        if step < 5:
            m = _dot(m, m, _BNN)
    attn = jnp.where(row >= col, _dot(qn, kn, _BNT) * decay, 0.0)
    q_dec = qn * jnp.exp(gc)
    gl = jnp.sum(jnp.where(rowl == C - 1, gc, 0.0), axis=1, keepdims=True)
    k_dec = kn * jnp.exp(gl - gc)
    v_new = u - _dot(w, state, _BNN)
    o = _dot(q_dec, state, _BNN) + _dot(attn, v_new, _BNN)
    state_new = state * jnp.exp(gl) + _dot(k_dec, v_new, _BTN)
    y = o * lax.rsqrt(jnp.mean(o * o, axis=-1, keepdims=True) + NORM_EPS) * norm_w
    y = y * (zg * jax.nn.sigmoid(zg))
    return state_new, y


def _dn_specs(order):
    C = DN_CHUNK
    return [pl.BlockSpec((C, 3 * DN_WIDTH), lambda i: (order(i), 0)),
            pl.BlockSpec((C, DN_WIDTH), lambda i: (order(i), Z_DZ // DN_WIDTH)),
            pl.BlockSpec((C, 128), lambda i: (order(i), Z_DBA // 128)),
            pl.BlockSpec((DN_HEADS, 1, DN_HEAD_DIM), lambda i: (0, 0, 0)),
            pl.BlockSpec((DN_HEADS, 1, DN_HEAD_DIM), lambda i: (0, 0, 0)),
            pl.BlockSpec((1, 1, DN_HEAD_DIM), lambda i: (0, 0, 0))]


def _dn_heads(ref, base=0):
    d = DN_HEAD_DIM
    return jnp.stack([ref[:, base + h * d:base + (h + 1) * d].astype(F32) for h in range(DN_HEADS)], axis=0)


def _dn_args(qkv_ref, zg_ref, ba_ref, alog_ref, dtb_ref, nw_ref):
    return [_dn_heads(qkv_ref), _dn_heads(qkv_ref, DN_WIDTH), _dn_heads(qkv_ref, 2 * DN_WIDTH), _dn_heads(zg_ref),
            ba_ref[...].astype(F32), alog_ref[...], dtb_ref[...], nw_ref[...]]


def _dn_fwd(qkv, z, a_log, dtb, norm_w, tag):
    S = qkv.shape[0]
    nchunk = S // DN_CHUNK
    d = DN_HEAD_DIM

    def body(qkv_ref, zg_ref, ba_ref, alog_ref, dtb_ref, nw_ref, y_ref, st_ref, state):
        @pl.when(pl.program_id(0) == 0)
        def _():
            state[...] = jnp.zeros_like(state)

        st_ref[...] = state[...]
        new, y = _dn_chunk_fn(state[...], *_dn_args(qkv_ref, zg_ref, ba_ref, alog_ref, dtb_ref, nw_ref))
        state[...] = new
        for h in range(DN_HEADS):
            y_ref[:, h * d:(h + 1) * d] = y[h].astype(y_ref.dtype)

    return pl.pallas_call(
        body, name=f"{tag}_deltanet", grid=(nchunk,), in_specs=_dn_specs(lambda i: i),
        out_specs=[pl.BlockSpec((DN_CHUNK, DN_WIDTH), lambda i: (i, 0)),
                   pl.BlockSpec((None, DN_HEADS, d, d), lambda i: (i, 0, 0, 0))],
        out_shape=[jax.ShapeDtypeStruct((S, DN_WIDTH), BF16), jax.ShapeDtypeStruct((nchunk, DN_HEADS, d, d), F32)],
        scratch_shapes=[pltpu.VMEM((DN_HEADS, d, d), F32)],
        compiler_params=_params(("arbitrary",)))(qkv, z, z, a_log, dtb, norm_w)


def _dn_bwd(qkv, z, a_log, dtb, norm_w, states, d_y, tag):
    S = qkv.shape[0]
    nchunk = S // DN_CHUNK
    d = DN_HEAD_DIM
    rev = lambda i: nchunk - 1 - i

    def body(qkv_ref, zg_ref, ba_ref, alog_ref, dtb_ref, nw_ref, st_ref, dy_ref,
             dqkv_ref, dzg_ref, dba_ref, dalog_ref, ddtb_ref, dnw_ref, d_state):
        @pl.when(pl.program_id(0) == 0)
        def _():
            d_state[...] = jnp.zeros_like(d_state)
            dalog_ref[...] = jnp.zeros_like(dalog_ref)
            ddtb_ref[...] = jnp.zeros_like(ddtb_ref)
            dnw_ref[...] = jnp.zeros_like(dnw_ref)

        args = [st_ref[...]] + _dn_args(qkv_ref, zg_ref, ba_ref, alog_ref, dtb_ref, nw_ref)
        _, vjp = jax.vjp(_dn_chunk_fn, *args)
        g = vjp((d_state[...], _dn_heads(dy_ref)))
        d_state[...] = g[0]
        for h in range(DN_HEADS):
            for n, base in enumerate((0, DN_WIDTH, 2 * DN_WIDTH)):
                dqkv_ref[:, base + h * d:base + (h + 1) * d] = g[1 + n][h]
            dzg_ref[:, h * d:(h + 1) * d] = g[4][h].astype(dzg_ref.dtype)
        dba_ref[...] = g[5].astype(dba_ref.dtype)
        lane_sum = lambda t: jnp.broadcast_to(jnp.sum(t, axis=-1, keepdims=True), t.shape)
        dalog_ref[...] += lane_sum(g[6])
        ddtb_ref[...] += lane_sum(g[7])
        dnw_ref[...] += g[8]

    hspec = pl.BlockSpec((DN_HEADS, 1, d), lambda i: (0, 0, 0))
    return pl.pallas_call(
        body, name=f"{tag}_deltanet_bwd", grid=(nchunk,),
        in_specs=_dn_specs(rev) + [pl.BlockSpec((None, DN_HEADS, d, d), lambda i: (rev(i), 0, 0, 0)),
                                   pl.BlockSpec((DN_CHUNK, DN_WIDTH), lambda i: (rev(i), 0))],
        out_specs=[pl.BlockSpec((DN_CHUNK, 3 * DN_WIDTH), lambda i: (rev(i), 0)),
                   pl.BlockSpec((DN_CHUNK, DN_WIDTH), lambda i: (rev(i), 0)),
                   pl.BlockSpec((DN_CHUNK, 128), lambda i: (rev(i), 0)),
                   hspec, hspec, pl.BlockSpec((1, 1, d), lambda i: (0, 0, 0))],
        out_shape=[jax.ShapeDtypeStruct((S, 3 * DN_WIDTH), F32), jax.ShapeDtypeStruct((S, DN_WIDTH), BF16),
                   jax.ShapeDtypeStruct((S, 128), BF16), jax.ShapeDtypeStruct((DN_HEADS, 1, d), F32),
                   jax.ShapeDtypeStruct((DN_HEADS, 1, d), F32), jax.ShapeDtypeStruct((1, 1, d), F32)],
        scratch_shapes=[pltpu.VMEM((DN_HEADS, d, d), F32)],
        compiler_params=_params(("arbitrary",)))(qkv, z, z, a_log, dtb, norm_w, states, d_y)


def _whole_fwd(fn, ins, outs, *, name):
    n = len(ins)

    def body(*refs):
        res = fn(*[r[...] for r in refs[:n]])
        for o_ref, o in zip(refs[n:], res):
            o_ref[...] = o

    return pl.pallas_call(body, name=name, out_shape=[jax.ShapeDtypeStruct(s, F32) for s in outs],
                          compiler_params=_params())(*ins)


def _whole_bwd(fn, ins, cts, n_grads, *, name, lane_sum=()):
    n, nt = len(ins), len(cts)

    def body(*refs):
        _, vjp = jax.vjp(fn, *[r[...] for r in refs[:n]])
        grads = vjp(tuple(r[...] for r in refs[n:n + nt]))
        for k in range(n_grads):
            g = grads[k]
            if k in lane_sum:
                g = jnp.broadcast_to(jnp.sum(g, axis=-1, keepdims=True), g.shape)
            refs[n + nt + k][...] = g

    return pl.pallas_call(body, name=name, out_shape=[jax.ShapeDtypeStruct(a.shape, F32) for a in ins[:n_grads]],
                          compiler_params=_params())(*ins, *cts)


S5_CHUNK = 256


def _s5_param_fn(a_re, a_im, ldt, bt_re, bt_im, expand):
    dt = jnp.exp(ldt)
    er = jnp.exp(a_re * dt)
    ab_re, ab_im = er * jnp.cos(a_im * dt), er * jnp.sin(a_im * dt)
    den = a_re * a_re + a_im * a_im
    co_re = ((ab_re - 1.0) * a_re + ab_im * a_im) / den
    co_im = (ab_im * a_re - (ab_re - 1.0) * a_im) / den
    cr, ci = _dot(expand, co_re), _dot(expand, co_im)
    return ab_re, ab_im, cr * bt_re - ci * bt_im, cr * bt_im + ci * bt_re


def _s5_scan(b_re, b_im, a_re, a_im, row, T, reverse):
    x_re, x_im, p_re, p_im = b_re, b_im, a_re, a_im
    d = 1
    while d < T:
        if reverse:
            s_re = jnp.where(row < T - d, pltpu.roll(x_re, T - d, 0), 0.0)
            s_im = jnp.where(row < T - d, pltpu.roll(x_im, T - d, 0), 0.0)
        else:
            s_re = jnp.where(row >= d, pltpu.roll(x_re, d, 0), 0.0)
            s_im = jnp.where(row >= d, pltpu.roll(x_im, d, 0), 0.0)
        x_re, x_im = x_re + p_re * s_re - p_im * s_im, x_im + p_re * s_im + p_im * s_re
        p_re, p_im = p_re * p_re - p_im * p_im, 2.0 * p_re * p_im
        d *= 2
    return x_re, x_im


def _s5_states(u, bre_ref, bim_ref, a_re, a_im, c_re, c_im, row, T):
    bu_re = _dot(u, bre_ref[...]) + jnp.where(row == 0, a_re * c_re - a_im * c_im, 0.0)
    bu_im = _dot(u, bim_ref[...]) + jnp.where(row == 0, a_re * c_im + a_im * c_re, 0.0)
    return _s5_scan(bu_re, bu_im, a_re, a_im, row, T, False)


def _s5_in_specs(order, T):
    full = lambda shape: pl.BlockSpec(shape, lambda i: (0,) * len(shape))
    return [pl.BlockSpec((T, S5_WIDTH), lambda i: (order(i), Z_SU // S5_WIDTH)),
            full((S5_WIDTH, S5_LANES)), full((S5_WIDTH, S5_LANES)), full((S5_WIDTH, S5_LANES)),
            full((S5_WIDTH, S5_LANES)), full((1, S5_LANES)), full((1, S5_LANES)), full((1, S5_WIDTH))]


def _s5_fwd(z, bre, bim, cre, cim, ab_re, ab_im, dskip, tag):
    S = z.shape[0]
    T = _pick(S, (S5_CHUNK, 128))
    nch = S // T

    def body(u_ref, bre_ref, bim_ref, cre_ref, cim_ref, are_ref, aim_ref, d_ref, y_ref, kre_ref, kim_ref, c_re, c_im):
        @pl.when(pl.program_id(0) == 0)
        def _():
            c_re[...] = jnp.zeros_like(c_re)
            c_im[...] = jnp.zeros_like(c_im)

        kre_ref[...] = c_re[...]
        kim_ref[...] = c_im[...]
        u = u_ref[...]
        row = lax.broadcasted_iota(jnp.int32, (T, S5_LANES), 0)
        x_re, x_im = _s5_states(u, bre_ref, bim_ref, are_ref[...], aim_ref[...], c_re[...], c_im[...], row, T)
        c_re[...] = jnp.sum(jnp.where(row == T - 1, x_re, 0.0), axis=0, keepdims=True)
        c_im[...] = jnp.sum(jnp.where(row == T - 1, x_im, 0.0), axis=0, keepdims=True)
        y_ref[...] = _dot(x_re, cre_ref[...], _NT) - _dot(x_im, cim_ref[...], _NT) + d_ref[...] * u

    return pl.pallas_call(
        body, name=f"{tag}_s5", grid=(nch,), in_specs=_s5_in_specs(lambda i: i, T),
        out_specs=[pl.BlockSpec((T, S5_WIDTH), lambda i: (i, 0)),
                   pl.BlockSpec((None, 1, S5_LANES), lambda i: (i, 0, 0)),
                   pl.BlockSpec((None, 1, S5_LANES), lambda i: (i, 0, 0))],
        out_shape=[jax.ShapeDtypeStruct((S, S5_WIDTH), F32), jax.ShapeDtypeStruct((nch, 1, S5_LANES), F32),
                   jax.ShapeDtypeStruct((nch, 1, S5_LANES), F32)],
        scratch_shapes=[pltpu.VMEM((1, S5_LANES), F32), pltpu.VMEM((1, S5_LANES), F32)],
        compiler_params=_params(("arbitrary",)))(z, bre, bim, cre, cim, ab_re, ab_im, dskip)


def _s5_bwd(z, bre, bim, cre, cim, ab_re, ab_im, dskip, kre, kim, d_y, tag):
    S = z.shape[0]
    T = _pick(S, (S5_CHUNK, 128))
    nch = S // T
    rev = lambda i: nch - 1 - i
    full = lambda shape: pl.BlockSpec(shape, lambda i: (0,) * len(shape))

    def body(u_ref, bre_ref, bim_ref, cre_ref, cim_ref, are_ref, aim_ref, d_ref, kre_ref, kim_ref, dy_ref,
             du_ref, dbre_ref, dbim_ref, dcre_ref, dcim_ref, dare_ref, daim_ref, dd_ref, g_re, g_im):
        @pl.when(pl.program_id(0) == 0)
        def _():
            g_re[...] = jnp.zeros_like(g_re)
            g_im[...] = jnp.zeros_like(g_im)
            for r in (dbre_ref, dbim_ref, dcre_ref, dcim_ref, dare_ref, daim_ref, dd_ref):
                r[...] = jnp.zeros_like(r)

        u = u_ref[...]
        dy = dy_ref[...].astype(F32)
        a_re, a_im = are_ref[...], aim_ref[...]
        row = lax.broadcasted_iota(jnp.int32, (T, S5_LANES), 0)
        x_re, x_im = _s5_states(u, bre_ref, bim_ref, a_re, a_im, kre_ref[...], kim_ref[...], row, T)
        dcre_ref[...] += _dot1(dy, x_re, _TN)
        dcim_ref[...] -= _dot1(dy, x_im, _TN)
        xp_re = jnp.where(row >= 1, pltpu.roll(x_re, 1, 0), 0.0) + jnp.where(row == 0, kre_ref[...], 0.0)
        xp_im = jnp.where(row >= 1, pltpu.roll(x_im, 1, 0), 0.0) + jnp.where(row == 0, kim_ref[...], 0.0)
        last = row == T - 1
        gd_re = _dot(dy, cre_ref[...]) + jnp.where(last, a_re * g_re[...] + a_im * g_im[...], 0.0)
        gd_im = -_dot(dy, cim_ref[...]) + jnp.where(last, a_re * g_im[...] - a_im * g_re[...], 0.0)
        t_re, t_im = _s5_scan(gd_re, gd_im, a_re, -a_im, row, T, True)
        g_re[...] = jnp.sum(jnp.where(row == 0, t_re, 0.0), axis=0, keepdims=True)
        g_im[...] = jnp.sum(jnp.where(row == 0, t_im, 0.0), axis=0, keepdims=True)
        du_ref[...] = (_dot1(t_re, bre_ref[...], _NT) + _dot1(t_im, bim_ref[...], _NT) + dy * d_ref[...]).astype(du_ref.dtype)
        dbre_ref[...] += _dot1(u, t_re, _TN)
        dbim_ref[...] += _dot1(u, t_im, _TN)
        dare_ref[...] += jnp.sum(t_re * xp_re + t_im * xp_im, axis=0, keepdims=True)
        daim_ref[...] += jnp.sum(t_im * xp_re - t_re * xp_im, axis=0, keepdims=True)
        dd_ref[...] += jnp.sum(dy * u, axis=0, keepdims=True)

    return pl.pallas_call(
        body, name=f"{tag}_s5_bwd", grid=(nch,),
        in_specs=_s5_in_specs(rev, T) + [pl.BlockSpec((None, 1, S5_LANES), lambda i: (rev(i), 0, 0)),
                                         pl.BlockSpec((None, 1, S5_LANES), lambda i: (rev(i), 0, 0)),
                                         pl.BlockSpec((T, S5_WIDTH), lambda i: (rev(i), 0))],
        out_specs=[pl.BlockSpec((T, S5_WIDTH), lambda i: (rev(i), 0))] + [full((S5_WIDTH, S5_LANES))] * 4
        + [full((1, S5_LANES))] * 2 + [full((1, S5_WIDTH))],
        out_shape=[jax.ShapeDtypeStruct((S, S5_WIDTH), BF16)] + [jax.ShapeDtypeStruct((S5_WIDTH, S5_LANES), F32)] * 4
        + [jax.ShapeDtypeStruct((1, S5_LANES), F32)] * 2 + [jax.ShapeDtypeStruct((1, S5_WIDTH), F32)],
        scratch_shapes=[pltpu.VMEM((1, S5_LANES), F32), pltpu.VMEM((1, S5_LANES), F32)],
        compiler_params=_params(("arbitrary",)))(z, bre, bim, cre, cim, ab_re, ab_im, dskip, kre, kim, d_y)


def _s5_glu_fn(y, glu_w, glu_b):
    g = 0.5 * y * (1.0 + jnp.tanh(math.sqrt(2.0 / math.pi) * (y + 0.044715 * (y * y * y))))
    lin = lax.dot_general(g.astype(BF16), glu_w.astype(BF16), (((1,), (0,)), ((), ())), preferred_element_type=F32)
    return (g * jax.nn.sigmoid(lin + glu_b),)


def _block_diag(m):
    G, H, P = S5_GROUPS, S5_GROUP_CH, S5_STATE
    eye = jnp.eye(G, dtype=m.dtype)
    return (m.reshape(G, H, 1, P) * eye[:, None, :, None]).reshape(G * H, G * P)


def _block_diag_take(m):
    G, H, P = S5_GROUPS, S5_GROUP_CH, S5_STATE
    eye = jnp.eye(G, dtype=m.dtype)
    return jnp.sum(m.reshape(G, H, G, P) * eye[:, None, :, None], axis=2).reshape(G * H, P)


def _loss_head(y, target, tag):
    S, D = y.shape
    tm = _pick(S, (256, 128, 64))

    def body(y_ref, t_ref, dy_ref, loss_ref):
        @pl.when(pl.program_id(0) == 0)
        def _():
            loss_ref[...] = jnp.zeros_like(loss_ref)

        err = y_ref[...] - t_ref[...]
        dy_ref[...] = err * (1.0 / D)
        part = 0.5 * jnp.sum(jnp.mean(err * err, axis=-1, keepdims=True), axis=0, keepdims=True)
        loss_ref[...] += jnp.broadcast_to(part, loss_ref.shape)

    return pl.pallas_call(
        body, name=f"{tag}_loss", grid=(S // tm,),
        in_specs=[pl.BlockSpec((tm, D), lambda i: (i, 0)), pl.BlockSpec((tm, D), lambda i: (i, 0))],
        out_specs=[pl.BlockSpec((tm, D), lambda i: (i, 0)), pl.BlockSpec((8, 128), lambda i: (0, 0))],
        out_shape=[jax.ShapeDtypeStruct((S, D), F32), jax.ShapeDtypeStruct((8, 128), F32)],
        compiler_params=_params(("arbitrary",)))(y, target)


def _adamw(w, g, m, v, name):
    shape = w.shape
    cols = shape[-1]
    rows = int(np.prod(shape[:-1]))
    tr = _pick(rows, [t for t in (512, 256, 128, 64, 32, 16, 8) if t * cols <= 256 * 1024] or [8])
    c1 = 1.0 - ADAM_B1 ** ADAM_STEP
    c2 = 1.0 - ADAM_B2 ** ADAM_STEP

    def body(w_ref, g_ref, m_ref, v_ref, d_ref, mo_ref, vo_ref):
        gg = g_ref[...]
        mn = ADAM_B1 * m_ref[...] + (1.0 - ADAM_B1) * gg
        vn = ADAM_B2 * v_ref[...] + (1.0 - ADAM_B2) * (gg * gg)
        d_ref[...] = -ADAM_LR * ((mn / c1) / (jnp.sqrt(vn / c2) + ADAM_EPS) + ADAM_WD * w_ref[...])
        mo_ref[...] = mn
        vo_ref[...] = vn

    spec = pl.BlockSpec((tr, cols), lambda i: (i, 0))
    outs = pl.pallas_call(
        body, name=name, grid=(rows // tr,), in_specs=[spec] * 4, out_specs=[spec] * 3,
        out_shape=[jax.ShapeDtypeStruct((rows, cols), F32)] * 3,
        compiler_params=_params(("parallel",)))(*[a.reshape(rows, cols) for a in (w, g, m, v)])
    return [o.reshape(shape) for o in outs]


def _sum_slots(slots, name):
    n, R, C = slots.shape
    tr = _pick(R, [t for t in (1024, 512, 256, 128, 64, 32, 16, 8) if n * t * C * 4 <= (2 << 20)] or [8])

    def body(s_ref, o_ref):
        acc = s_ref[0].astype(F32)
        for k in range(1, n):
            acc = acc + s_ref[k].astype(F32)
        o_ref[...] = acc

    return pl.pallas_call(
        body, name=name, grid=(R // tr,), in_specs=[pl.BlockSpec((n, tr, C), lambda i: (0, i, 0))],
        out_specs=pl.BlockSpec((tr, C), lambda i: (i, 0)),
        out_shape=jax.ShapeDtypeStruct((R, C), F32), compiler_params=_params(("parallel",)))(slots)


def _row_tile(R, C, itemsize, target=1 << 20):
    return _pick(R, [t for t in (2048, 1024, 512, 256, 128, 64, 32, 16) if t * C * itemsize <= target] or [16])


def _add_sibling(part, got, name):
    n, _, R2, C = part.shape
    tr = _row_tile(R2, C, 2)

    def body(p_ref, q_ref, o_ref):
        o_ref[...] = (p_ref[...].astype(F32) + q_ref[...].astype(F32)).astype(o_ref.dtype)

    spec = pl.BlockSpec((None, tr, C), lambda k, i: (k, i, 0))
    return pl.pallas_call(
        body, name=name, grid=(n, R2 // tr),
        in_specs=[pl.BlockSpec((None, None, tr, C), lambda k, i: (k, lax.axis_index("c"), i, 0)), spec], out_specs=spec,
        out_shape=jax.ShapeDtypeStruct((n, R2, C), BF16), compiler_params=_params(("parallel", "parallel")))(part, got)


def _sum_chips(chip_part, others, name):
    _, R2, C = chip_part.shape
    tr = _row_tile(R2, C, 4)

    def body(a_ref, b_ref, o_ref):
        acc = a_ref[...].astype(F32)
        for j in range(N_SHARD - 1):
            acc = acc + b_ref[j].astype(F32)
        o_ref[...] = acc

    return pl.pallas_call(
        body, name=name, grid=(R2 // tr,),
        in_specs=[pl.BlockSpec((None, tr, C), lambda i: (2 * lax.axis_index("x") + lax.axis_index("y"), i, 0)),
                  pl.BlockSpec((N_SHARD - 1, tr, C), lambda i: (0, i, 0))],
        out_specs=pl.BlockSpec((None, tr, C), lambda i: (lax.axis_index("c"), i, 0)),
        out_shape=jax.ShapeDtypeStruct((2, R2, C), F32), compiler_params=_params(("parallel",)))(chip_part, others)


_ANY = pl.BlockSpec(memory_space=pl.ANY)


def _place():
    return lax.axis_index("x"), lax.axis_index("y"), lax.axis_index("c")


def _other_chips(x, y):
    return [(1 - x, y), (x, 1 - y), (1 - x, 1 - y)]


def _remote(src, dst, send_sems, recv_sems, n, to):
    return pltpu.make_async_remote_copy(src_ref=src, dst_ref=dst, send_sem=send_sems.at[n], recv_sem=recv_sems.at[n],
                                        device_id=to, device_id_type=MESH_ID)


_JOB_SEMS = {"gather_ici": 3, "gather_d2d": 4, "scatter": 3}


def _job_out_shape(job):
    src = job["ins"][0]
    if job["kind"] == "gather_ici":
        return jax.ShapeDtypeStruct((N_SHARD,) + src.shape, src.dtype)
    if job["kind"] == "gather_d2d":
        return jax.ShapeDtypeStruct(src.shape, src.dtype)
    return jax.ShapeDtypeStruct((N_SHARD - 1,) + src.shape[1:], src.dtype)


def _job_copies(kind, ins, out, send_sems, recv_sems, base):
    x, y, c = _place()
    k = 2 * x + y
    sibling = (x, y, 1 - c)
    sends, recvs = [], []
    for j, (cx, cy) in enumerate(_other_chips(x, y)):
        kj = 2 * cx + cy
        if kind == "gather_ici":
            sends.append(_remote(ins[0].at[c], out.at[k, c], send_sems, recv_sems, base + j, (cx, cy, c)))
            recvs.append(_remote(out.at[kj, c], out.at[kj, c], send_sems, recv_sems, base + j, sibling))
        elif kind == "gather_d2d":
            sends.append(_remote(out.at[kj, c], out.at[kj, c], send_sems, recv_sems, base + 1 + j, sibling))
            recvs.append(_remote(out.at[kj, 1 - c], out.at[kj, 1 - c], send_sems, recv_sems, base + 1 + j, sibling))
        else:
            sends.append(_remote(ins[0].at[kj], out.at[j], send_sems, recv_sems, base + j, (cx, cy, c)))
            recvs.append(_remote(out.at[j], out.at[j], send_sems, recv_sems, base + j, sibling))
    if kind == "gather_d2d":
        sends.append(_remote(ins[1], out.at[k], send_sems, recv_sems, base, sibling))
        recvs.append(_remote(out.at[k], out.at[k], send_sems, recv_sems, base, sibling))
    return sends, recvs


def _gather_shards(shards, name):
    n = len(shards)
    per = 7

    def body(*refs):
        ins, outs = refs[:n], refs[n:2 * n]
        send_sems, recv_sems = refs[2 * n:]
        x, y, c = _place()
        k = 2 * x + y
        sibling = (x, y, 1 - c)
        chips = _other_chips(x, y)
        started = []
        for a in range(n):
            cp = _remote(ins[a], outs[a].at[k], send_sems, recv_sems, per * a, sibling)
            cp.start()
            started.append(cp)
            for j, (cx, cy) in enumerate(chips):
                cp = _remote(ins[a].at[c], outs[a].at[k, c], send_sems, recv_sems, per * a + 1 + j, (cx, cy, c))
                cp.start()
                started.append(cp)
        for a in range(n):
            for j, (cx, cy) in enumerate(chips):
                landed = outs[a].at[2 * cx + cy, c]
                _remote(landed, landed, send_sems, recv_sems, per * a + 1 + j, sibling).wait_recv()
                cp = _remote(landed, landed, send_sems, recv_sems, per * a + 4 + j, sibling)
                cp.start()
                started.append(cp)
        for a in range(n):
            own = outs[a].at[k]
            _remote(own, own, send_sems, recv_sems, per * a, sibling).wait_recv()
            for j, (cx, cy) in enumerate(chips):
                passed = outs[a].at[2 * cx + cy, 1 - c]
                _remote(passed, passed, send_sems, recv_sems, per * a + 4 + j, sibling).wait_recv()
        for cp in started:
            cp.wait_send()

    return pl.pallas_call(
        body, name=name, in_specs=[_ANY] * n, out_specs=[_ANY] * n,
        out_shape=[jax.ShapeDtypeStruct((N_SHARD,) + s.shape, s.dtype) for s in shards],
        scratch_shapes=[pltpu.SemaphoreType.DMA((per * n,)), pltpu.SemaphoreType.DMA((per * n,))],
        )(*shards)


def _swap_other_half(parts, name):
    n = len(parts)

    def body(*refs):
        ins, outs = refs[:n], refs[n:2 * n]
        send_sems, recv_sems = refs[2 * n:]
        x, y, c = _place()
        started = []
        for a in range(n):
            for k in range(N_SHARD):
                cp = _remote(ins[a].at[k, 1 - c], outs[a].at[k], send_sems, recv_sems, N_SHARD * a + k, (x, y, 1 - c))
                cp.start()
                started.append(cp)
        for cp in started:
            cp.wait()

    return pl.pallas_call(
        body, name=name, in_specs=[_ANY] * n, out_specs=[_ANY] * n,
        out_shape=[jax.ShapeDtypeStruct((N_SHARD,) + p.shape[2:], p.dtype) for p in parts],
        scratch_shapes=[pltpu.SemaphoreType.DMA((N_SHARD * n,)), pltpu.SemaphoreType.DMA((N_SHARD * n,))],
        )(*parts)


def _scatter_to_chips(parts, name):
    n = len(parts)
    per = N_SHARD - 1

    def body(*refs):
        ins, outs = refs[:n], refs[n:2 * n]
        send_sems, recv_sems = refs[2 * n:]
        x, y, c = _place()
        chips = _other_chips(x, y)
        started = []
        for a in range(n):
            for j, (cx, cy) in enumerate(chips):
                cp = _remote(ins[a].at[2 * cx + cy], outs[a].at[j], send_sems, recv_sems, per * a + j, (cx, cy, c))
                cp.start()
                started.append(cp)
        for cp in started:
            cp.wait()

    return pl.pallas_call(
        body, name=name, in_specs=[_ANY] * n, out_specs=[_ANY] * n,
        out_shape=[jax.ShapeDtypeStruct((per,) + p.shape[1:], p.dtype) for p in parts],
        scratch_shapes=[pltpu.SemaphoreType.DMA((per * n,)), pltpu.SemaphoreType.DMA((per * n,))],
        )(*parts)


def _join_halves(bufs, name):
    n = len(bufs)

    def body(*refs):
        outs = refs[n:2 * n]
        send_sems, recv_sems = refs[2 * n:]
        x, y, c = _place()
        started = []
        for a in range(n):
            cp = _remote(outs[a].at[c], outs[a].at[c], send_sems, recv_sems, a, (x, y, 1 - c))
            cp.start()
            started.append(cp)
        for a in range(n):
            arrives = outs[a].at[1 - c]
            _remote(arrives, arrives, send_sems, recv_sems, a, (x, y, 1 - c)).wait_recv()
        for cp in started:
            cp.wait_send()

    return pl.pallas_call(
        body, name=name, in_specs=[_ANY] * n, out_specs=[_ANY] * n,
        out_shape=[jax.ShapeDtypeStruct(b.shape, b.dtype) for b in bufs],
        input_output_aliases={a: a for a in range(n)},
        scratch_shapes=[pltpu.SemaphoreType.DMA((n,)), pltpu.SemaphoreType.DMA((n,))],
        )(*bufs)


def _gather_all_devices(vec, name):
    def body(in_ref, out_ref, send_sems, recv_sems, local_sem):
        x, y, c = _place()
        me = 4 * x + 2 * y + c
        own = pltpu.make_async_copy(in_ref, out_ref.at[me], local_sem)
        own.start()
        sends = []
        for r in range(1, 8):
            fx, fy, fc = (r >> 2) & 1, (r >> 1) & 1, r & 1
            to = (x ^ fx, y ^ fy, c ^ fc)
            sends.append(pltpu.make_async_remote_copy(
                src_ref=in_ref, dst_ref=out_ref.at[me], send_sem=send_sems.at[r - 1], recv_sem=recv_sems.at[r - 1],
                device_id=to, device_id_type=MESH_ID))
            sends[-1].start()
        for r in range(1, 8):
            fx, fy, fc = (r >> 2) & 1, (r >> 1) & 1, r & 1
            slot = out_ref.at[4 * (x ^ fx) + 2 * (y ^ fy) + (c ^ fc)]
            pltpu.make_async_remote_copy(src_ref=slot, dst_ref=slot, send_sem=send_sems.at[r - 1],
                                         recv_sem=recv_sems.at[r - 1], device_id=(x, y, c),
                                         device_id_type=MESH_ID).wait_recv()
        for cp in sends:
            cp.wait_send()
        own.wait()

    return pl.pallas_call(
        body, name=name, in_specs=[_ANY], out_specs=_ANY, out_shape=jax.ShapeDtypeStruct((8,) + vec.shape, vec.dtype),
        scratch_shapes=[pltpu.SemaphoreType.DMA((7,)), pltpu.SemaphoreType.DMA((7,)), pltpu.SemaphoreType.DMA(())],
        )(vec)


def _permute_w_in(w):
    cut = Z_SU
    return jnp.concatenate([w[:, :cut], w[:, cut + 2 * DN_HEADS:], w[:, cut:cut + 2 * DN_HEADS],
                            jnp.zeros((w.shape[0], Z_WIDTH - IN_WIDTH), w.dtype)], axis=1)


def _unpermute_w_in(wp):
    return jnp.concatenate([wp[:, :Z_SU], wp[:, Z_DBA:Z_DBA + 2 * DN_HEADS], wp[:, Z_SU:Z_DBA]], axis=1)


def _s5_inputs(sp):
    ab_re, ab_im, bb_re, bb_im = _whole_fwd(
        _s5_param_fn, sp["s5_pins"], [(S5_GROUPS, S5_STATE)] * 2 + [(S5_WIDTH, S5_STATE)] * 2, name="s5_params")
    return (_block_diag(bb_re), _block_diag(bb_im), _block_diag(sp["c_re"]), _block_diag(sp["c_im"]),
            ab_re.reshape(1, S5_LANES), ab_im.reshape(1, S5_LANES), sp["dskip"])


def _mixer_fwd(x, sp, weights, rope, mm):
    D = x.shape[1]
    (h,) = _rows_fwd(_prenorm_fn, [(x, D, 0)], [sp["mix_norm_pre"]], [(D, D, BF16)], name="mix_prenorm")
    w_in = weights("win")
    z = mm("win", h, w_in, out_dtype=F32, name="mix_in")
    w_out, glu_w = weights("wout")
    y_attn = _attn_fwd(z, rope[0], rope[1], sp["sinks"], "mix")
    qkv = _conv_fwd(z, sp["conv_w"], "mix")
    y_dn, states = _dn_fwd(qkv, z, sp["dn_a_log"], sp["dn_dt_bias"], sp["dn_norm_w"], "mix")
    s5_in = _s5_inputs(sp)
    y_lin, kre, kim = _s5_fwd(z, *s5_in, "mix")
    (y_s5,) = _rows_fwd(_s5_glu_fn, [(y_lin, S5_WIDTH, 0)], [glu_w, sp["glu_b"]], [(S5_WIDTH, S5_WIDTH, BF16)],
                        name="mix_s5_glu")
    cat = jnp.concatenate([y_attn, y_dn, y_s5], axis=1)
    mixed = mm("wout", cat, w_out, out_dtype=F32, name="mix_out")
    (x_new,) = _rows_fwd(_residual_fn(1.0), [(x, D, 0), (mixed, D, 0)], [sp["mix_norm_post"]], [(D, D, F32)],
                         name="mix_residual")
    return x_new, (h, z, qkv, states, s5_in, y_lin, kre, kim, cat, mixed, w_in, w_out, glu_w)


def _mixer_bwd(dx_new, x, sp, rope, saved, mm, sink):
    h, z, qkv, states, s5_in, y_lin, kre, kim, cat, mixed, w_in, w_out, glu_w = saved
    D = x.shape[1]
    G, H, P = S5_GROUPS, S5_GROUP_CH, S5_STATE
    d_mixed, d_g_post = _rows_bwd(_postnorm_fn(1.0), [(mixed, D, 0)], [sp["mix_norm_post"]], [(dx_new, D, 0)],
                                  [(0, D, BF16)], name="mix_postnorm_bwd")
    d_cat = mm(d_mixed, w_out, tb=True, out_dtype=BF16, name="mix_out_dx")
    sink("wout", mm(cat, d_mixed, ta=True, out_dtype=BF16, name="mix_out_dw"))
    d_attn, d_dn, d_s5 = d_cat[:, :ATTN_WIDTH], d_cat[:, ATTN_WIDTH:ATTN_WIDTH + DN_WIDTH], d_cat[:, ATTN_WIDTH + DN_WIDTH:]
    d_ylin, d_glu_w, d_glu_b = _rows_bwd(_s5_glu_fn, [(y_lin, S5_WIDTH, 0)], [glu_w, sp["glu_b"]],
                                         [(d_s5, S5_WIDTH, 0)], [(0, S5_WIDTH, F32)], name="mix_s5_glu_bwd")
    sink("glu", d_glu_w.astype(BF16))
    d_us5, d_bre, d_bim, d_cre, d_cim, d_are, d_aim, d_dskip = _s5_bwd(z, *s5_in, kre, kim, d_ylin, "mix")
    cts = [d_are.reshape(G, P), d_aim.reshape(G, P), _block_diag_take(d_bre), _block_diag_take(d_bim)]
    d_a_re, d_a_im, d_ldt, d_bt_re, d_bt_im = _whole_bwd(_s5_param_fn, sp["s5_pins"], cts, 5, name="s5_params_bwd",
                                                         lane_sum=(2,))
    from_t = lambda m: m.reshape(G, H, P).transpose(0, 2, 1)
    d_qkv, d_zg, d_ba, d_alog, d_dtb, d_nw = _dn_bwd(qkv, z, sp["dn_a_log"], sp["dn_dt_bias"], sp["dn_norm_w"], states,
                                                     d_dn, "mix")
    d_uconv, d_conv_w = _conv_bwd(z, sp["conv_w"], d_qkv, "mix")
    d_q, d_kv, d_sinks = _attn_bwd(z, rope[0], rope[1], sp["sinks"], d_attn, "mix")
    d_z = jnp.concatenate([d_q, d_kv, d_uconv, d_zg, d_us5, d_ba], axis=1)
    sink("win", mm(h, d_z, ta=True, out_dtype=BF16, name="mix_in_dw"))
    d_h = mm(d_z, w_in, tb=True, out_dtype=BF16, name="mix_in_dx")
    dx, d_g_pre = _rows_bwd(_prenorm_fn, [(x, D, 0)], [sp["mix_norm_pre"]], [(d_h, D, 0)], [(0, D, F32)],
                            name="mix_prenorm_bwd", add_to_first=dx_new)
    small = {
        "mix_norm_pre": d_g_pre[0], "mix_norm_post": d_g_post[0], "attn_sinks": d_sinks[:, 0], "dn_conv_w": d_conv_w,
        "dn_a_log": d_alog[:, 0, 0], "dn_dt_bias": d_dtb[:, 0, 0], "dn_norm_w": d_nw[0, 0],
        "s5_a_re": d_a_re, "s5_a_im": d_a_im, "s5_log_dt": d_ldt[:, 0], "s5_b_re": from_t(d_bt_re),
        "s5_b_im": from_t(d_bt_im), "s5_c_re": _block_diag_take(d_cre).reshape(G, H, P),
        "s5_c_im": _block_diag_take(d_cim).reshape(G, H, P), "s5_d": d_dskip[0], "s5_glu_b": d_glu_b[0],
    }
    return dx, small


BIG_PIECES = ("gu1", "wd1", "win", "wout", "glu", "gu2", "wd2")


def _halves(m):
    return m.reshape(m.shape[:-2] + (2, m.shape[-2] // 2, m.shape[-1]))


def _whole(m):
    return m.reshape(m.shape[:-3] + (2 * m.shape[-2], m.shape[-1]))


class _WeightGather:
    def __init__(self, shards):
        self.shards = shards
        pieces = list(shards[0])
        flat = [s for p in pieces for s in shards[0][p]]
        got = iter(_gather_shards(flat, "gather_weights"))
        self.ready = {(0, p): [next(got) for _ in shards[0][p]] for p in pieces}
        self.landing = None

    def weights(self, l, piece):
        return self.ready[(l, piece)]

    def mm(self, l, piece, a, b, **kw):
        jobs, done = [], None
        if self.landing is not None:
            done = self.landing
            jobs += [{"kind": "gather_d2d", "ins": [buf, s]} for buf, s in zip(done[2], self.shards[done[0]][done[1]])]
        n_done = len(jobs)
        if l + 1 < len(self.shards):
            jobs += [{"kind": "gather_ici", "ins": [s]} for s in self.shards[l + 1][piece]]
        if not jobs:
            return _mm(a, b, **kw)
        out, bufs = _mm(a, b, jobs=jobs, **kw)
        if done is not None:
            self.ready[done[:2]] = bufs[:n_done]
        self.landing = (l + 1, piece, bufs[n_done:]) if l + 1 < len(self.shards) else None
        return out


class _GradReduce:
    def __init__(self):
        self.waiting, self.landed = [], {}

    def add(self, piece, grad):
        part = _halves(grad)
        (got,) = _swap_other_half([part], "reduce_siblings")
        self.waiting.append((piece, _add_sibling(part, got, "reduce_siblings_add")))

    def mm(self, a, b, **kw):
        if not self.waiting:
            return _mm(a, b, **kw)
        out, others = _mm(a, b, jobs=[{"kind": "scatter", "ins": [cp]} for _, cp in self.waiting], **kw)
        for (piece, cp), o in zip(self.waiting, others):
            self.landed[piece] = (cp, o)
        self.waiting = []
        return out

    def finish(self):
        if self.waiting:
            others = _scatter_to_chips([cp for _, cp in self.waiting], "reduce_chips")
            for (piece, cp), o in zip(self.waiting, others):
                self.landed[piece] = (cp, o)
        pieces = list(self.landed)
        joined = _join_halves([_sum_chips(*self.landed[p], "reduce_chips_sum") for p in pieces], "reduce_join")
        return {p: _whole(m) for p, m in zip(pieces, joined)}


def _pack_small(arrs, extra=()):
    flat = jnp.concatenate([a.reshape(-1) for a in arrs] + list(extra))
    n = flat.shape[0]
    quantum = 8 * SMALL_LANES
    padded = -(-n // quantum) * quantum
    return jnp.concatenate([flat, jnp.zeros((padded - n,), flat.dtype)]).reshape(padded // SMALL_LANES, SMALL_LANES)


def _unpack_small(flat2d, shapes):
    flat = flat2d.reshape(-1)
    out, o = [], 0
    for s in shapes:
        n = int(np.prod(s))
        out.append(flat[o:o + n].reshape(s))
        o += n
    return out, flat[o:]


def _step(a):
    x, target = a["x"][0], a["loss_target"][0]
    S, D = x.shape
    L, _, Fs = a["ff1_w_gate"].shape
    px, py, pc = _place()
    chip = 2 * px + py

    def rows_of_chips(g):
        g = _whole(g)
        return g.reshape(N_SHARD * g.shape[1], g.shape[2])

    shards = []
    for l in range(L):
        half = lambda m: _halves(m.astype(BF16))
        shards.append({
            "gu1": [half(jnp.concatenate([a["ff1_w_gate"][l], a["ff1_w_up"][l]], axis=1))], "wd1": [half(a["ff1_w_down"][l])],
            "win": [half(a["w_in"][l])], "wout": [half(a["w_out"][l]), half(a["s5_glu_w"][l])],
            "gu2": [half(jnp.concatenate([a["ff2_w_gate"][l], a["ff2_w_up"][l]], axis=1))], "wd2": [half(a["ff2_w_down"][l])]})
    gather = _WeightGather(shards)

    def ffn_weights(l, f):
        def get(piece):
            (g,) = gather.weights(l, piece + f)
            return _whole(g) if piece == "gu" else rows_of_chips(g)
        return get

    def mixer_weights(l):
        def get(piece):
            got = gather.weights(l, piece)
            if piece == "win":
                return _permute_w_in(_whole(got[0]).transpose(1, 0, 2).reshape(D, IN_WIDTH))
            return rows_of_chips(got[0]), rows_of_chips(got[1])
        return get

    conv_local = a["dn_conv_w"].reshape(-1)
    conv_rows = -(-conv_local.shape[0] // (16 * FLAT_LANES)) * 16
    conv_pad = jnp.concatenate([conv_local, jnp.zeros((conv_rows * FLAT_LANES - conv_local.shape[0],), F32)])
    (conv_all,) = _gather_shards([conv_pad.reshape(2, conv_rows // 2, FLAT_LANES)], "gather_conv")
    conv_all = conv_all.reshape(N_SHARD, -1)[:, :conv_local.shape[0]].reshape(N_SHARD, L, DN_CONV, -1)
    conv_full = conv_all.transpose(1, 2, 0, 3).reshape(L, DN_CONV, 3 * DN_WIDTH)

    expand = jnp.repeat(jnp.eye(S5_GROUPS, dtype=F32), S5_GROUP_CH, axis=0)
    lanes = lambda v, n=128: jnp.broadcast_to(v[:, None], (v.shape[0], n))
    to_t = lambda m: m.transpose(0, 2, 1).reshape(S5_WIDTH, S5_STATE)

    def small_params(l):
        sp = {k: a[k][l][None] for k in ("ff1_norm_pre", "ff1_norm_post", "mix_norm_pre", "mix_norm_post",
                                         "ff2_norm_pre", "ff2_norm_post")}
        sp["sinks"] = lanes(a["attn_sinks"][l])
        sp["conv_w"] = conv_full[l]
        sp["dn_a_log"] = lanes(a["dn_a_log"][l])[:, None, :]
        sp["dn_dt_bias"] = lanes(a["dn_dt_bias"][l])[:, None, :]
        sp["dn_norm_w"] = a["dn_norm_w"][l][None, None]
        sp["s5_pins"] = [a["s5_a_re"][l], a["s5_a_im"][l], lanes(a["s5_log_dt"][l], S5_STATE),
                         to_t(a["s5_b_re"][l]), to_t(a["s5_b_im"][l]), expand]
        sp["c_re"] = a["s5_c_re"][l].reshape(S5_WIDTH, S5_STATE)
        sp["c_im"] = a["s5_c_im"][l].reshape(S5_WIDTH, S5_STATE)
        sp["dskip"] = a["s5_d"][l][None]
        sp["glu_b"] = a["s5_glu_b"][l][None]
        return sp

    rope = _rope_tables(S)
    sps = [small_params(l) for l in range(L)]

    saved = []
    for l in range(L):
        sp = sps[l]
        mm_of = lambda f: (lambda piece, p, q, **kw: gather.mm(l, piece + f, p, q, **kw))
        x1, s1 = _ffn_fwd(x, sp["ff1_norm_pre"], sp["ff1_norm_post"], ffn_weights(l, "1"), "ff1", mm_of("1"))
        x2, s2 = _mixer_fwd(x1, sp, mixer_weights(l), rope, mm_of(""))
        x3, s3 = _ffn_fwd(x2, sp["ff2_norm_pre"], sp["ff2_norm_post"], ffn_weights(l, "2"), "ff2", mm_of("2"))
        saved.append((x, s1, x1, s2, x2, s3))
        x = x3
    dx, loss_part = _loss_head(x, target, "head")

    big_grads, small_grads = [None] * L, [None] * L
    shard_major = lambda m: m.reshape(N_SHARD, m.shape[0] // N_SHARD, m.shape[1])
    for l in reversed(range(L)):
        sp = sps[l]
        x0, s1, x1, s2, x2, s3 = saved[l]
        red = _GradReduce()

        def ffn_sink(f):
            return lambda piece, g: red.add(piece + f, g if piece == "gu" else shard_major(g))

        def mixer_sink(piece, g):
            if piece == "win":
                g = _unpermute_w_in(g).reshape(D, N_SHARD, IN_WIDTH // N_SHARD).transpose(1, 0, 2)
            red.add(piece, g if piece == "win" else shard_major(g))

        dx, g_pre2, g_post2 = _ffn_bwd(dx, x2, sp["ff2_norm_pre"], sp["ff2_norm_post"], s3, "ff2", red.mm, ffn_sink("2"))
        dx, sg = _mixer_bwd(dx, x1, sp, rope, s2, red.mm, mixer_sink)
        dx, g_pre1, g_post1 = _ffn_bwd(dx, x0, sp["ff1_norm_pre"], sp["ff1_norm_post"], s1, "ff1", red.mm, ffn_sink("1"))
        big_grads[l] = red.finish()
        sg.update({"ff1_norm_pre": g_pre1[0], "ff1_norm_post": g_post1[0], "ff2_norm_pre": g_pre2[0],
                   "ff2_norm_post": g_post2[0]})
        small_grads[l] = sg
    grad_x = dx[None]

    grads = {}
    layers = lambda k: jnp.stack([big_grads[l][k] for l in range(L)])
    for f, (gu, wd) in (("ff1", ("gu1", "wd1")), ("ff2", ("gu2", "wd2"))):
        gus = layers(gu)
        grads[f + "_w_gate"], grads[f + "_w_up"] = gus[:, :, :Fs], gus[:, :, Fs:]
        grads[f + "_w_down"] = layers(wd)
    grads["w_in"], grads["w_out"], grads["s5_glu_w"] = layers("win"), layers("wout"), layers("glu")

    small_local = [jnp.stack([small_grads[l][n] for l in range(L)]) for n in SMALL]
    vec = _pack_small(small_local, extra=(loss_part[0, :1],))
    total = _sum_slots(_gather_all_devices(vec, "gather_small"), "sum_small")
    small_total, rest = _unpack_small(total, [g.shape for g in small_local])
    loss = rest[0]
    for n, g in zip(SMALL, small_total):
        grads[n] = g
    cw = 3 * DN_WIDTH // N_SHARD
    grads["dn_conv_w"] = lax.dynamic_slice_in_dim(grads["dn_conv_w"], chip * cw, cw, axis=2)

    delta, new_m, new_v = {}, {}, {}
    for n in BIG:
        delta[n], new_m[n], new_v[n] = _adamw(a[n], grads[n], a["m_" + n], a["v_" + n], "adamw_" + n)
    shapes = [a[n].shape for n in SMALL]
    packed = [_pack_small([src[n] for n in SMALL]) for src in
              (a, grads, {n: a["m_" + n] for n in SMALL}, {n: a["v_" + n] for n in SMALL})]
    for dst, res in zip((delta, new_m, new_v), _adamw(*packed, "adamw_small")):
        for n, val in zip(SMALL, _unpack_small(res, shapes)[0]):
            dst[n] = val
    return (loss, grad_x, *[grads[n] for n in WEIGHTS], *[delta[n] for n in WEIGHTS], *[new_m[n] for n in WEIGHTS],
            *[new_v[n] for n in WEIGHTS])


def kernel(x, ff1_norm_pre, ff1_w_gate, ff1_w_up, ff1_w_down, ff1_norm_post, mix_norm_pre, w_in, attn_sinks, dn_conv_w, dn_a_log, dn_dt_bias, dn_norm_w, s5_a_re, s5_a_im, s5_log_dt, s5_b_re, s5_b_im, s5_c_re, s5_c_im, s5_d, s5_glu_w, s5_glu_b, w_out, mix_norm_post, ff2_norm_pre, ff2_w_gate, ff2_w_up, ff2_w_down, ff2_norm_post, loss_target, m_ff1_norm_pre, m_ff1_w_gate, m_ff1_w_up, m_ff1_w_down, m_ff1_norm_post, m_mix_norm_pre, m_w_in, m_attn_sinks, m_dn_conv_w, m_dn_a_log, m_dn_dt_bias, m_dn_norm_w, m_s5_a_re, m_s5_a_im, m_s5_log_dt, m_s5_b_re, m_s5_b_im, m_s5_c_re, m_s5_c_im, m_s5_d, m_s5_glu_w, m_s5_glu_b, m_w_out, m_mix_norm_post, m_ff2_norm_pre, m_ff2_w_gate, m_ff2_w_up, m_ff2_w_down, m_ff2_norm_post, v_ff1_norm_pre, v_ff1_w_gate, v_ff1_w_up, v_ff1_w_down, v_ff1_norm_post, v_mix_norm_pre, v_w_in, v_attn_sinks, v_dn_conv_w, v_dn_a_log, v_dn_dt_bias, v_dn_norm_w, v_s5_a_re, v_s5_a_im, v_s5_log_dt, v_s5_b_re, v_s5_b_im, v_s5_c_re, v_s5_c_im, v_s5_d, v_s5_glu_w, v_s5_glu_b, v_w_out, v_mix_norm_post, v_ff2_norm_pre, v_ff2_w_gate, v_ff2_w_up, v_ff2_w_down, v_ff2_norm_post):
    return _step(dict(locals()))
```

```python
import functools
import math

import numpy as np
import jax
import jax.numpy as jnp
from jax import lax
from jax.experimental import pallas as pl
from jax.experimental.pallas import tpu as pltpu

F32 = jnp.float32
BF16 = jnp.bfloat16
HI = lax.Precision.HIGHEST
MESH_ID = pl.DeviceIdType.MESH

NORM_EPS = 1e-6
FFN_RES_WEIGHT = 0.5
ATTN_HEADS, ATTN_KV_HEADS, HEAD_DIM, WINDOW = 8, 2, 128, 128
ROPE_THETA = 10000.0
DN_HEADS, DN_HEAD_DIM, DN_CONV, DN_CHUNK = 4, 128, 4, 64
S5_GROUPS, S5_GROUP_CH, S5_STATE = 32, 16, 64
ATTN_WIDTH = ATTN_HEADS * HEAD_DIM
ATTN_KV_WIDTH = ATTN_KV_HEADS * HEAD_DIM
DN_WIDTH = DN_HEADS * DN_HEAD_DIM
S5_WIDTH = S5_GROUPS * S5_GROUP_CH
S5_LANES = S5_GROUPS * S5_STATE
MIX_WIDTH = ATTN_WIDTH + DN_WIDTH + S5_WIDTH
IN_SPLITS = (ATTN_WIDTH, ATTN_KV_WIDTH, ATTN_KV_WIDTH, 3 * DN_WIDTH, DN_WIDTH, DN_HEADS, DN_HEADS, S5_WIDTH)
IN_WIDTH = sum(IN_SPLITS)
Z_AQ, Z_AK, Z_AV = 0, ATTN_WIDTH, ATTN_WIDTH + ATTN_KV_WIDTH
Z_DQKV = ATTN_WIDTH + 2 * ATTN_KV_WIDTH
Z_DZ = Z_DQKV + 3 * DN_WIDTH
Z_SU = Z_DZ + DN_WIDTH
Z_DBA = Z_SU + S5_WIDTH
Z_WIDTH = Z_DBA + 128

ADAM_LR, ADAM_B1, ADAM_B2, ADAM_EPS, ADAM_WD, ADAM_STEP = 0.001, 0.9, 0.999, 1e-08, 0.01, 10

N_SHARD = 4
FLAT_LANES = 512
SMALL_LANES = 1024
VMEM_LIMIT = 56 * 1024 * 1024

WEIGHTS = ['ff1_norm_pre', 'ff1_w_gate', 'ff1_w_up', 'ff1_w_down', 'ff1_norm_post', 'mix_norm_pre', 'w_in',
           'attn_sinks', 'dn_conv_w', 'dn_a_log', 'dn_dt_bias', 'dn_norm_w', 's5_a_re', 's5_a_im', 's5_log_dt',
           's5_b_re', 's5_b_im', 's5_c_re', 's5_c_im', 's5_d', 's5_glu_w', 's5_glu_b', 'w_out', 'mix_norm_post',
           'ff2_norm_pre', 'ff2_w_gate', 'ff2_w_up', 'ff2_w_down', 'ff2_norm_post']
BIG = ['ff1_w_gate', 'ff1_w_up', 'ff1_w_down', 'w_in', 's5_glu_w', 'w_out', 'ff2_w_gate', 'ff2_w_up', 'ff2_w_down']
SMALL = [n for n in WEIGHTS if n not in BIG]


def _pick(dim, cands):
    for c in cands:
        if dim % c == 0:
            return c
    return dim


def _params(sem=None):
    return pltpu.CompilerParams(dimension_semantics=sem, vmem_limit_bytes=VMEM_LIMIT)


def _silu_mul(gate, up):
    return gate * jax.nn.sigmoid(gate) * up


def _mm(a, b, *, ta=False, tb=False, out_dtype=F32, name, b_groups=None, bk_groups=None, out_groups=None, jobs=(),
        a_swiglu=None, swiglu_bwd_of=None):
    K, M = a.shape if ta else a.shape[::-1]
    if a_swiglu:
        K, M = (K, M // 2) if ta else (K // 2, M)
    Ng = Kg = None
    if b_groups:
        assert not tb
        _, Kb, Ng = b.shape
        N = b_groups * Ng
    elif bk_groups:
        assert tb
        _, N, Kg = b.shape
        Kb = bk_groups * Kg
    else:
        N, Kb = (b.shape if tb else b.shape[::-1])
    assert K == Kb, (a.shape, b.shape, ta, tb)
    tiles = (1408, 1024, 512, 384, 256, 128)
    tm = _pick(M, (1024, 512, 256, 128))
    tk = _pick(Kg if bk_groups else K, (2048,) + tiles)
    tn = _pick(N // out_groups if out_groups else (Ng if b_groups else N), tiles)
    if a_swiglu and ta:
        tm, tk = _pick(a_swiglu, tiles), _pick(K, tiles[1:])
    elif a_swiglu:
        tm, tk = _pick(M, (512, 256, 128)), _pick(a_swiglu, tiles)
    if swiglu_bwd_of is not None:
        tm, tn = _pick(M, (512, 256, 128)), swiglu_bwd_of.shape[1] // (2 * N_SHARD)
        assert tk == K
    nk = K // tk
    dims = (((0,) if ta else (1,), (1,) if tb else (0,)), ((), ()))

    grid = (M // tm, N // tn, nk)
    n_job_in = sum(len(j["ins"]) for j in jobs)
    n_sems = sum(_JOB_SEMS[j["kind"]] for j in jobs)
    n_a = 2 if a_swiglu else 1
    n_lead = n_a + 1 + (1 if swiglu_bwd_of is not None else 0)

    def body(*refs):
        b_ref = refs[n_a]
        if a_swiglu:
            a_val = _silu_mul(refs[0][...].astype(F32), refs[1][...].astype(F32)).astype(BF16)
        else:
            a_val = refs[0][...].astype(BF16)
        job_ins = refs[n_lead:n_lead + n_job_in]
        o_ref = refs[n_lead + n_job_in]
        job_outs = refs[n_lead + 1 + n_job_in:n_lead + 1 + n_job_in + len(jobs)]
        scratch = refs[n_lead + 1 + n_job_in + len(jobs):]
        if jobs:
            send_sems, recv_sems = scratch[-2:]
            scratch = scratch[:-2]
            ids = [pl.program_id(d) for d in range(3)]
            first = functools.reduce(jnp.logical_and, [i == 0 for i in ids])
            last = functools.reduce(jnp.logical_and, [i == g - 1 for i, g in zip(ids, grid)])

            def copies():
                sends, recvs, at, sem = [], [], 0, 0
                for j, out in zip(jobs, job_outs):
                    s, r = _job_copies(j["kind"], job_ins[at:at + len(j["ins"])], out, send_sems, recv_sems, sem)
                    sends, recvs = sends + s, recvs + r
                    at, sem = at + len(j["ins"]), sem + _JOB_SEMS[j["kind"]]
                return sends, recvs

            @pl.when(first)
            def _():
                for cp in copies()[0]:
                    cp.start()

        part = lax.dot_general(a_val, b_ref[...].astype(BF16), dims, preferred_element_type=F32)
        if swiglu_bwd_of is not None:
            gate, up = refs[n_a + 1][:, :tn].astype(F32), refs[n_a + 1][:, tn:].astype(F32)
            sg = jax.nn.sigmoid(gate)
            o_ref[:, :tn] = (part * up * sg * (1.0 + gate * (1.0 - sg))).astype(o_ref.dtype)
            o_ref[:, tn:] = (part * gate * sg).astype(o_ref.dtype)
        elif nk == 1:
            o_ref[...] = part.astype(o_ref.dtype)
        else:
            acc_ref, = scratch
            k = pl.program_id(2)

            @pl.when(k == 0)
            def _():
                acc_ref[...] = part

            @pl.when(k > 0)
            def _():
                acc_ref[...] += part

            @pl.when(k == nk - 1)
            def _():
                o_ref[...] = acc_ref[...].astype(o_ref.dtype)

        if jobs:
            @pl.when(last)
            def _():
                sends, recvs = copies()
                for cp in recvs:
                    cp.wait_recv()
                for cp in sends:
                    cp.wait_send()

    if a_swiglu and ta:
        u = a_swiglu // tm
        a_specs = [pl.BlockSpec((tk, tm), lambda i, j, k, o=o: (k, (i // u) * 2 * u + o + i % u)) for o in (0, u)]
    elif a_swiglu:
        u = a_swiglu // tk
        a_specs = [pl.BlockSpec((tm, tk), lambda i, j, k, o=o: (i, (k // u) * 2 * u + o + k % u)) for o in (0, u)]
    else:
        a_specs = [pl.BlockSpec((tk, tm), lambda i, j, k: (k, i)) if ta else pl.BlockSpec((tm, tk), lambda i, j, k: (i, k))]
    if b_groups:
        per = Ng // tn
        b_spec = pl.BlockSpec((None, tk, tn), lambda i, j, k: (j // per, k, j % per))
    elif bk_groups:
        per = Kg // tk
        b_spec = pl.BlockSpec((None, tn, tk), lambda i, j, k: (k // per, j, k % per))
    else:
        b_spec = pl.BlockSpec((tn, tk), lambda i, j, k: (j, k)) if tb else pl.BlockSpec((tk, tn), lambda i, j, k: (k, j))
    if out_groups:
        pero = (N // out_groups) // tn
        o_spec = pl.BlockSpec((None, tm, tn), lambda i, j, k: (j // pero, i, j % pero))
        o_shape = jax.ShapeDtypeStruct((out_groups, M, N // out_groups), out_dtype)
    else:
        o_spec = pl.BlockSpec((tm, tn), lambda i, j, k: (i, j))
        o_shape = jax.ShapeDtypeStruct((M, N), out_dtype)
    lead_specs, lead_args = a_specs + [b_spec], [a] * n_a + [b]
    if swiglu_bwd_of is not None:
        o_spec = pl.BlockSpec((tm, 2 * tn), lambda i, j, k: (i, j))
        o_shape = jax.ShapeDtypeStruct((M, 2 * N), out_dtype)
        lead_specs, lead_args = lead_specs + [o_spec], lead_args + [swiglu_bwd_of]
    scratch = [pltpu.VMEM((tm, tn), F32)] if nk > 1 else []
    if not jobs:
        return pl.pallas_call(
            body, name=name, grid=grid, in_specs=lead_specs, out_specs=o_spec, out_shape=o_shape,
            scratch_shapes=scratch, compiler_params=_params(("parallel", "parallel", "arbitrary")))(*lead_args)
    job_args = [x for j in jobs for x in j["ins"]]
    aliases, at = {}, n_lead
    for n, j in enumerate(jobs):
        if j["kind"] == "gather_d2d":
            aliases[at] = 1 + n
        at += len(j["ins"])
    res = pl.pallas_call(
        body, name=name, grid=grid, in_specs=lead_specs + [_ANY] * n_job_in,
        out_specs=[o_spec] + [_ANY] * len(jobs), out_shape=[o_shape] + [_job_out_shape(j) for j in jobs],
        input_output_aliases=aliases,
        scratch_shapes=scratch + [pltpu.SemaphoreType.DMA((n_sems,)), pltpu.SemaphoreType.DMA((n_sems,))],
        compiler_params=_params(("arbitrary", "arbitrary", "arbitrary")))(*lead_args, *job_args)
    return res[0], list(res[1:])


def _row_spec(tm, width, off):
    return pl.BlockSpec((tm, width), lambda i, j: (i, off + j))


def _const_spec(shape):
    return pl.BlockSpec(shape, lambda i, j: (0,) * len(shape))


def _rows_fwd(fn, rows, consts, outs, *, name, tm=256, ncol=1):
    S = rows[0][0].shape[0]
    tm = _pick(S, (tm, 128, 64))
    nr, nc = len(rows), len(consts)

    def body(*refs):
        vals = [r[...].astype(F32) for r in refs[:nr + nc]]
        res = fn(*vals)
        for o_ref, o in zip(refs[nr + nc:], res):
            o_ref[...] = o.astype(o_ref.dtype)

    return pl.pallas_call(
        body, name=name, grid=(S // tm, ncol),
        in_specs=[_row_spec(tm, w, off) for _, w, off in rows] + [_const_spec(c.shape) for c in consts],
        out_specs=[_row_spec(tm, bw, 0) for _, bw, _ in outs],
        out_shape=[jax.ShapeDtypeStruct((S, tw), dt) for tw, _, dt in outs],
        compiler_params=_params(("parallel", "parallel")))(*[r[0] for r in rows], *consts)


def _rows_bwd(fn, rows, consts, cts, row_grads, *, name, tm=256, ncol=1, add_to_first=None, lane_sum_consts=()):
    S = rows[0][0].shape[0]
    tm = _pick(S, (tm, 128, 64))
    nr, nc, nt = len(rows), len(consts), len(cts)
    n_in = nr + nc + nt + (1 if add_to_first is not None else 0)

    def body(*refs):
        vals = [r[...].astype(F32) for r in refs[:nr + nc]]
        ct = tuple(r[...].astype(F32) for r in refs[nr + nc:nr + nc + nt])
        _, vjp = jax.vjp(fn, *vals)
        grads = vjp(ct)
        outs = refs[n_in:]
        for n, (idx, _, _) in enumerate(row_grads):
            g = grads[idx]
            if n == 0 and add_to_first is not None:
                g = g + refs[n_in - 1][...].astype(F32)
            outs[n][...] = g.astype(outs[n].dtype)
        first = jnp.logical_and(pl.program_id(0) == 0, pl.program_id(1) == 0)
        for c in range(nc):
            o_ref = outs[len(row_grads) + c]
            g = grads[nr + c]
            if c in lane_sum_consts:
                g = jnp.broadcast_to(jnp.sum(g, axis=-1, keepdims=True), g.shape)

            @pl.when(first)
            def _():
                o_ref[...] = jnp.zeros_like(o_ref)

            o_ref[...] += g

    in_specs = ([_row_spec(tm, w, off) for _, w, off in rows] + [_const_spec(c.shape) for c in consts]
                + [_row_spec(tm, w, off) for _, w, off in cts])
    args = [r[0] for r in rows] + list(consts) + [c[0] for c in cts]
    if add_to_first is not None:
        in_specs.append(_row_spec(tm, rows[row_grads[0][0]][1], 0))
        args.append(add_to_first)
    out_specs = [_row_spec(tm, rows[idx][1], 0) for idx, _, _ in row_grads] + [_const_spec(c.shape) for c in consts]
    out_shape = ([jax.ShapeDtypeStruct((S, tw), dt) for _, tw, dt in row_grads]
                 + [jax.ShapeDtypeStruct(c.shape, F32) for c in consts])
    return pl.pallas_call(
        body, name=name, grid=(S // tm, ncol), in_specs=in_specs, out_specs=out_specs, out_shape=out_shape,
        compiler_params=_params(("arbitrary", "arbitrary")))(*args)


def _rms(x, gain):
    return x * lax.rsqrt(jnp.mean(x * x, axis=-1, keepdims=True) + NORM_EPS) * gain


def _prenorm_fn(x, gain):
    return (_rms(x, gain),)


def _postnorm_fn(weight):
    def fn(y, gain):
        return (weight * _rms(y, gain),)
    return fn


def _residual_fn(weight):
    def fn(x, y, gain):
        return (x + weight * _rms(y, gain),)
    return fn


def _swiglu_fn(blk):
    tf = blk.shape[1] // 2
    gate, up = blk[:, :tf], blk[:, tf:]
    return (gate * jax.nn.sigmoid(gate) * up,)


def _ffn_fwd(x, g_pre, g_post, weights, tag, mm):
    D = x.shape[1]
    (h,) = _rows_fwd(_prenorm_fn, [(x, D, 0)], [g_pre], [(D, D, BF16)], name=f"{tag}_prenorm")
    wgu = weights("gu")
    Fs = wgu.shape[2] // 2
    gu = mm("gu", h, wgu, b_groups=N_SHARD, out_dtype=BF16, name=f"{tag}_gate_up")
    wd = weights("wd")
    y = mm("wd", gu, wd, a_swiglu=Fs, out_dtype=F32, name=f"{tag}_down")
    (x_new,) = _rows_fwd(_residual_fn(FFN_RES_WEIGHT), [(x, D, 0), (y, D, 0)], [g_post], [(D, D, F32)],
                         name=f"{tag}_residual")
    return x_new, (h, gu, y, wgu, wd)


def _ffn_bwd(dx_new, x, g_pre, g_post, saved, tag, mm, sink):
    h, gu, y, wgu, wd = saved
    D = x.shape[1]
    Fs = wgu.shape[2] // 2
    d_y, d_g_post = _rows_bwd(_postnorm_fn(FFN_RES_WEIGHT), [(y, D, 0)], [g_post], [(dx_new, D, 0)],
                              [(0, D, BF16)], name=f"{tag}_postnorm_bwd")
    d_gu = mm(d_y, wd, tb=True, swiglu_bwd_of=gu, out_dtype=BF16, name=f"{tag}_down_dx")
    sink("wd", mm(gu, d_y, ta=True, a_swiglu=Fs, out_dtype=BF16, name=f"{tag}_down_dw"))
    sink("gu", mm(h, d_gu, ta=True, out_dtype=BF16, out_groups=N_SHARD, name=f"{tag}_gate_up_dw"))
    d_h = mm(d_gu, wgu, tb=True, bk_groups=N_SHARD, out_dtype=BF16, name=f"{tag}_gate_up_dx")
    dx, d_g_pre = _rows_bwd(_prenorm_fn, [(x, D, 0)], [g_pre], [(d_h, D, 0)], [(0, D, F32)],
                            name=f"{tag}_prenorm_bwd", add_to_first=dx_new)
    return dx, d_g_pre, d_g_post


@jax.custom_vjp
def _swap_halves(x):
    return pltpu.roll(x, HEAD_DIM // 2, 1)


def _swap_fwd(x):
    return _swap_halves(x), None


def _swap_bwd(_, g):
    return (_swap_halves(g),)


_swap_halves.defvjp(_swap_fwd, _swap_bwd)


def _rope(x, cc, ss):
    return x * cc + _swap_halves(x) * ss


def _attn_block_fn(has_prev):
    grp = ATTN_HEADS // ATTN_KV_HEADS
    scale = HEAD_DIM ** -0.5
    nt = (((1,), (1,)), ((), ()))
    nn = (((1,), (0,)), ((), ()))

    def fn(*a):
        q = a[:8]
        kp, kc, vp, vc = a[8:10], a[10:12], a[12:14], a[14:16]
        cc, ss, ccp, ssp, sinks = a[16:21]
        row = lax.broadcasted_iota(jnp.int32, (WINDOW, WINDOW), 0)
        col = lax.broadcasted_iota(jnp.int32, (WINDOW, WINDOW), 1)
        m_cur = col <= row
        m_prev = jnp.logical_and(col > row, has_prev)
        outs = []
        for h in range(ATTN_HEADS):
            g = h // grp
            qr = _rope(q[h], cc, ss).astype(BF16)
            kcr = _rope(kc[g], cc, ss).astype(BF16)
            kpr = _rope(kp[g], ccp, ssp).astype(BF16)
            s_c = jnp.where(m_cur, lax.dot_general(qr, kcr, nt, preferred_element_type=F32) * scale, -jnp.inf)
            s_p = jnp.where(m_prev, lax.dot_general(qr, kpr, nt, preferred_element_type=F32) * scale, -jnp.inf)
            sink = sinks[h:h + 1, :]
            m = jnp.maximum(jnp.maximum(jnp.max(s_c, axis=-1, keepdims=True), jnp.max(s_p, axis=-1, keepdims=True)),
                            sink)
            p_c = jnp.exp(s_c - m)
            p_p = jnp.exp(s_p - m)
            den = (jnp.sum(p_c, axis=-1, keepdims=True) + jnp.sum(p_p, axis=-1, keepdims=True) + jnp.exp(sink - m))
            inv = 1.0 / den
            o = (lax.dot_general((p_c * inv).astype(BF16), vc[g].astype(BF16), nn, preferred_element_type=F32)
                 + lax.dot_general((p_p * inv).astype(BF16), vp[g].astype(BF16), nn, preferred_element_type=F32))
            outs.append(o)
        return tuple(outs)

    return fn


def _attn_specs(order):
    kvb = ATTN_WIDTH // (2 * ATTN_KV_WIDTH)
    return [
        pl.BlockSpec((WINDOW, ATTN_WIDTH), lambda i: (order(i), 0)),
        pl.BlockSpec((WINDOW, 2 * ATTN_KV_WIDTH), lambda i: (jnp.maximum(order(i) - 1, 0), kvb)),
        pl.BlockSpec((WINDOW, 2 * ATTN_KV_WIDTH), lambda i: (order(i), kvb)),
        pl.BlockSpec((WINDOW, HEAD_DIM), lambda i: (order(i), 0)),
        pl.BlockSpec((WINDOW, HEAD_DIM), lambda i: (order(i), 0)),
        pl.BlockSpec((WINDOW, HEAD_DIM), lambda i: (jnp.maximum(order(i) - 1, 0), 0)),
        pl.BlockSpec((WINDOW, HEAD_DIM), lambda i: (jnp.maximum(order(i) - 1, 0), 0)),
        pl.BlockSpec((ATTN_HEADS, HEAD_DIM), lambda i: (0, 0)),
    ]


def _attn_args(q_ref, kvp_ref, kvc_ref, cc, ss, ccp, ssp, sinks):
    d = HEAD_DIM
    q = [q_ref[:, h * d:(h + 1) * d].astype(F32) for h in range(ATTN_HEADS)]
    kp = [kvp_ref[:, g * d:(g + 1) * d].astype(F32) for g in range(ATTN_KV_HEADS)]
    vp = [kvp_ref[:, ATTN_KV_WIDTH + g * d:ATTN_KV_WIDTH + (g + 1) * d].astype(F32) for g in range(ATTN_KV_HEADS)]
    kc = [kvc_ref[:, g * d:(g + 1) * d].astype(F32) for g in range(ATTN_KV_HEADS)]
    vc = [kvc_ref[:, ATTN_KV_WIDTH + g * d:ATTN_KV_WIDTH + (g + 1) * d].astype(F32) for g in range(ATTN_KV_HEADS)]
    return q + kp + kc + vp + vc + [cc[...], ss[...], ccp[...], ssp[...], sinks[...]]


def _attn_fwd(z, cc, ss, sinks_b, tag):
    S = z.shape[0]
    nb = S // WINDOW

    def body(q_ref, kvp_ref, kvc_ref, cc_r, ss_r, ccp_r, ssp_r, sink_r, o_ref):
        n = pl.program_id(0)
        outs = _attn_block_fn(n > 0)(*_attn_args(q_ref, kvp_ref, kvc_ref, cc_r, ss_r, ccp_r, ssp_r, sink_r))
        for h in range(ATTN_HEADS):
            o_ref[:, h * HEAD_DIM:(h + 1) * HEAD_DIM] = outs[h].astype(o_ref.dtype)

    return pl.pallas_call(
        body, name=f"{tag}_attn", grid=(nb,), in_specs=_attn_specs(lambda i: i),
        out_specs=pl.BlockSpec((WINDOW, ATTN_WIDTH), lambda i: (i, 0)),
        out_shape=jax.ShapeDtypeStruct((S, ATTN_WIDTH), BF16),
        compiler_params=_params(("parallel",)))(z, z, z, cc, ss, cc, ss, sinks_b)


def _attn_bwd(z, cc, ss, sinks_b, d_out, tag):
    S = z.shape[0]
    nb = S // WINDOW
    d = HEAD_DIM
    rev = lambda i: nb - 1 - i

    def body(q_ref, kvp_ref, kvc_ref, cc_r, ss_r, ccp_r, ssp_r, sink_r, do_ref, dq_ref, dkv_ref, dsink_ref, carry):
        i = pl.program_id(0)
        n = nb - 1 - i

        @pl.when(i == 0)
        def _():
            carry[...] = jnp.zeros_like(carry)
            dsink_ref[...] = jnp.zeros_like(dsink_ref)

        args = _attn_args(q_ref, kvp_ref, kvc_ref, cc_r, ss_r, ccp_r, ssp_r, sink_r)
        _, vjp = jax.vjp(_attn_block_fn(n > 0), *args)
        g = vjp(tuple(do_ref[:, h * d:(h + 1) * d].astype(F32) for h in range(ATTN_HEADS)))
        for h in range(ATTN_HEADS):
            dq_ref[:, h * d:(h + 1) * d] = g[h].astype(dq_ref.dtype)
        for gi in range(ATTN_KV_HEADS):
            ks = slice(gi * d, (gi + 1) * d)
            vs = slice(ATTN_KV_WIDTH + gi * d, ATTN_KV_WIDTH + (gi + 1) * d)
            dkv_ref[:, ks] = (g[10 + gi] + carry[:, ks]).astype(dkv_ref.dtype)
            dkv_ref[:, vs] = (g[14 + gi] + carry[:, vs]).astype(dkv_ref.dtype)
            carry[:, ks] = g[8 + gi]
            carry[:, vs] = g[12 + gi]
        ds = g[20]
        dsink_ref[...] += jnp.broadcast_to(jnp.sum(ds, axis=-1, keepdims=True), ds.shape)

    return pl.pallas_call(
        body, name=f"{tag}_attn_bwd", grid=(nb,),
        in_specs=_attn_specs(rev) + [pl.BlockSpec((WINDOW, ATTN_WIDTH), lambda i: (rev(i), 0))],
        out_specs=[pl.BlockSpec((WINDOW, ATTN_WIDTH), lambda i: (rev(i), 0)),
                   pl.BlockSpec((WINDOW, 2 * ATTN_KV_WIDTH), lambda i: (rev(i), 0)),
                   pl.BlockSpec((ATTN_HEADS, HEAD_DIM), lambda i: (0, 0))],
        out_shape=[jax.ShapeDtypeStruct((S, ATTN_WIDTH), BF16), jax.ShapeDtypeStruct((S, 2 * ATTN_KV_WIDTH), BF16),
                   jax.ShapeDtypeStruct((ATTN_HEADS, HEAD_DIM), F32)],
        scratch_shapes=[pltpu.VMEM((WINDOW, 2 * ATTN_KV_WIDTH), F32)],
        compiler_params=_params(("arbitrary",)))(z, z, z, cc, ss, cc, ss, sinks_b, d_out)


def _rope_tables(seq):
    half = HEAD_DIM // 2
    inv_freq = ROPE_THETA ** (-jnp.arange(half, dtype=F32) / half)
    ang = jnp.arange(seq, dtype=F32)[:, None] * inv_freq[None, :]
    cos, sin = jnp.cos(ang), jnp.sin(ang)
    return jnp.concatenate([cos, cos], axis=1), jnp.concatenate([-sin, sin], axis=1)


CONV_COLS = 128


def _conv_pre(u, w_ref, S):
    row = lax.broadcasted_iota(jnp.int32, u.shape, 0)
    shifted = [u] + [jnp.where(row >= s, pltpu.roll(u, s, 0), 0.0) for s in range(1, DN_CONV)]
    y = shifted[0] * w_ref[DN_CONV - 1:DN_CONV, :]
    for s in range(1, DN_CONV):
        y = y + shifted[s] * w_ref[DN_CONV - 1 - s:DN_CONV - s, :]
    return y, shifted, row


def _conv_fwd(z, conv_w, tag):
    S = z.shape[0]
    ncol = 3 * DN_WIDTH // CONV_COLS

    def body(u_ref, w_ref, o_ref):
        y, _, _ = _conv_pre(u_ref[...], w_ref, S)
        o_ref[...] = y * jax.nn.sigmoid(y)

    return pl.pallas_call(
        body, name=f"{tag}_conv", grid=(ncol,),
        in_specs=[pl.BlockSpec((S, CONV_COLS), lambda j: (0, Z_DQKV // CONV_COLS + j)),
                  pl.BlockSpec((DN_CONV, CONV_COLS), lambda j: (0, j))],
        out_specs=pl.BlockSpec((S, CONV_COLS), lambda j: (0, j)),
        out_shape=jax.ShapeDtypeStruct((S, 3 * DN_WIDTH), F32),
        compiler_params=_params(("parallel",)))(z, conv_w)


def _conv_bwd(z, conv_w, d_out, tag):
    S = z.shape[0]
    ncol = 3 * DN_WIDTH // CONV_COLS

    def body(u_ref, w_ref, do_ref, du_ref, dw_ref):
        y, shifted, row = _conv_pre(u_ref[...], w_ref, S)
        sg = jax.nn.sigmoid(y)
        d_y = do_ref[...] * (sg * (1.0 + y * (1.0 - sg)))
        d_u = d_y * w_ref[DN_CONV - 1:DN_CONV, :]
        dw_ref[DN_CONV - 1:DN_CONV, :] = jnp.sum(d_y * shifted[0], axis=0, keepdims=True)
        for s in range(1, DN_CONV):
            back = jnp.where(row < S - s, pltpu.roll(d_y, S - s, 0), 0.0)
            d_u = d_u + back * w_ref[DN_CONV - 1 - s:DN_CONV - s, :]
            dw_ref[DN_CONV - 1 - s:DN_CONV - s, :] = jnp.sum(d_y * shifted[s], axis=0, keepdims=True)
        du_ref[...] = d_u.astype(du_ref.dtype)

    return pl.pallas_call(
        body, name=f"{tag}_conv_bwd", grid=(ncol,),
        in_specs=[pl.BlockSpec((S, CONV_COLS), lambda j: (0, Z_DQKV // CONV_COLS + j)),
                  pl.BlockSpec((DN_CONV, CONV_COLS), lambda j: (0, j)),
                  pl.BlockSpec((S, CONV_COLS), lambda j: (0, j))],
        out_specs=[pl.BlockSpec((S, CONV_COLS), lambda j: (0, j)), pl.BlockSpec((DN_CONV, CONV_COLS), lambda j: (0, j))],
        out_shape=[jax.ShapeDtypeStruct((S, 3 * DN_WIDTH), BF16), jax.ShapeDtypeStruct((DN_CONV, 3 * DN_WIDTH), F32)],
        compiler_params=_params(("parallel",)))(z, conv_w, d_out)


_NN = (((1,), (0,)), ((), ()))
_NT = (((1,), (1,)), ((), ()))
_TN = (((0,), (0,)), ((), ()))


def _dot3(a, b, dims):
    a_hi, b_hi = a.astype(BF16), b.astype(BF16)
    a_lo, b_lo = (a - a_hi.astype(F32)).astype(BF16), (b - b_hi.astype(F32)).astype(BF16)
    mm = lambda p, q: lax.dot_general(p, q, dims, preferred_element_type=F32)
    return mm(a_hi, b_hi) + (mm(a_hi, b_lo) + mm(a_lo, b_hi))


@functools.partial(jax.custom_vjp, nondiff_argnums=(2,))
def _dot_vjp(a, b, dims):
    return _dot3(a, b, dims)


def _dot_vjp_fwd(a, b, dims):
    return _dot3(a, b, dims), (a, b)


_BATCH = ((0,), (0,))
_BNN, _BNT, _BTN = (((2,), (1,)), _BATCH), (((2,), (2,)), _BATCH), (((1,), (1,)), _BATCH)


def _dot_vjp_bwd(dims, res, g):
    a, b = res
    nn, nt, tn = (_NN, _NT, _TN) if dims in (_NN, _NT, _TN) else (_BNN, _BNT, _BTN)
    if dims == nn:
        return _dot3(g, b, nt), _dot3(a, g, tn)
    if dims == nt:
        return _dot3(g, b, nn), _dot3(g, a, tn)
    return _dot3(b, g, nt), _dot3(a, g, nn)


_dot_vjp.defvjp(_dot_vjp_fwd, _dot_vjp_bwd)


def _dot(a, b, dims=_NN):
    return _dot_vjp(a, b, dims)


def _dot1(a, b, dims=_NN):
    return lax.dot_general(a.astype(BF16), b.astype(BF16), dims, preferred_element_type=F32)


def _dn_chunk_fn(state, q, k, v, zg, ba, a_log, dtb, norm_w):
    H, C, dk = DN_HEADS, DN_CHUNK, DN_HEAD_DIM
    head = lax.broadcasted_iota(jnp.int32, (H, C, dk), 0)
    rowl = lax.broadcasted_iota(jnp.int32, (H, C, dk), 1)
    lane = lax.broadcasted_iota(jnp.int32, (H, C, dk), 2)
    row = lax.broadcasted_iota(jnp.int32, (H, C, C), 1)
    col = lax.broadcasted_iota(jnp.int32, (H, C, C), 2)
    ba3 = jnp.broadcast_to(ba[None], (H, C, dk))
    bcol = jnp.sum(jnp.where(lane == head, ba3, 0.0), axis=-1, keepdims=True)
    acol = jnp.sum(jnp.where(lane == H + head, ba3, 0.0), axis=-1, keepdims=True)
    qn = q * lax.rsqrt(jnp.sum(q * q, axis=-1, keepdims=True) + NORM_EPS) * (dk ** -0.5)
    kn = k * lax.rsqrt(jnp.sum(k * k, axis=-1, keepdims=True) + NORM_EPS)
    beta = jax.nn.sigmoid(bcol)
    sp_in = acol + dtb
    softplus = jnp.maximum(sp_in, 0.0) + jnp.log(1.0 + jnp.exp(-jnp.abs(sp_in)))
    gt = -jnp.exp(a_log) * softplus
    gc = _dot((row >= col).astype(F32), gt, _BNN)
    gcol = jnp.mean(gc, axis=-1, keepdims=True)
    grow = _dot(jnp.full((H, C, dk), 1.0 / dk, F32), gc, _BNT)
    decay = jnp.exp(jnp.where(row >= col, gcol - grow, -jnp.inf))
    kb = kn * beta
    m = -jnp.where(row > col, _dot(kb, kn, _BNT) * decay, 0.0)
    u = v * beta
    w = kb * jnp.exp(gc)
    for step in range(6):
        u = u + _dot(m, u, _BNN)
        w = w + _dot(m, w, _BNN)
        if step < 5:
            m = _dot(m, m, _BNN)
    attn = jnp.where(row >= col, _dot(qn, kn, _BNT) * decay, 0.0)
    q_dec = qn * jnp.exp(gc)
    gl = jnp.sum(jnp.where(rowl == C - 1, gc, 0.0), axis=1, keepdims=True)
    k_dec = kn * jnp.exp(gl - gc)
    v_new = u - _dot(w, state, _BNN)
    o = _dot(q_dec, state, _BNN) + _dot(attn, v_new, _BNN)
    state_new = state * jnp.exp(gl) + _dot(k_dec, v_new, _BTN)
    y = o * lax.rsqrt(jnp.mean(o * o, axis=-1, keepdims=True) + NORM_EPS) * norm_w
    y = y * (zg * jax.nn.sigmoid(zg))
    return state_new, y


def _dn_specs(order):
    C = DN_CHUNK
    return [pl.BlockSpec((C, 3 * DN_WIDTH), lambda i: (order(i), 0)),
            pl.BlockSpec((C, DN_WIDTH), lambda i: (order(i), Z_DZ // DN_WIDTH)),
            pl.BlockSpec((C, 128), lambda i: (order(i), Z_DBA // 128)),
            pl.BlockSpec((DN_HEADS, 1, DN_HEAD_DIM), lambda i: (0, 0, 0)),
            pl.BlockSpec((DN_HEADS, 1, DN_HEAD_DIM), lambda i: (0, 0, 0)),
            pl.BlockSpec((1, 1, DN_HEAD_DIM), lambda i: (0, 0, 0))]


def _dn_heads(ref, base=0):
    d = DN_HEAD_DIM
    return jnp.stack([ref[:, base + h * d:base + (h + 1) * d].astype(F32) for h in range(DN_HEADS)], axis=0)


def _dn_args(qkv_ref, zg_ref, ba_ref, alog_ref, dtb_ref, nw_ref):
    return [_dn_heads(qkv_ref), _dn_heads(qkv_ref, DN_WIDTH), _dn_heads(qkv_ref, 2 * DN_WIDTH), _dn_heads(zg_ref),
            ba_ref[...].astype(F32), alog_ref[...], dtb_ref[...], nw_ref[...]]


def _dn_fwd(qkv, z, a_log, dtb, norm_w, tag):
    S = qkv.shape[0]
    nchunk = S // DN_CHUNK
    d = DN_HEAD_DIM

    def body(qkv_ref, zg_ref, ba_ref, alog_ref, dtb_ref, nw_ref, y_ref, st_ref, state):
        @pl.when(pl.program_id(0) == 0)
        def _():
            state[...] = jnp.zeros_like(state)

        st_ref[...] = state[...]
        new, y = _dn_chunk_fn(state[...], *_dn_args(qkv_ref, zg_ref, ba_ref, alog_ref, dtb_ref, nw_ref))
        state[...] = new
        for h in range(DN_HEADS):
            y_ref[:, h * d:(h + 1) * d] = y[h].astype(y_ref.dtype)

    return pl.pallas_call(
        body, name=f"{tag}_deltanet", grid=(nchunk,), in_specs=_dn_specs(lambda i: i),
        out_specs=[pl.BlockSpec((DN_CHUNK, DN_WIDTH), lambda i: (i, 0)),
                   pl.BlockSpec((None, DN_HEADS, d, d), lambda i: (i, 0, 0, 0))],
        out_shape=[jax.ShapeDtypeStruct((S, DN_WIDTH), BF16), jax.ShapeDtypeStruct((nchunk, DN_HEADS, d, d), F32)],
        scratch_shapes=[pltpu.VMEM((DN_HEADS, d, d), F32)],
        compiler_params=_params(("arbitrary",)))(qkv, z, z, a_log, dtb, norm_w)


def _dn_bwd(qkv, z, a_log, dtb, norm_w, states, d_y, tag):
    S = qkv.shape[0]
    nchunk = S // DN_CHUNK
    d = DN_HEAD_DIM
    rev = lambda i: nchunk - 1 - i

    def body(qkv_ref, zg_ref, ba_ref, alog_ref, dtb_ref, nw_ref, st_ref, dy_ref,
             dqkv_ref, dzg_ref, dba_ref, dalog_ref, ddtb_ref, dnw_ref, d_state):
        @pl.when(pl.program_id(0) == 0)
        def _():
            d_state[...] = jnp.zeros_like(d_state)
            dalog_ref[...] = jnp.zeros_like(dalog_ref)
            ddtb_ref[...] = jnp.zeros_like(ddtb_ref)
            dnw_ref[...] = jnp.zeros_like(dnw_ref)

        args = [st_ref[...]] + _dn_args(qkv_ref, zg_ref, ba_ref, alog_ref, dtb_ref, nw_ref)
        _, vjp = jax.vjp(_dn_chunk_fn, *args)
        g = vjp((d_state[...], _dn_heads(dy_ref)))
        d_state[...] = g[0]
        for h in range(DN_HEADS):
            for n, base in enumerate((0, DN_WIDTH, 2 * DN_WIDTH)):
                dqkv_ref[:, base + h * d:base + (h + 1) * d] = g[1 + n][h]
            dzg_ref[:, h * d:(h + 1) * d] = g[4][h].astype(dzg_ref.dtype)
        dba_ref[...] = g[5].astype(dba_ref.dtype)
        lane_sum = lambda t: jnp.broadcast_to(jnp.sum(t, axis=-1, keepdims=True), t.shape)
        dalog_ref[...] += lane_sum(g[6])
        ddtb_ref[...] += lane_sum(g[7])
        dnw_ref[...] += g[8]

    hspec = pl.BlockSpec((DN_HEADS, 1, d), lambda i: (0, 0, 0))
    return pl.pallas_call(
        body, name=f"{tag}_deltanet_bwd", grid=(nchunk,),
        in_specs=_dn_specs(rev) + [pl.BlockSpec((None, DN_HEADS, d, d), lambda i: (rev(i), 0, 0, 0)),
                                   pl.BlockSpec((DN_CHUNK, DN_WIDTH), lambda i: (rev(i), 0))],
        out_specs=[pl.BlockSpec((DN_CHUNK, 3 * DN_WIDTH), lambda i: (rev(i), 0)),
                   pl.BlockSpec((DN_CHUNK, DN_WIDTH), lambda i: (rev(i), 0)),
                   pl.BlockSpec((DN_CHUNK, 128), lambda i: (rev(i), 0)),
                   hspec, hspec, pl.BlockSpec((1, 1, d), lambda i: (0, 0, 0))],
        out_shape=[jax.ShapeDtypeStruct((S, 3 * DN_WIDTH), F32), jax.ShapeDtypeStruct((S, DN_WIDTH), BF16),
                   jax.ShapeDtypeStruct((S, 128), BF16), jax.ShapeDtypeStruct((DN_HEADS, 1, d), F32),
                   jax.ShapeDtypeStruct((DN_HEADS, 1, d), F32), jax.ShapeDtypeStruct((1, 1, d), F32)],
        scratch_shapes=[pltpu.VMEM((DN_HEADS, d, d), F32)],
        compiler_params=_params(("arbitrary",)))(qkv, z, z, a_log, dtb, norm_w, states, d_y)


def _whole_fwd(fn, ins, outs, *, name):
    n = len(ins)

    def body(*refs):
        res = fn(*[r[...] for r in refs[:n]])
        for o_ref, o in zip(refs[n:], res):
            o_ref[...] = o

    return pl.pallas_call(body, name=name, out_shape=[jax.ShapeDtypeStruct(s, F32) for s in outs],
                          compiler_params=_params())(*ins)


def _whole_bwd(fn, ins, cts, n_grads, *, name, lane_sum=()):
    n, nt = len(ins), len(cts)

    def body(*refs):
        _, vjp = jax.vjp(fn, *[r[...] for r in refs[:n]])
        grads = vjp(tuple(r[...] for r in refs[n:n + nt]))
        for k in range(n_grads):
            g = grads[k]
            if k in lane_sum:
                g = jnp.broadcast_to(jnp.sum(g, axis=-1, keepdims=True), g.shape)
            refs[n + nt + k][...] = g

    return pl.pallas_call(body, name=name, out_shape=[jax.ShapeDtypeStruct(a.shape, F32) for a in ins[:n_grads]],
                          compiler_params=_params())(*ins, *cts)


S5_CHUNK = 256


def _s5_param_fn(a_re, a_im, ldt, bt_re, bt_im, expand):
    dt = jnp.exp(ldt)
    er = jnp.exp(a_re * dt)
    ab_re, ab_im = er * jnp.cos(a_im * dt), er * jnp.sin(a_im * dt)
    den = a_re * a_re + a_im * a_im
    co_re = ((ab_re - 1.0) * a_re + ab_im * a_im) / den
    co_im = (ab_im * a_re - (ab_re - 1.0) * a_im) / den
    cr, ci = _dot(expand, co_re), _dot(expand, co_im)
    return ab_re, ab_im, cr * bt_re - ci * bt_im, cr * bt_im + ci * bt_re


def _s5_scan(b_re, b_im, a_re, a_im, row, T, reverse):
    x_re, x_im, p_re, p_im = b_re, b_im, a_re, a_im
    d = 1
    while d < T:
        if reverse:
            s_re = jnp.where(row < T - d, pltpu.roll(x_re, T - d, 0), 0.0)
            s_im = jnp.where(row < T - d, pltpu.roll(x_im, T - d, 0), 0.0)
        else:
            s_re = jnp.where(row >= d, pltpu.roll(x_re, d, 0), 0.0)
            s_im = jnp.where(row >= d, pltpu.roll(x_im, d, 0), 0.0)
        x_re, x_im = x_re + p_re * s_re - p_im * s_im, x_im + p_re * s_im + p_im * s_re
        p_re, p_im = p_re * p_re - p_im * p_im, 2.0 * p_re * p_im
        d *= 2
    return x_re, x_im


def _s5_states(u, bre_ref, bim_ref, a_re, a_im, c_re, c_im, row, T):
    bu_re = _dot(u, bre_ref[...]) + jnp.where(row == 0, a_re * c_re - a_im * c_im, 0.0)
    bu_im = _dot(u, bim_ref[...]) + jnp.where(row == 0, a_re * c_im + a_im * c_re, 0.0)
    return _s5_scan(bu_re, bu_im, a_re, a_im, row, T, False)


def _s5_in_specs(order, T):
    full = lambda shape: pl.BlockSpec(shape, lambda i: (0,) * len(shape), pipeline_mode=pl.Buffered(1))
    return [pl.BlockSpec((T, S5_WIDTH), lambda i: (order(i), Z_SU // S5_WIDTH)),
            full((S5_WIDTH, S5_LANES)), full((S5_WIDTH, S5_LANES)), full((S5_WIDTH, S5_LANES)),
            full((S5_WIDTH, S5_LANES)), full((1, S5_LANES)), full((1, S5_LANES)), full((1, S5_WIDTH))]


def _s5_fwd(z, bre, bim, cre, cim, ab_re, ab_im, dskip, tag):
    S = z.shape[0]
    T = _pick(S, (S5_CHUNK, 128))
    nch = S // T

    def body(u_ref, bre_ref, bim_ref, cre_ref, cim_ref, are_ref, aim_ref, d_ref, y_ref, xre_ref, xim_ref, c_re, c_im):
        @pl.when(pl.program_id(0) == 0)
        def _():
            c_re[...] = jnp.zeros_like(c_re)
            c_im[...] = jnp.zeros_like(c_im)

        u = u_ref[...]
        row = lax.broadcasted_iota(jnp.int32, (T, S5_LANES), 0)
        x_re, x_im = _s5_states(u, bre_ref, bim_ref, are_ref[...], aim_ref[...], c_re[...], c_im[...], row, T)
        c_re[...] = jnp.sum(jnp.where(row == T - 1, x_re, 0.0), axis=0, keepdims=True)
        c_im[...] = jnp.sum(jnp.where(row == T - 1, x_im, 0.0), axis=0, keepdims=True)
        xre_ref[...] = x_re
        xim_ref[...] = x_im
        y_ref[...] = _dot(x_re, cre_ref[...], _NT) - _dot(x_im, cim_ref[...], _NT) + d_ref[...] * u

    return pl.pallas_call(
        body, name=f"{tag}_s5", grid=(nch,), in_specs=_s5_in_specs(lambda i: i, T),
        out_specs=[pl.BlockSpec((T, S5_WIDTH), lambda i: (i, 0)),
                   pl.BlockSpec((T, S5_LANES), lambda i: (i, 0)), pl.BlockSpec((T, S5_LANES), lambda i: (i, 0))],
        out_shape=[jax.ShapeDtypeStruct((S, S5_WIDTH), F32), jax.ShapeDtypeStruct((S, S5_LANES), F32),
                   jax.ShapeDtypeStruct((S, S5_LANES), F32)],
        scratch_shapes=[pltpu.VMEM((1, S5_LANES), F32), pltpu.VMEM((1, S5_LANES), F32)],
        compiler_params=_params(("arbitrary",)))(z, bre, bim, cre, cim, ab_re, ab_im, dskip)


S5_BWD_CHUNK = 128


def _s5_bwd(z, bre, bim, cre, cim, ab_re, ab_im, dskip, xre, xim, d_y, tag):
    S = z.shape[0]
    T = _pick(S, (S5_BWD_CHUNK, 64))
    nch = S // T
    rev = lambda i: nch - 1 - i
    above = lambda i: (jnp.maximum(rev(i) * (T // 8) - 1, 0), 0)
    full = lambda shape: pl.BlockSpec(shape, lambda i: (0,) * len(shape))

    def body(u_ref, bre_ref, bim_ref, cre_ref, cim_ref, are_ref, aim_ref, d_ref, xre_ref, xim_ref, pre_ref, pim_ref,
             dy_ref, du_ref, dbre_ref, dbim_ref, dcre_ref, dcim_ref, dare_ref, daim_ref, dd_ref, g_re, g_im):
        @pl.when(pl.program_id(0) == 0)
        def _():
            g_re[...] = jnp.zeros_like(g_re)
            g_im[...] = jnp.zeros_like(g_im)
            for r in (dbre_ref, dbim_ref, dcre_ref, dcim_ref, dare_ref, daim_ref, dd_ref):
                r[...] = jnp.zeros_like(r)

        u = u_ref[...]
        dy = dy_ref[...].astype(F32)
        a_re, a_im = are_ref[...], aim_ref[...]
        row = lax.broadcasted_iota(jnp.int32, (T, S5_LANES), 0)
        x_re, x_im = xre_ref[...], xim_ref[...]
        dcre_ref[...] += _dot1(dy, x_re, _TN)
        dcim_ref[...] -= _dot1(dy, x_im, _TN)
        has_before = (pl.program_id(0) < nch - 1).astype(F32)
        row8 = lax.broadcasted_iota(jnp.int32, (8, S5_LANES), 0)
        before = lambda ref: has_before * jnp.sum(jnp.where(row8 == 7, ref[...], 0.0), axis=0, keepdims=True)
        xp_re = jnp.where(row >= 1, pltpu.roll(x_re, 1, 0), 0.0) + jnp.where(row == 0, before(pre_ref), 0.0)
        xp_im = jnp.where(row >= 1, pltpu.roll(x_im, 1, 0), 0.0) + jnp.where(row == 0, before(pim_ref), 0.0)
        last = row == T - 1
        gd_re = _dot(dy, cre_ref[...]) + jnp.where(last, a_re * g_re[...] + a_im * g_im[...], 0.0)
        gd_im = -_dot(dy, cim_ref[...]) + jnp.where(last, a_re * g_im[...] - a_im * g_re[...], 0.0)
        t_re, t_im = _s5_scan(gd_re, gd_im, a_re, -a_im, row, T, True)
        g_re[...] = jnp.sum(jnp.where(row == 0, t_re, 0.0), axis=0, keepdims=True)
        g_im[...] = jnp.sum(jnp.where(row == 0, t_im, 0.0), axis=0, keepdims=True)
        du_ref[...] = (_dot1(t_re, bre_ref[...], _NT) + _dot1(t_im, bim_ref[...], _NT) + dy * d_ref[...]).astype(du_ref.dtype)
        dbre_ref[...] += _dot1(u, t_re, _TN)
        dbim_ref[...] += _dot1(u, t_im, _TN)
        dare_ref[...] += jnp.sum(t_re * xp_re + t_im * xp_im, axis=0, keepdims=True)
        daim_ref[...] += jnp.sum(t_im * xp_re - t_re * xp_im, axis=0, keepdims=True)
        dd_ref[...] += jnp.sum(dy * u, axis=0, keepdims=True)

    return pl.pallas_call(
        body, name=f"{tag}_s5_bwd", grid=(nch,),
        in_specs=_s5_in_specs(rev, T) + [pl.BlockSpec((T, S5_LANES), lambda i: (rev(i), 0)),
                                         pl.BlockSpec((T, S5_LANES), lambda i: (rev(i), 0)),
                                         pl.BlockSpec((8, S5_LANES), above), pl.BlockSpec((8, S5_LANES), above),
                                         pl.BlockSpec((T, S5_WIDTH), lambda i: (rev(i), 0))],
        out_specs=[pl.BlockSpec((T, S5_WIDTH), lambda i: (rev(i), 0))] + [full((S5_WIDTH, S5_LANES))] * 4
        + [full((1, S5_LANES))] * 2 + [full((1, S5_WIDTH))],
        out_shape=[jax.ShapeDtypeStruct((S, S5_WIDTH), BF16)] + [jax.ShapeDtypeStruct((S5_WIDTH, S5_LANES), F32)] * 4
        + [jax.ShapeDtypeStruct((1, S5_LANES), F32)] * 2 + [jax.ShapeDtypeStruct((1, S5_WIDTH), F32)],
        scratch_shapes=[pltpu.VMEM((1, S5_LANES), F32), pltpu.VMEM((1, S5_LANES), F32)],
        compiler_params=_params(("arbitrary",)))(z, bre, bim, cre, cim, ab_re, ab_im, dskip, xre, xim, xre, xim, d_y)


def _s5_glu_fn(y, glu_w, glu_b):
    g = 0.5 * y * (1.0 + jnp.tanh(math.sqrt(2.0 / math.pi) * (y + 0.044715 * (y * y * y))))
    lin = lax.dot_general(g.astype(BF16), glu_w.astype(BF16), (((1,), (0,)), ((), ())), preferred_element_type=F32)
    return (g * jax.nn.sigmoid(lin + glu_b),)


def _block_diag(m):
    G, H, P = S5_GROUPS, S5_GROUP_CH, S5_STATE
    eye = jnp.eye(G, dtype=m.dtype)
    return (m.reshape(G, H, 1, P) * eye[:, None, :, None]).reshape(G * H, G * P)


def _block_diag_take(m):
    G, H, P = S5_GROUPS, S5_GROUP_CH, S5_STATE
    eye = jnp.eye(G, dtype=m.dtype)
    return jnp.sum(m.reshape(G, H, G, P) * eye[:, None, :, None], axis=2).reshape(G * H, P)


def _loss_head(y, target, tag):
    S, D = y.shape
    tm = _pick(S, (256, 128, 64))

    def body(y_ref, t_ref, dy_ref, loss_ref):
        @pl.when(pl.program_id(0) == 0)
        def _():
            loss_ref[...] = jnp.zeros_like(loss_ref)

        err = y_ref[...] - t_ref[...]
        dy_ref[...] = err * (1.0 / D)
        part = 0.5 * jnp.sum(jnp.mean(err * err, axis=-1, keepdims=True), axis=0, keepdims=True)
        loss_ref[...] += jnp.broadcast_to(part, loss_ref.shape)

    return pl.pallas_call(
        body, name=f"{tag}_loss", grid=(S // tm,),
        in_specs=[pl.BlockSpec((tm, D), lambda i: (i, 0)), pl.BlockSpec((tm, D), lambda i: (i, 0))],
        out_specs=[pl.BlockSpec((tm, D), lambda i: (i, 0)), pl.BlockSpec((8, 128), lambda i: (0, 0))],
        out_shape=[jax.ShapeDtypeStruct((S, D), F32), jax.ShapeDtypeStruct((8, 128), F32)],
        compiler_params=_params(("arbitrary",)))(y, target)


def _adamw(w, g, m, v, name):
    shape = w.shape
    cols = shape[-1]
    rows = int(np.prod(shape[:-1]))
    tr = _pick(rows, [t for t in (512, 256, 128, 64, 32, 16, 8) if t * cols <= 256 * 1024] or [8])
    c1 = 1.0 - ADAM_B1 ** ADAM_STEP
    c2 = 1.0 - ADAM_B2 ** ADAM_STEP

    def body(w_ref, g_ref, m_ref, v_ref, d_ref, mo_ref, vo_ref):
        gg = g_ref[...]
        mn = ADAM_B1 * m_ref[...] + (1.0 - ADAM_B1) * gg
        vn = ADAM_B2 * v_ref[...] + (1.0 - ADAM_B2) * (gg * gg)
        d_ref[...] = -ADAM_LR * ((mn / c1) / (jnp.sqrt(vn / c2) + ADAM_EPS) + ADAM_WD * w_ref[...])
        mo_ref[...] = mn
        vo_ref[...] = vn

    spec = pl.BlockSpec((tr, cols), lambda i: (i, 0))
    outs = pl.pallas_call(
        body, name=name, grid=(rows // tr,), in_specs=[spec] * 4, out_specs=[spec] * 3,
        out_shape=[jax.ShapeDtypeStruct((rows, cols), F32)] * 3,
        compiler_params=_params(("parallel",)))(*[a.reshape(rows, cols) for a in (w, g, m, v)])
    return [o.reshape(shape) for o in outs]


def _sum_slots(slots, name):
    n, R, C = slots.shape
    tr = _pick(R, [t for t in (1024, 512, 256, 128, 64, 32, 16, 8) if n * t * C * 4 <= (2 << 20)] or [8])

    def body(s_ref, o_ref):
        acc = s_ref[0].astype(F32)
        for k in range(1, n):
            acc = acc + s_ref[k].astype(F32)
        o_ref[...] = acc

    return pl.pallas_call(
        body, name=name, grid=(R // tr,), in_specs=[pl.BlockSpec((n, tr, C), lambda i: (0, i, 0))],
        out_specs=pl.BlockSpec((tr, C), lambda i: (i, 0)),
        out_shape=jax.ShapeDtypeStruct((R, C), F32), compiler_params=_params(("parallel",)))(slots)


def _row_tile(R, C, itemsize, target=1 << 20):
    return _pick(R, [t for t in (2048, 1024, 512, 256, 128, 64, 32, 16) if t * C * itemsize <= target] or [16])


def _add_sibling(part, got, name):
    n, _, R2, C = part.shape
    tr = _row_tile(R2, C, 2)

    def body(p_ref, q_ref, o_ref):
        o_ref[...] = (p_ref[...].astype(F32) + q_ref[...].astype(F32)).astype(o_ref.dtype)

    spec = pl.BlockSpec((None, tr, C), lambda k, i: (k, i, 0))
    return pl.pallas_call(
        body, name=name, grid=(n, R2 // tr),
        in_specs=[pl.BlockSpec((None, None, tr, C), lambda k, i: (k, lax.axis_index("c"), i, 0)), spec], out_specs=spec,
        out_shape=jax.ShapeDtypeStruct((n, R2, C), BF16), compiler_params=_params(("parallel", "parallel")))(part, got)


def _sum_chips(chip_part, others, name):
    _, R2, C = chip_part.shape
    tr = _row_tile(R2, C, 4)

    def body(a_ref, b_ref, o_ref):
        acc = a_ref[...].astype(F32)
        for j in range(N_SHARD - 1):
            acc = acc + b_ref[j].astype(F32)
        o_ref[...] = acc

    return pl.pallas_call(
        body, name=name, grid=(R2 // tr,),
        in_specs=[pl.BlockSpec((None, tr, C), lambda i: (2 * lax.axis_index("x") + lax.axis_index("y"), i, 0)),
                  pl.BlockSpec((N_SHARD - 1, tr, C), lambda i: (0, i, 0))],
        out_specs=pl.BlockSpec((None, tr, C), lambda i: (lax.axis_index("c"), i, 0)),
        out_shape=jax.ShapeDtypeStruct((2, R2, C), F32), compiler_params=_params(("parallel",)))(chip_part, others)


_ANY = pl.BlockSpec(memory_space=pl.ANY)


def _place():
    return lax.axis_index("x"), lax.axis_index("y"), lax.axis_index("c")


def _other_chips(x, y):
    return [(1 - x, y), (x, 1 - y), (1 - x, 1 - y)]


def _remote(src, dst, send_sems, recv_sems, n, to):
    return pltpu.make_async_remote_copy(src_ref=src, dst_ref=dst, send_sem=send_sems.at[n], recv_sem=recv_sems.at[n],
                                        device_id=to, device_id_type=MESH_ID)


_JOB_SEMS = {"gather_ici": 3, "gather_d2d": 4, "scatter": 3}


def _job_out_shape(job):
    src = job["ins"][0]
    if job["kind"] == "gather_ici":
        return jax.ShapeDtypeStruct((N_SHARD,) + src.shape, src.dtype)
    if job["kind"] == "gather_d2d":
        return jax.ShapeDtypeStruct(src.shape, src.dtype)
    return jax.ShapeDtypeStruct((N_SHARD - 1,) + src.shape[1:], src.dtype)


def _job_copies(kind, ins, out, send_sems, recv_sems, base):
    x, y, c = _place()
    k = 2 * x + y
    sibling = (x, y, 1 - c)
    sends, recvs = [], []
    for j, (cx, cy) in enumerate(_other_chips(x, y)):
        kj = 2 * cx + cy
        if kind == "gather_ici":
            sends.append(_remote(ins[0].at[c], out.at[k, c], send_sems, recv_sems, base + j, (cx, cy, c)))
            recvs.append(_remote(out.at[kj, c], out.at[kj, c], send_sems, recv_sems, base + j, sibling))
        elif kind == "gather_d2d":
            sends.append(_remote(out.at[kj, c], out.at[kj, c], send_sems, recv_sems, base + 1 + j, sibling))
            recvs.append(_remote(out.at[kj, 1 - c], out.at[kj, 1 - c], send_sems, recv_sems, base + 1 + j, sibling))
        else:
            sends.append(_remote(ins[0].at[kj], out.at[j], send_sems, recv_sems, base + j, (cx, cy, c)))
            recvs.append(_remote(out.at[j], out.at[j], send_sems, recv_sems, base + j, sibling))
    if kind == "gather_d2d":
        sends.append(_remote(ins[1], out.at[k], send_sems, recv_sems, base, sibling))
        recvs.append(_remote(out.at[k], out.at[k], send_sems, recv_sems, base, sibling))
    return sends, recvs


def _gather_shards(shards, name):
    n = len(shards)
    per = 7

    def body(*refs):
        ins, outs = refs[:n], refs[n:2 * n]
        send_sems, recv_sems = refs[2 * n:]
        x, y, c = _place()
        k = 2 * x + y
        sibling = (x, y, 1 - c)
        chips = _other_chips(x, y)
        started = []
        for a in range(n):
            cp = _remote(ins[a], outs[a].at[k], send_sems, recv_sems, per * a, sibling)
            cp.start()
            started.append(cp)
            for j, (cx, cy) in enumerate(chips):
                cp = _remote(ins[a].at[c], outs[a].at[k, c], send_sems, recv_sems, per * a + 1 + j, (cx, cy, c))
                cp.start()
                started.append(cp)
        for a in range(n):
            for j, (cx, cy) in enumerate(chips):
                landed = outs[a].at[2 * cx + cy, c]
                _remote(landed, landed, send_sems, recv_sems, per * a + 1 + j, sibling).wait_recv()
                cp = _remote(landed, landed, send_sems, recv_sems, per * a + 4 + j, sibling)
                cp.start()
                started.append(cp)
        for a in range(n):
            own = outs[a].at[k]
            _remote(own, own, send_sems, recv_sems, per * a, sibling).wait_recv()
            for j, (cx, cy) in enumerate(chips):
                passed = outs[a].at[2 * cx + cy, 1 - c]
                _remote(passed, passed, send_sems, recv_sems, per * a + 4 + j, sibling).wait_recv()
        for cp in started:
            cp.wait_send()

    return pl.pallas_call(
        body, name=name, in_specs=[_ANY] * n, out_specs=[_ANY] * n,
        out_shape=[jax.ShapeDtypeStruct((N_SHARD,) + s.shape, s.dtype) for s in shards],
        scratch_shapes=[pltpu.SemaphoreType.DMA((per * n,)), pltpu.SemaphoreType.DMA((per * n,))],
        )(*shards)


def _swap_other_half(parts, name):
    n = len(parts)

    def body(*refs):
        ins, outs = refs[:n], refs[n:2 * n]
        send_sems, recv_sems = refs[2 * n:]
        x, y, c = _place()
        started = []
        for a in range(n):
            for k in range(N_SHARD):
                cp = _remote(ins[a].at[k, 1 - c], outs[a].at[k], send_sems, recv_sems, N_SHARD * a + k, (x, y, 1 - c))
                cp.start()
                started.append(cp)
        for cp in started:
            cp.wait()

    return pl.pallas_call(
        body, name=name, in_specs=[_ANY] * n, out_specs=[_ANY] * n,
        out_shape=[jax.ShapeDtypeStruct((N_SHARD,) + p.shape[2:], p.dtype) for p in parts],
        scratch_shapes=[pltpu.SemaphoreType.DMA((N_SHARD * n,)), pltpu.SemaphoreType.DMA((N_SHARD * n,))],
        )(*parts)


def _scatter_to_chips(parts, name):
    n = len(parts)
    per = N_SHARD - 1

    def body(*refs):
        ins, outs = refs[:n], refs[n:2 * n]
        send_sems, recv_sems = refs[2 * n:]
        x, y, c = _place()
        chips = _other_chips(x, y)
        started = []
        for a in range(n):
            for j, (cx, cy) in enumerate(chips):
                cp = _remote(ins[a].at[2 * cx + cy], outs[a].at[j], send_sems, recv_sems, per * a + j, (cx, cy, c))
                cp.start()
                started.append(cp)
        for cp in started:
            cp.wait()

    return pl.pallas_call(
        body, name=name, in_specs=[_ANY] * n, out_specs=[_ANY] * n,
        out_shape=[jax.ShapeDtypeStruct((per,) + p.shape[1:], p.dtype) for p in parts],
        scratch_shapes=[pltpu.SemaphoreType.DMA((per * n,)), pltpu.SemaphoreType.DMA((per * n,))],
        )(*parts)


def _join_halves(bufs, name):
    n = len(bufs)

    def body(*refs):
        outs = refs[n:2 * n]
        send_sems, recv_sems = refs[2 * n:]
        x, y, c = _place()
        started = []
        for a in range(n):
            cp = _remote(outs[a].at[c], outs[a].at[c], send_sems, recv_sems, a, (x, y, 1 - c))
            cp.start()
            started.append(cp)
        for a in range(n):
            arrives = outs[a].at[1 - c]
            _remote(arrives, arrives, send_sems, recv_sems, a, (x, y, 1 - c)).wait_recv()
        for cp in started:
            cp.wait_send()

    return pl.pallas_call(
        body, name=name, in_specs=[_ANY] * n, out_specs=[_ANY] * n,
        out_shape=[jax.ShapeDtypeStruct(b.shape, b.dtype) for b in bufs],
        input_output_aliases={a: a for a in range(n)},
        scratch_shapes=[pltpu.SemaphoreType.DMA((n,)), pltpu.SemaphoreType.DMA((n,))],
        )(*bufs)


def _gather_all_devices(vec, name):
    def body(in_ref, out_ref, send_sems, recv_sems, local_sem):
        x, y, c = _place()
        me = 4 * x + 2 * y + c
        own = pltpu.make_async_copy(in_ref, out_ref.at[me], local_sem)
        own.start()
        sends = []
        for r in range(1, 8):
            fx, fy, fc = (r >> 2) & 1, (r >> 1) & 1, r & 1
            to = (x ^ fx, y ^ fy, c ^ fc)
            sends.append(pltpu.make_async_remote_copy(
                src_ref=in_ref, dst_ref=out_ref.at[me], send_sem=send_sems.at[r - 1], recv_sem=recv_sems.at[r - 1],
                device_id=to, device_id_type=MESH_ID))
            sends[-1].start()
        for r in range(1, 8):
            fx, fy, fc = (r >> 2) & 1, (r >> 1) & 1, r & 1
            slot = out_ref.at[4 * (x ^ fx) + 2 * (y ^ fy) + (c ^ fc)]
            pltpu.make_async_remote_copy(src_ref=slot, dst_ref=slot, send_sem=send_sems.at[r - 1],
                                         recv_sem=recv_sems.at[r - 1], device_id=(x, y, c),
                                         device_id_type=MESH_ID).wait_recv()
        for cp in sends:
            cp.wait_send()
        own.wait()

    return pl.pallas_call(
        body, name=name, in_specs=[_ANY], out_specs=_ANY, out_shape=jax.ShapeDtypeStruct((8,) + vec.shape, vec.dtype),
        scratch_shapes=[pltpu.SemaphoreType.DMA((7,)), pltpu.SemaphoreType.DMA((7,)), pltpu.SemaphoreType.DMA(())],
        )(vec)


def _permute_w_in(w):
    cut = Z_SU
    return jnp.concatenate([w[:, :cut], w[:, cut + 2 * DN_HEADS:], w[:, cut:cut + 2 * DN_HEADS],
                            jnp.zeros((w.shape[0], Z_WIDTH - IN_WIDTH), w.dtype)], axis=1)


def _unpermute_w_in(wp):
    return jnp.concatenate([wp[:, :Z_SU], wp[:, Z_DBA:Z_DBA + 2 * DN_HEADS], wp[:, Z_SU:Z_DBA]], axis=1)


def _s5_inputs(sp):
    ab_re, ab_im, bb_re, bb_im = _whole_fwd(
        _s5_param_fn, sp["s5_pins"], [(S5_GROUPS, S5_STATE)] * 2 + [(S5_WIDTH, S5_STATE)] * 2, name="s5_params")
    return (_block_diag(bb_re), _block_diag(bb_im), _block_diag(sp["c_re"]), _block_diag(sp["c_im"]),
            ab_re.reshape(1, S5_LANES), ab_im.reshape(1, S5_LANES), sp["dskip"])


def _mixer_fwd(x, sp, weights, rope, mm):
    D = x.shape[1]
    (h,) = _rows_fwd(_prenorm_fn, [(x, D, 0)], [sp["mix_norm_pre"]], [(D, D, BF16)], name="mix_prenorm")
    w_in = weights("win")
    z = mm("win", h, w_in, out_dtype=F32, name="mix_in")
    w_out, glu_w = weights("wout")
    y_attn = _attn_fwd(z, rope[0], rope[1], sp["sinks"], "mix")
    qkv = _conv_fwd(z, sp["conv_w"], "mix")
    y_dn, states = _dn_fwd(qkv, z, sp["dn_a_log"], sp["dn_dt_bias"], sp["dn_norm_w"], "mix")
    s5_in = _s5_inputs(sp)
    y_lin, *s5_states = _s5_fwd(z, *s5_in, "mix")
    (y_s5,) = _rows_fwd(_s5_glu_fn, [(y_lin, S5_WIDTH, 0)], [glu_w, sp["glu_b"]], [(S5_WIDTH, S5_WIDTH, BF16)],
                        name="mix_s5_glu")
    cat = jnp.concatenate([y_attn, y_dn, y_s5], axis=1)
    mixed = mm("wout", cat, w_out, out_dtype=F32, name="mix_out")
    (x_new,) = _rows_fwd(_residual_fn(1.0), [(x, D, 0), (mixed, D, 0)], [sp["mix_norm_post"]], [(D, D, F32)],
                         name="mix_residual")
    return x_new, (h, z, qkv, states, s5_in, y_lin, s5_states, cat, mixed, w_in, w_out, glu_w)


def _mixer_bwd(dx_new, x, sp, rope, saved, mm, sink):
    h, z, qkv, states, s5_in, y_lin, s5_states, cat, mixed, w_in, w_out, glu_w = saved
    D = x.shape[1]
    G, H, P = S5_GROUPS, S5_GROUP_CH, S5_STATE
    d_mixed, d_g_post = _rows_bwd(_postnorm_fn(1.0), [(mixed, D, 0)], [sp["mix_norm_post"]], [(dx_new, D, 0)],
                                  [(0, D, BF16)], name="mix_postnorm_bwd")
    d_cat = mm(d_mixed, w_out, tb=True, out_dtype=BF16, name="mix_out_dx")
    sink("wout", mm(cat, d_mixed, ta=True, out_dtype=BF16, name="mix_out_dw"))
    d_attn, d_dn, d_s5 = d_cat[:, :ATTN_WIDTH], d_cat[:, ATTN_WIDTH:ATTN_WIDTH + DN_WIDTH], d_cat[:, ATTN_WIDTH + DN_WIDTH:]
    d_ylin, d_glu_w, d_glu_b = _rows_bwd(_s5_glu_fn, [(y_lin, S5_WIDTH, 0)], [glu_w, sp["glu_b"]],
                                         [(d_s5, S5_WIDTH, 0)], [(0, S5_WIDTH, F32)], name="mix_s5_glu_bwd")
    sink("glu", d_glu_w.astype(BF16))
    d_us5, d_bre, d_bim, d_cre, d_cim, d_are, d_aim, d_dskip = _s5_bwd(z, *s5_in, *s5_states, d_ylin, "mix")
    cts = [d_are.reshape(G, P), d_aim.reshape(G, P), _block_diag_take(d_bre), _block_diag_take(d_bim)]
    d_a_re, d_a_im, d_ldt, d_bt_re, d_bt_im = _whole_bwd(_s5_param_fn, sp["s5_pins"], cts, 5, name="s5_params_bwd",
                                                         lane_sum=(2,))
    from_t = lambda m: m.reshape(G, H, P).transpose(0, 2, 1)
    d_qkv, d_zg, d_ba, d_alog, d_dtb, d_nw = _dn_bwd(qkv, z, sp["dn_a_log"], sp["dn_dt_bias"], sp["dn_norm_w"], states,
                                                     d_dn, "mix")
    d_uconv, d_conv_w = _conv_bwd(z, sp["conv_w"], d_qkv, "mix")
    d_q, d_kv, d_sinks = _attn_bwd(z, rope[0], rope[1], sp["sinks"], d_attn, "mix")
    d_z = jnp.concatenate([d_q, d_kv, d_uconv, d_zg, d_us5, d_ba], axis=1)
    sink("win", mm(h, d_z, ta=True, out_dtype=BF16, name="mix_in_dw"))
    d_h = mm(d_z, w_in, tb=True, out_dtype=BF16, name="mix_in_dx")
    dx, d_g_pre = _rows_bwd(_prenorm_fn, [(x, D, 0)], [sp["mix_norm_pre"]], [(d_h, D, 0)], [(0, D, F32)],
                            name="mix_prenorm_bwd", add_to_first=dx_new)
    small = {
        "mix_norm_pre": d_g_pre[0], "mix_norm_post": d_g_post[0], "attn_sinks": d_sinks[:, 0], "dn_conv_w": d_conv_w,
        "dn_a_log": d_alog[:, 0, 0], "dn_dt_bias": d_dtb[:, 0, 0], "dn_norm_w": d_nw[0, 0],
        "s5_a_re": d_a_re, "s5_a_im": d_a_im, "s5_log_dt": d_ldt[:, 0], "s5_b_re": from_t(d_bt_re),
        "s5_b_im": from_t(d_bt_im), "s5_c_re": _block_diag_take(d_cre).reshape(G, H, P),
        "s5_c_im": _block_diag_take(d_cim).reshape(G, H, P), "s5_d": d_dskip[0], "s5_glu_b": d_glu_b[0],
    }
    return dx, small


BIG_PIECES = ("gu1", "wd1", "win", "wout", "glu", "gu2", "wd2")


def _halves(m):
    return m.reshape(m.shape[:-2] + (2, m.shape[-2] // 2, m.shape[-1]))


def _whole(m):
    return m.reshape(m.shape[:-3] + (2 * m.shape[-2], m.shape[-1]))


class _WeightGather:
    def __init__(self, shards):
        self.shards = shards
        pieces = list(shards[0])
        flat = [s for p in pieces for s in shards[0][p]]
        got = iter(_gather_shards(flat, "gather_weights"))
        self.ready = {(0, p): [next(got) for _ in shards[0][p]] for p in pieces}
        self.landing = None

    def weights(self, l, piece):
        return self.ready[(l, piece)]

    def mm(self, l, piece, a, b, **kw):
        jobs, done = [], None
        if self.landing is not None:
            done = self.landing
            jobs += [{"kind": "gather_d2d", "ins": [buf, s]} for buf, s in zip(done[2], self.shards[done[0]][done[1]])]
        n_done = len(jobs)
        if l + 1 < len(self.shards):
            jobs += [{"kind": "gather_ici", "ins": [s]} for s in self.shards[l + 1][piece]]
        if not jobs:
            return _mm(a, b, **kw)
        out, bufs = _mm(a, b, jobs=jobs, **kw)
        if done is not None:
            self.ready[done[:2]] = bufs[:n_done]
        self.landing = (l + 1, piece, bufs[n_done:]) if l + 1 < len(self.shards) else None
        return out


class _GradReduce:
    def __init__(self):
        self.waiting, self.landed = [], {}

    def add(self, piece, grad):
        part = _halves(grad)
        (got,) = _swap_other_half([part], "reduce_siblings")
        self.waiting.append((piece, _add_sibling(part, got, "reduce_siblings_add")))

    def mm(self, a, b, **kw):
        if not self.waiting:
            return _mm(a, b, **kw)
        out, others = _mm(a, b, jobs=[{"kind": "scatter", "ins": [cp]} for _, cp in self.waiting], **kw)
        for (piece, cp), o in zip(self.waiting, others):
            self.landed[piece] = (cp, o)
        self.waiting = []
        return out

    def finish(self):
        if self.waiting:
            others = _scatter_to_chips([cp for _, cp in self.waiting], "reduce_chips")
            for (piece, cp), o in zip(self.waiting, others):
                self.landed[piece] = (cp, o)
        pieces = list(self.landed)
        joined = _join_halves([_sum_chips(*self.landed[p], "reduce_chips_sum") for p in pieces], "reduce_join")
        return {p: _whole(m) for p, m in zip(pieces, joined)}


def _pack_small(arrs, extra=()):
    flat = jnp.concatenate([a.reshape(-1) for a in arrs] + list(extra))
    n = flat.shape[0]
    quantum = 8 * SMALL_LANES
    padded = -(-n // quantum) * quantum
    return jnp.concatenate([flat, jnp.zeros((padded - n,), flat.dtype)]).reshape(padded // SMALL_LANES, SMALL_LANES)


def _unpack_small(flat2d, shapes):
    flat = flat2d.reshape(-1)
    out, o = [], 0
    for s in shapes:
        n = int(np.prod(s))
        out.append(flat[o:o + n].reshape(s))
        o += n
    return out, flat[o:]


def _step(a):
    x, target = a["x"][0], a["loss_target"][0]
    S, D = x.shape
    L, _, Fs = a["ff1_w_gate"].shape
    px, py, pc = _place()
    chip = 2 * px + py

    def rows_of_chips(g):
        g = _whole(g)
        return g.reshape(N_SHARD * g.shape[1], g.shape[2])

    shards = []
    for l in range(L):
        half = lambda m: _halves(m.astype(BF16))
        shards.append({
            "gu1": [half(jnp.concatenate([a["ff1_w_gate"][l], a["ff1_w_up"][l]], axis=1))], "wd1": [half(a["ff1_w_down"][l])],
            "win": [half(a["w_in"][l])], "wout": [half(a["w_out"][l]), half(a["s5_glu_w"][l])],
            "gu2": [half(jnp.concatenate([a["ff2_w_gate"][l], a["ff2_w_up"][l]], axis=1))], "wd2": [half(a["ff2_w_down"][l])]})
    gather = _WeightGather(shards)

    def ffn_weights(l, f):
        def get(piece):
            (g,) = gather.weights(l, piece + f)
            return _whole(g) if piece == "gu" else rows_of_chips(g)
        return get

    def mixer_weights(l):
        def get(piece):
            got = gather.weights(l, piece)
            if piece == "win":
                return _permute_w_in(_whole(got[0]).transpose(1, 0, 2).reshape(D, IN_WIDTH))
            return rows_of_chips(got[0]), rows_of_chips(got[1])
        return get

    conv_local = a["dn_conv_w"].reshape(-1)
    conv_rows = -(-conv_local.shape[0] // (16 * FLAT_LANES)) * 16
    conv_pad = jnp.concatenate([conv_local, jnp.zeros((conv_rows * FLAT_LANES - conv_local.shape[0],), F32)])
    (conv_all,) = _gather_shards([conv_pad.reshape(2, conv_rows // 2, FLAT_LANES)], "gather_conv")
    conv_all = conv_all.reshape(N_SHARD, -1)[:, :conv_local.shape[0]].reshape(N_SHARD, L, DN_CONV, -1)
    conv_full = conv_all.transpose(1, 2, 0, 3).reshape(L, DN_CONV, 3 * DN_WIDTH)

    expand = jnp.repeat(jnp.eye(S5_GROUPS, dtype=F32), S5_GROUP_CH, axis=0)
    lanes = lambda v, n=128: jnp.broadcast_to(v[:, None], (v.shape[0], n))
    to_t = lambda m: m.transpose(0, 2, 1).reshape(S5_WIDTH, S5_STATE)

    def small_params(l):
        sp = {k: a[k][l][None] for k in ("ff1_norm_pre", "ff1_norm_post", "mix_norm_pre", "mix_norm_post",
                                         "ff2_norm_pre", "ff2_norm_post")}
        sp["sinks"] = lanes(a["attn_sinks"][l])
        sp["conv_w"] = conv_full[l]
        sp["dn_a_log"] = lanes(a["dn_a_log"][l])[:, None, :]
        sp["dn_dt_bias"] = lanes(a["dn_dt_bias"][l])[:, None, :]
        sp["dn_norm_w"] = a["dn_norm_w"][l][None, None]
        sp["s5_pins"] = [a["s5_a_re"][l], a["s5_a_im"][l], lanes(a["s5_log_dt"][l], S5_STATE),
                         to_t(a["s5_b_re"][l]), to_t(a["s5_b_im"][l]), expand]
        sp["c_re"] = a["s5_c_re"][l].reshape(S5_WIDTH, S5_STATE)
        sp["c_im"] = a["s5_c_im"][l].reshape(S5_WIDTH, S5_STATE)
        sp["dskip"] = a["s5_d"][l][None]
        sp["glu_b"] = a["s5_glu_b"][l][None]
        return sp

    rope = _rope_tables(S)
    sps = [small_params(l) for l in range(L)]

    saved = []
    for l in range(L):
        sp = sps[l]
        mm_of = lambda f: (lambda piece, p, q, **kw: gather.mm(l, piece + f, p, q, **kw))
        x1, s1 = _ffn_fwd(x, sp["ff1_norm_pre"], sp["ff1_norm_post"], ffn_weights(l, "1"), "ff1", mm_of("1"))
        x2, s2 = _mixer_fwd(x1, sp, mixer_weights(l), rope, mm_of(""))
        x3, s3 = _ffn_fwd(x2, sp["ff2_norm_pre"], sp["ff2_norm_post"], ffn_weights(l, "2"), "ff2", mm_of("2"))
        saved.append((x, s1, x1, s2, x2, s3))
        x = x3
    dx, loss_part = _loss_head(x, target, "head")

    big_grads, small_grads = [None] * L, [None] * L
    shard_major = lambda m: m.reshape(N_SHARD, m.shape[0] // N_SHARD, m.shape[1])
    for l in reversed(range(L)):
        sp = sps[l]
        x0, s1, x1, s2, x2, s3 = saved[l]
        red = _GradReduce()

        def ffn_sink(f):
            return lambda piece, g: red.add(piece + f, g if piece == "gu" else shard_major(g))

        def mixer_sink(piece, g):
            if piece == "win":
                g = _unpermute_w_in(g).reshape(D, N_SHARD, IN_WIDTH // N_SHARD).transpose(1, 0, 2)
            red.add(piece, g if piece == "win" else shard_major(g))

        dx, g_pre2, g_post2 = _ffn_bwd(dx, x2, sp["ff2_norm_pre"], sp["ff2_norm_post"], s3, "ff2", red.mm, ffn_sink("2"))
        dx, sg = _mixer_bwd(dx, x1, sp, rope, s2, red.mm, mixer_sink)
        dx, g_pre1, g_post1 = _ffn_bwd(dx, x0, sp["ff1_norm_pre"], sp["ff1_norm_post"], s1, "ff1", red.mm, ffn_sink("1"))
        big_grads[l] = red.finish()
        sg.update({"ff1_norm_pre": g_pre1[0], "ff1_norm_post": g_post1[0], "ff2_norm_pre": g_pre2[0],
                   "ff2_norm_post": g_post2[0]})
        small_grads[l] = sg
    grad_x = dx[None]

    grads = {}
    layers = lambda k: jnp.stack([big_grads[l][k] for l in range(L)])
    for f, (gu, wd) in (("ff1", ("gu1", "wd1")), ("ff2", ("gu2", "wd2"))):
        gus = layers(gu)
        grads[f + "_w_gate"], grads[f + "_w_up"] = gus[:, :, :Fs], gus[:, :, Fs:]
        grads[f + "_w_down"] = layers(wd)
    grads["w_in"], grads["w_out"], grads["s5_glu_w"] = layers("win"), layers("wout"), layers("glu")

    small_local = [jnp.stack([small_grads[l][n] for l in range(L)]) for n in SMALL]
    vec = _pack_small(small_local, extra=(loss_part[0, :1],))
    total = _sum_slots(_gather_all_devices(vec, "gather_small"), "sum_small")
    small_total, rest = _unpack_small(total, [g.shape for g in small_local])
    loss = rest[0]
    for n, g in zip(SMALL, small_total):
        grads[n] = g
    cw = 3 * DN_WIDTH // N_SHARD
    grads["dn_conv_w"] = lax.dynamic_slice_in_dim(grads["dn_conv_w"], chip * cw, cw, axis=2)

    delta, new_m, new_v = {}, {}, {}
    for n in BIG:
        delta[n], new_m[n], new_v[n] = _adamw(a[n], grads[n], a["m_" + n], a["v_" + n], "adamw_" + n)
    shapes = [a[n].shape for n in SMALL]
    packed = [_pack_small([src[n] for n in SMALL]) for src in
              (a, grads, {n: a["m_" + n] for n in SMALL}, {n: a["v_" + n] for n in SMALL})]
    for dst, res in zip((delta, new_m, new_v), _adamw(*packed, "adamw_small")):
        for n, val in zip(SMALL, _unpack_small(res, shapes)[0]):
            dst[n] = val
    return (loss, grad_x, *[grads[n] for n in WEIGHTS], *[delta[n] for n in WEIGHTS], *[new_m[n] for n in WEIGHTS],
            *[new_v[n] for n in WEIGHTS])


def kernel(x, ff1_norm_pre, ff1_w_gate, ff1_w_up, ff1_w_down, ff1_norm_post, mix_norm_pre, w_in, attn_sinks, dn_conv_w, dn_a_log, dn_dt_bias, dn_norm_w, s5_a_re, s5_a_im, s5_log_dt, s5_b_re, s5_b_im, s5_c_re, s5_c_im, s5_d, s5_glu_w, s5_glu_b, w_out, mix_norm_post, ff2_norm_pre, ff2_w_gate, ff2_w_up, ff2_w_down, ff2_norm_post, loss_target, m_ff1_norm_pre, m_ff1_w_gate, m_ff1_w_up, m_ff1_w_down, m_ff1_norm_post, m_mix_norm_pre, m_w_in, m_attn_sinks, m_dn_conv_w, m_dn_a_log, m_dn_dt_bias, m_dn_norm_w, m_s5_a_re, m_s5_a_im, m_s5_log_dt, m_s5_b_re, m_s5_b_im, m_s5_c_re, m_s5_c_im, m_s5_d, m_s5_glu_w, m_s5_glu_b, m_w_out, m_mix_norm_post, m_ff2_norm_pre, m_ff2_w_gate, m_ff2_w_up, m_ff2_w_down, m_ff2_norm_post, v_ff1_norm_pre, v_ff1_w_gate, v_ff1_w_up, v_ff1_w_down, v_ff1_norm_post, v_mix_norm_pre, v_w_in, v_attn_sinks, v_dn_conv_w, v_dn_a_log, v_dn_dt_bias, v_dn_norm_w, v_s5_a_re, v_s5_a_im, v_s5_log_dt, v_s5_b_re, v_s5_b_im, v_s5_c_re, v_s5_c_im, v_s5_d, v_s5_glu_w, v_s5_glu_b, v_w_out, v_mix_norm_post, v_ff2_norm_pre, v_ff2_w_gate, v_ff2_w_up, v_ff2_w_down, v_ff2_norm_post):
    return _step(dict(locals()))
```

```python
import functools
import math

import numpy as np
import jax
import jax.numpy as jnp
from jax import lax
from jax.experimental import pallas as pl
from jax.experimental.pallas import tpu as pltpu

F32 = jnp.float32
BF16 = jnp.bfloat16
HI = lax.Precision.HIGHEST
MESH_ID = pl.DeviceIdType.MESH

NORM_EPS = 1e-6
FFN_RES_WEIGHT = 0.5
ATTN_HEADS, ATTN_KV_HEADS, HEAD_DIM, WINDOW = 8, 2, 128, 128
ROPE_THETA = 10000.0
DN_HEADS, DN_HEAD_DIM, DN_CONV, DN_CHUNK = 4, 128, 4, 64
S5_GROUPS, S5_GROUP_CH, S5_STATE = 32, 16, 64
ATTN_WIDTH = ATTN_HEADS * HEAD_DIM
ATTN_KV_WIDTH = ATTN_KV_HEADS * HEAD_DIM
DN_WIDTH = DN_HEADS * DN_HEAD_DIM
S5_WIDTH = S5_GROUPS * S5_GROUP_CH
S5_LANES = S5_GROUPS * S5_STATE
MIX_WIDTH = ATTN_WIDTH + DN_WIDTH + S5_WIDTH
IN_SPLITS = (ATTN_WIDTH, ATTN_KV_WIDTH, ATTN_KV_WIDTH, 3 * DN_WIDTH, DN_WIDTH, DN_HEADS, DN_HEADS, S5_WIDTH)
IN_WIDTH = sum(IN_SPLITS)
Z_AQ, Z_AK, Z_AV = 0, ATTN_WIDTH, ATTN_WIDTH + ATTN_KV_WIDTH
Z_DQKV = ATTN_WIDTH + 2 * ATTN_KV_WIDTH
Z_DZ = Z_DQKV + 3 * DN_WIDTH
Z_SU = Z_DZ + DN_WIDTH
Z_DBA = Z_SU + S5_WIDTH
Z_WIDTH = Z_DBA + 128

ADAM_LR, ADAM_B1, ADAM_B2, ADAM_EPS, ADAM_WD, ADAM_STEP = 0.001, 0.9, 0.999, 1e-08, 0.01, 10

N_SHARD = 4
FLAT_LANES = 512
SMALL_LANES = 1024
VMEM_LIMIT = 56 * 1024 * 1024

WEIGHTS = ['ff1_norm_pre', 'ff1_w_gate', 'ff1_w_up', 'ff1_w_down', 'ff1_norm_post', 'mix_norm_pre', 'w_in',
           'attn_sinks', 'dn_conv_w', 'dn_a_log', 'dn_dt_bias', 'dn_norm_w', 's5_a_re', 's5_a_im', 's5_log_dt',
           's5_b_re', 's5_b_im', 's5_c_re', 's5_c_im', 's5_d', 's5_glu_w', 's5_glu_b', 'w_out', 'mix_norm_post',
           'ff2_norm_pre', 'ff2_w_gate', 'ff2_w_up', 'ff2_w_down', 'ff2_norm_post']
BIG = ['ff1_w_gate', 'ff1_w_up', 'ff1_w_down', 'w_in', 's5_glu_w', 'w_out', 'ff2_w_gate', 'ff2_w_up', 'ff2_w_down']
SMALL = [n for n in WEIGHTS if n not in BIG]


def _pick(dim, cands):
    for c in cands:
        if dim % c == 0:
            return c
    return dim


def _params(sem=None):
    return pltpu.CompilerParams(dimension_semantics=sem, vmem_limit_bytes=VMEM_LIMIT)


def _silu_mul(gate, up):
    return gate * jax.nn.sigmoid(gate) * up


def _mm(a, b, *, ta=False, tb=False, out_dtype=F32, name, b_groups=None, bk_groups=None, out_groups=None, jobs=(),
        a_swiglu=None, swiglu_bwd_of=None):
    K, M = a.shape if ta else a.shape[::-1]
    if a_swiglu:
        K, M = (K, M // 2) if ta else (K // 2, M)
    Ng = Kg = None
    if b_groups:
        assert not tb
        _, Kb, Ng = b.shape
        N = b_groups * Ng
    elif bk_groups:
        assert tb
        _, N, Kg = b.shape
        Kb = bk_groups * Kg
    else:
        N, Kb = (b.shape if tb else b.shape[::-1])
    assert K == Kb, (a.shape, b.shape, ta, tb)
    tiles = (1408, 1024, 512, 384, 256, 128)
    tm = _pick(M, (1024, 512, 256, 128))
    tk = _pick(Kg if bk_groups else K, (2048,) + tiles)
    tn = _pick(N // out_groups if out_groups else (Ng if b_groups else N), tiles)
    if a_swiglu and ta:
        tm, tk = _pick(a_swiglu, tiles), _pick(K, tiles[1:])
    elif a_swiglu:
        tm, tk, tn = _pick(M, (512, 256, 128)), _pick(a_swiglu, tiles), _pick(N, (2048,) + tiles)
    if swiglu_bwd_of is not None:
        tm, tn = _pick(M, (512, 256, 128)), swiglu_bwd_of.shape[1] // (2 * N_SHARD)
        assert tk == K
    nk = K // tk
    dims = (((0,) if ta else (1,), (1,) if tb else (0,)), ((), ()))

    grid = (M // tm, N // tn, nk)
    n_job_in = sum(len(j["ins"]) for j in jobs)
    n_sems = sum(_JOB_SEMS[j["kind"]] for j in jobs)
    n_a = 2 if a_swiglu else 1
    n_lead = n_a + 1 + (1 if swiglu_bwd_of is not None else 0)

    def body(*refs):
        b_ref = refs[n_a]
        if a_swiglu:
            a_val = _silu_mul(refs[0][...].astype(F32), refs[1][...].astype(F32)).astype(BF16)
        else:
            a_val = refs[0][...].astype(BF16)
        job_ins = refs[n_lead:n_lead + n_job_in]
        o_ref = refs[n_lead + n_job_in]
        job_outs = refs[n_lead + 1 + n_job_in:n_lead + 1 + n_job_in + len(jobs)]
        scratch = refs[n_lead + 1 + n_job_in + len(jobs):]
        if jobs:
            send_sems, recv_sems = scratch[-2:]
            scratch = scratch[:-2]
            ids = [pl.program_id(d) for d in range(3)]
            first = functools.reduce(jnp.logical_and, [i == 0 for i in ids])
            last = functools.reduce(jnp.logical_and, [i == g - 1 for i, g in zip(ids, grid)])

            def copies():
                sends, recvs, at, sem = [], [], 0, 0
                for j, out in zip(jobs, job_outs):
                    s, r = _job_copies(j["kind"], job_ins[at:at + len(j["ins"])], out, send_sems, recv_sems, sem)
                    sends, recvs = sends + s, recvs + r
                    at, sem = at + len(j["ins"]), sem + _JOB_SEMS[j["kind"]]
                return sends, recvs

            @pl.when(first)
            def _():
                for cp in copies()[0]:
                    cp.start()

        part = lax.dot_general(a_val, b_ref[...].astype(BF16), dims, preferred_element_type=F32)
        if swiglu_bwd_of is not None:
            gate, up = refs[n_a + 1][:, :tn].astype(F32), refs[n_a + 1][:, tn:].astype(F32)
            sg = jax.nn.sigmoid(gate)
            o_ref[:, :tn] = (part * up * sg * (1.0 + gate * (1.0 - sg))).astype(o_ref.dtype)
            o_ref[:, tn:] = (part * gate * sg).astype(o_ref.dtype)
        elif nk == 1:
            o_ref[...] = part.astype(o_ref.dtype)
        else:
            acc_ref, = scratch
            k = pl.program_id(2)

            @pl.when(k == 0)
            def _():
                acc_ref[...] = part

            @pl.when(k > 0)
            def _():
                acc_ref[...] += part

            @pl.when(k == nk - 1)
            def _():
                o_ref[...] = acc_ref[...].astype(o_ref.dtype)

        if jobs:
            @pl.when(last)
            def _():
                sends, recvs = copies()
                for cp in recvs:
                    cp.wait_recv()
                for cp in sends:
                    cp.wait_send()

    if a_swiglu and ta:
        u = a_swiglu // tm
        a_specs = [pl.BlockSpec((tk, tm), lambda i, j, k, o=o: (k, (i // u) * 2 * u + o + i % u)) for o in (0, u)]
    elif a_swiglu:
        u = a_swiglu // tk
        a_specs = [pl.BlockSpec((tm, tk), lambda i, j, k, o=o: (i, (k // u) * 2 * u + o + k % u)) for o in (0, u)]
    else:
        a_specs = [pl.BlockSpec((tk, tm), lambda i, j, k: (k, i)) if ta else pl.BlockSpec((tm, tk), lambda i, j, k: (i, k))]
    if b_groups:
        per = Ng // tn
        b_spec = pl.BlockSpec((None, tk, tn), lambda i, j, k: (j // per, k, j % per))
    elif bk_groups:
        per = Kg // tk
        b_spec = pl.BlockSpec((None, tn, tk), lambda i, j, k: (k // per, j, k % per))
    else:
        b_spec = pl.BlockSpec((tn, tk), lambda i, j, k: (j, k)) if tb else pl.BlockSpec((tk, tn), lambda i, j, k: (k, j))
    if out_groups:
        pero = (N // out_groups) // tn
        o_spec = pl.BlockSpec((None, tm, tn), lambda i, j, k: (j // pero, i, j % pero))
        o_shape = jax.ShapeDtypeStruct((out_groups, M, N // out_groups), out_dtype)
    else:
        o_spec = pl.BlockSpec((tm, tn), lambda i, j, k: (i, j))
        o_shape = jax.ShapeDtypeStruct((M, N), out_dtype)
    lead_specs, lead_args = a_specs + [b_spec], [a] * n_a + [b]
    if swiglu_bwd_of is not None:
        o_spec = pl.BlockSpec((tm, 2 * tn), lambda i, j, k: (i, j))
        o_shape = jax.ShapeDtypeStruct((M, 2 * N), out_dtype)
        lead_specs, lead_args = lead_specs + [o_spec], lead_args + [swiglu_bwd_of]
    scratch = [pltpu.VMEM((tm, tn), F32)] if nk > 1 else []
    if not jobs:
        return pl.pallas_call(
            body, name=name, grid=grid, in_specs=lead_specs, out_specs=o_spec, out_shape=o_shape,
            scratch_shapes=scratch, compiler_params=_params(("parallel", "parallel", "arbitrary")))(*lead_args)
    job_args = [x for j in jobs for x in j["ins"]]
    aliases, at = {}, n_lead
    for n, j in enumerate(jobs):
        if j["kind"] == "gather_d2d":
            aliases[at] = 1 + n
        at += len(j["ins"])
    res = pl.pallas_call(
        body, name=name, grid=grid, in_specs=lead_specs + [_ANY] * n_job_in,
        out_specs=[o_spec] + [_ANY] * len(jobs), out_shape=[o_shape] + [_job_out_shape(j) for j in jobs],
        input_output_aliases=aliases,
        scratch_shapes=scratch + [pltpu.SemaphoreType.DMA((n_sems,)), pltpu.SemaphoreType.DMA((n_sems,))],
        compiler_params=_params(("arbitrary", "arbitrary", "arbitrary")))(*lead_args, *job_args)
    return res[0], list(res[1:])


def _row_spec(tm, width, off):
    return pl.BlockSpec((tm, width), lambda i, j: (i, off + j))


def _const_spec(shape):
    return pl.BlockSpec(shape, lambda i, j: (0,) * len(shape))


def _rows_fwd(fn, rows, consts, outs, *, name, tm=256, ncol=1):
    S = rows[0][0].shape[0]
    tm = _pick(S, (tm, 128, 64))
    nr, nc = len(rows), len(consts)

    def body(*refs):
        vals = [r[...].astype(F32) for r in refs[:nr + nc]]
        res = fn(*vals)
        for o_ref, o in zip(refs[nr + nc:], res):
            o_ref[...] = o.astype(o_ref.dtype)

    return pl.pallas_call(
        body, name=name, grid=(S // tm, ncol),
        in_specs=[_row_spec(tm, w, off) for _, w, off in rows] + [_const_spec(c.shape) for c in consts],
        out_specs=[_row_spec(tm, bw, 0) for _, bw, _ in outs],
        out_shape=[jax.ShapeDtypeStruct((S, tw), dt) for tw, _, dt in outs],
        compiler_params=_params(("parallel", "parallel")))(*[r[0] for r in rows], *consts)


def _rows_bwd(fn, rows, consts, cts, row_grads, *, name, tm=256, ncol=1, add_to_first=None, lane_sum_consts=()):
    S = rows[0][0].shape[0]
    tm = _pick(S, (tm, 128, 64))
    nr, nc, nt = len(rows), len(consts), len(cts)
    n_in = nr + nc + nt + (1 if add_to_first is not None else 0)

    def body(*refs):
        vals = [r[...].astype(F32) for r in refs[:nr + nc]]
        ct = tuple(r[...].astype(F32) for r in refs[nr + nc:nr + nc + nt])
        _, vjp = jax.vjp(fn, *vals)
        grads = vjp(ct)
        outs = refs[n_in:]
        for n, (idx, _, _) in enumerate(row_grads):
            g = grads[idx]
            if n == 0 and add_to_first is not None:
                g = g + refs[n_in - 1][...].astype(F32)
            outs[n][...] = g.astype(outs[n].dtype)
        first = jnp.logical_and(pl.program_id(0) == 0, pl.program_id(1) == 0)
        for c in range(nc):
            o_ref = outs[len(row_grads) + c]
            g = grads[nr + c]
            if c in lane_sum_consts:
                g = jnp.broadcast_to(jnp.sum(g, axis=-1, keepdims=True), g.shape)

            @pl.when(first)
            def _():
                o_ref[...] = jnp.zeros_like(o_ref)

            o_ref[...] += g

    in_specs = ([_row_spec(tm, w, off) for _, w, off in rows] + [_const_spec(c.shape) for c in consts]
                + [_row_spec(tm, w, off) for _, w, off in cts])
    args = [r[0] for r in rows] + list(consts) + [c[0] for c in cts]
    if add_to_first is not None:
        in_specs.append(_row_spec(tm, rows[row_grads[0][0]][1], 0))
        args.append(add_to_first)
    out_specs = [_row_spec(tm, rows[idx][1], 0) for idx, _, _ in row_grads] + [_const_spec(c.shape) for c in consts]
    out_shape = ([jax.ShapeDtypeStruct((S, tw), dt) for _, tw, dt in row_grads]
                 + [jax.ShapeDtypeStruct(c.shape, F32) for c in consts])
    return pl.pallas_call(
        body, name=name, grid=(S // tm, ncol), in_specs=in_specs, out_specs=out_specs, out_shape=out_shape,
        compiler_params=_params(("arbitrary", "arbitrary")))(*args)


def _rms(x, gain):
    return x * lax.rsqrt(jnp.mean(x * x, axis=-1, keepdims=True) + NORM_EPS) * gain


def _prenorm_fn(x, gain):
    return (_rms(x, gain),)


def _postnorm_fn(weight):
    def fn(y, gain):
        return (weight * _rms(y, gain),)
    return fn


def _residual_fn(weight):
    def fn(x, y, gain):
        return (x + weight * _rms(y, gain),)
    return fn


def _swiglu_fn(blk):
    tf = blk.shape[1] // 2
    gate, up = blk[:, :tf], blk[:, tf:]
    return (gate * jax.nn.sigmoid(gate) * up,)


def _ffn_fwd(x, g_pre, g_post, weights, tag, mm):
    D = x.shape[1]
    (h,) = _rows_fwd(_prenorm_fn, [(x, D, 0)], [g_pre], [(D, D, BF16)], name=f"{tag}_prenorm")
    wgu = weights("gu")
    Fs = wgu.shape[2] // 2
    gu = mm("gu", h, wgu, b_groups=N_SHARD, out_dtype=BF16, name=f"{tag}_gate_up")
    wd = weights("wd")
    y = mm("wd", gu, wd, a_swiglu=Fs, out_dtype=F32, name=f"{tag}_down")
    (x_new,) = _rows_fwd(_residual_fn(FFN_RES_WEIGHT), [(x, D, 0), (y, D, 0)], [g_post], [(D, D, F32)],
                         name=f"{tag}_residual")
    return x_new, (h, gu, y, wgu, wd)


def _ffn_bwd(dx_new, x, g_pre, g_post, saved, tag, mm, sink):
    h, gu, y, wgu, wd = saved
    D = x.shape[1]
    Fs = wgu.shape[2] // 2
    d_y, d_g_post = _rows_bwd(_postnorm_fn(FFN_RES_WEIGHT), [(y, D, 0)], [g_post], [(dx_new, D, 0)],
                              [(0, D, BF16)], name=f"{tag}_postnorm_bwd")
    d_gu = mm(d_y, wd, tb=True, swiglu_bwd_of=gu, out_dtype=BF16, name=f"{tag}_down_dx")
    sink("wd", mm(gu, d_y, ta=True, a_swiglu=Fs, out_dtype=BF16, name=f"{tag}_down_dw"))
    sink("gu", mm(h, d_gu, ta=True, out_dtype=BF16, out_groups=N_SHARD, name=f"{tag}_gate_up_dw"))
    d_h = mm(d_gu, wgu, tb=True, bk_groups=N_SHARD, out_dtype=BF16, name=f"{tag}_gate_up_dx")
    dx, d_g_pre = _rows_bwd(_prenorm_fn, [(x, D, 0)], [g_pre], [(d_h, D, 0)], [(0, D, F32)],
                            name=f"{tag}_prenorm_bwd", add_to_first=dx_new)
    return dx, d_g_pre, d_g_post


@jax.custom_vjp
def _swap_halves(x):
    return pltpu.roll(x, HEAD_DIM // 2, 1)


def _swap_fwd(x):
    return _swap_halves(x), None


def _swap_bwd(_, g):
    return (_swap_halves(g),)


_swap_halves.defvjp(_swap_fwd, _swap_bwd)


def _rope(x, cc, ss):
    return x * cc + _swap_halves(x) * ss


def _attn_block_fn(has_prev):
    grp = ATTN_HEADS // ATTN_KV_HEADS
    scale = HEAD_DIM ** -0.5
    nt = (((1,), (1,)), ((), ()))
    nn = (((1,), (0,)), ((), ()))

    def fn(*a):
        q = a[:8]
        kp, kc, vp, vc = a[8:10], a[10:12], a[12:14], a[14:16]
        cc, ss, ccp, ssp, sinks = a[16:21]
        row = lax.broadcasted_iota(jnp.int32, (WINDOW, WINDOW), 0)
        col = lax.broadcasted_iota(jnp.int32, (WINDOW, WINDOW), 1)
        m_cur = col <= row
        m_prev = jnp.logical_and(col > row, has_prev)
        outs = []
        for h in range(ATTN_HEADS):
            g = h // grp
            qr = _rope(q[h], cc, ss).astype(BF16)
            kcr = _rope(kc[g], cc, ss).astype(BF16)
            kpr = _rope(kp[g], ccp, ssp).astype(BF16)
            s_c = jnp.where(m_cur, lax.dot_general(qr, kcr, nt, preferred_element_type=F32) * scale, -jnp.inf)
            s_p = jnp.where(m_prev, lax.dot_general(qr, kpr, nt, preferred_element_type=F32) * scale, -jnp.inf)
            sink = sinks[h:h + 1, :]
            m = jnp.maximum(jnp.maximum(jnp.max(s_c, axis=-1, keepdims=True), jnp.max(s_p, axis=-1, keepdims=True)),
                            sink)
            p_c = jnp.exp(s_c - m)
            p_p = jnp.exp(s_p - m)
            den = (jnp.sum(p_c, axis=-1, keepdims=True) + jnp.sum(p_p, axis=-1, keepdims=True) + jnp.exp(sink - m))
            inv = 1.0 / den
            o = (lax.dot_general((p_c * inv).astype(BF16), vc[g].astype(BF16), nn, preferred_element_type=F32)
                 + lax.dot_general((p_p * inv).astype(BF16), vp[g].astype(BF16), nn, preferred_element_type=F32))
            outs.append(o)
        return tuple(outs)

    return fn


def _attn_specs(order):
    kvb = ATTN_WIDTH // (2 * ATTN_KV_WIDTH)
    return [
        pl.BlockSpec((WINDOW, ATTN_WIDTH), lambda i: (order(i), 0)),
        pl.BlockSpec((WINDOW, 2 * ATTN_KV_WIDTH), lambda i: (jnp.maximum(order(i) - 1, 0), kvb)),
        pl.BlockSpec((WINDOW, 2 * ATTN_KV_WIDTH), lambda i: (order(i), kvb)),
        pl.BlockSpec((WINDOW, HEAD_DIM), lambda i: (order(i), 0)),
        pl.BlockSpec((WINDOW, HEAD_DIM), lambda i: (order(i), 0)),
        pl.BlockSpec((WINDOW, HEAD_DIM), lambda i: (jnp.maximum(order(i) - 1, 0), 0)),
        pl.BlockSpec((WINDOW, HEAD_DIM), lambda i: (jnp.maximum(order(i) - 1, 0), 0)),
        pl.BlockSpec((ATTN_HEADS, HEAD_DIM), lambda i: (0, 0)),
    ]


def _attn_args(q_ref, kvp_ref, kvc_ref, cc, ss, ccp, ssp, sinks):
    d = HEAD_DIM
    q = [q_ref[:, h * d:(h + 1) * d].astype(F32) for h in range(ATTN_HEADS)]
    kp = [kvp_ref[:, g * d:(g + 1) * d].astype(F32) for g in range(ATTN_KV_HEADS)]
    vp = [kvp_ref[:, ATTN_KV_WIDTH + g * d:ATTN_KV_WIDTH + (g + 1) * d].astype(F32) for g in range(ATTN_KV_HEADS)]
    kc = [kvc_ref[:, g * d:(g + 1) * d].astype(F32) for g in range(ATTN_KV_HEADS)]
    vc = [kvc_ref[:, ATTN_KV_WIDTH + g * d:ATTN_KV_WIDTH + (g + 1) * d].astype(F32) for g in range(ATTN_KV_HEADS)]
    return q + kp + kc + vp + vc + [cc[...], ss[...], ccp[...], ssp[...], sinks[...]]


def _attn_fwd(z, cc, ss, sinks_b, tag):
    S = z.shape[0]
    nb = S // WINDOW

    def body(q_ref, kvp_ref, kvc_ref, cc_r, ss_r, ccp_r, ssp_r, sink_r, o_ref):
        n = pl.program_id(0)
        outs = _attn_block_fn(n > 0)(*_attn_args(q_ref, kvp_ref, kvc_ref, cc_r, ss_r, ccp_r, ssp_r, sink_r))
        for h in range(ATTN_HEADS):
            o_ref[:, h * HEAD_DIM:(h + 1) * HEAD_DIM] = outs[h].astype(o_ref.dtype)

    return pl.pallas_call(
        body, name=f"{tag}_attn", grid=(nb,), in_specs=_attn_specs(lambda i: i),
        out_specs=pl.BlockSpec((WINDOW, ATTN_WIDTH), lambda i: (i, 0)),
        out_shape=jax.ShapeDtypeStruct((S, ATTN_WIDTH), BF16),
        compiler_params=_params(("parallel",)))(z, z, z, cc, ss, cc, ss, sinks_b)


def _attn_bwd(z, cc, ss, sinks_b, d_out, tag):
    S = z.shape[0]
    nb = S // WINDOW
    d = HEAD_DIM
    rev = lambda i: nb - 1 - i

    def body(q_ref, kvp_ref, kvc_ref, cc_r, ss_r, ccp_r, ssp_r, sink_r, do_ref, dq_ref, dkv_ref, dsink_ref, carry):
        i = pl.program_id(0)
        n = nb - 1 - i

        @pl.when(i == 0)
        def _():
            carry[...] = jnp.zeros_like(carry)
            dsink_ref[...] = jnp.zeros_like(dsink_ref)

        args = _attn_args(q_ref, kvp_ref, kvc_ref, cc_r, ss_r, ccp_r, ssp_r, sink_r)
        _, vjp = jax.vjp(_attn_block_fn(n > 0), *args)
        g = vjp(tuple(do_ref[:, h * d:(h + 1) * d].astype(F32) for h in range(ATTN_HEADS)))
        for h in range(ATTN_HEADS):
            dq_ref[:, h * d:(h + 1) * d] = g[h].astype(dq_ref.dtype)
        for gi in range(ATTN_KV_HEADS):
            ks = slice(gi * d, (gi + 1) * d)
            vs = slice(ATTN_KV_WIDTH + gi * d, ATTN_KV_WIDTH + (gi + 1) * d)
            dkv_ref[:, ks] = (g[10 + gi] + carry[:, ks]).astype(dkv_ref.dtype)
            dkv_ref[:, vs] = (g[14 + gi] + carry[:, vs]).astype(dkv_ref.dtype)
            carry[:, ks] = g[8 + gi]
            carry[:, vs] = g[12 + gi]
        ds = g[20]
        dsink_ref[...] += jnp.broadcast_to(jnp.sum(ds, axis=-1, keepdims=True), ds.shape)

    return pl.pallas_call(
        body, name=f"{tag}_attn_bwd", grid=(nb,),
        in_specs=_attn_specs(rev) + [pl.BlockSpec((WINDOW, ATTN_WIDTH), lambda i: (rev(i), 0))],
        out_specs=[pl.BlockSpec((WINDOW, ATTN_WIDTH), lambda i: (rev(i), 0)),
                   pl.BlockSpec((WINDOW, 2 * ATTN_KV_WIDTH), lambda i: (rev(i), 0)),
                   pl.BlockSpec((ATTN_HEADS, HEAD_DIM), lambda i: (0, 0))],
        out_shape=[jax.ShapeDtypeStruct((S, ATTN_WIDTH), BF16), jax.ShapeDtypeStruct((S, 2 * ATTN_KV_WIDTH), BF16),
                   jax.ShapeDtypeStruct((ATTN_HEADS, HEAD_DIM), F32)],
        scratch_shapes=[pltpu.VMEM((WINDOW, 2 * ATTN_KV_WIDTH), F32)],
        compiler_params=_params(("arbitrary",)))(z, z, z, cc, ss, cc, ss, sinks_b, d_out)


def _rope_tables(seq):
    half = HEAD_DIM // 2
    inv_freq = ROPE_THETA ** (-jnp.arange(half, dtype=F32) / half)
    ang = jnp.arange(seq, dtype=F32)[:, None] * inv_freq[None, :]
    cos, sin = jnp.cos(ang), jnp.sin(ang)
    return jnp.concatenate([cos, cos], axis=1), jnp.concatenate([-sin, sin], axis=1)


CONV_COLS = 128


def _conv_pre(u, w_ref, S):
    row = lax.broadcasted_iota(jnp.int32, u.shape, 0)
    shifted = [u] + [jnp.where(row >= s, pltpu.roll(u, s, 0), 0.0) for s in range(1, DN_CONV)]
    y = shifted[0] * w_ref[DN_CONV - 1:DN_CONV, :]
    for s in range(1, DN_CONV):
        y = y + shifted[s] * w_ref[DN_CONV - 1 - s:DN_CONV - s, :]
    return y, shifted, row


def _conv_fwd(z, conv_w, tag):
    S = z.shape[0]
    ncol = 3 * DN_WIDTH // CONV_COLS

    def body(u_ref, w_ref, o_ref):
        y, _, _ = _conv_pre(u_ref[...], w_ref, S)
        o_ref[...] = y * jax.nn.sigmoid(y)

    return pl.pallas_call(
        body, name=f"{tag}_conv", grid=(ncol,),
        in_specs=[pl.BlockSpec((S, CONV_COLS), lambda j: (0, Z_DQKV // CONV_COLS + j)),
                  pl.BlockSpec((DN_CONV, CONV_COLS), lambda j: (0, j))],
        out_specs=pl.BlockSpec((S, CONV_COLS), lambda j: (0, j)),
        out_shape=jax.ShapeDtypeStruct((S, 3 * DN_WIDTH), F32),
        compiler_params=_params(("parallel",)))(z, conv_w)


def _conv_bwd(z, conv_w, d_out, tag):
    S = z.shape[0]
    ncol = 3 * DN_WIDTH // CONV_COLS

    def body(u_ref, w_ref, do_ref, du_ref, dw_ref):
        y, shifted, row = _conv_pre(u_ref[...], w_ref, S)
        sg = jax.nn.sigmoid(y)
        d_y = do_ref[...] * (sg * (1.0 + y * (1.0 - sg)))
        d_u = d_y * w_ref[DN_CONV - 1:DN_CONV, :]
        dw_ref[DN_CONV - 1:DN_CONV, :] = jnp.sum(d_y * shifted[0], axis=0, keepdims=True)
        for s in range(1, DN_CONV):
            back = jnp.where(row < S - s, pltpu.roll(d_y, S - s, 0), 0.0)
            d_u = d_u + back * w_ref[DN_CONV - 1 - s:DN_CONV - s, :]
            dw_ref[DN_CONV - 1 - s:DN_CONV - s, :] = jnp.sum(d_y * shifted[s], axis=0, keepdims=True)
        du_ref[...] = d_u.astype(du_ref.dtype)

    return pl.pallas_call(
        body, name=f"{tag}_conv_bwd", grid=(ncol,),
        in_specs=[pl.BlockSpec((S, CONV_COLS), lambda j: (0, Z_DQKV // CONV_COLS + j)),
                  pl.BlockSpec((DN_CONV, CONV_COLS), lambda j: (0, j)),
                  pl.BlockSpec((S, CONV_COLS), lambda j: (0, j))],
        out_specs=[pl.BlockSpec((S, CONV_COLS), lambda j: (0, j)), pl.BlockSpec((DN_CONV, CONV_COLS), lambda j: (0, j))],
        out_shape=[jax.ShapeDtypeStruct((S, 3 * DN_WIDTH), BF16), jax.ShapeDtypeStruct((DN_CONV, 3 * DN_WIDTH), F32)],
        compiler_params=_params(("parallel",)))(z, conv_w, d_out)


_NN = (((1,), (0,)), ((), ()))
_NT = (((1,), (1,)), ((), ()))
_TN = (((0,), (0,)), ((), ()))


def _dot3(a, b, dims):
    a_hi, b_hi = a.astype(BF16), b.astype(BF16)
    a_lo, b_lo = (a - a_hi.astype(F32)).astype(BF16), (b - b_hi.astype(F32)).astype(BF16)
    mm = lambda p, q: lax.dot_general(p, q, dims, preferred_element_type=F32)
    return mm(a_hi, b_hi) + (mm(a_hi, b_lo) + mm(a_lo, b_hi))


@functools.partial(jax.custom_vjp, nondiff_argnums=(2,))
def _dot_vjp(a, b, dims):
    return _dot3(a, b, dims)


def _dot_vjp_fwd(a, b, dims):
    return _dot3(a, b, dims), (a, b)


_BATCH = ((0,), (0,))
_BNN, _BNT, _BTN = (((2,), (1,)), _BATCH), (((2,), (2,)), _BATCH), (((1,), (1,)), _BATCH)


def _dot_vjp_bwd(dims, res, g):
    a, b = res
    nn, nt, tn = (_NN, _NT, _TN) if dims in (_NN, _NT, _TN) else (_BNN, _BNT, _BTN)
    if dims == nn:
        return _dot3(g, b, nt), _dot3(a, g, tn)
    if dims == nt:
        return _dot3(g, b, nn), _dot3(g, a, tn)
    return _dot3(b, g, nt), _dot3(a, g, nn)


_dot_vjp.defvjp(_dot_vjp_fwd, _dot_vjp_bwd)


def _dot(a, b, dims=_NN):
    return _dot_vjp(a, b, dims)


def _dot1(a, b, dims=_NN):
    return lax.dot_general(a.astype(BF16), b.astype(BF16), dims, preferred_element_type=F32)


def _nilpotent_inverse(m):
    H, C, _ = m.shape
    eye = (lax.broadcasted_iota(jnp.int32, (H, C, C), 1) == lax.broadcasted_iota(jnp.int32, (H, C, C), 2)).astype(F32)
    inv = eye + m
    for _ in range(5):
        m = _dot3(m, m, _BNN)
        inv = inv + _dot3(m, inv, _BNN)
    return inv


@jax.custom_vjp
def _unit_lower_solve(m, ru, rw):
    inv = _nilpotent_inverse(m)
    return _dot3(inv, ru, _BNN), _dot3(inv, rw, _BNN)


def _unit_lower_solve_fwd(m, ru, rw):
    inv = _nilpotent_inverse(m)
    xu, xw = _dot3(inv, ru, _BNN), _dot3(inv, rw, _BNN)
    return (xu, xw), (inv, xu, xw)


def _unit_lower_solve_bwd(res, g):
    inv, xu, xw = res
    dru, drw = _dot3(inv, g[0], _BTN), _dot3(inv, g[1], _BTN)
    return _dot3(dru, xu, _BNT) + _dot3(drw, xw, _BNT), dru, drw


_unit_lower_solve.defvjp(_unit_lower_solve_fwd, _unit_lower_solve_bwd)


def _dn_chunk_fn(state, q, k, v, zg, ba, a_log, dtb, norm_w):
    H, C, dk = DN_HEADS, DN_CHUNK, DN_HEAD_DIM
    head = lax.broadcasted_iota(jnp.int32, (H, C, dk), 0)
    rowl = lax.broadcasted_iota(jnp.int32, (H, C, dk), 1)
    lane = lax.broadcasted_iota(jnp.int32, (H, C, dk), 2)
    row = lax.broadcasted_iota(jnp.int32, (H, C, C), 1)
    col = lax.broadcasted_iota(jnp.int32, (H, C, C), 2)
    ba3 = jnp.broadcast_to(ba[None], (H, C, dk))
    bcol = jnp.sum(jnp.where(lane == head, ba3, 0.0), axis=-1, keepdims=True)
    acol = jnp.sum(jnp.where(lane == H + head, ba3, 0.0), axis=-1, keepdims=True)
    qn = q * lax.rsqrt(jnp.sum(q * q, axis=-1, keepdims=True) + NORM_EPS) * (dk ** -0.5)
    kn = k * lax.rsqrt(jnp.sum(k * k, axis=-1, keepdims=True) + NORM_EPS)
    beta = jax.nn.sigmoid(bcol)
    sp_in = acol + dtb
    softplus = jnp.maximum(sp_in, 0.0) + jnp.log(1.0 + jnp.exp(-jnp.abs(sp_in)))
    gt = -jnp.exp(a_log) * softplus
    gc = _dot((row >= col).astype(F32), gt, _BNN)
    gcol = jnp.mean(gc, axis=-1, keepdims=True)
    grow = _dot(jnp.full((H, C, dk), 1.0 / dk, F32), gc, _BNT)
    decay = jnp.exp(jnp.where(row >= col, gcol - grow, -jnp.inf))
    kb = kn * beta
    m = -jnp.where(row > col, _dot(kb, kn, _BNT) * decay, 0.0)
    u, w = _unit_lower_solve(m, v * beta, kb * jnp.exp(gc))
    attn = jnp.where(row >= col, _dot(qn, kn, _BNT) * decay, 0.0)
    q_dec = qn * jnp.exp(gc)
    gl = jnp.sum(jnp.where(rowl == C - 1, gc, 0.0), axis=1, keepdims=True)
    k_dec = kn * jnp.exp(gl - gc)
    v_new = u - _dot(w, state, _BNN)
    o = _dot(q_dec, state, _BNN) + _dot(attn, v_new, _BNN)
    state_new = state * jnp.exp(gl) + _dot(k_dec, v_new, _BTN)
    y = o * lax.rsqrt(jnp.mean(o * o, axis=-1, keepdims=True) + NORM_EPS) * norm_w
    y = y * (zg * jax.nn.sigmoid(zg))
    return state_new, y


def _dn_specs(order):
    C = DN_CHUNK
    return [pl.BlockSpec((C, 3 * DN_WIDTH), lambda i: (order(i), 0)),
            pl.BlockSpec((C, DN_WIDTH), lambda i: (order(i), Z_DZ // DN_WIDTH)),
            pl.BlockSpec((C, 128), lambda i: (order(i), Z_DBA // 128)),
            pl.BlockSpec((DN_HEADS, 1, DN_HEAD_DIM), lambda i: (0, 0, 0)),
            pl.BlockSpec((DN_HEADS, 1, DN_HEAD_DIM), lambda i: (0, 0, 0)),
            pl.BlockSpec((1, 1, DN_HEAD_DIM), lambda i: (0, 0, 0))]


def _dn_heads(ref, base=0):
    d = DN_HEAD_DIM
    return jnp.stack([ref[:, base + h * d:base + (h + 1) * d].astype(F32) for h in range(DN_HEADS)], axis=0)


def _dn_args(qkv_ref, zg_ref, ba_ref, alog_ref, dtb_ref, nw_ref):
    return [_dn_heads(qkv_ref), _dn_heads(qkv_ref, DN_WIDTH), _dn_heads(qkv_ref, 2 * DN_WIDTH), _dn_heads(zg_ref),
            ba_ref[...].astype(F32), alog_ref[...], dtb_ref[...], nw_ref[...]]


def _dn_fwd(qkv, z, a_log, dtb, norm_w, tag):
    S = qkv.shape[0]
    nchunk = S // DN_CHUNK
    d = DN_HEAD_DIM

    def body(qkv_ref, zg_ref, ba_ref, alog_ref, dtb_ref, nw_ref, y_ref, st_ref, state):
        @pl.when(pl.program_id(0) == 0)
        def _():
            state[...] = jnp.zeros_like(state)

        st_ref[...] = state[...]
        new, y = _dn_chunk_fn(state[...], *_dn_args(qkv_ref, zg_ref, ba_ref, alog_ref, dtb_ref, nw_ref))
        state[...] = new
        for h in range(DN_HEADS):
            y_ref[:, h * d:(h + 1) * d] = y[h].astype(y_ref.dtype)

    return pl.pallas_call(
        body, name=f"{tag}_deltanet", grid=(nchunk,), in_specs=_dn_specs(lambda i: i),
        out_specs=[pl.BlockSpec((DN_CHUNK, DN_WIDTH), lambda i: (i, 0)),
                   pl.BlockSpec((None, DN_HEADS, d, d), lambda i: (i, 0, 0, 0))],
        out_shape=[jax.ShapeDtypeStruct((S, DN_WIDTH), BF16), jax.ShapeDtypeStruct((nchunk, DN_HEADS, d, d), F32)],
        scratch_shapes=[pltpu.VMEM((DN_HEADS, d, d), F32)],
        compiler_params=_params(("arbitrary",)))(qkv, z, z, a_log, dtb, norm_w)


def _dn_bwd(qkv, z, a_log, dtb, norm_w, states, d_y, tag):
    S = qkv.shape[0]
    nchunk = S // DN_CHUNK
    d = DN_HEAD_DIM
    rev = lambda i: nchunk - 1 - i

    def body(qkv_ref, zg_ref, ba_ref, alog_ref, dtb_ref, nw_ref, st_ref, dy_ref,
             dqkv_ref, dzg_ref, dba_ref, dalog_ref, ddtb_ref, dnw_ref, d_state):
        @pl.when(pl.program_id(0) == 0)
        def _():
            d_state[...] = jnp.zeros_like(d_state)
            dalog_ref[...] = jnp.zeros_like(dalog_ref)
            ddtb_ref[...] = jnp.zeros_like(ddtb_ref)
            dnw_ref[...] = jnp.zeros_like(dnw_ref)

        args = [st_ref[...]] + _dn_args(qkv_ref, zg_ref, ba_ref, alog_ref, dtb_ref, nw_ref)
        _, vjp = jax.vjp(_dn_chunk_fn, *args)
        g = vjp((d_state[...], _dn_heads(dy_ref)))
        d_state[...] = g[0]
        for h in range(DN_HEADS):
            for n, base in enumerate((0, DN_WIDTH, 2 * DN_WIDTH)):
                dqkv_ref[:, base + h * d:base + (h + 1) * d] = g[1 + n][h]
            dzg_ref[:, h * d:(h + 1) * d] = g[4][h].astype(dzg_ref.dtype)
        dba_ref[...] = g[5].astype(dba_ref.dtype)
        lane_sum = lambda t: jnp.broadcast_to(jnp.sum(t, axis=-1, keepdims=True), t.shape)
        dalog_ref[...] += lane_sum(g[6])
        ddtb_ref[...] += lane_sum(g[7])
        dnw_ref[...] += g[8]

    hspec = pl.BlockSpec((DN_HEADS, 1, d), lambda i: (0, 0, 0))
    return pl.pallas_call(
        body, name=f"{tag}_deltanet_bwd", grid=(nchunk,),
        in_specs=_dn_specs(rev) + [pl.BlockSpec((None, DN_HEADS, d, d), lambda i: (rev(i), 0, 0, 0)),
                                   pl.BlockSpec((DN_CHUNK, DN_WIDTH), lambda i: (rev(i), 0))],
        out_specs=[pl.BlockSpec((DN_CHUNK, 3 * DN_WIDTH), lambda i: (rev(i), 0)),
                   pl.BlockSpec((DN_CHUNK, DN_WIDTH), lambda i: (rev(i), 0)),
                   pl.BlockSpec((DN_CHUNK, 128), lambda i: (rev(i), 0)),
                   hspec, hspec, pl.BlockSpec((1, 1, d), lambda i: (0, 0, 0))],
        out_shape=[jax.ShapeDtypeStruct((S, 3 * DN_WIDTH), F32), jax.ShapeDtypeStruct((S, DN_WIDTH), BF16),
                   jax.ShapeDtypeStruct((S, 128), BF16), jax.ShapeDtypeStruct((DN_HEADS, 1, d), F32),
                   jax.ShapeDtypeStruct((DN_HEADS, 1, d), F32), jax.ShapeDtypeStruct((1, 1, d), F32)],
        scratch_shapes=[pltpu.VMEM((DN_HEADS, d, d), F32)],
        compiler_params=_params(("arbitrary",)))(qkv, z, z, a_log, dtb, norm_w, states, d_y)


def _whole_fwd(fn, ins, outs, *, name):
    n = len(ins)

    def body(*refs):
        res = fn(*[r[...] for r in refs[:n]])
        for o_ref, o in zip(refs[n:], res):
            o_ref[...] = o

    return pl.pallas_call(body, name=name, out_shape=[jax.ShapeDtypeStruct(s, F32) for s in outs],
                          compiler_params=_params())(*ins)


def _whole_bwd(fn, ins, cts, n_grads, *, name, lane_sum=()):
    n, nt = len(ins), len(cts)

    def body(*refs):
        _, vjp = jax.vjp(fn, *[r[...] for r in refs[:n]])
        grads = vjp(tuple(r[...] for r in refs[n:n + nt]))
        for k in range(n_grads):
            g = grads[k]
            if k in lane_sum:
                g = jnp.broadcast_to(jnp.sum(g, axis=-1, keepdims=True), g.shape)
            refs[n + nt + k][...] = g

    return pl.pallas_call(body, name=name, out_shape=[jax.ShapeDtypeStruct(a.shape, F32) for a in ins[:n_grads]],
                          compiler_params=_params())(*ins, *cts)


S5_CHUNK = 256


def _s5_param_fn(a_re, a_im, ldt, bt_re, bt_im, expand):
    dt = jnp.exp(ldt)
    er = jnp.exp(a_re * dt)
    ab_re, ab_im = er * jnp.cos(a_im * dt), er * jnp.sin(a_im * dt)
    den = a_re * a_re + a_im * a_im
    co_re = ((ab_re - 1.0) * a_re + ab_im * a_im) / den
    co_im = (ab_im * a_re - (ab_re - 1.0) * a_im) / den
    cr, ci = _dot(expand, co_re), _dot(expand, co_im)
    return ab_re, ab_im, cr * bt_re - ci * bt_im, cr * bt_im + ci * bt_re


def _s5_scan(b_re, b_im, a_re, a_im, row, T, reverse):
    x_re, x_im, p_re, p_im = b_re, b_im, a_re, a_im
    d = 1
    while d < T:
        if reverse:
            s_re = jnp.where(row < T - d, pltpu.roll(x_re, T - d, 0), 0.0)
            s_im = jnp.where(row < T - d, pltpu.roll(x_im, T - d, 0), 0.0)
        else:
            s_re = jnp.where(row >= d, pltpu.roll(x_re, d, 0), 0.0)
            s_im = jnp.where(row >= d, pltpu.roll(x_im, d, 0), 0.0)
        x_re, x_im = x_re + p_re * s_re - p_im * s_im, x_im + p_re * s_im + p_im * s_re
        p_re, p_im = p_re * p_re - p_im * p_im, 2.0 * p_re * p_im
        d *= 2
    return x_re, x_im


def _s5_states(u, bre_ref, bim_ref, a_re, a_im, c_re, c_im, row, T):
    bu_re = _dot(u, bre_ref[...]) + jnp.where(row == 0, a_re * c_re - a_im * c_im, 0.0)
    bu_im = _dot(u, bim_ref[...]) + jnp.where(row == 0, a_re * c_im + a_im * c_re, 0.0)
    return _s5_scan(bu_re, bu_im, a_re, a_im, row, T, False)


def _s5_in_specs(order, T):
    full = lambda shape: pl.BlockSpec(shape, lambda i: (0,) * len(shape), pipeline_mode=pl.Buffered(1))
    return [pl.BlockSpec((T, S5_WIDTH), lambda i: (order(i), Z_SU // S5_WIDTH)),
            full((S5_WIDTH, S5_LANES)), full((S5_WIDTH, S5_LANES)), full((S5_WIDTH, S5_LANES)),
            full((S5_WIDTH, S5_LANES)), full((1, S5_LANES)), full((1, S5_LANES)), full((1, S5_WIDTH))]


def _s5_fwd(z, bre, bim, cre, cim, ab_re, ab_im, dskip, tag):
    S = z.shape[0]
    T = _pick(S, (S5_CHUNK, 128))
    nch = S // T

    def body(u_ref, bre_ref, bim_ref, cre_ref, cim_ref, are_ref, aim_ref, d_ref, y_ref, xre_ref, xim_ref, c_re, c_im):
        @pl.when(pl.program_id(0) == 0)
        def _():
            c_re[...] = jnp.zeros_like(c_re)
            c_im[...] = jnp.zeros_like(c_im)

        u = u_ref[...]
        row = lax.broadcasted_iota(jnp.int32, (T, S5_LANES), 0)
        x_re, x_im = _s5_states(u, bre_ref, bim_ref, are_ref[...], aim_ref[...], c_re[...], c_im[...], row, T)
        c_re[...] = jnp.sum(jnp.where(row == T - 1, x_re, 0.0), axis=0, keepdims=True)
        c_im[...] = jnp.sum(jnp.where(row == T - 1, x_im, 0.0), axis=0, keepdims=True)
        xre_ref[...] = x_re
        xim_ref[...] = x_im
        y_ref[...] = _dot(x_re, cre_ref[...], _NT) - _dot(x_im, cim_ref[...], _NT) + d_ref[...] * u

    return pl.pallas_call(
        body, name=f"{tag}_s5", grid=(nch,), in_specs=_s5_in_specs(lambda i: i, T),
        out_specs=[pl.BlockSpec((T, S5_WIDTH), lambda i: (i, 0)),
                   pl.BlockSpec((T, S5_LANES), lambda i: (i, 0)), pl.BlockSpec((T, S5_LANES), lambda i: (i, 0))],
        out_shape=[jax.ShapeDtypeStruct((S, S5_WIDTH), F32), jax.ShapeDtypeStruct((S, S5_LANES), F32),
                   jax.ShapeDtypeStruct((S, S5_LANES), F32)],
        scratch_shapes=[pltpu.VMEM((1, S5_LANES), F32), pltpu.VMEM((1, S5_LANES), F32)],
        compiler_params=_params(("arbitrary",)))(z, bre, bim, cre, cim, ab_re, ab_im, dskip)


S5_BWD_CHUNK = 128


def _s5_bwd(z, bre, bim, cre, cim, ab_re, ab_im, dskip, xre, xim, d_y, tag):
    S = z.shape[0]
    T = _pick(S, (S5_BWD_CHUNK, 64))
    nch = S // T
    rev = lambda i: nch - 1 - i
    above = lambda i: (jnp.maximum(rev(i) * (T // 8) - 1, 0), 0)
    full = lambda shape: pl.BlockSpec(shape, lambda i: (0,) * len(shape))

    def body(u_ref, bre_ref, bim_ref, cre_ref, cim_ref, are_ref, aim_ref, d_ref, xre_ref, xim_ref, pre_ref, pim_ref,
             dy_ref, du_ref, dbre_ref, dbim_ref, dcre_ref, dcim_ref, dare_ref, daim_ref, dd_ref, g_re, g_im):
        @pl.when(pl.program_id(0) == 0)
        def _():
            g_re[...] = jnp.zeros_like(g_re)
            g_im[...] = jnp.zeros_like(g_im)
            for r in (dbre_ref, dbim_ref, dcre_ref, dcim_ref, dare_ref, daim_ref, dd_ref):
                r[...] = jnp.zeros_like(r)

        u = u_ref[...]
        dy = dy_ref[...].astype(F32)
        a_re, a_im = are_ref[...], aim_ref[...]
        row = lax.broadcasted_iota(jnp.int32, (T, S5_LANES), 0)
        x_re, x_im = xre_ref[...], xim_ref[...]
        dcre_ref[...] += _dot1(dy, x_re, _TN)
        dcim_ref[...] -= _dot1(dy, x_im, _TN)
        has_before = (pl.program_id(0) < nch - 1).astype(F32)
        row8 = lax.broadcasted_iota(jnp.int32, (8, S5_LANES), 0)
        before = lambda ref: has_before * jnp.sum(jnp.where(row8 == 7, ref[...], 0.0), axis=0, keepdims=True)
        xp_re = jnp.where(row >= 1, pltpu.roll(x_re, 1, 0), 0.0) + jnp.where(row == 0, before(pre_ref), 0.0)
        xp_im = jnp.where(row >= 1, pltpu.roll(x_im, 1, 0), 0.0) + jnp.where(row == 0, before(pim_ref), 0.0)
        last = row == T - 1
        gd_re = _dot(dy, cre_ref[...]) + jnp.where(last, a_re * g_re[...] + a_im * g_im[...], 0.0)
        gd_im = -_dot(dy, cim_ref[...]) + jnp.where(last, a_re * g_im[...] - a_im * g_re[...], 0.0)
        t_re, t_im = _s5_scan(gd_re, gd_im, a_re, -a_im, row, T, True)
        g_re[...] = jnp.sum(jnp.where(row == 0, t_re, 0.0), axis=0, keepdims=True)
        g_im[...] = jnp.sum(jnp.where(row == 0, t_im, 0.0), axis=0, keepdims=True)
        du_ref[...] = (_dot1(t_re, bre_ref[...], _NT) + _dot1(t_im, bim_ref[...], _NT) + dy * d_ref[...]).astype(du_ref.dtype)
        dbre_ref[...] += _dot1(u, t_re, _TN)
        dbim_ref[...] += _dot1(u, t_im, _TN)
        dare_ref[...] += jnp.sum(t_re * xp_re + t_im * xp_im, axis=0, keepdims=True)
        daim_ref[...] += jnp.sum(t_im * xp_re - t_re * xp_im, axis=0, keepdims=True)
        dd_ref[...] += jnp.sum(dy * u, axis=0, keepdims=True)

    return pl.pallas_call(
        body, name=f"{tag}_s5_bwd", grid=(nch,),
        in_specs=_s5_in_specs(rev, T) + [pl.BlockSpec((T, S5_LANES), lambda i: (rev(i), 0)),
                                         pl.BlockSpec((T, S5_LANES), lambda i: (rev(i), 0)),
                                         pl.BlockSpec((8, S5_LANES), above), pl.BlockSpec((8, S5_LANES), above),
                                         pl.BlockSpec((T, S5_WIDTH), lambda i: (rev(i), 0))],
        out_specs=[pl.BlockSpec((T, S5_WIDTH), lambda i: (rev(i), 0))] + [full((S5_WIDTH, S5_LANES))] * 4
        + [full((1, S5_LANES))] * 2 + [full((1, S5_WIDTH))],
        out_shape=[jax.ShapeDtypeStruct((S, S5_WIDTH), BF16)] + [jax.ShapeDtypeStruct((S5_WIDTH, S5_LANES), F32)] * 4
        + [jax.ShapeDtypeStruct((1, S5_LANES), F32)] * 2 + [jax.ShapeDtypeStruct((1, S5_WIDTH), F32)],
        scratch_shapes=[pltpu.VMEM((1, S5_LANES), F32), pltpu.VMEM((1, S5_LANES), F32)],
        compiler_params=_params(("arbitrary",)))(z, bre, bim, cre, cim, ab_re, ab_im, dskip, xre, xim, xre, xim, d_y)


def _s5_glu_fn(y, glu_w, glu_b):
    g = 0.5 * y * (1.0 + jnp.tanh(math.sqrt(2.0 / math.pi) * (y + 0.044715 * (y * y * y))))
    lin = lax.dot_general(g.astype(BF16), glu_w.astype(BF16), (((1,), (0,)), ((), ())), preferred_element_type=F32)
    return (g * jax.nn.sigmoid(lin + glu_b),)


def _block_diag(m):
    G, H, P = S5_GROUPS, S5_GROUP_CH, S5_STATE
    eye = jnp.eye(G, dtype=m.dtype)
    return (m.reshape(G, H, 1, P) * eye[:, None, :, None]).reshape(G * H, G * P)


def _block_diag_take(m):
    G, H, P = S5_GROUPS, S5_GROUP_CH, S5_STATE
    eye = jnp.eye(G, dtype=m.dtype)
    return jnp.sum(m.reshape(G, H, G, P) * eye[:, None, :, None], axis=2).reshape(G * H, P)


def _loss_head(y, target, tag):
    S, D = y.shape
    tm = _pick(S, (256, 128, 64))

    def body(y_ref, t_ref, dy_ref, loss_ref):
        @pl.when(pl.program_id(0) == 0)
        def _():
            loss_ref[...] = jnp.zeros_like(loss_ref)

        err = y_ref[...] - t_ref[...]
        dy_ref[...] = err * (1.0 / D)
        part = 0.5 * jnp.sum(jnp.mean(err * err, axis=-1, keepdims=True), axis=0, keepdims=True)
        loss_ref[...] += jnp.broadcast_to(part, loss_ref.shape)

    return pl.pallas_call(
        body, name=f"{tag}_loss", grid=(S // tm,),
        in_specs=[pl.BlockSpec((tm, D), lambda i: (i, 0)), pl.BlockSpec((tm, D), lambda i: (i, 0))],
        out_specs=[pl.BlockSpec((tm, D), lambda i: (i, 0)), pl.BlockSpec((8, 128), lambda i: (0, 0))],
        out_shape=[jax.ShapeDtypeStruct((S, D), F32), jax.ShapeDtypeStruct((8, 128), F32)],
        compiler_params=_params(("arbitrary",)))(y, target)


def _adamw(w, g, m, v, name):
    shape = w.shape
    cols = shape[-1]
    rows = int(np.prod(shape[:-1]))
    tr = _pick(rows, [t for t in (512, 256, 128, 64, 32, 16, 8) if t * cols <= 256 * 1024] or [8])
    c1 = 1.0 - ADAM_B1 ** ADAM_STEP
    c2 = 1.0 - ADAM_B2 ** ADAM_STEP

    def body(w_ref, g_ref, m_ref, v_ref, d_ref, mo_ref, vo_ref):
        gg = g_ref[...]
        mn = ADAM_B1 * m_ref[...] + (1.0 - ADAM_B1) * gg
        vn = ADAM_B2 * v_ref[...] + (1.0 - ADAM_B2) * (gg * gg)
        d_ref[...] = -ADAM_LR * ((mn / c1) / (jnp.sqrt(vn / c2) + ADAM_EPS) + ADAM_WD * w_ref[...])
        mo_ref[...] = mn
        vo_ref[...] = vn

    spec = pl.BlockSpec((tr, cols), lambda i: (i, 0))
    outs = pl.pallas_call(
        body, name=name, grid=(rows // tr,), in_specs=[spec] * 4, out_specs=[spec] * 3,
        out_shape=[jax.ShapeDtypeStruct((rows, cols), F32)] * 3,
        compiler_params=_params(("parallel",)))(*[a.reshape(rows, cols) for a in (w, g, m, v)])
    return [o.reshape(shape) for o in outs]


def _sum_slots(slots, name):
    n, R, C = slots.shape
    tr = _pick(R, [t for t in (1024, 512, 256, 128, 64, 32, 16, 8) if n * t * C * 4 <= (2 << 20)] or [8])

    def body(s_ref, o_ref):
        acc = s_ref[0].astype(F32)
        for k in range(1, n):
            acc = acc + s_ref[k].astype(F32)
        o_ref[...] = acc

    return pl.pallas_call(
        body, name=name, grid=(R // tr,), in_specs=[pl.BlockSpec((n, tr, C), lambda i: (0, i, 0))],
        out_specs=pl.BlockSpec((tr, C), lambda i: (i, 0)),
        out_shape=jax.ShapeDtypeStruct((R, C), F32), compiler_params=_params(("parallel",)))(slots)


def _row_tile(R, C, itemsize, target=1 << 20):
    return _pick(R, [t for t in (2048, 1024, 512, 256, 128, 64, 32, 16) if t * C * itemsize <= target] or [16])


def _add_sibling(part, got, name):
    n, _, R2, C = part.shape
    tr = _row_tile(R2, C, 2)

    def body(p_ref, q_ref, o_ref):
        o_ref[...] = (p_ref[...].astype(F32) + q_ref[...].astype(F32)).astype(o_ref.dtype)

    spec = pl.BlockSpec((None, tr, C), lambda k, i: (k, i, 0))
    return pl.pallas_call(
        body, name=name, grid=(n, R2 // tr),
        in_specs=[pl.BlockSpec((None, None, tr, C), lambda k, i: (k, lax.axis_index("c"), i, 0)), spec], out_specs=spec,
        out_shape=jax.ShapeDtypeStruct((n, R2, C), BF16), compiler_params=_params(("parallel", "parallel")))(part, got)


def _sum_chips(chip_part, others, name):
    _, R2, C = chip_part.shape
    tr = _row_tile(R2, C, 4)

    def body(a_ref, b_ref, o_ref):
        acc = a_ref[...].astype(F32)
        for j in range(N_SHARD - 1):
            acc = acc + b_ref[j].astype(F32)
        o_ref[...] = acc

    return pl.pallas_call(
        body, name=name, grid=(R2 // tr,),
        in_specs=[pl.BlockSpec((None, tr, C), lambda i: (2 * lax.axis_index("x") + lax.axis_index("y"), i, 0)),
                  pl.BlockSpec((N_SHARD - 1, tr, C), lambda i: (0, i, 0))],
        out_specs=pl.BlockSpec((None, tr, C), lambda i: (lax.axis_index("c"), i, 0)),
        out_shape=jax.ShapeDtypeStruct((2, R2, C), F32), compiler_params=_params(("parallel",)))(chip_part, others)


_ANY = pl.BlockSpec(memory_space=pl.ANY)


def _place():
    return lax.axis_index("x"), lax.axis_index("y"), lax.axis_index("c")


def _other_chips(x, y):
    return [(1 - x, y), (x, 1 - y), (1 - x, 1 - y)]


def _remote(src, dst, send_sems, recv_sems, n, to):
    return pltpu.make_async_remote_copy(src_ref=src, dst_ref=dst, send_sem=send_sems.at[n], recv_sem=recv_sems.at[n],
                                        device_id=to, device_id_type=MESH_ID)


_JOB_SEMS = {"gather_ici": 3, "gather_d2d": 4, "scatter": 3}


def _job_out_shape(job):
    src = job["ins"][0]
    if job["kind"] == "gather_ici":
        return jax.ShapeDtypeStruct((N_SHARD,) + src.shape, src.dtype)
    if job["kind"] == "gather_d2d":
        return jax.ShapeDtypeStruct(src.shape, src.dtype)
    return jax.ShapeDtypeStruct((N_SHARD - 1,) + src.shape[1:], src.dtype)


def _job_copies(kind, ins, out, send_sems, recv_sems, base):
    x, y, c = _place()
    k = 2 * x + y
    sibling = (x, y, 1 - c)
    sends, recvs = [], []
    for j, (cx, cy) in enumerate(_other_chips(x, y)):
        kj = 2 * cx + cy
        if kind == "gather_ici":
            sends.append(_remote(ins[0].at[c], out.at[k, c], send_sems, recv_sems, base + j, (cx, cy, c)))
            recvs.append(_remote(out.at[kj, c], out.at[kj, c], send_sems, recv_sems, base + j, sibling))
        elif kind == "gather_d2d":
            sends.append(_remote(out.at[kj, c], out.at[kj, c], send_sems, recv_sems, base + 1 + j, sibling))
            recvs.append(_remote(out.at[kj, 1 - c], out.at[kj, 1 - c], send_sems, recv_sems, base + 1 + j, sibling))
        else:
            sends.append(_remote(ins[0].at[kj], out.at[j], send_sems, recv_sems, base + j, (cx, cy, c)))
            recvs.append(_remote(out.at[j], out.at[j], send_sems, recv_sems, base + j, sibling))
    if kind == "gather_d2d":
        sends.append(_remote(ins[1], out.at[k], send_sems, recv_sems, base, sibling))
        recvs.append(_remote(out.at[k], out.at[k], send_sems, recv_sems, base, sibling))
    return sends, recvs


def _gather_shards(shards, name):
    n = len(shards)
    per = 7

    def body(*refs):
        ins, outs = refs[:n], refs[n:2 * n]
        send_sems, recv_sems = refs[2 * n:]
        x, y, c = _place()
        k = 2 * x + y
        sibling = (x, y, 1 - c)
        chips = _other_chips(x, y)
        started = []
        for a in range(n):
            cp = _remote(ins[a], outs[a].at[k], send_sems, recv_sems, per * a, sibling)
            cp.start()
            started.append(cp)
            for j, (cx, cy) in enumerate(chips):
                cp = _remote(ins[a].at[c], outs[a].at[k, c], send_sems, recv_sems, per * a + 1 + j, (cx, cy, c))
                cp.start()
                started.append(cp)
        for a in range(n):
            for j, (cx, cy) in enumerate(chips):
                landed = outs[a].at[2 * cx + cy, c]
                _remote(landed, landed, send_sems, recv_sems, per * a + 1 + j, sibling).wait_recv()
                cp = _remote(landed, landed, send_sems, recv_sems, per * a + 4 + j, sibling)
                cp.start()
                started.append(cp)
        for a in range(n):
            own = outs[a].at[k]
            _remote(own, own, send_sems, recv_sems, per * a, sibling).wait_recv()
            for j, (cx, cy) in enumerate(chips):
                passed = outs[a].at[2 * cx + cy, 1 - c]
                _remote(passed, passed, send_sems, recv_sems, per * a + 4 + j, sibling).wait_recv()
        for cp in started:
            cp.wait_send()

    return pl.pallas_call(
        body, name=name, in_specs=[_ANY] * n, out_specs=[_ANY] * n,
        out_shape=[jax.ShapeDtypeStruct((N_SHARD,) + s.shape, s.dtype) for s in shards],
        scratch_shapes=[pltpu.SemaphoreType.DMA((per * n,)), pltpu.SemaphoreType.DMA((per * n,))],
        )(*shards)


def _swap_other_half(parts, name):
    n = len(parts)

    def body(*refs):
        ins, outs = refs[:n], refs[n:2 * n]
        send_sems, recv_sems = refs[2 * n:]
        x, y, c = _place()
        started = []
        for a in range(n):
            for k in range(N_SHARD):
                cp = _remote(ins[a].at[k, 1 - c], outs[a].at[k], send_sems, recv_sems, N_SHARD * a + k, (x, y, 1 - c))
                cp.start()
                started.append(cp)
        for cp in started:
            cp.wait()

    return pl.pallas_call(
        body, name=name, in_specs=[_ANY] * n, out_specs=[_ANY] * n,
        out_shape=[jax.ShapeDtypeStruct((N_SHARD,) + p.shape[2:], p.dtype) for p in parts],
        scratch_shapes=[pltpu.SemaphoreType.DMA((N_SHARD * n,)), pltpu.SemaphoreType.DMA((N_SHARD * n,))],
        )(*parts)


def _scatter_to_chips(parts, name):
    n = len(parts)
    per = N_SHARD - 1

    def body(*refs):
        ins, outs = refs[:n], refs[n:2 * n]
        send_sems, recv_sems = refs[2 * n:]
        x, y, c = _place()
        chips = _other_chips(x, y)
        started = []
        for a in range(n):
            for j, (cx, cy) in enumerate(chips):
                cp = _remote(ins[a].at[2 * cx + cy], outs[a].at[j], send_sems, recv_sems, per * a + j, (cx, cy, c))
                cp.start()
                started.append(cp)
        for cp in started:
            cp.wait()

    return pl.pallas_call(
        body, name=name, in_specs=[_ANY] * n, out_specs=[_ANY] * n,
        out_shape=[jax.ShapeDtypeStruct((per,) + p.shape[1:], p.dtype) for p in parts],
        scratch_shapes=[pltpu.SemaphoreType.DMA((per * n,)), pltpu.SemaphoreType.DMA((per * n,))],
        )(*parts)


def _join_halves(bufs, name):
    n = len(bufs)

    def body(*refs):
        outs = refs[n:2 * n]
        send_sems, recv_sems = refs[2 * n:]
        x, y, c = _place()
        started = []
        for a in range(n):
            cp = _remote(outs[a].at[c], outs[a].at[c], send_sems, recv_sems, a, (x, y, 1 - c))
            cp.start()
            started.append(cp)
        for a in range(n):
            arrives = outs[a].at[1 - c]
            _remote(arrives, arrives, send_sems, recv_sems, a, (x, y, 1 - c)).wait_recv()
        for cp in started:
            cp.wait_send()

    return pl.pallas_call(
        body, name=name, in_specs=[_ANY] * n, out_specs=[_ANY] * n,
        out_shape=[jax.ShapeDtypeStruct(b.shape, b.dtype) for b in bufs],
        input_output_aliases={a: a for a in range(n)},
        scratch_shapes=[pltpu.SemaphoreType.DMA((n,)), pltpu.SemaphoreType.DMA((n,))],
        )(*bufs)


def _gather_all_devices(vec, name):
    def body(in_ref, out_ref, send_sems, recv_sems, local_sem):
        x, y, c = _place()
        me = 4 * x + 2 * y + c
        own = pltpu.make_async_copy(in_ref, out_ref.at[me], local_sem)
        own.start()
        sends = []
        for r in range(1, 8):
            fx, fy, fc = (r >> 2) & 1, (r >> 1) & 1, r & 1
            to = (x ^ fx, y ^ fy, c ^ fc)
            sends.append(pltpu.make_async_remote_copy(
                src_ref=in_ref, dst_ref=out_ref.at[me], send_sem=send_sems.at[r - 1], recv_sem=recv_sems.at[r - 1],
                device_id=to, device_id_type=MESH_ID))
            sends[-1].start()
        for r in range(1, 8):
            fx, fy, fc = (r >> 2) & 1, (r >> 1) & 1, r & 1
            slot = out_ref.at[4 * (x ^ fx) + 2 * (y ^ fy) + (c ^ fc)]
            pltpu.make_async_remote_copy(src_ref=slot, dst_ref=slot, send_sem=send_sems.at[r - 1],
                                         recv_sem=recv_sems.at[r - 1], device_id=(x, y, c),
                                         device_id_type=MESH_ID).wait_recv()
        for cp in sends:
            cp.wait_send()
        own.wait()

    return pl.pallas_call(
        body, name=name, in_specs=[_ANY], out_specs=_ANY, out_shape=jax.ShapeDtypeStruct((8,) + vec.shape, vec.dtype),
        scratch_shapes=[pltpu.SemaphoreType.DMA((7,)), pltpu.SemaphoreType.DMA((7,)), pltpu.SemaphoreType.DMA(())],
        )(vec)


def _permute_w_in(w):
    cut = Z_SU
    return jnp.concatenate([w[:, :cut], w[:, cut + 2 * DN_HEADS:], w[:, cut:cut + 2 * DN_HEADS],
                            jnp.zeros((w.shape[0], Z_WIDTH - IN_WIDTH), w.dtype)], axis=1)


def _unpermute_w_in(wp):
    return jnp.concatenate([wp[:, :Z_SU], wp[:, Z_DBA:Z_DBA + 2 * DN_HEADS], wp[:, Z_SU:Z_DBA]], axis=1)


def _s5_inputs(sp):
    ab_re, ab_im, bb_re, bb_im = _whole_fwd(
        _s5_param_fn, sp["s5_pins"], [(S5_GROUPS, S5_STATE)] * 2 + [(S5_WIDTH, S5_STATE)] * 2, name="s5_params")
    return (_block_diag(bb_re), _block_diag(bb_im), _block_diag(sp["c_re"]), _block_diag(sp["c_im"]),
            ab_re.reshape(1, S5_LANES), ab_im.reshape(1, S5_LANES), sp["dskip"])


def _mixer_fwd(x, sp, weights, rope, mm):
    D = x.shape[1]
    (h,) = _rows_fwd(_prenorm_fn, [(x, D, 0)], [sp["mix_norm_pre"]], [(D, D, BF16)], name="mix_prenorm")
    w_in = weights("win")
    z = mm("win", h, w_in, out_dtype=F32, name="mix_in")
    w_out, glu_w = weights("wout")
    y_attn = _attn_fwd(z, rope[0], rope[1], sp["sinks"], "mix")
    qkv = _conv_fwd(z, sp["conv_w"], "mix")
    y_dn, states = _dn_fwd(qkv, z, sp["dn_a_log"], sp["dn_dt_bias"], sp["dn_norm_w"], "mix")
    s5_in = _s5_inputs(sp)
    y_lin, *s5_states = _s5_fwd(z, *s5_in, "mix")
    (y_s5,) = _rows_fwd(_s5_glu_fn, [(y_lin, S5_WIDTH, 0)], [glu_w, sp["glu_b"]], [(S5_WIDTH, S5_WIDTH, BF16)],
                        name="mix_s5_glu")
    cat = jnp.concatenate([y_attn, y_dn, y_s5], axis=1)
    mixed = mm("wout", cat, w_out, out_dtype=F32, name="mix_out")
    (x_new,) = _rows_fwd(_residual_fn(1.0), [(x, D, 0), (mixed, D, 0)], [sp["mix_norm_post"]], [(D, D, F32)],
                         name="mix_residual")
    return x_new, (h, z, qkv, states, s5_in, y_lin, s5_states, cat, mixed, w_in, w_out, glu_w)


def _mixer_bwd(dx_new, x, sp, rope, saved, mm, sink):
    h, z, qkv, states, s5_in, y_lin, s5_states, cat, mixed, w_in, w_out, glu_w = saved
    D = x.shape[1]
    G, H, P = S5_GROUPS, S5_GROUP_CH, S5_STATE
    d_mixed, d_g_post = _rows_bwd(_postnorm_fn(1.0), [(mixed, D, 0)], [sp["mix_norm_post"]], [(dx_new, D, 0)],
                                  [(0, D, BF16)], name="mix_postnorm_bwd")
    d_cat = mm(d_mixed, w_out, tb=True, out_dtype=BF16, name="mix_out_dx")
    sink("wout", mm(cat, d_mixed, ta=True, out_dtype=BF16, name="mix_out_dw"))
    d_attn, d_dn, d_s5 = d_cat[:, :ATTN_WIDTH], d_cat[:, ATTN_WIDTH:ATTN_WIDTH + DN_WIDTH], d_cat[:, ATTN_WIDTH + DN_WIDTH:]
    d_ylin, d_glu_w, d_glu_b = _rows_bwd(_s5_glu_fn, [(y_lin, S5_WIDTH, 0)], [glu_w, sp["glu_b"]],
                                         [(d_s5, S5_WIDTH, 0)], [(0, S5_WIDTH, F32)], name="mix_s5_glu_bwd")
    sink("glu", d_glu_w.astype(BF16))
    d_us5, d_bre, d_bim, d_cre, d_cim, d_are, d_aim, d_dskip = _s5_bwd(z, *s5_in, *s5_states, d_ylin, "mix")
    cts = [d_are.reshape(G, P), d_aim.reshape(G, P), _block_diag_take(d_bre), _block_diag_take(d_bim)]
    d_a_re, d_a_im, d_ldt, d_bt_re, d_bt_im = _whole_bwd(_s5_param_fn, sp["s5_pins"], cts, 5, name="s5_params_bwd",
                                                         lane_sum=(2,))
    from_t = lambda m: m.reshape(G, H, P).transpose(0, 2, 1)
    d_qkv, d_zg, d_ba, d_alog, d_dtb, d_nw = _dn_bwd(qkv, z, sp["dn_a_log"], sp["dn_dt_bias"], sp["dn_norm_w"], states,
                                                     d_dn, "mix")
    d_uconv, d_conv_w = _conv_bwd(z, sp["conv_w"], d_qkv, "mix")
    d_q, d_kv, d_sinks = _attn_bwd(z, rope[0], rope[1], sp["sinks"], d_attn, "mix")
    d_z = jnp.concatenate([d_q, d_kv, d_uconv, d_zg, d_us5, d_ba], axis=1)
    sink("win", mm(h, d_z, ta=True, out_dtype=BF16, name="mix_in_dw"))
    d_h = mm(d_z, w_in, tb=True, out_dtype=BF16, name="mix_in_dx")
    dx, d_g_pre = _rows_bwd(_prenorm_fn, [(x, D, 0)], [sp["mix_norm_pre"]], [(d_h, D, 0)], [(0, D, F32)],
                            name="mix_prenorm_bwd", add_to_first=dx_new)
    small = {
        "mix_norm_pre": d_g_pre[0], "mix_norm_post": d_g_post[0], "attn_sinks": d_sinks[:, 0], "dn_conv_w": d_conv_w,
        "dn_a_log": d_alog[:, 0, 0], "dn_dt_bias": d_dtb[:, 0, 0], "dn_norm_w": d_nw[0, 0],
        "s5_a_re": d_a_re, "s5_a_im": d_a_im, "s5_log_dt": d_ldt[:, 0], "s5_b_re": from_t(d_bt_re),
        "s5_b_im": from_t(d_bt_im), "s5_c_re": _block_diag_take(d_cre).reshape(G, H, P),
        "s5_c_im": _block_diag_take(d_cim).reshape(G, H, P), "s5_d": d_dskip[0], "s5_glu_b": d_glu_b[0],
    }
    return dx, small


BIG_PIECES = ("gu1", "wd1", "win", "wout", "glu", "gu2", "wd2")


def _halves(m):
    return m.reshape(m.shape[:-2] + (2, m.shape[-2] // 2, m.shape[-1]))


def _whole(m):
    return m.reshape(m.shape[:-3] + (2 * m.shape[-2], m.shape[-1]))


class _WeightGather:
    def __init__(self, shards):
        self.shards = shards
        pieces = list(shards[0])
        flat = [s for p in pieces for s in shards[0][p]]
        got = iter(_gather_shards(flat, "gather_weights"))
        self.ready = {(0, p): [next(got) for _ in shards[0][p]] for p in pieces}
        self.landing = None

    def weights(self, l, piece):
        return self.ready[(l, piece)]

    def mm(self, l, piece, a, b, **kw):
        jobs, done = [], None
        if self.landing is not None:
            done = self.landing
            jobs += [{"kind": "gather_d2d", "ins": [buf, s]} for buf, s in zip(done[2], self.shards[done[0]][done[1]])]
        n_done = len(jobs)
        if l + 1 < len(self.shards):
            jobs += [{"kind": "gather_ici", "ins": [s]} for s in self.shards[l + 1][piece]]
        if not jobs:
            return _mm(a, b, **kw)
        out, bufs = _mm(a, b, jobs=jobs, **kw)
        if done is not None:
            self.ready[done[:2]] = bufs[:n_done]
        self.landing = (l + 1, piece, bufs[n_done:]) if l + 1 < len(self.shards) else None
        return out


class _GradReduce:
    def __init__(self):
        self.waiting, self.landed = [], {}

    def add(self, piece, grad):
        part = _halves(grad)
        (got,) = _swap_other_half([part], "reduce_siblings")
        self.waiting.append((piece, _add_sibling(part, got, "reduce_siblings_add")))

    def mm(self, a, b, **kw):
        if not self.waiting:
            return _mm(a, b, **kw)
        out, others = _mm(a, b, jobs=[{"kind": "scatter", "ins": [cp]} for _, cp in self.waiting], **kw)
        for (piece, cp), o in zip(self.waiting, others):
            self.landed[piece] = (cp, o)
        self.waiting = []
        return out

    def finish(self):
        if self.waiting:
            others = _scatter_to_chips([cp for _, cp in self.waiting], "reduce_chips")
            for (piece, cp), o in zip(self.waiting, others):
                self.landed[piece] = (cp, o)
        pieces = list(self.landed)
        joined = _join_halves([_sum_chips(*self.landed[p], "reduce_chips_sum") for p in pieces], "reduce_join")
        return {p: _whole(m) for p, m in zip(pieces, joined)}


def _pack_small(arrs, extra=()):
    flat = jnp.concatenate([a.reshape(-1) for a in arrs] + list(extra))
    n = flat.shape[0]
    quantum = 8 * SMALL_LANES
    padded = -(-n // quantum) * quantum
    return jnp.concatenate([flat, jnp.zeros((padded - n,), flat.dtype)]).reshape(padded // SMALL_LANES, SMALL_LANES)


def _unpack_small(flat2d, shapes):
    flat = flat2d.reshape(-1)
    out, o = [], 0
    for s in shapes:
        n = int(np.prod(s))
        out.append(flat[o:o + n].reshape(s))
        o += n
    return out, flat[o:]


def _step(a):
    x, target = a["x"][0], a["loss_target"][0]
    S, D = x.shape
    L, _, Fs = a["ff1_w_gate"].shape
    px, py, pc = _place()
    chip = 2 * px + py

    def rows_of_chips(g):
        g = _whole(g)
        return g.reshape(N_SHARD * g.shape[1], g.shape[2])

    shards = []
    for l in range(L):
        half = lambda m: _halves(m.astype(BF16))
        shards.append({
            "gu1": [half(jnp.concatenate([a["ff1_w_gate"][l], a["ff1_w_up"][l]], axis=1))], "wd1": [half(a["ff1_w_down"][l])],
            "win": [half(a["w_in"][l])], "wout": [half(a["w_out"][l]), half(a["s5_glu_w"][l])],
            "gu2": [half(jnp.concatenate([a["ff2_w_gate"][l], a["ff2_w_up"][l]], axis=1))], "wd2": [half(a["ff2_w_down"][l])]})
    gather = _WeightGather(shards)

    def ffn_weights(l, f):
        def get(piece):
            (g,) = gather.weights(l, piece + f)
            return _whole(g) if piece == "gu" else rows_of_chips(g)
        return get

    def mixer_weights(l):
        def get(piece):
            got = gather.weights(l, piece)
            if piece == "win":
                return _permute_w_in(_whole(got[0]).transpose(1, 0, 2).reshape(D, IN_WIDTH))
            return rows_of_chips(got[0]), rows_of_chips(got[1])
        return get

    conv_local = a["dn_conv_w"].reshape(-1)
    conv_rows = -(-conv_local.shape[0] // (16 * FLAT_LANES)) * 16
    conv_pad = jnp.concatenate([conv_local, jnp.zeros((conv_rows * FLAT_LANES - conv_local.shape[0],), F32)])
    (conv_all,) = _gather_shards([conv_pad.reshape(2, conv_rows // 2, FLAT_LANES)], "gather_conv")
    conv_all = conv_all.reshape(N_SHARD, -1)[:, :conv_local.shape[0]].reshape(N_SHARD, L, DN_CONV, -1)
    conv_full = conv_all.transpose(1, 2, 0, 3).reshape(L, DN_CONV, 3 * DN_WIDTH)

    expand = jnp.repeat(jnp.eye(S5_GROUPS, dtype=F32), S5_GROUP_CH, axis=0)
    lanes = lambda v, n=128: jnp.broadcast_to(v[:, None], (v.shape[0], n))
    to_t = lambda m: m.transpose(0, 2, 1).reshape(S5_WIDTH, S5_STATE)

    def small_params(l):
        sp = {k: a[k][l][None] for k in ("ff1_norm_pre", "ff1_norm_post", "mix_norm_pre", "mix_norm_post",
                                         "ff2_norm_pre", "ff2_norm_post")}
        sp["sinks"] = lanes(a["attn_sinks"][l])
        sp["conv_w"] = conv_full[l]
        sp["dn_a_log"] = lanes(a["dn_a_log"][l])[:, None, :]
        sp["dn_dt_bias"] = lanes(a["dn_dt_bias"][l])[:, None, :]
        sp["dn_norm_w"] = a["dn_norm_w"][l][None, None]
        sp["s5_pins"] = [a["s5_a_re"][l], a["s5_a_im"][l], lanes(a["s5_log_dt"][l], S5_STATE),
                         to_t(a["s5_b_re"][l]), to_t(a["s5_b_im"][l]), expand]
        sp["c_re"] = a["s5_c_re"][l].reshape(S5_WIDTH, S5_STATE)
        sp["c_im"] = a["s5_c_im"][l].reshape(S5_WIDTH, S5_STATE)
        sp["dskip"] = a["s5_d"][l][None]
        sp["glu_b"] = a["s5_glu_b"][l][None]
        return sp

    rope = _rope_tables(S)
    sps = [small_params(l) for l in range(L)]

    saved = []
    for l in range(L):
        sp = sps[l]
        mm_of = lambda f: (lambda piece, p, q, **kw: gather.mm(l, piece + f, p, q, **kw))
        x1, s1 = _ffn_fwd(x, sp["ff1_norm_pre"], sp["ff1_norm_post"], ffn_weights(l, "1"), "ff1", mm_of("1"))
        x2, s2 = _mixer_fwd(x1, sp, mixer_weights(l), rope, mm_of(""))
        x3, s3 = _ffn_fwd(x2, sp["ff2_norm_pre"], sp["ff2_norm_post"], ffn_weights(l, "2"), "ff2", mm_of("2"))
        saved.append((x, s1, x1, s2, x2, s3))
        x = x3
    dx, loss_part = _loss_head(x, target, "head")

    big_grads, small_grads = [None] * L, [None] * L
    shard_major = lambda m: m.reshape(N_SHARD, m.shape[0] // N_SHARD, m.shape[1])
    for l in reversed(range(L)):
        sp = sps[l]
        x0, s1, x1, s2, x2, s3 = saved[l]
        red = _GradReduce()

        def ffn_sink(f):
            return lambda piece, g: red.add(piece + f, g if piece == "gu" else shard_major(g))

        def mixer_sink(piece, g):
            if piece == "win":
                g = _unpermute_w_in(g).reshape(D, N_SHARD, IN_WIDTH // N_SHARD).transpose(1, 0, 2)
            red.add(piece, g if piece == "win" else shard_major(g))

        dx, g_pre2, g_post2 = _ffn_bwd(dx, x2, sp["ff2_norm_pre"], sp["ff2_norm_post"], s3, "ff2", red.mm, ffn_sink("2"))
        dx, sg = _mixer_bwd(dx, x1, sp, rope, s2, red.mm, mixer_sink)
        dx, g_pre1, g_post1 = _ffn_bwd(dx, x0, sp["ff1_norm_pre"], sp["ff1_norm_post"], s1, "ff1", red.mm, ffn_sink("1"))
        big_grads[l] = red.finish()
        sg.update({"ff1_norm_pre": g_pre1[0], "ff1_norm_post": g_post1[0], "ff2_norm_pre": g_pre2[0],
                   "ff2_norm_post": g_post2[0]})
        small_grads[l] = sg
    grad_x = dx[None]

    grads = {}
    layers = lambda k: jnp.stack([big_grads[l][k] for l in range(L)])
    for f, (gu, wd) in (("ff1", ("gu1", "wd1")), ("ff2", ("gu2", "wd2"))):
        gus = layers(gu)
        grads[f + "_w_gate"], grads[f + "_w_up"] = gus[:, :, :Fs], gus[:, :, Fs:]
        grads[f + "_w_down"] = layers(wd)
    grads["w_in"], grads["w_out"], grads["s5_glu_w"] = layers("win"), layers("wout"), layers("glu")

    small_local = [jnp.stack([small_grads[l][n] for l in range(L)]) for n in SMALL]
    vec = _pack_small(small_local, extra=(loss_part[0, :1],))
    total = _sum_slots(_gather_all_devices(vec, "gather_small"), "sum_small")
    small_total, rest = _unpack_small(total, [g.shape for g in small_local])
    loss = rest[0]
    for n, g in zip(SMALL, small_total):
        grads[n] = g
    cw = 3 * DN_WIDTH // N_SHARD
    grads["dn_conv_w"] = lax.dynamic_slice_in_dim(grads["dn_conv_w"], chip * cw, cw, axis=2)

    delta, new_m, new_v = {}, {}, {}
    for n in BIG:
        delta[n], new_m[n], new_v[n] = _adamw(a[n], grads[n], a["m_" + n], a["v_" + n], "adamw_" + n)
    shapes = [a[n].shape for n in SMALL]
    packed = [_pack_small([src[n] for n in SMALL]) for src in
              (a, grads, {n: a["m_" + n] for n in SMALL}, {n: a["v_" + n] for n in SMALL})]
    for dst, res in zip((delta, new_m, new_v), _adamw(*packed, "adamw_small")):
        for n, val in zip(SMALL, _unpack_small(res, shapes)[0]):
            dst[n] = val
    return (loss, grad_x, *[grads[n] for n in WEIGHTS], *[delta[n] for n in WEIGHTS], *[new_m[n] for n in WEIGHTS],
            *[new_v[n] for n in WEIGHTS])


def kernel(x, ff1_norm_pre, ff1_w_gate, ff1_w_up, ff1_w_down, ff1_norm_post, mix_norm_pre, w_in, attn_sinks, dn_conv_w, dn_a_log, dn_dt_bias, dn_norm_w, s5_a_re, s5_a_im, s5_log_dt, s5_b_re, s5_b_im, s5_c_re, s5_c_im, s5_d, s5_glu_w, s5_glu_b, w_out, mix_norm_post, ff2_norm_pre, ff2_w_gate, ff2_w_up, ff2_w_down, ff2_norm_post, loss_target, m_ff1_norm_pre, m_ff1_w_gate, m_ff1_w_up, m_ff1_w_down, m_ff1_norm_post, m_mix_norm_pre, m_w_in, m_attn_sinks, m_dn_conv_w, m_dn_a_log, m_dn_dt_bias, m_dn_norm_w, m_s5_a_re, m_s5_a_im, m_s5_log_dt, m_s5_b_re, m_s5_b_im, m_s5_c_re, m_s5_c_im, m_s5_d, m_s5_glu_w, m_s5_glu_b, m_w_out, m_mix_norm_post, m_ff2_norm_pre, m_ff2_w_gate, m_ff2_w_up, m_ff2_w_down, m_ff2_norm_post, v_ff1_norm_pre, v_ff1_w_gate, v_ff1_w_up, v_ff1_w_down, v_ff1_norm_post, v_mix_norm_pre, v_w_in, v_attn_sinks, v_dn_conv_w, v_dn_a_log, v_dn_dt_bias, v_dn_norm_w, v_s5_a_re, v_s5_a_im, v_s5_log_dt, v_s5_b_re, v_s5_b_im, v_s5_c_re, v_s5_c_im, v_s5_d, v_s5_glu_w, v_s5_glu_b, v_w_out, v_mix_norm_post, v_ff2_norm_pre, v_ff2_w_gate, v_ff2_w_up, v_ff2_w_down, v_ff2_norm_post):
    return _step(dict(locals()))
```

```python
import functools
import math

import numpy as np
import jax
import jax.numpy as jnp
from jax import lax
from jax.experimental import pallas as pl
from jax.experimental.pallas import tpu as pltpu

F32 = jnp.float32
BF16 = jnp.bfloat16
HI = lax.Precision.HIGHEST
MESH_ID = pl.DeviceIdType.MESH

NORM_EPS = 1e-6
FFN_RES_WEIGHT = 0.5
ATTN_HEADS, ATTN_KV_HEADS, HEAD_DIM, WINDOW = 8, 2, 128, 128
ROPE_THETA = 10000.0
DN_HEADS, DN_HEAD_DIM, DN_CONV, DN_CHUNK = 4, 128, 4, 64
S5_GROUPS, S5_GROUP_CH, S5_STATE = 32, 16, 64
ATTN_WIDTH = ATTN_HEADS * HEAD_DIM
ATTN_KV_WIDTH = ATTN_KV_HEADS * HEAD_DIM
DN_WIDTH = DN_HEADS * DN_HEAD_DIM
S5_WIDTH = S5_GROUPS * S5_GROUP_CH
S5_LANES = S5_GROUPS * S5_STATE
MIX_WIDTH = ATTN_WIDTH + DN_WIDTH + S5_WIDTH
IN_SPLITS = (ATTN_WIDTH, ATTN_KV_WIDTH, ATTN_KV_WIDTH, 3 * DN_WIDTH, DN_WIDTH, DN_HEADS, DN_HEADS, S5_WIDTH)
IN_WIDTH = sum(IN_SPLITS)
Z_AQ, Z_AK, Z_AV = 0, ATTN_WIDTH, ATTN_WIDTH + ATTN_KV_WIDTH
Z_DQKV = ATTN_WIDTH + 2 * ATTN_KV_WIDTH
Z_DZ = Z_DQKV + 3 * DN_WIDTH
Z_SU = Z_DZ + DN_WIDTH
Z_DBA = Z_SU + S5_WIDTH
Z_WIDTH = Z_DBA + 128

ADAM_LR, ADAM_B1, ADAM_B2, ADAM_EPS, ADAM_WD, ADAM_STEP = 0.001, 0.9, 0.999, 1e-08, 0.01, 10

N_SHARD = 4
FLAT_LANES = 512
SMALL_LANES = 1024
VMEM_LIMIT = 56 * 1024 * 1024

WEIGHTS = ['ff1_norm_pre', 'ff1_w_gate', 'ff1_w_up', 'ff1_w_down', 'ff1_norm_post', 'mix_norm_pre', 'w_in',
           'attn_sinks', 'dn_conv_w', 'dn_a_log', 'dn_dt_bias', 'dn_norm_w', 's5_a_re', 's5_a_im', 's5_log_dt',
           's5_b_re', 's5_b_im', 's5_c_re', 's5_c_im', 's5_d', 's5_glu_w', 's5_glu_b', 'w_out', 'mix_norm_post',
           'ff2_norm_pre', 'ff2_w_gate', 'ff2_w_up', 'ff2_w_down', 'ff2_norm_post']
BIG = ['ff1_w_gate', 'ff1_w_up', 'ff1_w_down', 'w_in', 's5_glu_w', 'w_out', 'ff2_w_gate', 'ff2_w_up', 'ff2_w_down']
SMALL = [n for n in WEIGHTS if n not in BIG]


def _pick(dim, cands):
    for c in cands:
        if dim % c == 0:
            return c
    return dim


def _params(sem=None):
    return pltpu.CompilerParams(dimension_semantics=sem, vmem_limit_bytes=VMEM_LIMIT)


def _silu_mul(gate, up):
    return gate * jax.nn.sigmoid(gate) * up


def _mm(a, b, *, ta=False, tb=False, out_dtype=F32, name, b_groups=None, bk_groups=None, out_groups=None, jobs=(),
        a_swiglu=None, swiglu_bwd_of=None):
    K, M = a.shape if ta else a.shape[::-1]
    if a_swiglu:
        K, M = (K, M // 2) if ta else (K // 2, M)
    Ng = Kg = None
    if b_groups:
        assert not tb
        _, Kb, Ng = b.shape
        N = b_groups * Ng
    elif bk_groups:
        assert tb
        _, N, Kg = b.shape
        Kb = bk_groups * Kg
    else:
        N, Kb = (b.shape if tb else b.shape[::-1])
    assert K == Kb, (a.shape, b.shape, ta, tb)
    tiles = (1408, 1024, 512, 384, 256, 128)
    tm = _pick(M, (1024, 512, 256, 128))
    tk = _pick(Kg if bk_groups else K, (2048,) + tiles)
    tn = _pick(N // out_groups if out_groups else (Ng if b_groups else N), tiles)
    if a_swiglu and ta:
        tm, tk = _pick(a_swiglu, tiles), _pick(K, tiles[1:])
    elif a_swiglu:
        tm, tk, tn = _pick(M, (512, 256, 128)), _pick(a_swiglu, tiles), _pick(N, (2048,) + tiles)
    if swiglu_bwd_of is not None:
        tm, tn = _pick(M, (512, 256, 128)), swiglu_bwd_of.shape[1] // (2 * N_SHARD)
        assert tk == K
    nk = K // tk
    dims = (((0,) if ta else (1,), (1,) if tb else (0,)), ((), ()))

    grid = (M // tm, N // tn, nk)
    n_job_in = sum(len(j["ins"]) for j in jobs)
    n_sems = sum(_JOB_SEMS[j["kind"]] for j in jobs)
    n_a = 2 if a_swiglu else 1
    n_lead = n_a + 1 + (1 if swiglu_bwd_of is not None else 0)

    def body(*refs):
        b_ref = refs[n_a]
        if a_swiglu:
            a_val = _silu_mul(refs[0][...].astype(F32), refs[1][...].astype(F32)).astype(BF16)
        else:
            a_val = refs[0][...].astype(BF16)
        job_ins = refs[n_lead:n_lead + n_job_in]
        o_ref = refs[n_lead + n_job_in]
        job_outs = refs[n_lead + 1 + n_job_in:n_lead + 1 + n_job_in + len(jobs)]
        scratch = refs[n_lead + 1 + n_job_in + len(jobs):]
        if jobs:
            send_sems, recv_sems = scratch[-2:]
            scratch = scratch[:-2]
            ids = [pl.program_id(d) for d in range(3)]
            first = functools.reduce(jnp.logical_and, [i == 0 for i in ids])
            last = functools.reduce(jnp.logical_and, [i == g - 1 for i, g in zip(ids, grid)])

            def copies():
                sends, recvs, at, sem = [], [], 0, 0
                for j, out in zip(jobs, job_outs):
                    s, r = _job_copies(j["kind"], job_ins[at:at + len(j["ins"])], out, send_sems, recv_sems, sem)
                    sends, recvs = sends + s, recvs + r
                    at, sem = at + len(j["ins"]), sem + _JOB_SEMS[j["kind"]]
                return sends, recvs

            @pl.when(first)
            def _():
                for cp in copies()[0]:
                    cp.start()

        part = lax.dot_general(a_val, b_ref[...].astype(BF16), dims, preferred_element_type=F32)
        if swiglu_bwd_of is not None:
            gate, up = refs[n_a + 1][:, :tn].astype(F32), refs[n_a + 1][:, tn:].astype(F32)
            sg = jax.nn.sigmoid(gate)
            o_ref[:, :tn] = (part * up * sg * (1.0 + gate * (1.0 - sg))).astype(o_ref.dtype)
            o_ref[:, tn:] = (part * gate * sg).astype(o_ref.dtype)
        elif nk == 1:
            o_ref[...] = part.astype(o_ref.dtype)
        else:
            acc_ref, = scratch
            k = pl.program_id(2)

            @pl.when(k == 0)
            def _():
                acc_ref[...] = part

            @pl.when(k > 0)
            def _():
                acc_ref[...] += part

            @pl.when(k == nk - 1)
            def _():
                o_ref[...] = acc_ref[...].astype(o_ref.dtype)

        if jobs:
            @pl.when(last)
            def _():
                sends, recvs = copies()
                for cp in recvs:
                    cp.wait_recv()
                for cp in sends:
                    cp.wait_send()

    if a_swiglu and ta:
        u = a_swiglu // tm
        a_specs = [pl.BlockSpec((tk, tm), lambda i, j, k, o=o: (k, (i // u) * 2 * u + o + i % u)) for o in (0, u)]
    elif a_swiglu:
        u = a_swiglu // tk
        a_specs = [pl.BlockSpec((tm, tk), lambda i, j, k, o=o: (i, (k // u) * 2 * u + o + k % u)) for o in (0, u)]
    else:
        a_specs = [pl.BlockSpec((tk, tm), lambda i, j, k: (k, i)) if ta else pl.BlockSpec((tm, tk), lambda i, j, k: (i, k))]
    if b_groups:
        per = Ng // tn
        b_spec = pl.BlockSpec((None, tk, tn), lambda i, j, k: (j // per, k, j % per))
    elif bk_groups:
        per = Kg // tk
        b_spec = pl.BlockSpec((None, tn, tk), lambda i, j, k: (k // per, j, k % per))
    else:
        b_spec = pl.BlockSpec((tn, tk), lambda i, j, k: (j, k)) if tb else pl.BlockSpec((tk, tn), lambda i, j, k: (k, j))
    if out_groups:
        pero = (N // out_groups) // tn
        o_spec = pl.BlockSpec((None, tm, tn), lambda i, j, k: (j // pero, i, j % pero))
        o_shape = jax.ShapeDtypeStruct((out_groups, M, N // out_groups), out_dtype)
    else:
        o_spec = pl.BlockSpec((tm, tn), lambda i, j, k: (i, j))
        o_shape = jax.ShapeDtypeStruct((M, N), out_dtype)
    lead_specs, lead_args = a_specs + [b_spec], [a] * n_a + [b]
    if swiglu_bwd_of is not None:
        o_spec = pl.BlockSpec((tm, 2 * tn), lambda i, j, k: (i, j))
        o_shape = jax.ShapeDtypeStruct((M, 2 * N), out_dtype)
        lead_specs, lead_args = lead_specs + [o_spec], lead_args + [swiglu_bwd_of]
    scratch = [pltpu.VMEM((tm, tn), F32)] if nk > 1 else []
    if not jobs:
        return pl.pallas_call(
            body, name=name, grid=grid, in_specs=lead_specs, out_specs=o_spec, out_shape=o_shape,
            scratch_shapes=scratch, compiler_params=_params(("parallel", "parallel", "arbitrary")))(*lead_args)
    job_args = [x for j in jobs for x in j["ins"]]
    aliases, at = {}, n_lead
    for n, j in enumerate(jobs):
        if j["kind"] == "gather_d2d":
            aliases[at] = 1 + n
        at += len(j["ins"])
    res = pl.pallas_call(
        body, name=name, grid=grid, in_specs=lead_specs + [_ANY] * n_job_in,
        out_specs=[o_spec] + [_ANY] * len(jobs), out_shape=[o_shape] + [_job_out_shape(j) for j in jobs],
        input_output_aliases=aliases,
        scratch_shapes=scratch + [pltpu.SemaphoreType.DMA((n_sems,)), pltpu.SemaphoreType.DMA((n_sems,))],
        compiler_params=_params(("arbitrary", "arbitrary", "arbitrary")))(*lead_args, *job_args)
    return res[0], list(res[1:])


def _row_spec(tm, width, off):
    return pl.BlockSpec((tm, width), lambda i, j: (i, off + j))


def _const_spec(shape):
    return pl.BlockSpec(shape, lambda i, j: (0,) * len(shape))


def _rows_fwd(fn, rows, consts, outs, *, name, tm=256, ncol=1):
    S = rows[0][0].shape[0]
    tm = _pick(S, (tm, 128, 64))
    nr, nc = len(rows), len(consts)

    def body(*refs):
        vals = [r[...].astype(F32) for r in refs[:nr + nc]]
        res = fn(*vals)
        for o_ref, o in zip(refs[nr + nc:], res):
            o_ref[...] = o.astype(o_ref.dtype)

    return pl.pallas_call(
        body, name=name, grid=(S // tm, ncol),
        in_specs=[_row_spec(tm, w, off) for _, w, off in rows] + [_const_spec(c.shape) for c in consts],
        out_specs=[_row_spec(tm, bw, 0) for _, bw, _ in outs],
        out_shape=[jax.ShapeDtypeStruct((S, tw), dt) for tw, _, dt in outs],
        compiler_params=_params(("parallel", "parallel")))(*[r[0] for r in rows], *consts)


def _rows_bwd(fn, rows, consts, cts, row_grads, *, name, tm=256, ncol=1, add_to_first=None, lane_sum_consts=()):
    S = rows[0][0].shape[0]
    tm = _pick(S, (tm, 128, 64))
    nr, nc, nt = len(rows), len(consts), len(cts)
    n_in = nr + nc + nt + (1 if add_to_first is not None else 0)

    def body(*refs):
        vals = [r[...].astype(F32) for r in refs[:nr + nc]]
        ct = tuple(r[...].astype(F32) for r in refs[nr + nc:nr + nc + nt])
        _, vjp = jax.vjp(fn, *vals)
        grads = vjp(ct)
        outs = refs[n_in:]
        for n, (idx, _, _) in enumerate(row_grads):
            g = grads[idx]
            if n == 0 and add_to_first is not None:
                g = g + refs[n_in - 1][...].astype(F32)
            outs[n][...] = g.astype(outs[n].dtype)
        first = jnp.logical_and(pl.program_id(0) == 0, pl.program_id(1) == 0)
        for c in range(nc):
            o_ref = outs[len(row_grads) + c]
            g = grads[nr + c]
            if c in lane_sum_consts:
                g = jnp.broadcast_to(jnp.sum(g, axis=-1, keepdims=True), g.shape)

            @pl.when(first)
            def _():
                o_ref[...] = jnp.zeros_like(o_ref)

            o_ref[...] += g

    in_specs = ([_row_spec(tm, w, off) for _, w, off in rows] + [_const_spec(c.shape) for c in consts]
                + [_row_spec(tm, w, off) for _, w, off in cts])
    args = [r[0] for r in rows] + list(consts) + [c[0] for c in cts]
    if add_to_first is not None:
        in_specs.append(_row_spec(tm, rows[row_grads[0][0]][1], 0))
        args.append(add_to_first)
    out_specs = [_row_spec(tm, rows[idx][1], 0) for idx, _, _ in row_grads] + [_const_spec(c.shape) for c in consts]
    out_shape = ([jax.ShapeDtypeStruct((S, tw), dt) for _, tw, dt in row_grads]
                 + [jax.ShapeDtypeStruct(c.shape, F32) for c in consts])
    return pl.pallas_call(
        body, name=name, grid=(S // tm, ncol), in_specs=in_specs, out_specs=out_specs, out_shape=out_shape,
        compiler_params=_params(("arbitrary", "arbitrary")))(*args)


def _rms(x, gain):
    return x * lax.rsqrt(jnp.mean(x * x, axis=-1, keepdims=True) + NORM_EPS) * gain


def _prenorm_fn(x, gain):
    return (_rms(x, gain),)


def _postnorm_fn(weight):
    def fn(y, gain):
        return (weight * _rms(y, gain),)
    return fn


def _residual_fn(weight):
    def fn(x, y, gain):
        return (x + weight * _rms(y, gain),)
    return fn


def _swiglu_fn(blk):
    tf = blk.shape[1] // 2
    gate, up = blk[:, :tf], blk[:, tf:]
    return (gate * jax.nn.sigmoid(gate) * up,)


def _ffn_fwd(x, g_pre, g_post, weights, tag, mm):
    D = x.shape[1]
    (h,) = _rows_fwd(_prenorm_fn, [(x, D, 0)], [g_pre], [(D, D, BF16)], name=f"{tag}_prenorm")
    wgu = weights("gu")
    Fs = wgu.shape[2] // 2
    gu = mm("gu", h, wgu, b_groups=N_SHARD, out_dtype=BF16, name=f"{tag}_gate_up")
    wd = weights("wd")
    y = mm("wd", gu, wd, a_swiglu=Fs, out_dtype=F32, name=f"{tag}_down")
    (x_new,) = _rows_fwd(_residual_fn(FFN_RES_WEIGHT), [(x, D, 0), (y, D, 0)], [g_post], [(D, D, F32)],
                         name=f"{tag}_residual")
    return x_new, (h, gu, y, wgu, wd)


def _ffn_bwd(dx_new, x, g_pre, g_post, saved, tag, mm, sink):
    h, gu, y, wgu, wd = saved
    D = x.shape[1]
    Fs = wgu.shape[2] // 2
    d_y, d_g_post = _rows_bwd(_postnorm_fn(FFN_RES_WEIGHT), [(y, D, 0)], [g_post], [(dx_new, D, 0)],
                              [(0, D, BF16)], name=f"{tag}_postnorm_bwd")
    d_gu = mm(d_y, wd, tb=True, swiglu_bwd_of=gu, out_dtype=BF16, name=f"{tag}_down_dx")
    sink("wd", mm(gu, d_y, ta=True, a_swiglu=Fs, out_dtype=BF16, name=f"{tag}_down_dw"))
    sink("gu", mm(h, d_gu, ta=True, out_dtype=BF16, out_groups=N_SHARD, name=f"{tag}_gate_up_dw"))
    d_h = mm(d_gu, wgu, tb=True, bk_groups=N_SHARD, out_dtype=BF16, name=f"{tag}_gate_up_dx")
    dx, d_g_pre = _rows_bwd(_prenorm_fn, [(x, D, 0)], [g_pre], [(d_h, D, 0)], [(0, D, F32)],
                            name=f"{tag}_prenorm_bwd", add_to_first=dx_new)
    return dx, d_g_pre, d_g_post


@jax.custom_vjp
def _swap_halves(x):
    return pltpu.roll(x, HEAD_DIM // 2, 1)


def _swap_fwd(x):
    return _swap_halves(x), None


def _swap_bwd(_, g):
    return (_swap_halves(g),)


_swap_halves.defvjp(_swap_fwd, _swap_bwd)


def _rope(x, cc, ss):
    return x * cc + _swap_halves(x) * ss


def _attn_block_fn(has_prev):
    grp = ATTN_HEADS // ATTN_KV_HEADS
    scale = HEAD_DIM ** -0.5
    nt = (((1,), (1,)), ((), ()))
    nn = (((1,), (0,)), ((), ()))

    def fn(*a):
        q = a[:8]
        kp, kc, vp, vc = a[8:10], a[10:12], a[12:14], a[14:16]
        cc, ss, ccp, ssp, sinks = a[16:21]
        row = lax.broadcasted_iota(jnp.int32, (WINDOW, WINDOW), 0)
        col = lax.broadcasted_iota(jnp.int32, (WINDOW, WINDOW), 1)
        m_cur = col <= row
        m_prev = jnp.logical_and(col > row, has_prev)
        outs = []
        for h in range(ATTN_HEADS):
            g = h // grp
            qr = _rope(q[h], cc, ss).astype(BF16)
            kcr = _rope(kc[g], cc, ss).astype(BF16)
            kpr = _rope(kp[g], ccp, ssp).astype(BF16)
            s_c = jnp.where(m_cur, lax.dot_general(qr, kcr, nt, preferred_element_type=F32) * scale, -jnp.inf)
            s_p = jnp.where(m_prev, lax.dot_general(qr, kpr, nt, preferred_element_type=F32) * scale, -jnp.inf)
            sink = sinks[h:h + 1, :]
            m = jnp.maximum(jnp.maximum(jnp.max(s_c, axis=-1, keepdims=True), jnp.max(s_p, axis=-1, keepdims=True)),
                            sink)
            p_c = jnp.exp(s_c - m)
            p_p = jnp.exp(s_p - m)
            den = (jnp.sum(p_c, axis=-1, keepdims=True) + jnp.sum(p_p, axis=-1, keepdims=True) + jnp.exp(sink - m))
            inv = 1.0 / den
            o = (lax.dot_general((p_c * inv).astype(BF16), vc[g].astype(BF16), nn, preferred_element_type=F32)
                 + lax.dot_general((p_p * inv).astype(BF16), vp[g].astype(BF16), nn, preferred_element_type=F32))
            outs.append(o)
        return tuple(outs)

    return fn


def _attn_specs(order):
    kvb = ATTN_WIDTH // (2 * ATTN_KV_WIDTH)
    return [
        pl.BlockSpec((WINDOW, ATTN_WIDTH), lambda i: (order(i), 0)),
        pl.BlockSpec((WINDOW, 2 * ATTN_KV_WIDTH), lambda i: (jnp.maximum(order(i) - 1, 0), kvb)),
        pl.BlockSpec((WINDOW, 2 * ATTN_KV_WIDTH), lambda i: (order(i), kvb)),
        pl.BlockSpec((WINDOW, HEAD_DIM), lambda i: (order(i), 0)),
        pl.BlockSpec((WINDOW, HEAD_DIM), lambda i: (order(i), 0)),
        pl.BlockSpec((WINDOW, HEAD_DIM), lambda i: (jnp.maximum(order(i) - 1, 0), 0)),
        pl.BlockSpec((WINDOW, HEAD_DIM), lambda i: (jnp.maximum(order(i) - 1, 0), 0)),
        pl.BlockSpec((ATTN_HEADS, HEAD_DIM), lambda i: (0, 0)),
    ]


def _attn_args(q_ref, kvp_ref, kvc_ref, cc, ss, ccp, ssp, sinks):
    d = HEAD_DIM
    q = [q_ref[:, h * d:(h + 1) * d].astype(F32) for h in range(ATTN_HEADS)]
    kp = [kvp_ref[:, g * d:(g + 1) * d].astype(F32) for g in range(ATTN_KV_HEADS)]
    vp = [kvp_ref[:, ATTN_KV_WIDTH + g * d:ATTN_KV_WIDTH + (g + 1) * d].astype(F32) for g in range(ATTN_KV_HEADS)]
    kc = [kvc_ref[:, g * d:(g + 1) * d].astype(F32) for g in range(ATTN_KV_HEADS)]
    vc = [kvc_ref[:, ATTN_KV_WIDTH + g * d:ATTN_KV_WIDTH + (g + 1) * d].astype(F32) for g in range(ATTN_KV_HEADS)]
    return q + kp + kc + vp + vc + [cc[...], ss[...], ccp[...], ssp[...], sinks[...]]


def _attn_fwd(z, cc, ss, sinks_b, tag):
    S = z.shape[0]
    nb = S // WINDOW

    def body(q_ref, kvp_ref, kvc_ref, cc_r, ss_r, ccp_r, ssp_r, sink_r, o_ref):
        n = pl.program_id(0)
        outs = _attn_block_fn(n > 0)(*_attn_args(q_ref, kvp_ref, kvc_ref, cc_r, ss_r, ccp_r, ssp_r, sink_r))
        for h in range(ATTN_HEADS):
            o_ref[:, h * HEAD_DIM:(h + 1) * HEAD_DIM] = outs[h].astype(o_ref.dtype)

    return pl.pallas_call(
        body, name=f"{tag}_attn", grid=(nb,), in_specs=_attn_specs(lambda i: i),
        out_specs=pl.BlockSpec((WINDOW, ATTN_WIDTH), lambda i: (i, 0)),
        out_shape=jax.ShapeDtypeStruct((S, ATTN_WIDTH), BF16),
        compiler_params=_params(("parallel",)))(z, z, z, cc, ss, cc, ss, sinks_b)


def _attn_bwd(z, cc, ss, sinks_b, d_out, tag):
    S = z.shape[0]
    nb = S // WINDOW
    d = HEAD_DIM
    rev = lambda i: nb - 1 - i

    def body(q_ref, kvp_ref, kvc_ref, cc_r, ss_r, ccp_r, ssp_r, sink_r, do_ref, dq_ref, dkv_ref, dsink_ref, carry):
        i = pl.program_id(0)
        n = nb - 1 - i

        @pl.when(i == 0)
        def _():
            carry[...] = jnp.zeros_like(carry)
            dsink_ref[...] = jnp.zeros_like(dsink_ref)

        args = _attn_args(q_ref, kvp_ref, kvc_ref, cc_r, ss_r, ccp_r, ssp_r, sink_r)
        _, vjp = jax.vjp(_attn_block_fn(n > 0), *args)
        g = vjp(tuple(do_ref[:, h * d:(h + 1) * d].astype(F32) for h in range(ATTN_HEADS)))
        for h in range(ATTN_HEADS):
            dq_ref[:, h * d:(h + 1) * d] = g[h].astype(dq_ref.dtype)
        for gi in range(ATTN_KV_HEADS):
            ks = slice(gi * d, (gi + 1) * d)
            vs = slice(ATTN_KV_WIDTH + gi * d, ATTN_KV_WIDTH + (gi + 1) * d)
            dkv_ref[:, ks] = (g[10 + gi] + carry[:, ks]).astype(dkv_ref.dtype)
            dkv_ref[:, vs] = (g[14 + gi] + carry[:, vs]).astype(dkv_ref.dtype)
            carry[:, ks] = g[8 + gi]
            carry[:, vs] = g[12 + gi]
        ds = g[20]
        dsink_ref[...] += jnp.broadcast_to(jnp.sum(ds, axis=-1, keepdims=True), ds.shape)

    return pl.pallas_call(
        body, name=f"{tag}_attn_bwd", grid=(nb,),
        in_specs=_attn_specs(rev) + [pl.BlockSpec((WINDOW, ATTN_WIDTH), lambda i: (rev(i), 0))],
        out_specs=[pl.BlockSpec((WINDOW, ATTN_WIDTH), lambda i: (rev(i), 0)),
                   pl.BlockSpec((WINDOW, 2 * ATTN_KV_WIDTH), lambda i: (rev(i), 0)),
                   pl.BlockSpec((ATTN_HEADS, HEAD_DIM), lambda i: (0, 0))],
        out_shape=[jax.ShapeDtypeStruct((S, ATTN_WIDTH), BF16), jax.ShapeDtypeStruct((S, 2 * ATTN_KV_WIDTH), BF16),
                   jax.ShapeDtypeStruct((ATTN_HEADS, HEAD_DIM), F32)],
        scratch_shapes=[pltpu.VMEM((WINDOW, 2 * ATTN_KV_WIDTH), F32)],
        compiler_params=_params(("arbitrary",)))(z, z, z, cc, ss, cc, ss, sinks_b, d_out)


def _rope_tables(seq):
    half = HEAD_DIM // 2
    inv_freq = ROPE_THETA ** (-jnp.arange(half, dtype=F32) / half)
    ang = jnp.arange(seq, dtype=F32)[:, None] * inv_freq[None, :]
    cos, sin = jnp.cos(ang), jnp.sin(ang)
    return jnp.concatenate([cos, cos], axis=1), jnp.concatenate([-sin, sin], axis=1)


CONV_COLS = 128


def _conv_pre(u, w_ref, S):
    row = lax.broadcasted_iota(jnp.int32, u.shape, 0)
    shifted = [u] + [jnp.where(row >= s, pltpu.roll(u, s, 0), 0.0) for s in range(1, DN_CONV)]
    y = shifted[0] * w_ref[DN_CONV - 1:DN_CONV, :]
    for s in range(1, DN_CONV):
        y = y + shifted[s] * w_ref[DN_CONV - 1 - s:DN_CONV - s, :]
    return y, shifted, row


def _conv_fwd(z, conv_w, tag):
    S = z.shape[0]
    ncol = 3 * DN_WIDTH // CONV_COLS

    def body(u_ref, w_ref, o_ref):
        y, _, _ = _conv_pre(u_ref[...], w_ref, S)
        o_ref[...] = y * jax.nn.sigmoid(y)

    return pl.pallas_call(
        body, name=f"{tag}_conv", grid=(ncol,),
        in_specs=[pl.BlockSpec((S, CONV_COLS), lambda j: (0, Z_DQKV // CONV_COLS + j)),
                  pl.BlockSpec((DN_CONV, CONV_COLS), lambda j: (0, j))],
        out_specs=pl.BlockSpec((S, CONV_COLS), lambda j: (0, j)),
        out_shape=jax.ShapeDtypeStruct((S, 3 * DN_WIDTH), F32),
        compiler_params=_params(("parallel",)))(z, conv_w)


def _conv_bwd(z, conv_w, d_out, tag):
    S = z.shape[0]
    ncol = 3 * DN_WIDTH // CONV_COLS

    def body(u_ref, w_ref, do_ref, du_ref, dw_ref):
        y, shifted, row = _conv_pre(u_ref[...], w_ref, S)
        sg = jax.nn.sigmoid(y)
        d_y = do_ref[...] * (sg * (1.0 + y * (1.0 - sg)))
        d_u = d_y * w_ref[DN_CONV - 1:DN_CONV, :]
        dw_ref[DN_CONV - 1:DN_CONV, :] = jnp.sum(d_y * shifted[0], axis=0, keepdims=True)
        for s in range(1, DN_CONV):
            back = jnp.where(row < S - s, pltpu.roll(d_y, S - s, 0), 0.0)
            d_u = d_u + back * w_ref[DN_CONV - 1 - s:DN_CONV - s, :]
            dw_ref[DN_CONV - 1 - s:DN_CONV - s, :] = jnp.sum(d_y * shifted[s], axis=0, keepdims=True)
        du_ref[...] = d_u.astype(du_ref.dtype)

    return pl.pallas_call(
        body, name=f"{tag}_conv_bwd", grid=(ncol,),
        in_specs=[pl.BlockSpec((S, CONV_COLS), lambda j: (0, Z_DQKV // CONV_COLS + j)),
                  pl.BlockSpec((DN_CONV, CONV_COLS), lambda j: (0, j)),
                  pl.BlockSpec((S, CONV_COLS), lambda j: (0, j))],
        out_specs=[pl.BlockSpec((S, CONV_COLS), lambda j: (0, j)), pl.BlockSpec((DN_CONV, CONV_COLS), lambda j: (0, j))],
        out_shape=[jax.ShapeDtypeStruct((S, 3 * DN_WIDTH), BF16), jax.ShapeDtypeStruct((DN_CONV, 3 * DN_WIDTH), F32)],
        compiler_params=_params(("parallel",)))(z, conv_w, d_out)


_NN = (((1,), (0,)), ((), ()))
_NT = (((1,), (1,)), ((), ()))
_TN = (((0,), (0,)), ((), ()))


def _dot3(a, b, dims):
    a_hi, b_hi = a.astype(BF16), b.astype(BF16)
    a_lo, b_lo = (a - a_hi.astype(F32)).astype(BF16), (b - b_hi.astype(F32)).astype(BF16)
    mm = lambda p, q: lax.dot_general(p, q, dims, preferred_element_type=F32)
    return mm(a_hi, b_hi) + (mm(a_hi, b_lo) + mm(a_lo, b_hi))


@functools.partial(jax.custom_vjp, nondiff_argnums=(2,))
def _dot_vjp(a, b, dims):
    return _dot3(a, b, dims)


def _dot_vjp_fwd(a, b, dims):
    return _dot3(a, b, dims), (a, b)


_BATCH = ((0,), (0,))
_BNN, _BNT, _BTN = (((2,), (1,)), _BATCH), (((2,), (2,)), _BATCH), (((1,), (1,)), _BATCH)


def _dot_vjp_bwd(dims, res, g):
    a, b = res
    nn, nt, tn = (_NN, _NT, _TN) if dims in (_NN, _NT, _TN) else (_BNN, _BNT, _BTN)
    if dims == nn:
        return _dot3(g, b, nt), _dot3(a, g, tn)
    if dims == nt:
        return _dot3(g, b, nn), _dot3(g, a, tn)
    return _dot3(b, g, nt), _dot3(a, g, nn)


_dot_vjp.defvjp(_dot_vjp_fwd, _dot_vjp_bwd)


def _dot(a, b, dims=_NN):
    return _dot_vjp(a, b, dims)


def _dot1(a, b, dims=_NN):
    return lax.dot_general(a.astype(BF16), b.astype(BF16), dims, preferred_element_type=F32)


def _nilpotent_inverse(m):
    H, C, _ = m.shape
    eye = (lax.broadcasted_iota(jnp.int32, (H, C, C), 1) == lax.broadcasted_iota(jnp.int32, (H, C, C), 2)).astype(F32)
    inv = eye + m
    for _ in range(5):
        m = _dot3(m, m, _BNN)
        inv = inv + _dot3(m, inv, _BNN)
    return inv


@jax.custom_vjp
def _unit_lower_solve(m, ru, rw):
    inv = _nilpotent_inverse(m)
    return _dot3(inv, ru, _BNN), _dot3(inv, rw, _BNN)


def _unit_lower_solve_fwd(m, ru, rw):
    inv = _nilpotent_inverse(m)
    xu, xw = _dot3(inv, ru, _BNN), _dot3(inv, rw, _BNN)
    return (xu, xw), (inv, xu, xw)


def _unit_lower_solve_bwd(res, g):
    inv, xu, xw = res
    dru, drw = _dot3(inv, g[0], _BTN), _dot3(inv, g[1], _BTN)
    return _dot3(dru, xu, _BNT) + _dot3(drw, xw, _BNT), dru, drw


_unit_lower_solve.defvjp(_unit_lower_solve_fwd, _unit_lower_solve_bwd)


def _dn_chunk_fn(state, q, k, v, zg, ba, a_log, dtb, norm_w):
    H, C, dk = DN_HEADS, DN_CHUNK, DN_HEAD_DIM
    head = lax.broadcasted_iota(jnp.int32, (H, C, dk), 0)
    rowl = lax.broadcasted_iota(jnp.int32, (H, C, dk), 1)
    lane = lax.broadcasted_iota(jnp.int32, (H, C, dk), 2)
    row = lax.broadcasted_iota(jnp.int32, (H, C, C), 1)
    col = lax.broadcasted_iota(jnp.int32, (H, C, C), 2)
    ba3 = jnp.broadcast_to(ba[None], (H, C, dk))
    bcol = jnp.sum(jnp.where(lane == head, ba3, 0.0), axis=-1, keepdims=True)
    acol = jnp.sum(jnp.where(lane == H + head, ba3, 0.0), axis=-1, keepdims=True)
    qn = q * lax.rsqrt(jnp.sum(q * q, axis=-1, keepdims=True) + NORM_EPS) * (dk ** -0.5)
    kn = k * lax.rsqrt(jnp.sum(k * k, axis=-1, keepdims=True) + NORM_EPS)
    beta = jax.nn.sigmoid(bcol)
    sp_in = acol + dtb
    softplus = jnp.maximum(sp_in, 0.0) + jnp.log(1.0 + jnp.exp(-jnp.abs(sp_in)))
    gt = -jnp.exp(a_log) * softplus
    gc = _dot((row >= col).astype(F32), gt, _BNN)
    gcol = jnp.mean(gc, axis=-1, keepdims=True)
    grow = _dot(jnp.full((H, C, dk), 1.0 / dk, F32), gc, _BNT)
    decay = jnp.exp(jnp.where(row >= col, gcol - grow, -jnp.inf))
    kb = kn * beta
    m = -jnp.where(row > col, _dot(kb, kn, _BNT) * decay, 0.0)
    u, w = _unit_lower_solve(m, v * beta, kb * jnp.exp(gc))
    attn = jnp.where(row >= col, _dot(qn, kn, _BNT) * decay, 0.0)
    q_dec = qn * jnp.exp(gc)
    gl = jnp.sum(jnp.where(rowl == C - 1, gc, 0.0), axis=1, keepdims=True)
    k_dec = kn * jnp.exp(gl - gc)
    v_new = u - _dot(w, state, _BNN)
    o = _dot(q_dec, state, _BNN) + _dot(attn, v_new, _BNN)
    state_new = state * jnp.exp(gl) + _dot(k_dec, v_new, _BTN)
    y = o * lax.rsqrt(jnp.mean(o * o, axis=-1, keepdims=True) + NORM_EPS) * norm_w
    y = y * (zg * jax.nn.sigmoid(zg))
    return state_new, y


def _dn_specs(order):
    C = DN_CHUNK
    return [pl.BlockSpec((C, 3 * DN_WIDTH), lambda i: (order(i), 0)),
            pl.BlockSpec((C, DN_WIDTH), lambda i: (order(i), Z_DZ // DN_WIDTH)),
            pl.BlockSpec((C, 128), lambda i: (order(i), Z_DBA // 128)),
            pl.BlockSpec((DN_HEADS, 1, DN_HEAD_DIM), lambda i: (0, 0, 0)),
            pl.BlockSpec((DN_HEADS, 1, DN_HEAD_DIM), lambda i: (0, 0, 0)),
            pl.BlockSpec((1, 1, DN_HEAD_DIM), lambda i: (0, 0, 0))]


def _dn_heads(ref, base=0):
    d = DN_HEAD_DIM
    return jnp.stack([ref[:, base + h * d:base + (h + 1) * d].astype(F32) for h in range(DN_HEADS)], axis=0)


def _dn_args(qkv_ref, zg_ref, ba_ref, alog_ref, dtb_ref, nw_ref):
    return [_dn_heads(qkv_ref), _dn_heads(qkv_ref, DN_WIDTH), _dn_heads(qkv_ref, 2 * DN_WIDTH), _dn_heads(zg_ref),
            ba_ref[...].astype(F32), alog_ref[...], dtb_ref[...], nw_ref[...]]


def _dn_fwd(qkv, z, a_log, dtb, norm_w, tag):
    S = qkv.shape[0]
    nchunk = S // DN_CHUNK
    d = DN_HEAD_DIM

    def body(qkv_ref, zg_ref, ba_ref, alog_ref, dtb_ref, nw_ref, y_ref, st_ref, state):
        @pl.when(pl.program_id(0) == 0)
        def _():
            state[...] = jnp.zeros_like(state)

        st_ref[...] = state[...]
        new, y = _dn_chunk_fn(state[...], *_dn_args(qkv_ref, zg_ref, ba_ref, alog_ref, dtb_ref, nw_ref))
        state[...] = new
        for h in range(DN_HEADS):
            y_ref[:, h * d:(h + 1) * d] = y[h].astype(y_ref.dtype)

    return pl.pallas_call(
        body, name=f"{tag}_deltanet", grid=(nchunk,), in_specs=_dn_specs(lambda i: i),
        out_specs=[pl.BlockSpec((DN_CHUNK, DN_WIDTH), lambda i: (i, 0)),
                   pl.BlockSpec((None, DN_HEADS, d, d), lambda i: (i, 0, 0, 0))],
        out_shape=[jax.ShapeDtypeStruct((S, DN_WIDTH), BF16), jax.ShapeDtypeStruct((nchunk, DN_HEADS, d, d), F32)],
        scratch_shapes=[pltpu.VMEM((DN_HEADS, d, d), F32)],
        compiler_params=_params(("arbitrary",)))(qkv, z, z, a_log, dtb, norm_w)


def _dn_bwd(qkv, z, a_log, dtb, norm_w, states, d_y, tag):
    S = qkv.shape[0]
    nchunk = S // DN_CHUNK
    d = DN_HEAD_DIM
    rev = lambda i: nchunk - 1 - i

    def body(qkv_ref, zg_ref, ba_ref, alog_ref, dtb_ref, nw_ref, st_ref, dy_ref,
             dqkv_ref, dzg_ref, dba_ref, dalog_ref, ddtb_ref, dnw_ref, d_state):
        @pl.when(pl.program_id(0) == 0)
        def _():
            d_state[...] = jnp.zeros_like(d_state)
            dalog_ref[...] = jnp.zeros_like(dalog_ref)
            ddtb_ref[...] = jnp.zeros_like(ddtb_ref)
            dnw_ref[...] = jnp.zeros_like(dnw_ref)

        args = [st_ref[...]] + _dn_args(qkv_ref, zg_ref, ba_ref, alog_ref, dtb_ref, nw_ref)
        _, vjp = jax.vjp(_dn_chunk_fn, *args)
        g = vjp((d_state[...], _dn_heads(dy_ref)))
        d_state[...] = g[0]
        for h in range(DN_HEADS):
            for n, base in enumerate((0, DN_WIDTH, 2 * DN_WIDTH)):
                dqkv_ref[:, base + h * d:base + (h + 1) * d] = g[1 + n][h]
            dzg_ref[:, h * d:(h + 1) * d] = g[4][h].astype(dzg_ref.dtype)
        dba_ref[...] = g[5].astype(dba_ref.dtype)
        lane_sum = lambda t: jnp.broadcast_to(jnp.sum(t, axis=-1, keepdims=True), t.shape)
        dalog_ref[...] += lane_sum(g[6])
        ddtb_ref[...] += lane_sum(g[7])
        dnw_ref[...] += g[8]

    hspec = pl.BlockSpec((DN_HEADS, 1, d), lambda i: (0, 0, 0))
    return pl.pallas_call(
        body, name=f"{tag}_deltanet_bwd", grid=(nchunk,),
        in_specs=_dn_specs(rev) + [pl.BlockSpec((None, DN_HEADS, d, d), lambda i: (rev(i), 0, 0, 0)),
                                   pl.BlockSpec((DN_CHUNK, DN_WIDTH), lambda i: (rev(i), 0))],
        out_specs=[pl.BlockSpec((DN_CHUNK, 3 * DN_WIDTH), lambda i: (rev(i), 0)),
                   pl.BlockSpec((DN_CHUNK, DN_WIDTH), lambda i: (rev(i), 0)),
                   pl.BlockSpec((DN_CHUNK, 128), lambda i: (rev(i), 0)),
                   hspec, hspec, pl.BlockSpec((1, 1, d), lambda i: (0, 0, 0))],
        out_shape=[jax.ShapeDtypeStruct((S, 3 * DN_WIDTH), F32), jax.ShapeDtypeStruct((S, DN_WIDTH), BF16),
                   jax.ShapeDtypeStruct((S, 128), BF16), jax.ShapeDtypeStruct((DN_HEADS, 1, d), F32),
                   jax.ShapeDtypeStruct((DN_HEADS, 1, d), F32), jax.ShapeDtypeStruct((1, 1, d), F32)],
        scratch_shapes=[pltpu.VMEM((DN_HEADS, d, d), F32)],
        compiler_params=_params(("arbitrary",)))(qkv, z, z, a_log, dtb, norm_w, states, d_y)


def _whole_fwd(fn, ins, outs, *, name):
    n = len(ins)

    def body(*refs):
        res = fn(*[r[...] for r in refs[:n]])
        for o_ref, o in zip(refs[n:], res):
            o_ref[...] = o

    return pl.pallas_call(body, name=name, out_shape=[jax.ShapeDtypeStruct(s, F32) for s in outs],
                          compiler_params=_params())(*ins)


def _whole_bwd(fn, ins, cts, n_grads, *, name, lane_sum=()):
    n, nt = len(ins), len(cts)

    def body(*refs):
        _, vjp = jax.vjp(fn, *[r[...] for r in refs[:n]])
        grads = vjp(tuple(r[...] for r in refs[n:n + nt]))
        for k in range(n_grads):
            g = grads[k]
            if k in lane_sum:
                g = jnp.broadcast_to(jnp.sum(g, axis=-1, keepdims=True), g.shape)
            refs[n + nt + k][...] = g

    return pl.pallas_call(body, name=name, out_shape=[jax.ShapeDtypeStruct(a.shape, F32) for a in ins[:n_grads]],
                          compiler_params=_params())(*ins, *cts)


S5_CHUNK = 256


def _s5_param_fn(a_re, a_im, ldt, bt_re, bt_im, expand):
    dt = jnp.exp(ldt)
    er = jnp.exp(a_re * dt)
    ab_re, ab_im = er * jnp.cos(a_im * dt), er * jnp.sin(a_im * dt)
    den = a_re * a_re + a_im * a_im
    co_re = ((ab_re - 1.0) * a_re + ab_im * a_im) / den
    co_im = (ab_im * a_re - (ab_re - 1.0) * a_im) / den
    cr, ci = _dot(expand, co_re), _dot(expand, co_im)
    return ab_re, ab_im, cr * bt_re - ci * bt_im, cr * bt_im + ci * bt_re


def _s5_scan(b_re, b_im, a_re, a_im, row, T, reverse):
    x_re, x_im, p_re, p_im = b_re, b_im, a_re, a_im
    d = 1
    while d < T:
        if reverse:
            s_re = jnp.where(row < T - d, pltpu.roll(x_re, T - d, 0), 0.0)
            s_im = jnp.where(row < T - d, pltpu.roll(x_im, T - d, 0), 0.0)
        else:
            s_re = jnp.where(row >= d, pltpu.roll(x_re, d, 0), 0.0)
            s_im = jnp.where(row >= d, pltpu.roll(x_im, d, 0), 0.0)
        x_re, x_im = x_re + p_re * s_re - p_im * s_im, x_im + p_re * s_im + p_im * s_re
        p_re, p_im = p_re * p_re - p_im * p_im, 2.0 * p_re * p_im
        d *= 2
    return x_re, x_im


def _s5_states(u, bre_ref, bim_ref, a_re, a_im, c_re, c_im, row, T):
    bu_re = _dot(u, bre_ref[...]) + jnp.where(row == 0, a_re * c_re - a_im * c_im, 0.0)
    bu_im = _dot(u, bim_ref[...]) + jnp.where(row == 0, a_re * c_im + a_im * c_re, 0.0)
    return _s5_scan(bu_re, bu_im, a_re, a_im, row, T, False)


def _s5_in_specs(order, T):
    full = lambda shape: pl.BlockSpec(shape, lambda i: (0,) * len(shape), pipeline_mode=pl.Buffered(1))
    return [pl.BlockSpec((T, S5_WIDTH), lambda i: (order(i), Z_SU // S5_WIDTH)),
            full((S5_WIDTH, S5_LANES)), full((S5_WIDTH, S5_LANES)), full((S5_WIDTH, S5_LANES)),
            full((S5_WIDTH, S5_LANES)), full((1, S5_LANES)), full((1, S5_LANES)), full((1, S5_WIDTH))]


def _s5_fwd(z, bre, bim, cre, cim, ab_re, ab_im, dskip, tag):
    S = z.shape[0]
    T = _pick(S, (S5_CHUNK, 128))
    nch = S // T

    def body(u_ref, bre_ref, bim_ref, cre_ref, cim_ref, are_ref, aim_ref, d_ref, y_ref, xre_ref, xim_ref, c_re, c_im):
        @pl.when(pl.program_id(0) == 0)
        def _():
            c_re[...] = jnp.zeros_like(c_re)
            c_im[...] = jnp.zeros_like(c_im)

        u = u_ref[...]
        row = lax.broadcasted_iota(jnp.int32, (T, S5_LANES), 0)
        x_re, x_im = _s5_states(u, bre_ref, bim_ref, are_ref[...], aim_ref[...], c_re[...], c_im[...], row, T)
        c_re[...] = jnp.sum(jnp.where(row == T - 1, x_re, 0.0), axis=0, keepdims=True)
        c_im[...] = jnp.sum(jnp.where(row == T - 1, x_im, 0.0), axis=0, keepdims=True)
        xre_ref[...] = x_re
        xim_ref[...] = x_im
        y_ref[...] = _dot(x_re, cre_ref[...], _NT) - _dot(x_im, cim_ref[...], _NT) + d_ref[...] * u

    return pl.pallas_call(
        body, name=f"{tag}_s5", grid=(nch,), in_specs=_s5_in_specs(lambda i: i, T),
        out_specs=[pl.BlockSpec((T, S5_WIDTH), lambda i: (i, 0)),
                   pl.BlockSpec((T, S5_LANES), lambda i: (i, 0)), pl.BlockSpec((T, S5_LANES), lambda i: (i, 0))],
        out_shape=[jax.ShapeDtypeStruct((S, S5_WIDTH), F32), jax.ShapeDtypeStruct((S, S5_LANES), F32),
                   jax.ShapeDtypeStruct((S, S5_LANES), F32)],
        scratch_shapes=[pltpu.VMEM((1, S5_LANES), F32), pltpu.VMEM((1, S5_LANES), F32)],
        compiler_params=_params(("arbitrary",)))(z, bre, bim, cre, cim, ab_re, ab_im, dskip)


S5_BWD_CHUNK = 128


def _s5_bwd(z, bre, bim, cre, cim, ab_re, ab_im, dskip, xre, xim, d_y, tag):
    S = z.shape[0]
    T = _pick(S, (S5_BWD_CHUNK, 64))
    nch = S // T
    rev = lambda i: nch - 1 - i
    above = lambda i: (jnp.maximum(rev(i) * (T // 8) - 1, 0), 0)
    full = lambda shape: pl.BlockSpec(shape, lambda i: (0,) * len(shape))

    def body(u_ref, bre_ref, bim_ref, cre_ref, cim_ref, are_ref, aim_ref, d_ref, xre_ref, xim_ref, pre_ref, pim_ref,
             dy_ref, du_ref, dbre_ref, dbim_ref, dcre_ref, dcim_ref, dare_ref, daim_ref, dd_ref, g_re, g_im):
        @pl.when(pl.program_id(0) == 0)
        def _():
            g_re[...] = jnp.zeros_like(g_re)
            g_im[...] = jnp.zeros_like(g_im)
            for r in (dbre_ref, dbim_ref, dcre_ref, dcim_ref, dare_ref, daim_ref, dd_ref):
                r[...] = jnp.zeros_like(r)

        u = u_ref[...]
        dy = dy_ref[...].astype(F32)
        a_re, a_im = are_ref[...], aim_ref[...]
        row = lax.broadcasted_iota(jnp.int32, (T, S5_LANES), 0)
        x_re, x_im = xre_ref[...], xim_ref[...]
        dcre_ref[...] += _dot1(dy, x_re, _TN)
        dcim_ref[...] -= _dot1(dy, x_im, _TN)
        has_before = (pl.program_id(0) < nch - 1).astype(F32)
        row8 = lax.broadcasted_iota(jnp.int32, (8, S5_LANES), 0)
        before = lambda ref: has_before * jnp.sum(jnp.where(row8 == 7, ref[...], 0.0), axis=0, keepdims=True)
        xp_re = jnp.where(row >= 1, pltpu.roll(x_re, 1, 0), 0.0) + jnp.where(row == 0, before(pre_ref), 0.0)
        xp_im = jnp.where(row >= 1, pltpu.roll(x_im, 1, 0), 0.0) + jnp.where(row == 0, before(pim_ref), 0.0)
        last = row == T - 1
        gd_re = _dot(dy, cre_ref[...]) + jnp.where(last, a_re * g_re[...] + a_im * g_im[...], 0.0)
        gd_im = -_dot(dy, cim_ref[...]) + jnp.where(last, a_re * g_im[...] - a_im * g_re[...], 0.0)
        t_re, t_im = _s5_scan(gd_re, gd_im, a_re, -a_im, row, T, True)
        g_re[...] = jnp.sum(jnp.where(row == 0, t_re, 0.0), axis=0, keepdims=True)
        g_im[...] = jnp.sum(jnp.where(row == 0, t_im, 0.0), axis=0, keepdims=True)
        du_ref[...] = (_dot1(t_re, bre_ref[...], _NT) + _dot1(t_im, bim_ref[...], _NT) + dy * d_ref[...]).astype(du_ref.dtype)
        dbre_ref[...] += _dot1(u, t_re, _TN)
        dbim_ref[...] += _dot1(u, t_im, _TN)
        dare_ref[...] += jnp.sum(t_re * xp_re + t_im * xp_im, axis=0, keepdims=True)
        daim_ref[...] += jnp.sum(t_im * xp_re - t_re * xp_im, axis=0, keepdims=True)
        dd_ref[...] += jnp.sum(dy * u, axis=0, keepdims=True)

    return pl.pallas_call(
        body, name=f"{tag}_s5_bwd", grid=(nch,),
        in_specs=_s5_in_specs(rev, T) + [pl.BlockSpec((T, S5_LANES), lambda i: (rev(i), 0)),
                                         pl.BlockSpec((T, S5_LANES), lambda i: (rev(i), 0)),
                                         pl.BlockSpec((8, S5_LANES), above), pl.BlockSpec((8, S5_LANES), above),
                                         pl.BlockSpec((T, S5_WIDTH), lambda i: (rev(i), 0))],
        out_specs=[pl.BlockSpec((T, S5_WIDTH), lambda i: (rev(i), 0))] + [full((S5_WIDTH, S5_LANES))] * 4
        + [full((1, S5_LANES))] * 2 + [full((1, S5_WIDTH))],
        out_shape=[jax.ShapeDtypeStruct((S, S5_WIDTH), BF16)] + [jax.ShapeDtypeStruct((S5_WIDTH, S5_LANES), F32)] * 4
        + [jax.ShapeDtypeStruct((1, S5_LANES), F32)] * 2 + [jax.ShapeDtypeStruct((1, S5_WIDTH), F32)],
        scratch_shapes=[pltpu.VMEM((1, S5_LANES), F32), pltpu.VMEM((1, S5_LANES), F32)],
        compiler_params=_params(("arbitrary",)))(z, bre, bim, cre, cim, ab_re, ab_im, dskip, xre, xim, xre, xim, d_y)


def _s5_glu_fn(y, glu_w, glu_b):
    g = 0.5 * y * (1.0 + jnp.tanh(math.sqrt(2.0 / math.pi) * (y + 0.044715 * (y * y * y))))
    lin = lax.dot_general(g.astype(BF16), glu_w.astype(BF16), (((1,), (0,)), ((), ())), preferred_element_type=F32)
    return (g * jax.nn.sigmoid(lin + glu_b),)


def _block_diag(m):
    G, H, P = S5_GROUPS, S5_GROUP_CH, S5_STATE
    eye = jnp.eye(G, dtype=m.dtype)
    return (m.reshape(G, H, 1, P) * eye[:, None, :, None]).reshape(G * H, G * P)


def _block_diag_take(m):
    G, H, P = S5_GROUPS, S5_GROUP_CH, S5_STATE
    eye = jnp.eye(G, dtype=m.dtype)
    return jnp.sum(m.reshape(G, H, G, P) * eye[:, None, :, None], axis=2).reshape(G * H, P)


def _loss_head(y, target, tag):
    S, D = y.shape
    tm = _pick(S, (256, 128, 64))

    def body(y_ref, t_ref, dy_ref, loss_ref):
        @pl.when(pl.program_id(0) == 0)
        def _():
            loss_ref[...] = jnp.zeros_like(loss_ref)

        err = y_ref[...] - t_ref[...]
        dy_ref[...] = err * (1.0 / D)
        part = 0.5 * jnp.sum(jnp.mean(err * err, axis=-1, keepdims=True), axis=0, keepdims=True)
        loss_ref[...] += jnp.broadcast_to(part, loss_ref.shape)

    return pl.pallas_call(
        body, name=f"{tag}_loss", grid=(S // tm,),
        in_specs=[pl.BlockSpec((tm, D), lambda i: (i, 0)), pl.BlockSpec((tm, D), lambda i: (i, 0))],
        out_specs=[pl.BlockSpec((tm, D), lambda i: (i, 0)), pl.BlockSpec((8, 128), lambda i: (0, 0))],
        out_shape=[jax.ShapeDtypeStruct((S, D), F32), jax.ShapeDtypeStruct((8, 128), F32)],
        compiler_params=_params(("arbitrary",)))(y, target)


def _adamw(w, g, m, v, name):
    shape = w.shape
    cols = shape[-1]
    rows = int(np.prod(shape[:-1]))
    tr = _pick(rows, [t for t in (512, 256, 128, 64, 32, 16, 8) if t * cols <= 256 * 1024] or [8])
    c1 = 1.0 - ADAM_B1 ** ADAM_STEP
    c2 = 1.0 - ADAM_B2 ** ADAM_STEP

    def body(w_ref, g_ref, m_ref, v_ref, d_ref, mo_ref, vo_ref):
        gg = g_ref[...]
        mn = ADAM_B1 * m_ref[...] + (1.0 - ADAM_B1) * gg
        vn = ADAM_B2 * v_ref[...] + (1.0 - ADAM_B2) * (gg * gg)
        d_ref[...] = -ADAM_LR * ((mn / c1) / (jnp.sqrt(vn / c2) + ADAM_EPS) + ADAM_WD * w_ref[...])
        mo_ref[...] = mn
        vo_ref[...] = vn

    spec = pl.BlockSpec((tr, cols), lambda i: (i, 0))
    outs = pl.pallas_call(
        body, name=name, grid=(rows // tr,), in_specs=[spec] * 4, out_specs=[spec] * 3,
        out_shape=[jax.ShapeDtypeStruct((rows, cols), F32)] * 3,
        compiler_params=_params(("parallel",)))(*[a.reshape(rows, cols) for a in (w, g, m, v)])
    return [o.reshape(shape) for o in outs]


def _sum_slots(slots, name):
    n, R, C = slots.shape
    tr = _pick(R, [t for t in (1024, 512, 256, 128, 64, 32, 16, 8) if n * t * C * 4 <= (2 << 20)] or [8])

    def body(s_ref, o_ref):
        acc = s_ref[0].astype(F32)
        for k in range(1, n):
            acc = acc + s_ref[k].astype(F32)
        o_ref[...] = acc

    return pl.pallas_call(
        body, name=name, grid=(R // tr,), in_specs=[pl.BlockSpec((n, tr, C), lambda i: (0, i, 0))],
        out_specs=pl.BlockSpec((tr, C), lambda i: (i, 0)),
        out_shape=jax.ShapeDtypeStruct((R, C), F32), compiler_params=_params(("parallel",)))(slots)


def _row_tile(R, C, itemsize, target=1 << 20):
    return _pick(R, [t for t in (2048, 1024, 512, 256, 128, 64, 32, 16) if t * C * itemsize <= target] or [16])


def _add_sibling(part, got, name):
    n, _, R2, C = part.shape
    tr = _row_tile(R2, C, 2)

    def body(p_ref, q_ref, o_ref):
        o_ref[...] = (p_ref[...].astype(F32) + q_ref[...].astype(F32)).astype(o_ref.dtype)

    spec = pl.BlockSpec((None, tr, C), lambda k, i: (k, i, 0))
    return pl.pallas_call(
        body, name=name, grid=(n, R2 // tr),
        in_specs=[pl.BlockSpec((None, None, tr, C), lambda k, i: (k, lax.axis_index("c"), i, 0)), spec], out_specs=spec,
        out_shape=jax.ShapeDtypeStruct((n, R2, C), BF16), compiler_params=_params(("parallel", "parallel")))(part, got)


def _sum_chips(chip_part, others, name):
    _, R2, C = chip_part.shape
    tr = _row_tile(R2, C, 4)

    def body(a_ref, b_ref, o_ref):
        acc = a_ref[...].astype(F32)
        for j in range(N_SHARD - 1):
            acc = acc + b_ref[j].astype(F32)
        o_ref[...] = acc

    return pl.pallas_call(
        body, name=name, grid=(R2 // tr,),
        in_specs=[pl.BlockSpec((None, tr, C), lambda i: (2 * lax.axis_index("x") + lax.axis_index("y"), i, 0)),
                  pl.BlockSpec((N_SHARD - 1, tr, C), lambda i: (0, i, 0))],
        out_specs=pl.BlockSpec((None, tr, C), lambda i: (lax.axis_index("c"), i, 0)),
        out_shape=jax.ShapeDtypeStruct((2, R2, C), F32), compiler_params=_params(("parallel",)))(chip_part, others)


_ANY = pl.BlockSpec(memory_space=pl.ANY)


def _place():
    return lax.axis_index("x"), lax.axis_index("y"), lax.axis_index("c")


def _other_chips(x, y):
    return [(1 - x, y), (x, 1 - y), (1 - x, 1 - y)]


def _remote(src, dst, send_sems, recv_sems, n, to):
    return pltpu.make_async_remote_copy(src_ref=src, dst_ref=dst, send_sem=send_sems.at[n], recv_sem=recv_sems.at[n],
                                        device_id=to, device_id_type=MESH_ID)


_JOB_SEMS = {"gather_ici": 3, "gather_d2d": 4, "scatter": 3, "swap": 4}


def _job_out_shape(job):
    src = job["ins"][0]
    if job["kind"] == "gather_ici":
        return jax.ShapeDtypeStruct((N_SHARD,) + src.shape, src.dtype)
    if job["kind"] == "gather_d2d":
        return jax.ShapeDtypeStruct(src.shape, src.dtype)
    if job["kind"] == "swap":
        return jax.ShapeDtypeStruct((N_SHARD,) + src.shape[2:], src.dtype)
    return jax.ShapeDtypeStruct((N_SHARD - 1,) + src.shape[1:], src.dtype)


def _job_copies(kind, ins, out, send_sems, recv_sems, base):
    x, y, c = _place()
    k = 2 * x + y
    sibling = (x, y, 1 - c)
    sends, recvs = [], []
    if kind == "swap":
        for s in range(N_SHARD):
            sends.append(_remote(ins[0].at[s, 1 - c], out.at[s], send_sems, recv_sems, base + s, sibling))
            recvs.append(_remote(out.at[s], out.at[s], send_sems, recv_sems, base + s, sibling))
        return sends, recvs
    for j, (cx, cy) in enumerate(_other_chips(x, y)):
        kj = 2 * cx + cy
        if kind == "gather_ici":
            sends.append(_remote(ins[0].at[c], out.at[k, c], send_sems, recv_sems, base + j, (cx, cy, c)))
            recvs.append(_remote(out.at[kj, c], out.at[kj, c], send_sems, recv_sems, base + j, sibling))
        elif kind == "gather_d2d":
            sends.append(_remote(out.at[kj, c], out.at[kj, c], send_sems, recv_sems, base + 1 + j, sibling))
            recvs.append(_remote(out.at[kj, 1 - c], out.at[kj, 1 - c], send_sems, recv_sems, base + 1 + j, sibling))
        else:
            sends.append(_remote(ins[0].at[kj], out.at[j], send_sems, recv_sems, base + j, (cx, cy, c)))
            recvs.append(_remote(out.at[j], out.at[j], send_sems, recv_sems, base + j, sibling))
    if kind == "gather_d2d":
        sends.append(_remote(ins[1], out.at[k], send_sems, recv_sems, base, sibling))
        recvs.append(_remote(out.at[k], out.at[k], send_sems, recv_sems, base, sibling))
    return sends, recvs


def _gather_shards(shards, name):
    n = len(shards)
    per = 7

    def body(*refs):
        ins, outs = refs[:n], refs[n:2 * n]
        send_sems, recv_sems = refs[2 * n:]
        x, y, c = _place()
        k = 2 * x + y
        sibling = (x, y, 1 - c)
        chips = _other_chips(x, y)
        started = []
        for a in range(n):
            cp = _remote(ins[a], outs[a].at[k], send_sems, recv_sems, per * a, sibling)
            cp.start()
            started.append(cp)
            for j, (cx, cy) in enumerate(chips):
                cp = _remote(ins[a].at[c], outs[a].at[k, c], send_sems, recv_sems, per * a + 1 + j, (cx, cy, c))
                cp.start()
                started.append(cp)
        for a in range(n):
            for j, (cx, cy) in enumerate(chips):
                landed = outs[a].at[2 * cx + cy, c]
                _remote(landed, landed, send_sems, recv_sems, per * a + 1 + j, sibling).wait_recv()
                cp = _remote(landed, landed, send_sems, recv_sems, per * a + 4 + j, sibling)
                cp.start()
                started.append(cp)
        for a in range(n):
            own = outs[a].at[k]
            _remote(own, own, send_sems, recv_sems, per * a, sibling).wait_recv()
            for j, (cx, cy) in enumerate(chips):
                passed = outs[a].at[2 * cx + cy, 1 - c]
                _remote(passed, passed, send_sems, recv_sems, per * a + 4 + j, sibling).wait_recv()
        for cp in started:
            cp.wait_send()

    return pl.pallas_call(
        body, name=name, in_specs=[_ANY] * n, out_specs=[_ANY] * n,
        out_shape=[jax.ShapeDtypeStruct((N_SHARD,) + s.shape, s.dtype) for s in shards],
        scratch_shapes=[pltpu.SemaphoreType.DMA((per * n,)), pltpu.SemaphoreType.DMA((per * n,))],
        )(*shards)


def _swap_other_half(parts, name):
    n = len(parts)

    def body(*refs):
        ins, outs = refs[:n], refs[n:2 * n]
        send_sems, recv_sems = refs[2 * n:]
        x, y, c = _place()
        started = []
        for a in range(n):
            for k in range(N_SHARD):
                cp = _remote(ins[a].at[k, 1 - c], outs[a].at[k], send_sems, recv_sems, N_SHARD * a + k, (x, y, 1 - c))
                cp.start()
                started.append(cp)
        for cp in started:
            cp.wait()

    return pl.pallas_call(
        body, name=name, in_specs=[_ANY] * n, out_specs=[_ANY] * n,
        out_shape=[jax.ShapeDtypeStruct((N_SHARD,) + p.shape[2:], p.dtype) for p in parts],
        scratch_shapes=[pltpu.SemaphoreType.DMA((N_SHARD * n,)), pltpu.SemaphoreType.DMA((N_SHARD * n,))],
        )(*parts)


def _scatter_to_chips(parts, name):
    n = len(parts)
    per = N_SHARD - 1

    def body(*refs):
        ins, outs = refs[:n], refs[n:2 * n]
        send_sems, recv_sems = refs[2 * n:]
        x, y, c = _place()
        chips = _other_chips(x, y)
        started = []
        for a in range(n):
            for j, (cx, cy) in enumerate(chips):
                cp = _remote(ins[a].at[2 * cx + cy], outs[a].at[j], send_sems, recv_sems, per * a + j, (cx, cy, c))
                cp.start()
                started.append(cp)
        for cp in started:
            cp.wait()

    return pl.pallas_call(
        body, name=name, in_specs=[_ANY] * n, out_specs=[_ANY] * n,
        out_shape=[jax.ShapeDtypeStruct((per,) + p.shape[1:], p.dtype) for p in parts],
        scratch_shapes=[pltpu.SemaphoreType.DMA((per * n,)), pltpu.SemaphoreType.DMA((per * n,))],
        )(*parts)


def _join_halves(bufs, name):
    n = len(bufs)

    def body(*refs):
        outs = refs[n:2 * n]
        send_sems, recv_sems = refs[2 * n:]
        x, y, c = _place()
        started = []
        for a in range(n):
            cp = _remote(outs[a].at[c], outs[a].at[c], send_sems, recv_sems, a, (x, y, 1 - c))
            cp.start()
            started.append(cp)
        for a in range(n):
            arrives = outs[a].at[1 - c]
            _remote(arrives, arrives, send_sems, recv_sems, a, (x, y, 1 - c)).wait_recv()
        for cp in started:
            cp.wait_send()

    return pl.pallas_call(
        body, name=name, in_specs=[_ANY] * n, out_specs=[_ANY] * n,
        out_shape=[jax.ShapeDtypeStruct(b.shape, b.dtype) for b in bufs],
        input_output_aliases={a: a for a in range(n)},
        scratch_shapes=[pltpu.SemaphoreType.DMA((n,)), pltpu.SemaphoreType.DMA((n,))],
        )(*bufs)


def _gather_all_devices(vec, name):
    def body(in_ref, out_ref, send_sems, recv_sems, local_sem):
        x, y, c = _place()
        me = 4 * x + 2 * y + c
        own = pltpu.make_async_copy(in_ref, out_ref.at[me], local_sem)
        own.start()
        sends = []
        for r in range(1, 8):
            fx, fy, fc = (r >> 2) & 1, (r >> 1) & 1, r & 1
            to = (x ^ fx, y ^ fy, c ^ fc)
            sends.append(pltpu.make_async_remote_copy(
                src_ref=in_ref, dst_ref=out_ref.at[me], send_sem=send_sems.at[r - 1], recv_sem=recv_sems.at[r - 1],
                device_id=to, device_id_type=MESH_ID))
            sends[-1].start()
        for r in range(1, 8):
            fx, fy, fc = (r >> 2) & 1, (r >> 1) & 1, r & 1
            slot = out_ref.at[4 * (x ^ fx) + 2 * (y ^ fy) + (c ^ fc)]
            pltpu.make_async_remote_copy(src_ref=slot, dst_ref=slot, send_sem=send_sems.at[r - 1],
                                         recv_sem=recv_sems.at[r - 1], device_id=(x, y, c),
                                         device_id_type=MESH_ID).wait_recv()
        for cp in sends:
            cp.wait_send()
        own.wait()

    return pl.pallas_call(
        body, name=name, in_specs=[_ANY], out_specs=_ANY, out_shape=jax.ShapeDtypeStruct((8,) + vec.shape, vec.dtype),
        scratch_shapes=[pltpu.SemaphoreType.DMA((7,)), pltpu.SemaphoreType.DMA((7,)), pltpu.SemaphoreType.DMA(())],
        )(vec)


def _permute_w_in(w):
    cut = Z_SU
    return jnp.concatenate([w[:, :cut], w[:, cut + 2 * DN_HEADS:], w[:, cut:cut + 2 * DN_HEADS],
                            jnp.zeros((w.shape[0], Z_WIDTH - IN_WIDTH), w.dtype)], axis=1)


def _unpermute_w_in(wp):
    return jnp.concatenate([wp[:, :Z_SU], wp[:, Z_DBA:Z_DBA + 2 * DN_HEADS], wp[:, Z_SU:Z_DBA]], axis=1)


def _s5_inputs(sp):
    ab_re, ab_im, bb_re, bb_im = _whole_fwd(
        _s5_param_fn, sp["s5_pins"], [(S5_GROUPS, S5_STATE)] * 2 + [(S5_WIDTH, S5_STATE)] * 2, name="s5_params")
    return (_block_diag(bb_re), _block_diag(bb_im), _block_diag(sp["c_re"]), _block_diag(sp["c_im"]),
            ab_re.reshape(1, S5_LANES), ab_im.reshape(1, S5_LANES), sp["dskip"])


def _mixer_fwd(x, sp, weights, rope, mm):
    D = x.shape[1]
    (h,) = _rows_fwd(_prenorm_fn, [(x, D, 0)], [sp["mix_norm_pre"]], [(D, D, BF16)], name="mix_prenorm")
    w_in = weights("win")
    z = mm("win", h, w_in, out_dtype=F32, name="mix_in")
    w_out, glu_w = weights("wout")
    y_attn = _attn_fwd(z, rope[0], rope[1], sp["sinks"], "mix")
    qkv = _conv_fwd(z, sp["conv_w"], "mix")
    y_dn, states = _dn_fwd(qkv, z, sp["dn_a_log"], sp["dn_dt_bias"], sp["dn_norm_w"], "mix")
    s5_in = _s5_inputs(sp)
    y_lin, *s5_states = _s5_fwd(z, *s5_in, "mix")
    (y_s5,) = _rows_fwd(_s5_glu_fn, [(y_lin, S5_WIDTH, 0)], [glu_w, sp["glu_b"]], [(S5_WIDTH, S5_WIDTH, BF16)],
                        name="mix_s5_glu")
    cat = jnp.concatenate([y_attn, y_dn, y_s5], axis=1)
    mixed = mm("wout", cat, w_out, out_dtype=F32, name="mix_out")
    (x_new,) = _rows_fwd(_residual_fn(1.0), [(x, D, 0), (mixed, D, 0)], [sp["mix_norm_post"]], [(D, D, F32)],
                         name="mix_residual")
    return x_new, (h, z, qkv, states, s5_in, y_lin, s5_states, cat, mixed, w_in, w_out, glu_w)


def _mixer_bwd(dx_new, x, sp, rope, saved, mm, sink):
    h, z, qkv, states, s5_in, y_lin, s5_states, cat, mixed, w_in, w_out, glu_w = saved
    D = x.shape[1]
    G, H, P = S5_GROUPS, S5_GROUP_CH, S5_STATE
    d_mixed, d_g_post = _rows_bwd(_postnorm_fn(1.0), [(mixed, D, 0)], [sp["mix_norm_post"]], [(dx_new, D, 0)],
                                  [(0, D, BF16)], name="mix_postnorm_bwd")
    d_cat = mm(d_mixed, w_out, tb=True, out_dtype=BF16, name="mix_out_dx")
    sink("wout", mm(cat, d_mixed, ta=True, out_dtype=BF16, name="mix_out_dw"))
    d_attn, d_dn, d_s5 = d_cat[:, :ATTN_WIDTH], d_cat[:, ATTN_WIDTH:ATTN_WIDTH + DN_WIDTH], d_cat[:, ATTN_WIDTH + DN_WIDTH:]
    d_ylin, d_glu_w, d_glu_b = _rows_bwd(_s5_glu_fn, [(y_lin, S5_WIDTH, 0)], [glu_w, sp["glu_b"]],
                                         [(d_s5, S5_WIDTH, 0)], [(0, S5_WIDTH, F32)], name="mix_s5_glu_bwd")
    sink("glu", d_glu_w.astype(BF16))
    d_us5, d_bre, d_bim, d_cre, d_cim, d_are, d_aim, d_dskip = _s5_bwd(z, *s5_in, *s5_states, d_ylin, "mix")
    cts = [d_are.reshape(G, P), d_aim.reshape(G, P), _block_diag_take(d_bre), _block_diag_take(d_bim)]
    d_a_re, d_a_im, d_ldt, d_bt_re, d_bt_im = _whole_bwd(_s5_param_fn, sp["s5_pins"], cts, 5, name="s5_params_bwd",
                                                         lane_sum=(2,))
    from_t = lambda m: m.reshape(G, H, P).transpose(0, 2, 1)
    d_qkv, d_zg, d_ba, d_alog, d_dtb, d_nw = _dn_bwd(qkv, z, sp["dn_a_log"], sp["dn_dt_bias"], sp["dn_norm_w"], states,
                                                     d_dn, "mix")
    d_uconv, d_conv_w = _conv_bwd(z, sp["conv_w"], d_qkv, "mix")
    d_q, d_kv, d_sinks = _attn_bwd(z, rope[0], rope[1], sp["sinks"], d_attn, "mix")
    d_z = jnp.concatenate([d_q, d_kv, d_uconv, d_zg, d_us5, d_ba], axis=1)
    sink("win", mm(h, d_z, ta=True, out_dtype=BF16, name="mix_in_dw"))
    d_h = mm(d_z, w_in, tb=True, out_dtype=BF16, name="mix_in_dx")
    dx, d_g_pre = _rows_bwd(_prenorm_fn, [(x, D, 0)], [sp["mix_norm_pre"]], [(d_h, D, 0)], [(0, D, F32)],
                            name="mix_prenorm_bwd", add_to_first=dx_new)
    small = {
        "mix_norm_pre": d_g_pre[0], "mix_norm_post": d_g_post[0], "attn_sinks": d_sinks[:, 0], "dn_conv_w": d_conv_w,
        "dn_a_log": d_alog[:, 0, 0], "dn_dt_bias": d_dtb[:, 0, 0], "dn_norm_w": d_nw[0, 0],
        "s5_a_re": d_a_re, "s5_a_im": d_a_im, "s5_log_dt": d_ldt[:, 0], "s5_b_re": from_t(d_bt_re),
        "s5_b_im": from_t(d_bt_im), "s5_c_re": _block_diag_take(d_cre).reshape(G, H, P),
        "s5_c_im": _block_diag_take(d_cim).reshape(G, H, P), "s5_d": d_dskip[0], "s5_glu_b": d_glu_b[0],
    }
    return dx, small


BIG_PIECES = ("gu1", "wd1", "win", "wout", "glu", "gu2", "wd2")


def _halves(m):
    return m.reshape(m.shape[:-2] + (2, m.shape[-2] // 2, m.shape[-1]))


def _whole(m):
    return m.reshape(m.shape[:-3] + (2 * m.shape[-2], m.shape[-1]))


class _WeightGather:
    def __init__(self, shards):
        self.shards = shards
        pieces = list(shards[0])
        flat = [s for p in pieces for s in shards[0][p]]
        got = iter(_gather_shards(flat, "gather_weights"))
        self.ready = {(0, p): [next(got) for _ in shards[0][p]] for p in pieces}
        self.landing = None

    def weights(self, l, piece):
        return self.ready[(l, piece)]

    def mm(self, l, piece, a, b, **kw):
        jobs, done = [], None
        if self.landing is not None:
            done = self.landing
            jobs += [{"kind": "gather_d2d", "ins": [buf, s]} for buf, s in zip(done[2], self.shards[done[0]][done[1]])]
        n_done = len(jobs)
        if l + 1 < len(self.shards):
            jobs += [{"kind": "gather_ici", "ins": [s]} for s in self.shards[l + 1][piece]]
        if not jobs:
            return _mm(a, b, **kw)
        out, bufs = _mm(a, b, jobs=jobs, **kw)
        if done is not None:
            self.ready[done[:2]] = bufs[:n_done]
        self.landing = (l + 1, piece, bufs[n_done:]) if l + 1 < len(self.shards) else None
        return out


class _GradReduce:
    def __init__(self):
        self.fresh, self.waiting, self.landed = [], [], {}

    def add(self, key, grad):
        self.fresh.append((key, _halves(grad)))

    def _summed(self, swapped):
        self.waiting += [(key, _add_sibling(part, got, "reduce_siblings_add"))
                         for (key, part), got in zip(self.fresh, swapped)]
        self.fresh = []

    def mm(self, a, b, **kw):
        if not (self.fresh or self.waiting):
            return _mm(a, b, **kw)
        jobs = ([{"kind": "swap", "ins": [part]} for _, part in self.fresh]
                + [{"kind": "scatter", "ins": [cp]} for _, cp in self.waiting])
        out, bufs = _mm(a, b, jobs=jobs, **kw)
        sent, self.waiting = self.waiting, []
        for (key, cp), o in zip(sent, bufs[len(self.fresh):]):
            self.landed[key] = (cp, o)
        self._summed(bufs[:len(self.fresh)])
        return out

    def finish(self):
        if self.fresh:
            self._summed(_swap_other_half([part for _, part in self.fresh], "reduce_siblings"))
        if self.waiting:
            others = _scatter_to_chips([cp for _, cp in self.waiting], "reduce_chips")
            for (key, cp), o in zip(self.waiting, others):
                self.landed[key] = (cp, o)
        keys = list(self.landed)
        joined = _join_halves([_sum_chips(*self.landed[k], "reduce_chips_sum") for k in keys], "reduce_join")
        return {k: _whole(m) for k, m in zip(keys, joined)}


def _pack_small(arrs, extra=()):
    flat = jnp.concatenate([a.reshape(-1) for a in arrs] + list(extra))
    n = flat.shape[0]
    quantum = 8 * SMALL_LANES
    padded = -(-n // quantum) * quantum
    return jnp.concatenate([flat, jnp.zeros((padded - n,), flat.dtype)]).reshape(padded // SMALL_LANES, SMALL_LANES)


def _unpack_small(flat2d, shapes):
    flat = flat2d.reshape(-1)
    out, o = [], 0
    for s in shapes:
        n = int(np.prod(s))
        out.append(flat[o:o + n].reshape(s))
        o += n
    return out, flat[o:]


def _step(a):
    x, target = a["x"][0], a["loss_target"][0]
    S, D = x.shape
    L, _, Fs = a["ff1_w_gate"].shape
    px, py, pc = _place()
    chip = 2 * px + py

    def rows_of_chips(g):
        g = _whole(g)
        return g.reshape(N_SHARD * g.shape[1], g.shape[2])

    shards = []
    for l in range(L):
        half = lambda m: _halves(m.astype(BF16))
        shards.append({
            "gu1": [half(jnp.concatenate([a["ff1_w_gate"][l], a["ff1_w_up"][l]], axis=1))], "wd1": [half(a["ff1_w_down"][l])],
            "win": [half(a["w_in"][l])], "wout": [half(a["w_out"][l]), half(a["s5_glu_w"][l])],
            "gu2": [half(jnp.concatenate([a["ff2_w_gate"][l], a["ff2_w_up"][l]], axis=1))], "wd2": [half(a["ff2_w_down"][l])]})
    gather = _WeightGather(shards)

    def ffn_weights(l, f):
        def get(piece):
            (g,) = gather.weights(l, piece + f)
            return _whole(g) if piece == "gu" else rows_of_chips(g)
        return get

    def mixer_weights(l):
        def get(piece):
            got = gather.weights(l, piece)
            if piece == "win":
                return _permute_w_in(_whole(got[0]).transpose(1, 0, 2).reshape(D, IN_WIDTH))
            return rows_of_chips(got[0]), rows_of_chips(got[1])
        return get

    conv_local = a["dn_conv_w"].reshape(-1)
    conv_rows = -(-conv_local.shape[0] // (16 * FLAT_LANES)) * 16
    conv_pad = jnp.concatenate([conv_local, jnp.zeros((conv_rows * FLAT_LANES - conv_local.shape[0],), F32)])
    (conv_all,) = _gather_shards([conv_pad.reshape(2, conv_rows // 2, FLAT_LANES)], "gather_conv")
    conv_all = conv_all.reshape(N_SHARD, -1)[:, :conv_local.shape[0]].reshape(N_SHARD, L, DN_CONV, -1)
    conv_full = conv_all.transpose(1, 2, 0, 3).reshape(L, DN_CONV, 3 * DN_WIDTH)

    expand = jnp.repeat(jnp.eye(S5_GROUPS, dtype=F32), S5_GROUP_CH, axis=0)
    lanes = lambda v, n=128: jnp.broadcast_to(v[:, None], (v.shape[0], n))
    to_t = lambda m: m.transpose(0, 2, 1).reshape(S5_WIDTH, S5_STATE)

    def small_params(l):
        sp = {k: a[k][l][None] for k in ("ff1_norm_pre", "ff1_norm_post", "mix_norm_pre", "mix_norm_post",
                                         "ff2_norm_pre", "ff2_norm_post")}
        sp["sinks"] = lanes(a["attn_sinks"][l])
        sp["conv_w"] = conv_full[l]
        sp["dn_a_log"] = lanes(a["dn_a_log"][l])[:, None, :]
        sp["dn_dt_bias"] = lanes(a["dn_dt_bias"][l])[:, None, :]
        sp["dn_norm_w"] = a["dn_norm_w"][l][None, None]
        sp["s5_pins"] = [a["s5_a_re"][l], a["s5_a_im"][l], lanes(a["s5_log_dt"][l], S5_STATE),
                         to_t(a["s5_b_re"][l]), to_t(a["s5_b_im"][l]), expand]
        sp["c_re"] = a["s5_c_re"][l].reshape(S5_WIDTH, S5_STATE)
        sp["c_im"] = a["s5_c_im"][l].reshape(S5_WIDTH, S5_STATE)
        sp["dskip"] = a["s5_d"][l][None]
        sp["glu_b"] = a["s5_glu_b"][l][None]
        return sp

    rope = _rope_tables(S)
    sps = [small_params(l) for l in range(L)]

    saved = []
    for l in range(L):
        sp = sps[l]
        mm_of = lambda f: (lambda piece, p, q, **kw: gather.mm(l, piece + f, p, q, **kw))
        x1, s1 = _ffn_fwd(x, sp["ff1_norm_pre"], sp["ff1_norm_post"], ffn_weights(l, "1"), "ff1", mm_of("1"))
        x2, s2 = _mixer_fwd(x1, sp, mixer_weights(l), rope, mm_of(""))
        x3, s3 = _ffn_fwd(x2, sp["ff2_norm_pre"], sp["ff2_norm_post"], ffn_weights(l, "2"), "ff2", mm_of("2"))
        saved.append((x, s1, x1, s2, x2, s3))
        x = x3
    dx, loss_part = _loss_head(x, target, "head")

    small_grads = [None] * L
    shard_major = lambda m: m.reshape(N_SHARD, m.shape[0] // N_SHARD, m.shape[1])
    red = _GradReduce()
    for l in reversed(range(L)):
        sp = sps[l]
        x0, s1, x1, s2, x2, s3 = saved[l]

        def ffn_sink(f):
            return lambda piece, g: red.add((l, piece + f), g if piece == "gu" else shard_major(g))

        def mixer_sink(piece, g):
            if piece == "win":
                g = _unpermute_w_in(g).reshape(D, N_SHARD, IN_WIDTH // N_SHARD).transpose(1, 0, 2)
            red.add((l, piece), g if piece == "win" else shard_major(g))

        dx, g_pre2, g_post2 = _ffn_bwd(dx, x2, sp["ff2_norm_pre"], sp["ff2_norm_post"], s3, "ff2", red.mm, ffn_sink("2"))
        dx, sg = _mixer_bwd(dx, x1, sp, rope, s2, red.mm, mixer_sink)
        dx, g_pre1, g_post1 = _ffn_bwd(dx, x0, sp["ff1_norm_pre"], sp["ff1_norm_post"], s1, "ff1", red.mm, ffn_sink("1"))
        sg.update({"ff1_norm_pre": g_pre1[0], "ff1_norm_post": g_post1[0], "ff2_norm_pre": g_pre2[0],
                   "ff2_norm_post": g_post2[0]})
        small_grads[l] = sg
    grad_x = dx[None]
    reduced = red.finish()

    grads = {}
    layers = lambda k: jnp.stack([reduced[(l, k)] for l in range(L)])
    for f, (gu, wd) in (("ff1", ("gu1", "wd1")), ("ff2", ("gu2", "wd2"))):
        gus = layers(gu)
        grads[f + "_w_gate"], grads[f + "_w_up"] = gus[:, :, :Fs], gus[:, :, Fs:]
        grads[f + "_w_down"] = layers(wd)
    grads["w_in"], grads["w_out"], grads["s5_glu_w"] = layers("win"), layers("wout"), layers("glu")

    small_local = [jnp.stack([small_grads[l][n] for l in range(L)]) for n in SMALL]
    vec = _pack_small(small_local, extra=(loss_part[0, :1],))
    total = _sum_slots(_gather_all_devices(vec, "gather_small"), "sum_small")
    small_total, rest = _unpack_small(total, [g.shape for g in small_local])
    loss = rest[0]
    for n, g in zip(SMALL, small_total):
        grads[n] = g
    cw = 3 * DN_WIDTH // N_SHARD
    grads["dn_conv_w"] = lax.dynamic_slice_in_dim(grads["dn_conv_w"], chip * cw, cw, axis=2)

    delta, new_m, new_v = {}, {}, {}
    for n in BIG:
        delta[n], new_m[n], new_v[n] = _adamw(a[n], grads[n], a["m_" + n], a["v_" + n], "adamw_" + n)
    shapes = [a[n].shape for n in SMALL]
    packed = [_pack_small([src[n] for n in SMALL]) for src in
              (a, grads, {n: a["m_" + n] for n in SMALL}, {n: a["v_" + n] for n in SMALL})]
    for dst, res in zip((delta, new_m, new_v), _adamw(*packed, "adamw_small")):
        for n, val in zip(SMALL, _unpack_small(res, shapes)[0]):
            dst[n] = val
    return (loss, grad_x, *[grads[n] for n in WEIGHTS], *[delta[n] for n in WEIGHTS], *[new_m[n] for n in WEIGHTS],
            *[new_v[n] for n in WEIGHTS])


def kernel(x, ff1_norm_pre, ff1_w_gate, ff1_w_up, ff1_w_down, ff1_norm_post, mix_norm_pre, w_in, attn_sinks, dn_conv_w, dn_a_log, dn_dt_bias, dn_norm_w, s5_a_re, s5_a_im, s5_log_dt, s5_b_re, s5_b_im, s5_c_re, s5_c_im, s5_d, s5_glu_w, s5_glu_b, w_out, mix_norm_post, ff2_norm_pre, ff2_w_gate, ff2_w_up, ff2_w_down, ff2_norm_post, loss_target, m_ff1_norm_pre, m_ff1_w_gate, m_ff1_w_up, m_ff1_w_down, m_ff1_norm_post, m_mix_norm_pre, m_w_in, m_attn_sinks, m_dn_conv_w, m_dn_a_log, m_dn_dt_bias, m_dn_norm_w, m_s5_a_re, m_s5_a_im, m_s5_log_dt, m_s5_b_re, m_s5_b_im, m_s5_c_re, m_s5_c_im, m_s5_d, m_s5_glu_w, m_s5_glu_b, m_w_out, m_mix_norm_post, m_ff2_norm_pre, m_ff2_w_gate, m_ff2_w_up, m_ff2_w_down, m_ff2_norm_post, v_ff1_norm_pre, v_ff1_w_gate, v_ff1_w_up, v_ff1_w_down, v_ff1_norm_post, v_mix_norm_pre, v_w_in, v_attn_sinks, v_dn_conv_w, v_dn_a_log, v_dn_dt_bias, v_dn_norm_w, v_s5_a_re, v_s5_a_im, v_s5_log_dt, v_s5_b_re, v_s5_b_im, v_s5_c_re, v_s5_c_im, v_s5_d, v_s5_glu_w, v_s5_glu_b, v_w_out, v_mix_norm_post, v_ff2_norm_pre, v_ff2_w_gate, v_ff2_w_up, v_ff2_w_down, v_ff2_norm_post):
    return _step(dict(locals()))
```

```python
import functools
import math

import numpy as np
import jax
import jax.numpy as jnp
from jax import lax
from jax.experimental import pallas as pl
from jax.experimental.pallas import tpu as pltpu

F32 = jnp.float32
BF16 = jnp.bfloat16
HI = lax.Precision.HIGHEST
MESH_ID = pl.DeviceIdType.MESH

NORM_EPS = 1e-6
FFN_RES_WEIGHT = 0.5
ATTN_HEADS, ATTN_KV_HEADS, HEAD_DIM, WINDOW = 8, 2, 128, 128
ROPE_THETA = 10000.0
DN_HEADS, DN_HEAD_DIM, DN_CONV, DN_CHUNK = 4, 128, 4, 64
S5_GROUPS, S5_GROUP_CH, S5_STATE = 32, 16, 64
ATTN_WIDTH = ATTN_HEADS * HEAD_DIM
ATTN_KV_WIDTH = ATTN_KV_HEADS * HEAD_DIM
DN_WIDTH = DN_HEADS * DN_HEAD_DIM
S5_WIDTH = S5_GROUPS * S5_GROUP_CH
S5_LANES = S5_GROUPS * S5_STATE
MIX_WIDTH = ATTN_WIDTH + DN_WIDTH + S5_WIDTH
IN_SPLITS = (ATTN_WIDTH, ATTN_KV_WIDTH, ATTN_KV_WIDTH, 3 * DN_WIDTH, DN_WIDTH, DN_HEADS, DN_HEADS, S5_WIDTH)
IN_WIDTH = sum(IN_SPLITS)
Z_AQ, Z_AK, Z_AV = 0, ATTN_WIDTH, ATTN_WIDTH + ATTN_KV_WIDTH
Z_DQKV = ATTN_WIDTH + 2 * ATTN_KV_WIDTH
Z_DZ = Z_DQKV + 3 * DN_WIDTH
Z_SU = Z_DZ + DN_WIDTH
Z_DBA = Z_SU + S5_WIDTH
Z_WIDTH = Z_DBA + 128

ADAM_LR, ADAM_B1, ADAM_B2, ADAM_EPS, ADAM_WD, ADAM_STEP = 0.001, 0.9, 0.999, 1e-08, 0.01, 10

N_SHARD = 4
FLAT_LANES = 512
SMALL_LANES = 1024
MXU_FLOPS_PER_US = 7.5e8
ICI_BYTES_PER_US = 6.8e4
VMEM_LIMIT = 56 * 1024 * 1024

WEIGHTS = ['ff1_norm_pre', 'ff1_w_gate', 'ff1_w_up', 'ff1_w_down', 'ff1_norm_post', 'mix_norm_pre', 'w_in',
           'attn_sinks', 'dn_conv_w', 'dn_a_log', 'dn_dt_bias', 'dn_norm_w', 's5_a_re', 's5_a_im', 's5_log_dt',
           's5_b_re', 's5_b_im', 's5_c_re', 's5_c_im', 's5_d', 's5_glu_w', 's5_glu_b', 'w_out', 'mix_norm_post',
           'ff2_norm_pre', 'ff2_w_gate', 'ff2_w_up', 'ff2_w_down', 'ff2_norm_post']
BIG = ['ff1_w_gate', 'ff1_w_up', 'ff1_w_down', 'w_in', 's5_glu_w', 'w_out', 'ff2_w_gate', 'ff2_w_up', 'ff2_w_down']
SMALL = [n for n in WEIGHTS if n not in BIG]


def _pick(dim, cands):
    for c in cands:
        if dim % c == 0:
            return c
    return dim


def _params(sem=None):
    return pltpu.CompilerParams(dimension_semantics=sem, vmem_limit_bytes=VMEM_LIMIT)


def _silu_mul(gate, up):
    return gate * jax.nn.sigmoid(gate) * up


def _mm(a, b, *, ta=False, tb=False, out_dtype=F32, name, b_groups=None, bk_groups=None, out_groups=None, jobs=(),
        a_swiglu=None, swiglu_bwd_of=None):
    K, M = a.shape if ta else a.shape[::-1]
    if a_swiglu:
        K, M = (K, M // 2) if ta else (K // 2, M)
    Ng = Kg = None
    if b_groups:
        assert not tb
        _, Kb, Ng = b.shape
        N = b_groups * Ng
    elif bk_groups:
        assert tb
        _, N, Kg = b.shape
        Kb = bk_groups * Kg
    else:
        N, Kb = (b.shape if tb else b.shape[::-1])
    assert K == Kb, (a.shape, b.shape, ta, tb)
    tiles = (1408, 1024, 512, 384, 256, 128)
    tm = _pick(M, (1024, 512, 256, 128))
    tk = _pick(Kg if bk_groups else K, (2048,) + tiles)
    tn = _pick(N // out_groups if out_groups else (Ng if b_groups else N), tiles)
    if a_swiglu and ta:
        tm, tk = _pick(a_swiglu, tiles), _pick(K, tiles[1:])
    elif a_swiglu:
        tm, tk, tn = _pick(M, (512, 256, 128)), _pick(a_swiglu, tiles), _pick(N, (2048,) + tiles)
    if swiglu_bwd_of is not None:
        tm, tn = _pick(M, (512, 256, 128)), swiglu_bwd_of.shape[1] // (2 * N_SHARD)
        assert tk == K
    nk = K // tk
    dims = (((0,) if ta else (1,), (1,) if tb else (0,)), ((), ()))

    grid = (M // tm, N // tn, nk)
    n_job_in = sum(len(j["ins"]) for j in jobs)
    n_sems = sum(_JOB_SEMS[j["kind"]] for j in jobs)
    n_a = 2 if a_swiglu else 1
    n_lead = n_a + 1 + (1 if swiglu_bwd_of is not None else 0)

    def body(*refs):
        b_ref = refs[n_a]
        if a_swiglu:
            a_val = _silu_mul(refs[0][...].astype(F32), refs[1][...].astype(F32)).astype(BF16)
        else:
            a_val = refs[0][...].astype(BF16)
        job_ins = refs[n_lead:n_lead + n_job_in]
        o_ref = refs[n_lead + n_job_in]
        job_outs = refs[n_lead + 1 + n_job_in:n_lead + 1 + n_job_in + len(jobs)]
        scratch = refs[n_lead + 1 + n_job_in + len(jobs):]
        if jobs:
            send_sems, recv_sems = scratch[-2:]
            scratch = scratch[:-2]
            ids = [pl.program_id(d) for d in range(3)]
            first = functools.reduce(jnp.logical_and, [i == 0 for i in ids])
            last = functools.reduce(jnp.logical_and, [i == g - 1 for i, g in zip(ids, grid)])

            def copies():
                sends, recvs, at, sem = [], [], 0, 0
                for j, out in zip(jobs, job_outs):
                    s, r = _job_copies(j["kind"], job_ins[at:at + len(j["ins"])], out, send_sems, recv_sems, sem)
                    sends, recvs = sends + s, recvs + r
                    at, sem = at + len(j["ins"]), sem + _JOB_SEMS[j["kind"]]
                return sends, recvs

            @pl.when(first)
            def _():
                for cp in copies()[0]:
                    cp.start()

        part = lax.dot_general(a_val, b_ref[...].astype(BF16), dims, preferred_element_type=F32)
        if swiglu_bwd_of is not None:
            gate, up = refs[n_a + 1][:, :tn].astype(F32), refs[n_a + 1][:, tn:].astype(F32)
            sg = jax.nn.sigmoid(gate)
            o_ref[:, :tn] = (part * up * sg * (1.0 + gate * (1.0 - sg))).astype(o_ref.dtype)
            o_ref[:, tn:] = (part * gate * sg).astype(o_ref.dtype)
        elif nk == 1:
            o_ref[...] = part.astype(o_ref.dtype)
        else:
            acc_ref, = scratch
            k = pl.program_id(2)

            @pl.when(k == 0)
            def _():
                acc_ref[...] = part

            @pl.when(k > 0)
            def _():
                acc_ref[...] += part

            @pl.when(k == nk - 1)
            def _():
                o_ref[...] = acc_ref[...].astype(o_ref.dtype)

        if jobs:
            @pl.when(last)
            def _():
                sends, recvs = copies()
                for cp in recvs:
                    cp.wait_recv()
                for cp in sends:
                    cp.wait_send()

    if a_swiglu and ta:
        u = a_swiglu // tm
        a_specs = [pl.BlockSpec((tk, tm), lambda i, j, k, o=o: (k, (i // u) * 2 * u + o + i % u)) for o in (0, u)]
    elif a_swiglu:
        u = a_swiglu // tk
        a_specs = [pl.BlockSpec((tm, tk), lambda i, j, k, o=o: (i, (k // u) * 2 * u + o + k % u)) for o in (0, u)]
    else:
        a_specs = [pl.BlockSpec((tk, tm), lambda i, j, k: (k, i)) if ta else pl.BlockSpec((tm, tk), lambda i, j, k: (i, k))]
    if b_groups:
        per = Ng // tn
        b_spec = pl.BlockSpec((None, tk, tn), lambda i, j, k: (j // per, k, j % per))
    elif bk_groups:
        per = Kg // tk
        b_spec = pl.BlockSpec((None, tn, tk), lambda i, j, k: (k // per, j, k % per))
    else:
        b_spec = pl.BlockSpec((tn, tk), lambda i, j, k: (j, k)) if tb else pl.BlockSpec((tk, tn), lambda i, j, k: (k, j))
    if out_groups:
        pero = (N // out_groups) // tn
        o_spec = pl.BlockSpec((None, tm, tn), lambda i, j, k: (j // pero, i, j % pero))
        o_shape = jax.ShapeDtypeStruct((out_groups, M, N // out_groups), out_dtype)
    else:
        o_spec = pl.BlockSpec((tm, tn), lambda i, j, k: (i, j))
        o_shape = jax.ShapeDtypeStruct((M, N), out_dtype)
    lead_specs, lead_args = a_specs + [b_spec], [a] * n_a + [b]
    if swiglu_bwd_of is not None:
        o_spec = pl.BlockSpec((tm, 2 * tn), lambda i, j, k: (i, j))
        o_shape = jax.ShapeDtypeStruct((M, 2 * N), out_dtype)
        lead_specs, lead_args = lead_specs + [o_spec], lead_args + [swiglu_bwd_of]
    scratch = [pltpu.VMEM((tm, tn), F32)] if nk > 1 else []
    if not jobs:
        return pl.pallas_call(
            body, name=name, grid=grid, in_specs=lead_specs, out_specs=o_spec, out_shape=o_shape,
            scratch_shapes=scratch, compiler_params=_params(("parallel", "parallel", "arbitrary")))(*lead_args)
    job_args = [x for j in jobs for x in j["ins"]]
    aliases, at = {}, n_lead
    for n, j in enumerate(jobs):
        if j["kind"] == "gather_d2d":
            aliases[at] = 1 + n
        at += len(j["ins"])
    res = pl.pallas_call(
        body, name=name, grid=grid, in_specs=lead_specs + [_ANY] * n_job_in,
        out_specs=[o_spec] + [_ANY] * len(jobs), out_shape=[o_shape] + [_job_out_shape(j) for j in jobs],
        input_output_aliases=aliases,
        scratch_shapes=scratch + [pltpu.SemaphoreType.DMA((n_sems,)), pltpu.SemaphoreType.DMA((n_sems,))],
        compiler_params=_params(("arbitrary", "arbitrary", "arbitrary")))(*lead_args, *job_args)
    return res[0], list(res[1:])


def _row_spec(tm, width, off):
    return pl.BlockSpec((tm, width), lambda i, j: (i, off + j))


def _const_spec(shape):
    return pl.BlockSpec(shape, lambda i, j: (0,) * len(shape))


def _rows_fwd(fn, rows, consts, outs, *, name, tm=256, ncol=1):
    S = rows[0][0].shape[0]
    tm = _pick(S, (tm, 128, 64))
    nr, nc = len(rows), len(consts)

    def body(*refs):
        vals = [r[...].astype(F32) for r in refs[:nr + nc]]
        res = fn(*vals)
        for o_ref, o in zip(refs[nr + nc:], res):
            o_ref[...] = o.astype(o_ref.dtype)

    return pl.pallas_call(
        body, name=name, grid=(S // tm, ncol),
        in_specs=[_row_spec(tm, w, off) for _, w, off in rows] + [_const_spec(c.shape) for c in consts],
        out_specs=[_row_spec(tm, bw, 0) for _, bw, _ in outs],
        out_shape=[jax.ShapeDtypeStruct((S, tw), dt) for tw, _, dt in outs],
        compiler_params=_params(("parallel", "parallel")))(*[r[0] for r in rows], *consts)


def _rows_bwd(fn, rows, consts, cts, row_grads, *, name, tm=256, ncol=1, add_to_first=None, lane_sum_consts=()):
    S = rows[0][0].shape[0]
    tm = _pick(S, (tm, 128, 64))
    nr, nc, nt = len(rows), len(consts), len(cts)
    n_in = nr + nc + nt + (1 if add_to_first is not None else 0)

    def body(*refs):
        vals = [r[...].astype(F32) for r in refs[:nr + nc]]
        ct = tuple(r[...].astype(F32) for r in refs[nr + nc:nr + nc + nt])
        _, vjp = jax.vjp(fn, *vals)
        grads = vjp(ct)
        outs = refs[n_in:]
        for n, (idx, _, _) in enumerate(row_grads):
            g = grads[idx]
            if n == 0 and add_to_first is not None:
                g = g + refs[n_in - 1][...].astype(F32)
            outs[n][...] = g.astype(outs[n].dtype)
        first = jnp.logical_and(pl.program_id(0) == 0, pl.program_id(1) == 0)
        for c in range(nc):
            o_ref = outs[len(row_grads) + c]
            g = grads[nr + c]
            if c in lane_sum_consts:
                g = jnp.broadcast_to(jnp.sum(g, axis=-1, keepdims=True), g.shape)

            @pl.when(first)
            def _():
                o_ref[...] = jnp.zeros_like(o_ref)

            o_ref[...] += g

    in_specs = ([_row_spec(tm, w, off) for _, w, off in rows] + [_const_spec(c.shape) for c in consts]
                + [_row_spec(tm, w, off) for _, w, off in cts])
    args = [r[0] for r in rows] + list(consts) + [c[0] for c in cts]
    if add_to_first is not None:
        in_specs.append(_row_spec(tm, rows[row_grads[0][0]][1], 0))
        args.append(add_to_first)
    out_specs = [_row_spec(tm, rows[idx][1], 0) for idx, _, _ in row_grads] + [_const_spec(c.shape) for c in consts]
    out_shape = ([jax.ShapeDtypeStruct((S, tw), dt) for _, tw, dt in row_grads]
                 + [jax.ShapeDtypeStruct(c.shape, F32) for c in consts])
    return pl.pallas_call(
        body, name=name, grid=(S // tm, ncol), in_specs=in_specs, out_specs=out_specs, out_shape=out_shape,
        compiler_params=_params(("arbitrary", "arbitrary")))(*args)


def _rms(x, gain):
    return x * lax.rsqrt(jnp.mean(x * x, axis=-1, keepdims=True) + NORM_EPS) * gain


def _prenorm_fn(x, gain):
    return (_rms(x, gain),)


def _postnorm_fn(weight):
    def fn(y, gain):
        return (weight * _rms(y, gain),)
    return fn


def _residual_fn(weight):
    def fn(x, y, gain):
        return (x + weight * _rms(y, gain),)
    return fn


def _swiglu_fn(blk):
    tf = blk.shape[1] // 2
    gate, up = blk[:, :tf], blk[:, tf:]
    return (gate * jax.nn.sigmoid(gate) * up,)


def _ffn_fwd(x, g_pre, g_post, weights, tag, mm):
    D = x.shape[1]
    (h,) = _rows_fwd(_prenorm_fn, [(x, D, 0)], [g_pre], [(D, D, BF16)], name=f"{tag}_prenorm")
    wgu = weights("gu")
    Fs = wgu.shape[2] // 2
    gu = mm("gu", h, wgu, b_groups=N_SHARD, out_dtype=BF16, name=f"{tag}_gate_up")
    wd = weights("wd")
    y = mm("wd", gu, wd, a_swiglu=Fs, out_dtype=F32, name=f"{tag}_down")
    (x_new,) = _rows_fwd(_residual_fn(FFN_RES_WEIGHT), [(x, D, 0), (y, D, 0)], [g_post], [(D, D, F32)],
                         name=f"{tag}_residual")
    return x_new, (h, gu, y, wgu, wd)


def _ffn_bwd(dx_new, x, g_pre, g_post, saved, tag, mm, sink):
    h, gu, y, wgu, wd = saved
    D = x.shape[1]
    Fs = wgu.shape[2] // 2
    d_y, d_g_post = _rows_bwd(_postnorm_fn(FFN_RES_WEIGHT), [(y, D, 0)], [g_post], [(dx_new, D, 0)],
                              [(0, D, BF16)], name=f"{tag}_postnorm_bwd")
    d_gu = mm(d_y, wd, tb=True, swiglu_bwd_of=gu, out_dtype=BF16, name=f"{tag}_down_dx")
    sink("wd", mm(gu, d_y, ta=True, a_swiglu=Fs, out_dtype=BF16, name=f"{tag}_down_dw"))
    sink("gu", mm(h, d_gu, ta=True, out_dtype=BF16, out_groups=N_SHARD, name=f"{tag}_gate_up_dw"))
    d_h = mm(d_gu, wgu, tb=True, bk_groups=N_SHARD, out_dtype=BF16, name=f"{tag}_gate_up_dx")
    dx, d_g_pre = _rows_bwd(_prenorm_fn, [(x, D, 0)], [g_pre], [(d_h, D, 0)], [(0, D, F32)],
                            name=f"{tag}_prenorm_bwd", add_to_first=dx_new)
    return dx, d_g_pre, d_g_post


@jax.custom_vjp
def _swap_halves(x):
    return pltpu.roll(x, HEAD_DIM // 2, 1)


def _swap_fwd(x):
    return _swap_halves(x), None


def _swap_bwd(_, g):
    return (_swap_halves(g),)


_swap_halves.defvjp(_swap_fwd, _swap_bwd)


def _rope(x, cc, ss):
    return x * cc + _swap_halves(x) * ss


def _attn_block_fn(has_prev):
    grp = ATTN_HEADS // ATTN_KV_HEADS
    scale = HEAD_DIM ** -0.5
    nt = (((1,), (1,)), ((), ()))
    nn = (((1,), (0,)), ((), ()))

    def fn(*a):
        q = a[:8]
        kp, kc, vp, vc = a[8:10], a[10:12], a[12:14], a[14:16]
        cc, ss, ccp, ssp, sinks = a[16:21]
        row = lax.broadcasted_iota(jnp.int32, (WINDOW, WINDOW), 0)
        col = lax.broadcasted_iota(jnp.int32, (WINDOW, WINDOW), 1)
        m_cur = col <= row
        m_prev = jnp.logical_and(col > row, has_prev)
        outs = []
        for h in range(ATTN_HEADS):
            g = h // grp
            qr = _rope(q[h], cc, ss).astype(BF16)
            kcr = _rope(kc[g], cc, ss).astype(BF16)
            kpr = _rope(kp[g], ccp, ssp).astype(BF16)
            s_c = jnp.where(m_cur, lax.dot_general(qr, kcr, nt, preferred_element_type=F32) * scale, -jnp.inf)
            s_p = jnp.where(m_prev, lax.dot_general(qr, kpr, nt, preferred_element_type=F32) * scale, -jnp.inf)
            sink = sinks[h:h + 1, :]
            m = jnp.maximum(jnp.maximum(jnp.max(s_c, axis=-1, keepdims=True), jnp.max(s_p, axis=-1, keepdims=True)),
                            sink)
            p_c = jnp.exp(s_c - m)
            p_p = jnp.exp(s_p - m)
            den = (jnp.sum(p_c, axis=-1, keepdims=True) + jnp.sum(p_p, axis=-1, keepdims=True) + jnp.exp(sink - m))
            inv = 1.0 / den
            o = (lax.dot_general((p_c * inv).astype(BF16), vc[g].astype(BF16), nn, preferred_element_type=F32)
                 + lax.dot_general((p_p * inv).astype(BF16), vp[g].astype(BF16), nn, preferred_element_type=F32))
            outs.append(o)
        return tuple(outs)

    return fn


def _attn_specs(order):
    kvb = ATTN_WIDTH // (2 * ATTN_KV_WIDTH)
    return [
        pl.BlockSpec((WINDOW, ATTN_WIDTH), lambda i: (order(i), 0)),
        pl.BlockSpec((WINDOW, 2 * ATTN_KV_WIDTH), lambda i: (jnp.maximum(order(i) - 1, 0), kvb)),
        pl.BlockSpec((WINDOW, 2 * ATTN_KV_WIDTH), lambda i: (order(i), kvb)),
        pl.BlockSpec((WINDOW, HEAD_DIM), lambda i: (order(i), 0)),
        pl.BlockSpec((WINDOW, HEAD_DIM), lambda i: (order(i), 0)),
        pl.BlockSpec((WINDOW, HEAD_DIM), lambda i: (jnp.maximum(order(i) - 1, 0), 0)),
        pl.BlockSpec((WINDOW, HEAD_DIM), lambda i: (jnp.maximum(order(i) - 1, 0), 0)),
        pl.BlockSpec((ATTN_HEADS, HEAD_DIM), lambda i: (0, 0)),
    ]


def _attn_args(q_ref, kvp_ref, kvc_ref, cc, ss, ccp, ssp, sinks):
    d = HEAD_DIM
    q = [q_ref[:, h * d:(h + 1) * d].astype(F32) for h in range(ATTN_HEADS)]
    kp = [kvp_ref[:, g * d:(g + 1) * d].astype(F32) for g in range(ATTN_KV_HEADS)]
    vp = [kvp_ref[:, ATTN_KV_WIDTH + g * d:ATTN_KV_WIDTH + (g + 1) * d].astype(F32) for g in range(ATTN_KV_HEADS)]
    kc = [kvc_ref[:, g * d:(g + 1) * d].astype(F32) for g in range(ATTN_KV_HEADS)]
    vc = [kvc_ref[:, ATTN_KV_WIDTH + g * d:ATTN_KV_WIDTH + (g + 1) * d].astype(F32) for g in range(ATTN_KV_HEADS)]
    return q + kp + kc + vp + vc + [cc[...], ss[...], ccp[...], ssp[...], sinks[...]]


def _attn_fwd(z, cc, ss, sinks_b, tag):
    S = z.shape[0]
    nb = S // WINDOW

    def body(q_ref, kvp_ref, kvc_ref, cc_r, ss_r, ccp_r, ssp_r, sink_r, o_ref):
        n = pl.program_id(0)
        outs = _attn_block_fn(n > 0)(*_attn_args(q_ref, kvp_ref, kvc_ref, cc_r, ss_r, ccp_r, ssp_r, sink_r))
        for h in range(ATTN_HEADS):
            o_ref[:, h * HEAD_DIM:(h + 1) * HEAD_DIM] = outs[h].astype(o_ref.dtype)

    return pl.pallas_call(
        body, name=f"{tag}_attn", grid=(nb,), in_specs=_attn_specs(lambda i: i),
        out_specs=pl.BlockSpec((WINDOW, ATTN_WIDTH), lambda i: (i, 0)),
        out_shape=jax.ShapeDtypeStruct((S, ATTN_WIDTH), BF16),
        compiler_params=_params(("parallel",)))(z, z, z, cc, ss, cc, ss, sinks_b)


def _attn_bwd(z, cc, ss, sinks_b, d_out, tag):
    S = z.shape[0]
    nb = S // WINDOW
    d = HEAD_DIM
    rev = lambda i: nb - 1 - i

    def body(q_ref, kvp_ref, kvc_ref, cc_r, ss_r, ccp_r, ssp_r, sink_r, do_ref, dq_ref, dkv_ref, dsink_ref, carry):
        i = pl.program_id(0)
        n = nb - 1 - i

        @pl.when(i == 0)
        def _():
            carry[...] = jnp.zeros_like(carry)
            dsink_ref[...] = jnp.zeros_like(dsink_ref)

        args = _attn_args(q_ref, kvp_ref, kvc_ref, cc_r, ss_r, ccp_r, ssp_r, sink_r)
        _, vjp = jax.vjp(_attn_block_fn(n > 0), *args)
        g = vjp(tuple(do_ref[:, h * d:(h + 1) * d].astype(F32) for h in range(ATTN_HEADS)))
        for h in range(ATTN_HEADS):
            dq_ref[:, h * d:(h + 1) * d] = g[h].astype(dq_ref.dtype)
        for gi in range(ATTN_KV_HEADS):
            ks = slice(gi * d, (gi + 1) * d)
            vs = slice(ATTN_KV_WIDTH + gi * d, ATTN_KV_WIDTH + (gi + 1) * d)
            dkv_ref[:, ks] = (g[10 + gi] + carry[:, ks]).astype(dkv_ref.dtype)
            dkv_ref[:, vs] = (g[14 + gi] + carry[:, vs]).astype(dkv_ref.dtype)
            carry[:, ks] = g[8 + gi]
            carry[:, vs] = g[12 + gi]
        ds = g[20]
        dsink_ref[...] += jnp.broadcast_to(jnp.sum(ds, axis=-1, keepdims=True), ds.shape)

    return pl.pallas_call(
        body, name=f"{tag}_attn_bwd", grid=(nb,),
        in_specs=_attn_specs(rev) + [pl.BlockSpec((WINDOW, ATTN_WIDTH), lambda i: (rev(i), 0))],
        out_specs=[pl.BlockSpec((WINDOW, ATTN_WIDTH), lambda i: (rev(i), 0)),
                   pl.BlockSpec((WINDOW, 2 * ATTN_KV_WIDTH), lambda i: (rev(i), 0)),
                   pl.BlockSpec((ATTN_HEADS, HEAD_DIM), lambda i: (0, 0))],
        out_shape=[jax.ShapeDtypeStruct((S, ATTN_WIDTH), BF16), jax.ShapeDtypeStruct((S, 2 * ATTN_KV_WIDTH), BF16),
                   jax.ShapeDtypeStruct((ATTN_HEADS, HEAD_DIM), F32)],
        scratch_shapes=[pltpu.VMEM((WINDOW, 2 * ATTN_KV_WIDTH), F32)],
        compiler_params=_params(("arbitrary",)))(z, z, z, cc, ss, cc, ss, sinks_b, d_out)


def _rope_tables(seq):
    half = HEAD_DIM // 2
    inv_freq = ROPE_THETA ** (-jnp.arange(half, dtype=F32) / half)
    ang = jnp.arange(seq, dtype=F32)[:, None] * inv_freq[None, :]
    cos, sin = jnp.cos(ang), jnp.sin(ang)
    return jnp.concatenate([cos, cos], axis=1), jnp.concatenate([-sin, sin], axis=1)


CONV_COLS = 128


def _conv_pre(u, w_ref, S):
    row = lax.broadcasted_iota(jnp.int32, u.shape, 0)
    shifted = [u] + [jnp.where(row >= s, pltpu.roll(u, s, 0), 0.0) for s in range(1, DN_CONV)]
    y = shifted[0] * w_ref[DN_CONV - 1:DN_CONV, :]
    for s in range(1, DN_CONV):
        y = y + shifted[s] * w_ref[DN_CONV - 1 - s:DN_CONV - s, :]
    return y, shifted, row


def _conv_fwd(z, conv_w, tag):
    S = z.shape[0]
    ncol = 3 * DN_WIDTH // CONV_COLS

    def body(u_ref, w_ref, o_ref):
        y, _, _ = _conv_pre(u_ref[...], w_ref, S)
        o_ref[...] = y * jax.nn.sigmoid(y)

    return pl.pallas_call(
        body, name=f"{tag}_conv", grid=(ncol,),
        in_specs=[pl.BlockSpec((S, CONV_COLS), lambda j: (0, Z_DQKV // CONV_COLS + j)),
                  pl.BlockSpec((DN_CONV, CONV_COLS), lambda j: (0, j))],
        out_specs=pl.BlockSpec((S, CONV_COLS), lambda j: (0, j)),
        out_shape=jax.ShapeDtypeStruct((S, 3 * DN_WIDTH), F32),
        compiler_params=_params(("parallel",)))(z, conv_w)


def _conv_bwd(z, conv_w, d_out, tag):
    S = z.shape[0]
    ncol = 3 * DN_WIDTH // CONV_COLS

    def body(u_ref, w_ref, do_ref, du_ref, dw_ref):
        y, shifted, row = _conv_pre(u_ref[...], w_ref, S)
        sg = jax.nn.sigmoid(y)
        d_y = do_ref[...] * (sg * (1.0 + y * (1.0 - sg)))
        d_u = d_y * w_ref[DN_CONV - 1:DN_CONV, :]
        dw_ref[DN_CONV - 1:DN_CONV, :] = jnp.sum(d_y * shifted[0], axis=0, keepdims=True)
        for s in range(1, DN_CONV):
            back = jnp.where(row < S - s, pltpu.roll(d_y, S - s, 0), 0.0)
            d_u = d_u + back * w_ref[DN_CONV - 1 - s:DN_CONV - s, :]
            dw_ref[DN_CONV - 1 - s:DN_CONV - s, :] = jnp.sum(d_y * shifted[s], axis=0, keepdims=True)
        du_ref[...] = d_u.astype(du_ref.dtype)

    return pl.pallas_call(
        body, name=f"{tag}_conv_bwd", grid=(ncol,),
        in_specs=[pl.BlockSpec((S, CONV_COLS), lambda j: (0, Z_DQKV // CONV_COLS + j)),
                  pl.BlockSpec((DN_CONV, CONV_COLS), lambda j: (0, j)),
                  pl.BlockSpec((S, CONV_COLS), lambda j: (0, j))],
        out_specs=[pl.BlockSpec((S, CONV_COLS), lambda j: (0, j)), pl.BlockSpec((DN_CONV, CONV_COLS), lambda j: (0, j))],
        out_shape=[jax.ShapeDtypeStruct((S, 3 * DN_WIDTH), BF16), jax.ShapeDtypeStruct((DN_CONV, 3 * DN_WIDTH), F32)],
        compiler_params=_params(("parallel",)))(z, conv_w, d_out)


_NN = (((1,), (0,)), ((), ()))
_NT = (((1,), (1,)), ((), ()))
_TN = (((0,), (0,)), ((), ()))


def _dot3(a, b, dims):
    a_hi, b_hi = a.astype(BF16), b.astype(BF16)
    a_lo, b_lo = (a - a_hi.astype(F32)).astype(BF16), (b - b_hi.astype(F32)).astype(BF16)
    mm = lambda p, q: lax.dot_general(p, q, dims, preferred_element_type=F32)
    return mm(a_hi, b_hi) + (mm(a_hi, b_lo) + mm(a_lo, b_hi))


@functools.partial(jax.custom_vjp, nondiff_argnums=(2,))
def _dot_vjp(a, b, dims):
    return _dot3(a, b, dims)


def _dot_vjp_fwd(a, b, dims):
    return _dot3(a, b, dims), (a, b)


_BATCH = ((0,), (0,))
_BNN, _BNT, _BTN = (((2,), (1,)), _BATCH), (((2,), (2,)), _BATCH), (((1,), (1,)), _BATCH)


def _dot_vjp_bwd(dims, res, g):
    a, b = res
    nn, nt, tn = (_NN, _NT, _TN) if dims in (_NN, _NT, _TN) else (_BNN, _BNT, _BTN)
    if dims == nn:
        return _dot3(g, b, nt), _dot3(a, g, tn)
    if dims == nt:
        return _dot3(g, b, nn), _dot3(g, a, tn)
    return _dot3(b, g, nt), _dot3(a, g, nn)


_dot_vjp.defvjp(_dot_vjp_fwd, _dot_vjp_bwd)


def _dot(a, b, dims=_NN):
    return _dot_vjp(a, b, dims)


def _dot1(a, b, dims=_NN):
    return lax.dot_general(a.astype(BF16), b.astype(BF16), dims, preferred_element_type=F32)


def _nilpotent_inverse(m):
    H, C, _ = m.shape
    eye = (lax.broadcasted_iota(jnp.int32, (H, C, C), 1) == lax.broadcasted_iota(jnp.int32, (H, C, C), 2)).astype(F32)
    inv = eye + m
    for _ in range(5):
        m = _dot3(m, m, _BNN)
        inv = inv + _dot3(m, inv, _BNN)
    return inv


@jax.custom_vjp
def _unit_lower_solve(m, ru, rw):
    inv = _nilpotent_inverse(m)
    return _dot3(inv, ru, _BNN), _dot3(inv, rw, _BNN)


def _unit_lower_solve_fwd(m, ru, rw):
    inv = _nilpotent_inverse(m)
    xu, xw = _dot3(inv, ru, _BNN), _dot3(inv, rw, _BNN)
    return (xu, xw), (inv, xu, xw)


def _unit_lower_solve_bwd(res, g):
    inv, xu, xw = res
    dru, drw = _dot3(inv, g[0], _BTN), _dot3(inv, g[1], _BTN)
    return _dot3(dru, xu, _BNT) + _dot3(drw, xw, _BNT), dru, drw


_unit_lower_solve.defvjp(_unit_lower_solve_fwd, _unit_lower_solve_bwd)


def _dn_chunk_fn(state, q, k, v, zg, ba, a_log, dtb, norm_w):
    H, C, dk = DN_HEADS, DN_CHUNK, DN_HEAD_DIM
    head = lax.broadcasted_iota(jnp.int32, (H, C, dk), 0)
    rowl = lax.broadcasted_iota(jnp.int32, (H, C, dk), 1)
    lane = lax.broadcasted_iota(jnp.int32, (H, C, dk), 2)
    row = lax.broadcasted_iota(jnp.int32, (H, C, C), 1)
    col = lax.broadcasted_iota(jnp.int32, (H, C, C), 2)
    ba3 = jnp.broadcast_to(ba[None], (H, C, dk))
    bcol = jnp.sum(jnp.where(lane == head, ba3, 0.0), axis=-1, keepdims=True)
    acol = jnp.sum(jnp.where(lane == H + head, ba3, 0.0), axis=-1, keepdims=True)
    qn = q * lax.rsqrt(jnp.sum(q * q, axis=-1, keepdims=True) + NORM_EPS) * (dk ** -0.5)
    kn = k * lax.rsqrt(jnp.sum(k * k, axis=-1, keepdims=True) + NORM_EPS)
    beta = jax.nn.sigmoid(bcol)
    sp_in = acol + dtb
    softplus = jnp.maximum(sp_in, 0.0) + jnp.log(1.0 + jnp.exp(-jnp.abs(sp_in)))
    gt = -jnp.exp(a_log) * softplus
    gc = _dot((row >= col).astype(F32), gt, _BNN)
    gcol = jnp.mean(gc, axis=-1, keepdims=True)
    grow = _dot(jnp.full((H, C, dk), 1.0 / dk, F32), gc, _BNT)
    decay = jnp.exp(jnp.where(row >= col, gcol - grow, -jnp.inf))
    kb = kn * beta
    m = -jnp.where(row > col, _dot(kb, kn, _BNT) * decay, 0.0)
    u, w = _unit_lower_solve(m, v * beta, kb * jnp.exp(gc))
    attn = jnp.where(row >= col, _dot(qn, kn, _BNT) * decay, 0.0)
    q_dec = qn * jnp.exp(gc)
    gl = jnp.sum(jnp.where(rowl == C - 1, gc, 0.0), axis=1, keepdims=True)
    k_dec = kn * jnp.exp(gl - gc)
    v_new = u - _dot(w, state, _BNN)
    o = _dot(q_dec, state, _BNN) + _dot(attn, v_new, _BNN)
    state_new = state * jnp.exp(gl) + _dot(k_dec, v_new, _BTN)
    y = o * lax.rsqrt(jnp.mean(o * o, axis=-1, keepdims=True) + NORM_EPS) * norm_w
    y = y * (zg * jax.nn.sigmoid(zg))
    return state_new, y


def _dn_specs(order):
    C = DN_CHUNK
    return [pl.BlockSpec((C, 3 * DN_WIDTH), lambda i: (order(i), 0)),
            pl.BlockSpec((C, DN_WIDTH), lambda i: (order(i), Z_DZ // DN_WIDTH)),
            pl.BlockSpec((C, 128), lambda i: (order(i), Z_DBA // 128)),
            pl.BlockSpec((DN_HEADS, 1, DN_HEAD_DIM), lambda i: (0, 0, 0)),
            pl.BlockSpec((DN_HEADS, 1, DN_HEAD_DIM), lambda i: (0, 0, 0)),
            pl.BlockSpec((1, 1, DN_HEAD_DIM), lambda i: (0, 0, 0))]


def _dn_heads(ref, base=0):
    d = DN_HEAD_DIM
    return jnp.stack([ref[:, base + h * d:base + (h + 1) * d].astype(F32) for h in range(DN_HEADS)], axis=0)


def _dn_args(qkv_ref, zg_ref, ba_ref, alog_ref, dtb_ref, nw_ref):
    return [_dn_heads(qkv_ref), _dn_heads(qkv_ref, DN_WIDTH), _dn_heads(qkv_ref, 2 * DN_WIDTH), _dn_heads(zg_ref),
            ba_ref[...].astype(F32), alog_ref[...], dtb_ref[...], nw_ref[...]]


def _dn_fwd(qkv, z, a_log, dtb, norm_w, tag):
    S = qkv.shape[0]
    nchunk = S // DN_CHUNK
    d = DN_HEAD_DIM

    def body(qkv_ref, zg_ref, ba_ref, alog_ref, dtb_ref, nw_ref, y_ref, st_ref, state):
        @pl.when(pl.program_id(0) == 0)
        def _():
            state[...] = jnp.zeros_like(state)

        st_ref[...] = state[...]
        new, y = _dn_chunk_fn(state[...], *_dn_args(qkv_ref, zg_ref, ba_ref, alog_ref, dtb_ref, nw_ref))
        state[...] = new
        for h in range(DN_HEADS):
            y_ref[:, h * d:(h + 1) * d] = y[h].astype(y_ref.dtype)

    return pl.pallas_call(
        body, name=f"{tag}_deltanet", grid=(nchunk,), in_specs=_dn_specs(lambda i: i),
        out_specs=[pl.BlockSpec((DN_CHUNK, DN_WIDTH), lambda i: (i, 0)),
                   pl.BlockSpec((None, DN_HEADS, d, d), lambda i: (i, 0, 0, 0))],
        out_shape=[jax.ShapeDtypeStruct((S, DN_WIDTH), BF16), jax.ShapeDtypeStruct((nchunk, DN_HEADS, d, d), F32)],
        scratch_shapes=[pltpu.VMEM((DN_HEADS, d, d), F32)],
        compiler_params=_params(("arbitrary",)))(qkv, z, z, a_log, dtb, norm_w)


def _dn_bwd(qkv, z, a_log, dtb, norm_w, states, d_y, tag):
    S = qkv.shape[0]
    nchunk = S // DN_CHUNK
    d = DN_HEAD_DIM
    rev = lambda i: nchunk - 1 - i

    def body(qkv_ref, zg_ref, ba_ref, alog_ref, dtb_ref, nw_ref, st_ref, dy_ref,
             dqkv_ref, dzg_ref, dba_ref, dalog_ref, ddtb_ref, dnw_ref, d_state):
        @pl.when(pl.program_id(0) == 0)
        def _():
            d_state[...] = jnp.zeros_like(d_state)
            dalog_ref[...] = jnp.zeros_like(dalog_ref)
            ddtb_ref[...] = jnp.zeros_like(ddtb_ref)
            dnw_ref[...] = jnp.zeros_like(dnw_ref)

        args = [st_ref[...]] + _dn_args(qkv_ref, zg_ref, ba_ref, alog_ref, dtb_ref, nw_ref)
        _, vjp = jax.vjp(_dn_chunk_fn, *args)
        g = vjp((d_state[...], _dn_heads(dy_ref)))
        d_state[...] = g[0]
        for h in range(DN_HEADS):
            for n, base in enumerate((0, DN_WIDTH, 2 * DN_WIDTH)):
                dqkv_ref[:, base + h * d:base + (h + 1) * d] = g[1 + n][h]
            dzg_ref[:, h * d:(h + 1) * d] = g[4][h].astype(dzg_ref.dtype)
        dba_ref[...] = g[5].astype(dba_ref.dtype)
        lane_sum = lambda t: jnp.broadcast_to(jnp.sum(t, axis=-1, keepdims=True), t.shape)
        dalog_ref[...] += lane_sum(g[6])
        ddtb_ref[...] += lane_sum(g[7])
        dnw_ref[...] += g[8]

    hspec = pl.BlockSpec((DN_HEADS, 1, d), lambda i: (0, 0, 0))
    return pl.pallas_call(
        body, name=f"{tag}_deltanet_bwd", grid=(nchunk,),
        in_specs=_dn_specs(rev) + [pl.BlockSpec((None, DN_HEADS, d, d), lambda i: (rev(i), 0, 0, 0)),
                                   pl.BlockSpec((DN_CHUNK, DN_WIDTH), lambda i: (rev(i), 0))],
        out_specs=[pl.BlockSpec((DN_CHUNK, 3 * DN_WIDTH), lambda i: (rev(i), 0)),
                   pl.BlockSpec((DN_CHUNK, DN_WIDTH), lambda i: (rev(i), 0)),
                   pl.BlockSpec((DN_CHUNK, 128), lambda i: (rev(i), 0)),
                   hspec, hspec, pl.BlockSpec((1, 1, d), lambda i: (0, 0, 0))],
        out_shape=[jax.ShapeDtypeStruct((S, 3 * DN_WIDTH), F32), jax.ShapeDtypeStruct((S, DN_WIDTH), BF16),
                   jax.ShapeDtypeStruct((S, 128), BF16), jax.ShapeDtypeStruct((DN_HEADS, 1, d), F32),
                   jax.ShapeDtypeStruct((DN_HEADS, 1, d), F32), jax.ShapeDtypeStruct((1, 1, d), F32)],
        scratch_shapes=[pltpu.VMEM((DN_HEADS, d, d), F32)],
        compiler_params=_params(("arbitrary",)))(qkv, z, z, a_log, dtb, norm_w, states, d_y)


def _whole_fwd(fn, ins, outs, *, name):
    n = len(ins)

    def body(*refs):
        res = fn(*[r[...] for r in refs[:n]])
        for o_ref, o in zip(refs[n:], res):
            o_ref[...] = o

    return pl.pallas_call(body, name=name, out_shape=[jax.ShapeDtypeStruct(s, F32) for s in outs],
                          compiler_params=_params())(*ins)


def _whole_bwd(fn, ins, cts, n_grads, *, name, lane_sum=()):
    n, nt = len(ins), len(cts)

    def body(*refs):
        _, vjp = jax.vjp(fn, *[r[...] for r in refs[:n]])
        grads = vjp(tuple(r[...] for r in refs[n:n + nt]))
        for k in range(n_grads):
            g = grads[k]
            if k in lane_sum:
                g = jnp.broadcast_to(jnp.sum(g, axis=-1, keepdims=True), g.shape)
            refs[n + nt + k][...] = g

    return pl.pallas_call(body, name=name, out_shape=[jax.ShapeDtypeStruct(a.shape, F32) for a in ins[:n_grads]],
                          compiler_params=_params())(*ins, *cts)


S5_CHUNK = 256


def _s5_param_fn(a_re, a_im, ldt, bt_re, bt_im, expand):
    dt = jnp.exp(ldt)
    er = jnp.exp(a_re * dt)
    ab_re, ab_im = er * jnp.cos(a_im * dt), er * jnp.sin(a_im * dt)
    den = a_re * a_re + a_im * a_im
    co_re = ((ab_re - 1.0) * a_re + ab_im * a_im) / den
    co_im = (ab_im * a_re - (ab_re - 1.0) * a_im) / den
    cr, ci = _dot(expand, co_re), _dot(expand, co_im)
    return ab_re, ab_im, cr * bt_re - ci * bt_im, cr * bt_im + ci * bt_re


def _s5_scan(b_re, b_im, a_re, a_im, row, T, reverse):
    x_re, x_im, p_re, p_im = b_re, b_im, a_re, a_im
    d = 1
    while d < T:
        if reverse:
            s_re = jnp.where(row < T - d, pltpu.roll(x_re, T - d, 0), 0.0)
            s_im = jnp.where(row < T - d, pltpu.roll(x_im, T - d, 0), 0.0)
        else:
            s_re = jnp.where(row >= d, pltpu.roll(x_re, d, 0), 0.0)
            s_im = jnp.where(row >= d, pltpu.roll(x_im, d, 0), 0.0)
        x_re, x_im = x_re + p_re * s_re - p_im * s_im, x_im + p_re * s_im + p_im * s_re
        p_re, p_im = p_re * p_re - p_im * p_im, 2.0 * p_re * p_im
        d *= 2
    return x_re, x_im


def _s5_states(u, bre_ref, bim_ref, a_re, a_im, c_re, c_im, row, T):
    bu_re = _dot(u, bre_ref[...]) + jnp.where(row == 0, a_re * c_re - a_im * c_im, 0.0)
    bu_im = _dot(u, bim_ref[...]) + jnp.where(row == 0, a_re * c_im + a_im * c_re, 0.0)
    return _s5_scan(bu_re, bu_im, a_re, a_im, row, T, False)


def _s5_in_specs(order, T):
    full = lambda shape: pl.BlockSpec(shape, lambda i: (0,) * len(shape), pipeline_mode=pl.Buffered(1))
    return [pl.BlockSpec((T, S5_WIDTH), lambda i: (order(i), Z_SU // S5_WIDTH)),
            full((S5_WIDTH, S5_LANES)), full((S5_WIDTH, S5_LANES)), full((S5_WIDTH, S5_LANES)),
            full((S5_WIDTH, S5_LANES)), full((1, S5_LANES)), full((1, S5_LANES)), full((1, S5_WIDTH))]


def _s5_fwd(z, bre, bim, cre, cim, ab_re, ab_im, dskip, tag):
    S = z.shape[0]
    T = _pick(S, (S5_CHUNK, 128))
    nch = S // T

    def body(u_ref, bre_ref, bim_ref, cre_ref, cim_ref, are_ref, aim_ref, d_ref, y_ref, xre_ref, xim_ref, c_re, c_im):
        @pl.when(pl.program_id(0) == 0)
        def _():
            c_re[...] = jnp.zeros_like(c_re)
            c_im[...] = jnp.zeros_like(c_im)

        u = u_ref[...]
        row = lax.broadcasted_iota(jnp.int32, (T, S5_LANES), 0)
        x_re, x_im = _s5_states(u, bre_ref, bim_ref, are_ref[...], aim_ref[...], c_re[...], c_im[...], row, T)
        c_re[...] = jnp.sum(jnp.where(row == T - 1, x_re, 0.0), axis=0, keepdims=True)
        c_im[...] = jnp.sum(jnp.where(row == T - 1, x_im, 0.0), axis=0, keepdims=True)
        xre_ref[...] = x_re
        xim_ref[...] = x_im
        y_ref[...] = _dot(x_re, cre_ref[...], _NT) - _dot(x_im, cim_ref[...], _NT) + d_ref[...] * u

    return pl.pallas_call(
        body, name=f"{tag}_s5", grid=(nch,), in_specs=_s5_in_specs(lambda i: i, T),
        out_specs=[pl.BlockSpec((T, S5_WIDTH), lambda i: (i, 0)),
                   pl.BlockSpec((T, S5_LANES), lambda i: (i, 0)), pl.BlockSpec((T, S5_LANES), lambda i: (i, 0))],
        out_shape=[jax.ShapeDtypeStruct((S, S5_WIDTH), F32), jax.ShapeDtypeStruct((S, S5_LANES), F32),
                   jax.ShapeDtypeStruct((S, S5_LANES), F32)],
        scratch_shapes=[pltpu.VMEM((1, S5_LANES), F32), pltpu.VMEM((1, S5_LANES), F32)],
        compiler_params=_params(("arbitrary",)))(z, bre, bim, cre, cim, ab_re, ab_im, dskip)


S5_BWD_CHUNK = 128


def _s5_bwd(z, bre, bim, cre, cim, ab_re, ab_im, dskip, xre, xim, d_y, tag):
    S = z.shape[0]
    T = _pick(S, (S5_BWD_CHUNK, 64))
    nch = S // T
    rev = lambda i: nch - 1 - i
    above = lambda i: (jnp.maximum(rev(i) * (T // 8) - 1, 0), 0)
    full = lambda shape: pl.BlockSpec(shape, lambda i: (0,) * len(shape))

    def body(u_ref, bre_ref, bim_ref, cre_ref, cim_ref, are_ref, aim_ref, d_ref, xre_ref, xim_ref, pre_ref, pim_ref,
             dy_ref, du_ref, dbre_ref, dbim_ref, dcre_ref, dcim_ref, dare_ref, daim_ref, dd_ref, g_re, g_im):
        @pl.when(pl.program_id(0) == 0)
        def _():
            g_re[...] = jnp.zeros_like(g_re)
            g_im[...] = jnp.zeros_like(g_im)
            for r in (dbre_ref, dbim_ref, dcre_ref, dcim_ref, dare_ref, daim_ref, dd_ref):
                r[...] = jnp.zeros_like(r)

        u = u_ref[...]
        dy = dy_ref[...].astype(F32)
        a_re, a_im = are_ref[...], aim_ref[...]
        row = lax.broadcasted_iota(jnp.int32, (T, S5_LANES), 0)
        x_re, x_im = xre_ref[...], xim_ref[...]
        dcre_ref[...] += _dot1(dy, x_re, _TN)
        dcim_ref[...] -= _dot1(dy, x_im, _TN)
        has_before = (pl.program_id(0) < nch - 1).astype(F32)
        row8 = lax.broadcasted_iota(jnp.int32, (8, S5_LANES), 0)
        before = lambda ref: has_before * jnp.sum(jnp.where(row8 == 7, ref[...], 0.0), axis=0, keepdims=True)
        xp_re = jnp.where(row >= 1, pltpu.roll(x_re, 1, 0), 0.0) + jnp.where(row == 0, before(pre_ref), 0.0)
        xp_im = jnp.where(row >= 1, pltpu.roll(x_im, 1, 0), 0.0) + jnp.where(row == 0, before(pim_ref), 0.0)
        last = row == T - 1
        gd_re = _dot(dy, cre_ref[...]) + jnp.where(last, a_re * g_re[...] + a_im * g_im[...], 0.0)
        gd_im = -_dot(dy, cim_ref[...]) + jnp.where(last, a_re * g_im[...] - a_im * g_re[...], 0.0)
        t_re, t_im = _s5_scan(gd_re, gd_im, a_re, -a_im, row, T, True)
        g_re[...] = jnp.sum(jnp.where(row == 0, t_re, 0.0), axis=0, keepdims=True)
        g_im[...] = jnp.sum(jnp.where(row == 0, t_im, 0.0), axis=0, keepdims=True)
        du_ref[...] = (_dot1(t_re, bre_ref[...], _NT) + _dot1(t_im, bim_ref[...], _NT) + dy * d_ref[...]).astype(du_ref.dtype)
        dbre_ref[...] += _dot1(u, t_re, _TN)
        dbim_ref[...] += _dot1(u, t_im, _TN)
        dare_ref[...] += jnp.sum(t_re * xp_re + t_im * xp_im, axis=0, keepdims=True)
        daim_ref[...] += jnp.sum(t_im * xp_re - t_re * xp_im, axis=0, keepdims=True)
        dd_ref[...] += jnp.sum(dy * u, axis=0, keepdims=True)

    return pl.pallas_call(
        body, name=f"{tag}_s5_bwd", grid=(nch,),
        in_specs=_s5_in_specs(rev, T) + [pl.BlockSpec((T, S5_LANES), lambda i: (rev(i), 0)),
                                         pl.BlockSpec((T, S5_LANES), lambda i: (rev(i), 0)),
                                         pl.BlockSpec((8, S5_LANES), above), pl.BlockSpec((8, S5_LANES), above),
                                         pl.BlockSpec((T, S5_WIDTH), lambda i: (rev(i), 0))],
        out_specs=[pl.BlockSpec((T, S5_WIDTH), lambda i: (rev(i), 0))] + [full((S5_WIDTH, S5_LANES))] * 4
        + [full((1, S5_LANES))] * 2 + [full((1, S5_WIDTH))],
        out_shape=[jax.ShapeDtypeStruct((S, S5_WIDTH), BF16)] + [jax.ShapeDtypeStruct((S5_WIDTH, S5_LANES), F32)] * 4
        + [jax.ShapeDtypeStruct((1, S5_LANES), F32)] * 2 + [jax.ShapeDtypeStruct((1, S5_WIDTH), F32)],
        scratch_shapes=[pltpu.VMEM((1, S5_LANES), F32), pltpu.VMEM((1, S5_LANES), F32)],
        compiler_params=_params(("arbitrary",)))(z, bre, bim, cre, cim, ab_re, ab_im, dskip, xre, xim, xre, xim, d_y)


def _s5_glu_fn(y, glu_w, glu_b):
    g = 0.5 * y * (1.0 + jnp.tanh(math.sqrt(2.0 / math.pi) * (y + 0.044715 * (y * y * y))))
    lin = lax.dot_general(g.astype(BF16), glu_w.astype(BF16), (((1,), (0,)), ((), ())), preferred_element_type=F32)
    return (g * jax.nn.sigmoid(lin + glu_b),)


def _block_diag(m):
    G, H, P = S5_GROUPS, S5_GROUP_CH, S5_STATE
    eye = jnp.eye(G, dtype=m.dtype)
    return (m.reshape(G, H, 1, P) * eye[:, None, :, None]).reshape(G * H, G * P)


def _block_diag_take(m):
    G, H, P = S5_GROUPS, S5_GROUP_CH, S5_STATE
    eye = jnp.eye(G, dtype=m.dtype)
    return jnp.sum(m.reshape(G, H, G, P) * eye[:, None, :, None], axis=2).reshape(G * H, P)


def _loss_head(y, target, tag):
    S, D = y.shape
    tm = _pick(S, (256, 128, 64))

    def body(y_ref, t_ref, dy_ref, loss_ref):
        @pl.when(pl.program_id(0) == 0)
        def _():
            loss_ref[...] = jnp.zeros_like(loss_ref)

        err = y_ref[...] - t_ref[...]
        dy_ref[...] = err * (1.0 / D)
        part = 0.5 * jnp.sum(jnp.mean(err * err, axis=-1, keepdims=True), axis=0, keepdims=True)
        loss_ref[...] += jnp.broadcast_to(part, loss_ref.shape)

    return pl.pallas_call(
        body, name=f"{tag}_loss", grid=(S // tm,),
        in_specs=[pl.BlockSpec((tm, D), lambda i: (i, 0)), pl.BlockSpec((tm, D), lambda i: (i, 0))],
        out_specs=[pl.BlockSpec((tm, D), lambda i: (i, 0)), pl.BlockSpec((8, 128), lambda i: (0, 0))],
        out_shape=[jax.ShapeDtypeStruct((S, D), F32), jax.ShapeDtypeStruct((8, 128), F32)],
        compiler_params=_params(("arbitrary",)))(y, target)


def _adamw(w, g, m, v, name):
    shape = w.shape
    cols = shape[-1]
    rows = int(np.prod(shape[:-1]))
    tr = _pick(rows, [t for t in (512, 256, 128, 64, 32, 16, 8) if t * cols <= 256 * 1024] or [8])
    c1 = 1.0 - ADAM_B1 ** ADAM_STEP
    c2 = 1.0 - ADAM_B2 ** ADAM_STEP

    def body(w_ref, g_ref, m_ref, v_ref, d_ref, mo_ref, vo_ref):
        gg = g_ref[...]
        mn = ADAM_B1 * m_ref[...] + (1.0 - ADAM_B1) * gg
        vn = ADAM_B2 * v_ref[...] + (1.0 - ADAM_B2) * (gg * gg)
        d_ref[...] = -ADAM_LR * ((mn / c1) / (jnp.sqrt(vn / c2) + ADAM_EPS) + ADAM_WD * w_ref[...])
        mo_ref[...] = mn
        vo_ref[...] = vn

    spec = pl.BlockSpec((tr, cols), lambda i: (i, 0))
    outs = pl.pallas_call(
        body, name=name, grid=(rows // tr,), in_specs=[spec] * 4, out_specs=[spec] * 3,
        out_shape=[jax.ShapeDtypeStruct((rows, cols), F32)] * 3,
        compiler_params=_params(("parallel",)))(*[a.reshape(rows, cols) for a in (w, g, m, v)])
    return [o.reshape(shape) for o in outs]


def _sum_slots(slots, name):
    n, R, C = slots.shape
    tr = _pick(R, [t for t in (1024, 512, 256, 128, 64, 32, 16, 8) if n * t * C * 4 <= (2 << 20)] or [8])

    def body(s_ref, o_ref):
        acc = s_ref[0].astype(F32)
        for k in range(1, n):
            acc = acc + s_ref[k].astype(F32)
        o_ref[...] = acc

    return pl.pallas_call(
        body, name=name, grid=(R // tr,), in_specs=[pl.BlockSpec((n, tr, C), lambda i: (0, i, 0))],
        out_specs=pl.BlockSpec((tr, C), lambda i: (i, 0)),
        out_shape=jax.ShapeDtypeStruct((R, C), F32), compiler_params=_params(("parallel",)))(slots)


def _row_tile(R, C, itemsize, target=1 << 20):
    return _pick(R, [t for t in (2048, 1024, 512, 256, 128, 64, 32, 16) if t * C * itemsize <= target] or [16])


def _add_sibling(part, got, name):
    n, _, R2, C = part.shape
    tr = _row_tile(R2, C, 2)

    def body(p_ref, q_ref, o_ref):
        o_ref[...] = (p_ref[...].astype(F32) + q_ref[...].astype(F32)).astype(o_ref.dtype)

    spec = pl.BlockSpec((None, tr, C), lambda k, i: (k, i, 0))
    return pl.pallas_call(
        body, name=name, grid=(n, R2 // tr),
        in_specs=[pl.BlockSpec((None, None, tr, C), lambda k, i: (k, lax.axis_index("c"), i, 0)), spec], out_specs=spec,
        out_shape=jax.ShapeDtypeStruct((n, R2, C), BF16), compiler_params=_params(("parallel", "parallel")))(part, got)


def _sum_chips(chip_part, others, name):
    _, R2, C = chip_part.shape
    tr = _row_tile(R2, C, 4)

    def body(a_ref, b_ref, o_ref):
        acc = a_ref[...].astype(F32)
        for j in range(N_SHARD - 1):
            acc = acc + b_ref[j].astype(F32)
        o_ref[...] = acc

    return pl.pallas_call(
        body, name=name, grid=(R2 // tr,),
        in_specs=[pl.BlockSpec((None, tr, C), lambda i: (2 * lax.axis_index("x") + lax.axis_index("y"), i, 0)),
                  pl.BlockSpec((N_SHARD - 1, tr, C), lambda i: (0, i, 0))],
        out_specs=pl.BlockSpec((None, tr, C), lambda i: (lax.axis_index("c"), i, 0)),
        out_shape=jax.ShapeDtypeStruct((2, R2, C), F32), compiler_params=_params(("parallel",)))(chip_part, others)


_ANY = pl.BlockSpec(memory_space=pl.ANY)


def _place():
    return lax.axis_index("x"), lax.axis_index("y"), lax.axis_index("c")


def _other_chips(x, y):
    return [(1 - x, y), (x, 1 - y), (1 - x, 1 - y)]


def _remote(src, dst, send_sems, recv_sems, n, to):
    return pltpu.make_async_remote_copy(src_ref=src, dst_ref=dst, send_sem=send_sems.at[n], recv_sem=recv_sems.at[n],
                                        device_id=to, device_id_type=MESH_ID)


_JOB_SEMS = {"gather_ici": 3, "gather_d2d": 4, "scatter": 3, "swap": 4}


def _job_out_shape(job):
    src = job["ins"][0]
    if job["kind"] == "gather_ici":
        return jax.ShapeDtypeStruct((N_SHARD,) + src.shape, src.dtype)
    if job["kind"] == "gather_d2d":
        return jax.ShapeDtypeStruct(src.shape, src.dtype)
    if job["kind"] == "swap":
        return jax.ShapeDtypeStruct((N_SHARD,) + src.shape[2:], src.dtype)
    return jax.ShapeDtypeStruct((N_SHARD - 1,) + src.shape[1:], src.dtype)


def _job_copies(kind, ins, out, send_sems, recv_sems, base):
    x, y, c = _place()
    k = 2 * x + y
    sibling = (x, y, 1 - c)
    sends, recvs = [], []
    if kind == "swap":
        for s in range(N_SHARD):
            sends.append(_remote(ins[0].at[s, 1 - c], out.at[s], send_sems, recv_sems, base + s, sibling))
            recvs.append(_remote(out.at[s], out.at[s], send_sems, recv_sems, base + s, sibling))
        return sends, recvs
    for j, (cx, cy) in enumerate(_other_chips(x, y)):
        kj = 2 * cx + cy
        if kind == "gather_ici":
            sends.append(_remote(ins[0].at[c], out.at[k, c], send_sems, recv_sems, base + j, (cx, cy, c)))
            recvs.append(_remote(out.at[kj, c], out.at[kj, c], send_sems, recv_sems, base + j, sibling))
        elif kind == "gather_d2d":
            sends.append(_remote(out.at[kj, c], out.at[kj, c], send_sems, recv_sems, base + 1 + j, sibling))
            recvs.append(_remote(out.at[kj, 1 - c], out.at[kj, 1 - c], send_sems, recv_sems, base + 1 + j, sibling))
        else:
            sends.append(_remote(ins[0].at[kj], out.at[j], send_sems, recv_sems, base + j, (cx, cy, c)))
            recvs.append(_remote(out.at[j], out.at[j], send_sems, recv_sems, base + j, sibling))
    if kind == "gather_d2d":
        sends.append(_remote(ins[1], out.at[k], send_sems, recv_sems, base, sibling))
        recvs.append(_remote(out.at[k], out.at[k], send_sems, recv_sems, base, sibling))
    return sends, recvs


def _gather_shards(shards, name):
    n = len(shards)
    per = 7

    def body(*refs):
        ins, outs = refs[:n], refs[n:2 * n]
        send_sems, recv_sems = refs[2 * n:]
        x, y, c = _place()
        k = 2 * x + y
        sibling = (x, y, 1 - c)
        chips = _other_chips(x, y)
        started = []
        for a in range(n):
            cp = _remote(ins[a], outs[a].at[k], send_sems, recv_sems, per * a, sibling)
            cp.start()
            started.append(cp)
            for j, (cx, cy) in enumerate(chips):
                cp = _remote(ins[a].at[c], outs[a].at[k, c], send_sems, recv_sems, per * a + 1 + j, (cx, cy, c))
                cp.start()
                started.append(cp)
        for a in range(n):
            for j, (cx, cy) in enumerate(chips):
                landed = outs[a].at[2 * cx + cy, c]
                _remote(landed, landed, send_sems, recv_sems, per * a + 1 + j, sibling).wait_recv()
                cp = _remote(landed, landed, send_sems, recv_sems, per * a + 4 + j, sibling)
                cp.start()
                started.append(cp)
        for a in range(n):
            own = outs[a].at[k]
            _remote(own, own, send_sems, recv_sems, per * a, sibling).wait_recv()
            for j, (cx, cy) in enumerate(chips):
                passed = outs[a].at[2 * cx + cy, 1 - c]
                _remote(passed, passed, send_sems, recv_sems, per * a + 4 + j, sibling).wait_recv()
        for cp in started:
            cp.wait_send()

    return pl.pallas_call(
        body, name=name, in_specs=[_ANY] * n, out_specs=[_ANY] * n,
        out_shape=[jax.ShapeDtypeStruct((N_SHARD,) + s.shape, s.dtype) for s in shards],
        scratch_shapes=[pltpu.SemaphoreType.DMA((per * n,)), pltpu.SemaphoreType.DMA((per * n,))],
        )(*shards)


def _swap_other_half(parts, name):
    n = len(parts)

    def body(*refs):
        ins, outs = refs[:n], refs[n:2 * n]
        send_sems, recv_sems = refs[2 * n:]
        x, y, c = _place()
        started = []
        for a in range(n):
            for k in range(N_SHARD):
                cp = _remote(ins[a].at[k, 1 - c], outs[a].at[k], send_sems, recv_sems, N_SHARD * a + k, (x, y, 1 - c))
                cp.start()
                started.append(cp)
        for cp in started:
            cp.wait()

    return pl.pallas_call(
        body, name=name, in_specs=[_ANY] * n, out_specs=[_ANY] * n,
        out_shape=[jax.ShapeDtypeStruct((N_SHARD,) + p.shape[2:], p.dtype) for p in parts],
        scratch_shapes=[pltpu.SemaphoreType.DMA((N_SHARD * n,)), pltpu.SemaphoreType.DMA((N_SHARD * n,))],
        )(*parts)


def _scatter_to_chips(parts, name):
    n = len(parts)
    per = N_SHARD - 1

    def body(*refs):
        ins, outs = refs[:n], refs[n:2 * n]
        send_sems, recv_sems = refs[2 * n:]
        x, y, c = _place()
        chips = _other_chips(x, y)
        started = []
        for a in range(n):
            for j, (cx, cy) in enumerate(chips):
                cp = _remote(ins[a].at[2 * cx + cy], outs[a].at[j], send_sems, recv_sems, per * a + j, (cx, cy, c))
                cp.start()
                started.append(cp)
        for cp in started:
            cp.wait()

    return pl.pallas_call(
        body, name=name, in_specs=[_ANY] * n, out_specs=[_ANY] * n,
        out_shape=[jax.ShapeDtypeStruct((per,) + p.shape[1:], p.dtype) for p in parts],
        scratch_shapes=[pltpu.SemaphoreType.DMA((per * n,)), pltpu.SemaphoreType.DMA((per * n,))],
        )(*parts)


def _join_halves(bufs, name):
    n = len(bufs)

    def body(*refs):
        outs = refs[n:2 * n]
        send_sems, recv_sems = refs[2 * n:]
        x, y, c = _place()
        started = []
        for a in range(n):
            cp = _remote(outs[a].at[c], outs[a].at[c], send_sems, recv_sems, a, (x, y, 1 - c))
            cp.start()
            started.append(cp)
        for a in range(n):
            arrives = outs[a].at[1 - c]
            _remote(arrives, arrives, send_sems, recv_sems, a, (x, y, 1 - c)).wait_recv()
        for cp in started:
            cp.wait_send()

    return pl.pallas_call(
        body, name=name, in_specs=[_ANY] * n, out_specs=[_ANY] * n,
        out_shape=[jax.ShapeDtypeStruct(b.shape, b.dtype) for b in bufs],
        input_output_aliases={a: a for a in range(n)},
        scratch_shapes=[pltpu.SemaphoreType.DMA((n,)), pltpu.SemaphoreType.DMA((n,))],
        )(*bufs)


def _gather_all_devices(vec, name):
    def body(in_ref, out_ref, send_sems, recv_sems, local_sem):
        x, y, c = _place()
        me = 4 * x + 2 * y + c
        own = pltpu.make_async_copy(in_ref, out_ref.at[me], local_sem)
        own.start()
        sends = []
        for r in range(1, 8):
            fx, fy, fc = (r >> 2) & 1, (r >> 1) & 1, r & 1
            to = (x ^ fx, y ^ fy, c ^ fc)
            sends.append(pltpu.make_async_remote_copy(
                src_ref=in_ref, dst_ref=out_ref.at[me], send_sem=send_sems.at[r - 1], recv_sem=recv_sems.at[r - 1],
                device_id=to, device_id_type=MESH_ID))
            sends[-1].start()
        for r in range(1, 8):
            fx, fy, fc = (r >> 2) & 1, (r >> 1) & 1, r & 1
            slot = out_ref.at[4 * (x ^ fx) + 2 * (y ^ fy) + (c ^ fc)]
            pltpu.make_async_remote_copy(src_ref=slot, dst_ref=slot, send_sem=send_sems.at[r - 1],
                                         recv_sem=recv_sems.at[r - 1], device_id=(x, y, c),
                                         device_id_type=MESH_ID).wait_recv()
        for cp in sends:
            cp.wait_send()
        own.wait()

    return pl.pallas_call(
        body, name=name, in_specs=[_ANY], out_specs=_ANY, out_shape=jax.ShapeDtypeStruct((8,) + vec.shape, vec.dtype),
        scratch_shapes=[pltpu.SemaphoreType.DMA((7,)), pltpu.SemaphoreType.DMA((7,)), pltpu.SemaphoreType.DMA(())],
        )(vec)


def _permute_w_in(w):
    cut = Z_SU
    return jnp.concatenate([w[:, :cut], w[:, cut + 2 * DN_HEADS:], w[:, cut:cut + 2 * DN_HEADS],
                            jnp.zeros((w.shape[0], Z_WIDTH - IN_WIDTH), w.dtype)], axis=1)


def _unpermute_w_in(wp):
    return jnp.concatenate([wp[:, :Z_SU], wp[:, Z_DBA:Z_DBA + 2 * DN_HEADS], wp[:, Z_SU:Z_DBA]], axis=1)


def _s5_inputs(sp):
    ab_re, ab_im, bb_re, bb_im = _whole_fwd(
        _s5_param_fn, sp["s5_pins"], [(S5_GROUPS, S5_STATE)] * 2 + [(S5_WIDTH, S5_STATE)] * 2, name="s5_params")
    return (_block_diag(bb_re), _block_diag(bb_im), _block_diag(sp["c_re"]), _block_diag(sp["c_im"]),
            ab_re.reshape(1, S5_LANES), ab_im.reshape(1, S5_LANES), sp["dskip"])


def _mixer_fwd(x, sp, weights, rope, mm):
    D = x.shape[1]
    (h,) = _rows_fwd(_prenorm_fn, [(x, D, 0)], [sp["mix_norm_pre"]], [(D, D, BF16)], name="mix_prenorm")
    w_in = weights("win")
    z = mm("win", h, w_in, out_dtype=F32, name="mix_in")
    w_out, glu_w = weights("wout")
    y_attn = _attn_fwd(z, rope[0], rope[1], sp["sinks"], "mix")
    qkv = _conv_fwd(z, sp["conv_w"], "mix")
    y_dn, states = _dn_fwd(qkv, z, sp["dn_a_log"], sp["dn_dt_bias"], sp["dn_norm_w"], "mix")
    s5_in = _s5_inputs(sp)
    y_lin, *s5_states = _s5_fwd(z, *s5_in, "mix")
    (y_s5,) = _rows_fwd(_s5_glu_fn, [(y_lin, S5_WIDTH, 0)], [glu_w, sp["glu_b"]], [(S5_WIDTH, S5_WIDTH, BF16)],
                        name="mix_s5_glu")
    cat = jnp.concatenate([y_attn, y_dn, y_s5], axis=1)
    mixed = mm("wout", cat, w_out, out_dtype=F32, name="mix_out")
    (x_new,) = _rows_fwd(_residual_fn(1.0), [(x, D, 0), (mixed, D, 0)], [sp["mix_norm_post"]], [(D, D, F32)],
                         name="mix_residual")
    return x_new, (h, z, qkv, states, s5_in, y_lin, s5_states, cat, mixed, w_in, w_out, glu_w)


def _mixer_bwd(dx_new, x, sp, rope, saved, mm, sink):
    h, z, qkv, states, s5_in, y_lin, s5_states, cat, mixed, w_in, w_out, glu_w = saved
    D = x.shape[1]
    G, H, P = S5_GROUPS, S5_GROUP_CH, S5_STATE
    d_mixed, d_g_post = _rows_bwd(_postnorm_fn(1.0), [(mixed, D, 0)], [sp["mix_norm_post"]], [(dx_new, D, 0)],
                                  [(0, D, BF16)], name="mix_postnorm_bwd")
    d_cat = mm(d_mixed, w_out, tb=True, out_dtype=BF16, name="mix_out_dx")
    sink("wout", mm(cat, d_mixed, ta=True, out_dtype=BF16, name="mix_out_dw"))
    d_attn, d_dn, d_s5 = d_cat[:, :ATTN_WIDTH], d_cat[:, ATTN_WIDTH:ATTN_WIDTH + DN_WIDTH], d_cat[:, ATTN_WIDTH + DN_WIDTH:]
    d_ylin, d_glu_w, d_glu_b = _rows_bwd(_s5_glu_fn, [(y_lin, S5_WIDTH, 0)], [glu_w, sp["glu_b"]],
                                         [(d_s5, S5_WIDTH, 0)], [(0, S5_WIDTH, F32)], name="mix_s5_glu_bwd")
    sink("glu", d_glu_w.astype(BF16))
    d_us5, d_bre, d_bim, d_cre, d_cim, d_are, d_aim, d_dskip = _s5_bwd(z, *s5_in, *s5_states, d_ylin, "mix")
    cts = [d_are.reshape(G, P), d_aim.reshape(G, P), _block_diag_take(d_bre), _block_diag_take(d_bim)]
    d_a_re, d_a_im, d_ldt, d_bt_re, d_bt_im = _whole_bwd(_s5_param_fn, sp["s5_pins"], cts, 5, name="s5_params_bwd",
                                                         lane_sum=(2,))
    from_t = lambda m: m.reshape(G, H, P).transpose(0, 2, 1)
    d_qkv, d_zg, d_ba, d_alog, d_dtb, d_nw = _dn_bwd(qkv, z, sp["dn_a_log"], sp["dn_dt_bias"], sp["dn_norm_w"], states,
                                                     d_dn, "mix")
    d_uconv, d_conv_w = _conv_bwd(z, sp["conv_w"], d_qkv, "mix")
    d_q, d_kv, d_sinks = _attn_bwd(z, rope[0], rope[1], sp["sinks"], d_attn, "mix")
    d_z = jnp.concatenate([d_q, d_kv, d_uconv, d_zg, d_us5, d_ba], axis=1)
    sink("win", mm(h, d_z, ta=True, out_dtype=BF16, name="mix_in_dw"))
    d_h = mm(d_z, w_in, tb=True, out_dtype=BF16, name="mix_in_dx")
    dx, d_g_pre = _rows_bwd(_prenorm_fn, [(x, D, 0)], [sp["mix_norm_pre"]], [(d_h, D, 0)], [(0, D, F32)],
                            name="mix_prenorm_bwd", add_to_first=dx_new)
    small = {
        "mix_norm_pre": d_g_pre[0], "mix_norm_post": d_g_post[0], "attn_sinks": d_sinks[:, 0], "dn_conv_w": d_conv_w,
        "dn_a_log": d_alog[:, 0, 0], "dn_dt_bias": d_dtb[:, 0, 0], "dn_norm_w": d_nw[0, 0],
        "s5_a_re": d_a_re, "s5_a_im": d_a_im, "s5_log_dt": d_ldt[:, 0], "s5_b_re": from_t(d_bt_re),
        "s5_b_im": from_t(d_bt_im), "s5_c_re": _block_diag_take(d_cre).reshape(G, H, P),
        "s5_c_im": _block_diag_take(d_cim).reshape(G, H, P), "s5_d": d_dskip[0], "s5_glu_b": d_glu_b[0],
    }
    return dx, small


BIG_PIECES = ("gu1", "wd1", "win", "wout", "glu", "gu2", "wd2")


def _halves(m):
    return m.reshape(m.shape[:-2] + (2, m.shape[-2] // 2, m.shape[-1]))


def _whole(m):
    return m.reshape(m.shape[:-3] + (2 * m.shape[-2], m.shape[-1]))


class _WeightGather:
    def __init__(self, shards):
        self.shards = shards
        pieces = list(shards[0])
        flat = [s for p in pieces for s in shards[0][p]]
        got = iter(_gather_shards(flat, "gather_weights"))
        self.ready = {(0, p): [next(got) for _ in shards[0][p]] for p in pieces}
        self.landing = None

    def weights(self, l, piece):
        return self.ready[(l, piece)]

    def mm(self, l, piece, a, b, **kw):
        jobs, done = [], None
        if self.landing is not None:
            done = self.landing
            jobs += [{"kind": "gather_d2d", "ins": [buf, s]} for buf, s in zip(done[2], self.shards[done[0]][done[1]])]
        n_done = len(jobs)
        if l + 1 < len(self.shards):
            jobs += [{"kind": "gather_ici", "ins": [s]} for s in self.shards[l + 1][piece]]
        if not jobs:
            return _mm(a, b, **kw)
        out, bufs = _mm(a, b, jobs=jobs, **kw)
        if done is not None:
            self.ready[done[:2]] = bufs[:n_done]
        self.landing = (l + 1, piece, bufs[n_done:]) if l + 1 < len(self.shards) else None
        return out


class _GradReduce:
    def __init__(self):
        self.fresh, self.waiting, self.landed = [], [], {}

    def add(self, key, grad):
        self.fresh.append((key, _halves(grad)))

    def _summed(self, swapped):
        self.waiting += [(key, _add_sibling(part, got, "reduce_siblings_add"))
                         for (key, part), got in zip(self.fresh, swapped)]
        self.fresh = []

    def mm(self, a, b, **kw):
        k = a.shape[0] if kw.get("ta") else a.shape[1] // (2 if kw.get("a_swiglu") else 1)
        room = 2.0 * a.size * (b.size // k) / (2 if kw.get("a_swiglu") else 1) / MXU_FLOPS_PER_US * 1.15
        sent, kept = [], []
        for key, cp in self.waiting:
            cost = (N_SHARD - 1) * (cp.size // N_SHARD) * cp.dtype.itemsize / ICI_BYTES_PER_US
            if cost <= room:
                sent.append((key, cp))
                room -= cost
            else:
                kept.append((key, cp))
        if not (self.fresh or sent):
            return _mm(a, b, **kw)
        jobs = ([{"kind": "swap", "ins": [part]} for _, part in self.fresh]
                + [{"kind": "scatter", "ins": [cp]} for _, cp in sent])
        out, bufs = _mm(a, b, jobs=jobs, **kw)
        self.waiting = kept
        for (key, cp), o in zip(sent, bufs[len(self.fresh):]):
            self.landed[key] = (cp, o)
        self._summed(bufs[:len(self.fresh)])
        return out

    def finish(self):
        if self.fresh:
            self._summed(_swap_other_half([part for _, part in self.fresh], "reduce_siblings"))
        if self.waiting:
            others = _scatter_to_chips([cp for _, cp in self.waiting], "reduce_chips")
            for (key, cp), o in zip(self.waiting, others):
                self.landed[key] = (cp, o)
        keys = list(self.landed)
        joined = _join_halves([_sum_chips(*self.landed[k], "reduce_chips_sum") for k in keys], "reduce_join")
        return {k: _whole(m) for k, m in zip(keys, joined)}


def _pack_small(arrs, extra=()):
    flat = jnp.concatenate([a.reshape(-1) for a in arrs] + list(extra))
    n = flat.shape[0]
    quantum = 8 * SMALL_LANES
    padded = -(-n // quantum) * quantum
    return jnp.concatenate([flat, jnp.zeros((padded - n,), flat.dtype)]).reshape(padded // SMALL_LANES, SMALL_LANES)


def _unpack_small(flat2d, shapes):
    flat = flat2d.reshape(-1)
    out, o = [], 0
    for s in shapes:
        n = int(np.prod(s))
        out.append(flat[o:o + n].reshape(s))
        o += n
    return out, flat[o:]


def _step(a):
    x, target = a["x"][0], a["loss_target"][0]
    S, D = x.shape
    L, _, Fs = a["ff1_w_gate"].shape
    px, py, pc = _place()
    chip = 2 * px + py

    def rows_of_chips(g):
        g = _whole(g)
        return g.reshape(N_SHARD * g.shape[1], g.shape[2])

    shards = []
    for l in range(L):
        half = lambda m: _halves(m.astype(BF16))
        shards.append({
            "gu1": [half(jnp.concatenate([a["ff1_w_gate"][l], a["ff1_w_up"][l]], axis=1))], "wd1": [half(a["ff1_w_down"][l])],
            "win": [half(a["w_in"][l])], "wout": [half(a["w_out"][l]), half(a["s5_glu_w"][l])],
            "gu2": [half(jnp.concatenate([a["ff2_w_gate"][l], a["ff2_w_up"][l]], axis=1))], "wd2": [half(a["ff2_w_down"][l])]})
    gather = _WeightGather(shards)

    def ffn_weights(l, f):
        def get(piece):
            (g,) = gather.weights(l, piece + f)
            return _whole(g) if piece == "gu" else rows_of_chips(g)
        return get

    def mixer_weights(l):
        def get(piece):
            got = gather.weights(l, piece)
            if piece == "win":
                return _permute_w_in(_whole(got[0]).transpose(1, 0, 2).reshape(D, IN_WIDTH))
            return rows_of_chips(got[0]), rows_of_chips(got[1])
        return get

    conv_local = a["dn_conv_w"].reshape(-1)
    conv_rows = -(-conv_local.shape[0] // (16 * FLAT_LANES)) * 16
    conv_pad = jnp.concatenate([conv_local, jnp.zeros((conv_rows * FLAT_LANES - conv_local.shape[0],), F32)])
    (conv_all,) = _gather_shards([conv_pad.reshape(2, conv_rows // 2, FLAT_LANES)], "gather_conv")
    conv_all = conv_all.reshape(N_SHARD, -1)[:, :conv_local.shape[0]].reshape(N_SHARD, L, DN_CONV, -1)
    conv_full = conv_all.transpose(1, 2, 0, 3).reshape(L, DN_CONV, 3 * DN_WIDTH)

    expand = jnp.repeat(jnp.eye(S5_GROUPS, dtype=F32), S5_GROUP_CH, axis=0)
    lanes = lambda v, n=128: jnp.broadcast_to(v[:, None], (v.shape[0], n))
    to_t = lambda m: m.transpose(0, 2, 1).reshape(S5_WIDTH, S5_STATE)

    def small_params(l):
        sp = {k: a[k][l][None] for k in ("ff1_norm_pre", "ff1_norm_post", "mix_norm_pre", "mix_norm_post",
                                         "ff2_norm_pre", "ff2_norm_post")}
        sp["sinks"] = lanes(a["attn_sinks"][l])
        sp["conv_w"] = conv_full[l]
        sp["dn_a_log"] = lanes(a["dn_a_log"][l])[:, None, :]
        sp["dn_dt_bias"] = lanes(a["dn_dt_bias"][l])[:, None, :]
        sp["dn_norm_w"] = a["dn_norm_w"][l][None, None]
        sp["s5_pins"] = [a["s5_a_re"][l], a["s5_a_im"][l], lanes(a["s5_log_dt"][l], S5_STATE),
                         to_t(a["s5_b_re"][l]), to_t(a["s5_b_im"][l]), expand]
        sp["c_re"] = a["s5_c_re"][l].reshape(S5_WIDTH, S5_STATE)
        sp["c_im"] = a["s5_c_im"][l].reshape(S5_WIDTH, S5_STATE)
        sp["dskip"] = a["s5_d"][l][None]
        sp["glu_b"] = a["s5_glu_b"][l][None]
        return sp

    rope = _rope_tables(S)
    sps = [small_params(l) for l in range(L)]

    saved = []
    for l in range(L):
        sp = sps[l]
        mm_of = lambda f: (lambda piece, p, q, **kw: gather.mm(l, piece + f, p, q, **kw))
        x1, s1 = _ffn_fwd(x, sp["ff1_norm_pre"], sp["ff1_norm_post"], ffn_weights(l, "1"), "ff1", mm_of("1"))
        x2, s2 = _mixer_fwd(x1, sp, mixer_weights(l), rope, mm_of(""))
        x3, s3 = _ffn_fwd(x2, sp["ff2_norm_pre"], sp["ff2_norm_post"], ffn_weights(l, "2"), "ff2", mm_of("2"))
        saved.append((x, s1, x1, s2, x2, s3))
        x = x3
    dx, loss_part = _loss_head(x, target, "head")

    small_grads = [None] * L
    shard_major = lambda m: m.reshape(N_SHARD, m.shape[0] // N_SHARD, m.shape[1])
    red = _GradReduce()
    for l in reversed(range(L)):
        sp = sps[l]
        x0, s1, x1, s2, x2, s3 = saved[l]

        def ffn_sink(f):
            return lambda piece, g: red.add((l, piece + f), g if piece == "gu" else shard_major(g))

        def mixer_sink(piece, g):
            if piece == "win":
                g = _unpermute_w_in(g).reshape(D, N_SHARD, IN_WIDTH // N_SHARD).transpose(1, 0, 2)
            red.add((l, piece), g if piece == "win" else shard_major(g))

        dx, g_pre2, g_post2 = _ffn_bwd(dx, x2, sp["ff2_norm_pre"], sp["ff2_norm_post"], s3, "ff2", red.mm, ffn_sink("2"))
        dx, sg = _mixer_bwd(dx, x1, sp, rope, s2, red.mm, mixer_sink)
        dx, g_pre1, g_post1 = _ffn_bwd(dx, x0, sp["ff1_norm_pre"], sp["ff1_norm_post"], s1, "ff1", red.mm, ffn_sink("1"))
        sg.update({"ff1_norm_pre": g_pre1[0], "ff1_norm_post": g_post1[0], "ff2_norm_pre": g_pre2[0],
                   "ff2_norm_post": g_post2[0]})
        small_grads[l] = sg
    grad_x = dx[None]
    reduced = red.finish()

    grads = {}
    layers = lambda k: jnp.stack([reduced[(l, k)] for l in range(L)])
    for f, (gu, wd) in (("ff1", ("gu1", "wd1")), ("ff2", ("gu2", "wd2"))):
        gus = layers(gu)
        grads[f + "_w_gate"], grads[f + "_w_up"] = gus[:, :, :Fs], gus[:, :, Fs:]
        grads[f + "_w_down"] = layers(wd)
    grads["w_in"], grads["w_out"], grads["s5_glu_w"] = layers("win"), layers("wout"), layers("glu")

    small_local = [jnp.stack([small_grads[l][n] for l in range(L)]) for n in SMALL]
    vec = _pack_small(small_local, extra=(loss_part[0, :1],))
    total = _sum_slots(_gather_all_devices(vec, "gather_small"), "sum_small")
    small_total, rest = _unpack_small(total, [g.shape for g in small_local])
    loss = rest[0]
    for n, g in zip(SMALL, small_total):
        grads[n] = g
    cw = 3 * DN_WIDTH // N_SHARD
    grads["dn_conv_w"] = lax.dynamic_slice_in_dim(grads["dn_conv_w"], chip * cw, cw, axis=2)

    delta, new_m, new_v = {}, {}, {}
    for n in BIG:
        delta[n], new_m[n], new_v[n] = _adamw(a[n], grads[n], a["m_" + n], a["v_" + n], "adamw_" + n)
    shapes = [a[n].shape for n in SMALL]
    packed = [_pack_small([src[n] for n in SMALL]) for src in
              (a, grads, {n: a["m_" + n] for n in SMALL}, {n: a["v_" + n] for n in SMALL})]
    for dst, res in zip((delta, new_m, new_v), _adamw(*packed, "adamw_small")):
        for n, val in zip(SMALL, _unpack_small(res, shapes)[0]):
            dst[n] = val
    return (loss, grad_x, *[grads[n] for n in WEIGHTS], *[delta[n] for n in WEIGHTS], *[new_m[n] for n in WEIGHTS],
            *[new_v[n] for n in WEIGHTS])


def kernel(x, ff1_norm_pre, ff1_w_gate, ff1_w_up, ff1_w_down, ff1_norm_post, mix_norm_pre, w_in, attn_sinks, dn_conv_w, dn_a_log, dn_dt_bias, dn_norm_w, s5_a_re, s5_a_im, s5_log_dt, s5_b_re, s5_b_im, s5_c_re, s5_c_im, s5_d, s5_glu_w, s5_glu_b, w_out, mix_norm_post, ff2_norm_pre, ff2_w_gate, ff2_w_up, ff2_w_down, ff2_norm_post, loss_target, m_ff1_norm_pre, m_ff1_w_gate, m_ff1_w_up, m_ff1_w_down, m_ff1_norm_post, m_mix_norm_pre, m_w_in, m_attn_sinks, m_dn_conv_w, m_dn_a_log, m_dn_dt_bias, m_dn_norm_w, m_s5_a_re, m_s5_a_im, m_s5_log_dt, m_s5_b_re, m_s5_b_im, m_s5_c_re, m_s5_c_im, m_s5_d, m_s5_glu_w, m_s5_glu_b, m_w_out, m_mix_norm_post, m_ff2_norm_pre, m_ff2_w_gate, m_ff2_w_up, m_ff2_w_down, m_ff2_norm_post, v_ff1_norm_pre, v_ff1_w_gate, v_ff1_w_up, v_ff1_w_down, v_ff1_norm_post, v_mix_norm_pre, v_w_in, v_attn_sinks, v_dn_conv_w, v_dn_a_log, v_dn_dt_bias, v_dn_norm_w, v_s5_a_re, v_s5_a_im, v_s5_log_dt, v_s5_b_re, v_s5_b_im, v_s5_c_re, v_s5_c_im, v_s5_d, v_s5_glu_w, v_s5_glu_b, v_w_out, v_mix_norm_post, v_ff2_norm_pre, v_ff2_w_gate, v_ff2_w_up, v_ff2_w_down, v_ff2_norm_post):
    return _step(dict(locals()))
```

```python
import functools
import math

import numpy as np
import jax
import jax.numpy as jnp
from jax import lax
from jax.experimental import pallas as pl
from jax.experimental.pallas import tpu as pltpu

F32 = jnp.float32
BF16 = jnp.bfloat16
HI = lax.Precision.HIGHEST
MESH_ID = pl.DeviceIdType.MESH

NORM_EPS = 1e-6
FFN_RES_WEIGHT = 0.5
ATTN_HEADS, ATTN_KV_HEADS, HEAD_DIM, WINDOW = 8, 2, 128, 128
ROPE_THETA = 10000.0
DN_HEADS, DN_HEAD_DIM, DN_CONV, DN_CHUNK = 4, 128, 4, 64
S5_GROUPS, S5_GROUP_CH, S5_STATE = 32, 16, 64
ATTN_WIDTH = ATTN_HEADS * HEAD_DIM
ATTN_KV_WIDTH = ATTN_KV_HEADS * HEAD_DIM
DN_WIDTH = DN_HEADS * DN_HEAD_DIM
S5_WIDTH = S5_GROUPS * S5_GROUP_CH
S5_LANES = S5_GROUPS * S5_STATE
MIX_WIDTH = ATTN_WIDTH + DN_WIDTH + S5_WIDTH
IN_SPLITS = (ATTN_WIDTH, ATTN_KV_WIDTH, ATTN_KV_WIDTH, 3 * DN_WIDTH, DN_WIDTH, DN_HEADS, DN_HEADS, S5_WIDTH)
IN_WIDTH = sum(IN_SPLITS)
Z_AQ, Z_AK, Z_AV = 0, ATTN_WIDTH, ATTN_WIDTH + ATTN_KV_WIDTH
Z_DQKV = ATTN_WIDTH + 2 * ATTN_KV_WIDTH
Z_DZ = Z_DQKV + 3 * DN_WIDTH
Z_SU = Z_DZ + DN_WIDTH
Z_DBA = Z_SU + S5_WIDTH
Z_WIDTH = Z_DBA + 128

ADAM_LR, ADAM_B1, ADAM_B2, ADAM_EPS, ADAM_WD, ADAM_STEP = 0.001, 0.9, 0.999, 1e-08, 0.01, 10

N_SHARD = 4
FLAT_LANES = 512
SMALL_LANES = 1024
MXU_FLOPS_PER_US = 7.5e8
ICI_BYTES_PER_US = 6.8e4
VMEM_LIMIT = 56 * 1024 * 1024

WEIGHTS = ['ff1_norm_pre', 'ff1_w_gate', 'ff1_w_up', 'ff1_w_down', 'ff1_norm_post', 'mix_norm_pre', 'w_in',
           'attn_sinks', 'dn_conv_w', 'dn_a_log', 'dn_dt_bias', 'dn_norm_w', 's5_a_re', 's5_a_im', 's5_log_dt',
           's5_b_re', 's5_b_im', 's5_c_re', 's5_c_im', 's5_d', 's5_glu_w', 's5_glu_b', 'w_out', 'mix_norm_post',
           'ff2_norm_pre', 'ff2_w_gate', 'ff2_w_up', 'ff2_w_down', 'ff2_norm_post']
BIG = ['ff1_w_gate', 'ff1_w_up', 'ff1_w_down', 'w_in', 's5_glu_w', 'w_out', 'ff2_w_gate', 'ff2_w_up', 'ff2_w_down']
SMALL = [n for n in WEIGHTS if n not in BIG]


def _pick(dim, cands):
    for c in cands:
        if dim % c == 0:
            return c
    return dim


def _params(sem=None):
    return pltpu.CompilerParams(dimension_semantics=sem, vmem_limit_bytes=VMEM_LIMIT)


def _silu_mul(gate, up):
    return gate * jax.nn.sigmoid(gate) * up


def _mm(a, b, *, ta=False, tb=False, out_dtype=F32, name, b_groups=None, bk_groups=None, out_groups=None, jobs=(),
        a_swiglu=None, swiglu_bwd_of=None):
    K, M = a.shape if ta else a.shape[::-1]
    if a_swiglu:
        K, M = (K, M // 2) if ta else (K // 2, M)
    Ng = Kg = None
    if b_groups:
        assert not tb
        _, Kb, Ng = b.shape
        N = b_groups * Ng
    elif bk_groups:
        assert tb
        _, N, Kg = b.shape
        Kb = bk_groups * Kg
    else:
        N, Kb = (b.shape if tb else b.shape[::-1])
    assert K == Kb, (a.shape, b.shape, ta, tb)
    tiles = (1408, 1024, 512, 384, 256, 128)
    tm = _pick(M, (1024, 512, 256, 128))
    tk = _pick(Kg if bk_groups else K, (2048,) + tiles)
    tn = _pick(N // out_groups if out_groups else (Ng if b_groups else N), tiles)
    if a_swiglu and ta:
        tm, tk = _pick(a_swiglu, tiles), _pick(K, tiles[1:])
    elif a_swiglu:
        tm, tk, tn = _pick(M, (512, 256, 128)), _pick(a_swiglu, tiles), _pick(N, (2048,) + tiles)
    if swiglu_bwd_of is not None:
        tm, tn = _pick(M, (512, 256, 128)), swiglu_bwd_of.shape[1] // (2 * N_SHARD)
        assert tk == K
    nk = K // tk
    dims = (((0,) if ta else (1,), (1,) if tb else (0,)), ((), ()))

    grid = (M // tm, N // tn, nk)
    n_job_in = sum(len(j["ins"]) for j in jobs)
    n_sems = sum(_JOB_SEMS[j["kind"]] for j in jobs)
    n_a = 2 if a_swiglu else 1
    n_lead = n_a + 1 + (1 if swiglu_bwd_of is not None else 0)

    def body(*refs):
        b_ref = refs[n_a]
        if a_swiglu:
            a_val = _silu_mul(refs[0][...].astype(F32), refs[1][...].astype(F32)).astype(BF16)
        else:
            a_val = refs[0][...].astype(BF16)
        job_ins = refs[n_lead:n_lead + n_job_in]
        o_ref = refs[n_lead + n_job_in]
        job_outs = refs[n_lead + 1 + n_job_in:n_lead + 1 + n_job_in + len(jobs)]
        scratch = refs[n_lead + 1 + n_job_in + len(jobs):]
        if jobs:
            send_sems, recv_sems = scratch[-2:]
            scratch = scratch[:-2]
            ids = [pl.program_id(d) for d in range(3)]
            first = functools.reduce(jnp.logical_and, [i == 0 for i in ids])
            last = functools.reduce(jnp.logical_and, [i == g - 1 for i, g in zip(ids, grid)])

            def copies():
                sends, recvs, at, sem = [], [], 0, 0
                for j, out in zip(jobs, job_outs):
                    s, r = _job_copies(j["kind"], job_ins[at:at + len(j["ins"])], out, send_sems, recv_sems, sem)
                    sends, recvs = sends + s, recvs + r
                    at, sem = at + len(j["ins"]), sem + _JOB_SEMS[j["kind"]]
                return sends, recvs

            @pl.when(first)
            def _():
                for cp in copies()[0]:
                    cp.start()

        part = lax.dot_general(a_val, b_ref[...].astype(BF16), dims, preferred_element_type=F32)
        if swiglu_bwd_of is not None:
            gate, up = refs[n_a + 1][:, :tn].astype(F32), refs[n_a + 1][:, tn:].astype(F32)
            sg = jax.nn.sigmoid(gate)
            o_ref[:, :tn] = (part * up * sg * (1.0 + gate * (1.0 - sg))).astype(o_ref.dtype)
            o_ref[:, tn:] = (part * gate * sg).astype(o_ref.dtype)
        elif nk == 1:
            o_ref[...] = part.astype(o_ref.dtype)
        else:
            acc_ref, = scratch
            k = pl.program_id(2)

            @pl.when(k == 0)
            def _():
                acc_ref[...] = part

            @pl.when(k > 0)
            def _():
                acc_ref[...] += part

            @pl.when(k == nk - 1)
            def _():
                o_ref[...] = acc_ref[...].astype(o_ref.dtype)

        if jobs:
            @pl.when(last)
            def _():
                sends, recvs = copies()
                for cp in recvs:
                    cp.wait_recv()
                for cp in sends:
                    cp.wait_send()

    if a_swiglu and ta:
        u = a_swiglu // tm
        a_specs = [pl.BlockSpec((tk, tm), lambda i, j, k, o=o: (k, (i // u) * 2 * u + o + i % u)) for o in (0, u)]
    elif a_swiglu:
        u = a_swiglu // tk
        a_specs = [pl.BlockSpec((tm, tk), lambda i, j, k, o=o: (i, (k // u) * 2 * u + o + k % u)) for o in (0, u)]
    else:
        a_specs = [pl.BlockSpec((tk, tm), lambda i, j, k: (k, i)) if ta else pl.BlockSpec((tm, tk), lambda i, j, k: (i, k))]
    if b_groups:
        per = Ng // tn
        b_spec = pl.BlockSpec((None, tk, tn), lambda i, j, k: (j // per, k, j % per))
    elif bk_groups:
        per = Kg // tk
        b_spec = pl.BlockSpec((None, tn, tk), lambda i, j, k: (k // per, j, k % per))
    else:
        b_spec = pl.BlockSpec((tn, tk), lambda i, j, k: (j, k)) if tb else pl.BlockSpec((tk, tn), lambda i, j, k: (k, j))
    if out_groups:
        pero = (N // out_groups) // tn
        o_spec = pl.BlockSpec((None, tm, tn), lambda i, j, k: (j // pero, i, j % pero))
        o_shape = jax.ShapeDtypeStruct((out_groups, M, N // out_groups), out_dtype)
    else:
        o_spec = pl.BlockSpec((tm, tn), lambda i, j, k: (i, j))
        o_shape = jax.ShapeDtypeStruct((M, N), out_dtype)
    lead_specs, lead_args = a_specs + [b_spec], [a] * n_a + [b]
    if swiglu_bwd_of is not None:
        o_spec = pl.BlockSpec((tm, 2 * tn), lambda i, j, k: (i, j))
        o_shape = jax.ShapeDtypeStruct((M, 2 * N), out_dtype)
        lead_specs, lead_args = lead_specs + [o_spec], lead_args + [swiglu_bwd_of]
    scratch = [pltpu.VMEM((tm, tn), F32)] if nk > 1 else []
    if not jobs:
        return pl.pallas_call(
            body, name=name, grid=grid, in_specs=lead_specs, out_specs=o_spec, out_shape=o_shape,
            scratch_shapes=scratch, compiler_params=_params(("parallel", "parallel", "arbitrary")))(*lead_args)
    job_args = [x for j in jobs for x in j["ins"]]
    aliases, at = {}, n_lead
    for n, j in enumerate(jobs):
        if j["kind"] == "gather_d2d":
            aliases[at] = 1 + n
        at += len(j["ins"])
    res = pl.pallas_call(
        body, name=name, grid=grid, in_specs=lead_specs + [_ANY] * n_job_in,
        out_specs=[o_spec] + [_ANY] * len(jobs), out_shape=[o_shape] + [_job_out_shape(j) for j in jobs],
        input_output_aliases=aliases,
        scratch_shapes=scratch + [pltpu.SemaphoreType.DMA((n_sems,)), pltpu.SemaphoreType.DMA((n_sems,))],
        compiler_params=_params(("arbitrary", "arbitrary", "arbitrary")))(*lead_args, *job_args)
    return res[0], list(res[1:])


def _row_spec(tm, width, off):
    return pl.BlockSpec((tm, width), lambda i, j: (i, off + j))


def _const_spec(shape):
    return pl.BlockSpec(shape, lambda i, j: (0,) * len(shape))


def _rows_fwd(fn, rows, consts, outs, *, name, tm=256, ncol=1):
    S = rows[0][0].shape[0]
    tm = _pick(S, (tm, 128, 64))
    nr, nc = len(rows), len(consts)

    def body(*refs):
        vals = [r[...].astype(F32) for r in refs[:nr + nc]]
        res = fn(*vals)
        for o_ref, o in zip(refs[nr + nc:], res):
            o_ref[...] = o.astype(o_ref.dtype)

    return pl.pallas_call(
        body, name=name, grid=(S // tm, ncol),
        in_specs=[_row_spec(tm, w, off) for _, w, off in rows] + [_const_spec(c.shape) for c in consts],
        out_specs=[_row_spec(tm, bw, 0) for _, bw, _ in outs],
        out_shape=[jax.ShapeDtypeStruct((S, tw), dt) for tw, _, dt in outs],
        compiler_params=_params(("parallel", "parallel")))(*[r[0] for r in rows], *consts)


def _rows_bwd(fn, rows, consts, cts, row_grads, *, name, tm=256, ncol=1, add_to_first=None, lane_sum_consts=()):
    S = rows[0][0].shape[0]
    tm = _pick(S, (tm, 128, 64))
    nr, nc, nt = len(rows), len(consts), len(cts)
    n_in = nr + nc + nt + (1 if add_to_first is not None else 0)

    def body(*refs):
        vals = [r[...].astype(F32) for r in refs[:nr + nc]]
        ct = tuple(r[...].astype(F32) for r in refs[nr + nc:nr + nc + nt])
        _, vjp = jax.vjp(fn, *vals)
        grads = vjp(ct)
        outs = refs[n_in:]
        for n, (idx, _, _) in enumerate(row_grads):
            g = grads[idx]
            if n == 0 and add_to_first is not None:
                g = g + refs[n_in - 1][...].astype(F32)
            outs[n][...] = g.astype(outs[n].dtype)
        first = jnp.logical_and(pl.program_id(0) == 0, pl.program_id(1) == 0)
        for c in range(nc):
            o_ref = outs[len(row_grads) + c]
            g = grads[nr + c]
            if c in lane_sum_consts:
                g = jnp.broadcast_to(jnp.sum(g, axis=-1, keepdims=True), g.shape)

            @pl.when(first)
            def _():
                o_ref[...] = jnp.zeros_like(o_ref)

            o_ref[...] += g

    in_specs = ([_row_spec(tm, w, off) for _, w, off in rows] + [_const_spec(c.shape) for c in consts]
                + [_row_spec(tm, w, off) for _, w, off in cts])
    args = [r[0] for r in rows] + list(consts) + [c[0] for c in cts]
    if add_to_first is not None:
        in_specs.append(_row_spec(tm, rows[row_grads[0][0]][1], 0))
        args.append(add_to_first)
    out_specs = [_row_spec(tm, rows[idx][1], 0) for idx, _, _ in row_grads] + [_const_spec(c.shape) for c in consts]
    out_shape = ([jax.ShapeDtypeStruct((S, tw), dt) for _, tw, dt in row_grads]
                 + [jax.ShapeDtypeStruct(c.shape, F32) for c in consts])
    return pl.pallas_call(
        body, name=name, grid=(S // tm, ncol), in_specs=in_specs, out_specs=out_specs, out_shape=out_shape,
        compiler_params=_params(("arbitrary", "arbitrary")))(*args)


def _rms(x, gain):
    return x * lax.rsqrt(jnp.mean(x * x, axis=-1, keepdims=True) + NORM_EPS) * gain


def _prenorm_fn(x, gain):
    return (_rms(x, gain),)


def _postnorm_fn(weight):
    def fn(y, gain):
        return (weight * _rms(y, gain),)
    return fn


def _residual_fn(weight):
    def fn(x, y, gain):
        return (x + weight * _rms(y, gain),)
    return fn


def _swiglu_fn(blk):
    tf = blk.shape[1] // 2
    gate, up = blk[:, :tf], blk[:, tf:]
    return (gate * jax.nn.sigmoid(gate) * up,)


def _ffn_fwd(x, g_pre, g_post, weights, tag, mm):
    D = x.shape[1]
    (h,) = _rows_fwd(_prenorm_fn, [(x, D, 0)], [g_pre], [(D, D, BF16)], name=f"{tag}_prenorm")
    wgu = weights("gu")
    Fs = wgu.shape[2] // 2
    gu = mm("gu", h, wgu, b_groups=N_SHARD, out_dtype=BF16, name=f"{tag}_gate_up")
    wd = weights("wd")
    y = mm("wd", gu, wd, a_swiglu=Fs, out_dtype=F32, name=f"{tag}_down")
    (x_new,) = _rows_fwd(_residual_fn(FFN_RES_WEIGHT), [(x, D, 0), (y, D, 0)], [g_post], [(D, D, F32)],
                         name=f"{tag}_residual")
    return x_new, (h, gu, y, wgu, wd)


def _ffn_bwd(dx_new, x, g_pre, g_post, saved, tag, mm, sink):
    h, gu, y, wgu, wd = saved
    D = x.shape[1]
    Fs = wgu.shape[2] // 2
    d_y, d_g_post = _rows_bwd(_postnorm_fn(FFN_RES_WEIGHT), [(y, D, 0)], [g_post], [(dx_new, D, 0)],
                              [(0, D, BF16)], name=f"{tag}_postnorm_bwd")
    d_gu = mm(d_y, wd, tb=True, swiglu_bwd_of=gu, out_dtype=BF16, name=f"{tag}_down_dx")
    sink("wd", mm(gu, d_y, ta=True, a_swiglu=Fs, out_dtype=BF16, name=f"{tag}_down_dw"))
    sink("gu", mm(h, d_gu, ta=True, out_dtype=BF16, out_groups=N_SHARD, name=f"{tag}_gate_up_dw"))
    d_h = mm(d_gu, wgu, tb=True, bk_groups=N_SHARD, out_dtype=BF16, name=f"{tag}_gate_up_dx")
    dx, d_g_pre = _rows_bwd(_prenorm_fn, [(x, D, 0)], [g_pre], [(d_h, D, 0)], [(0, D, F32)],
                            name=f"{tag}_prenorm_bwd", add_to_first=dx_new)
    return dx, d_g_pre, d_g_post


@jax.custom_vjp
def _swap_halves(x):
    return pltpu.roll(x, HEAD_DIM // 2, 1)


def _swap_fwd(x):
    return _swap_halves(x), None


def _swap_bwd(_, g):
    return (_swap_halves(g),)


_swap_halves.defvjp(_swap_fwd, _swap_bwd)


def _rope(x, cc, ss):
    return x * cc + _swap_halves(x) * ss


def _attn_block_fn(has_prev):
    grp = ATTN_HEADS // ATTN_KV_HEADS
    scale = HEAD_DIM ** -0.5
    nt = (((1,), (1,)), ((), ()))
    nn = (((1,), (0,)), ((), ()))

    def fn(*a):
        q = a[:8]
        kp, kc, vp, vc = a[8:10], a[10:12], a[12:14], a[14:16]
        cc, ss, ccp, ssp, sinks = a[16:21]
        row = lax.broadcasted_iota(jnp.int32, (WINDOW, WINDOW), 0)
        col = lax.broadcasted_iota(jnp.int32, (WINDOW, WINDOW), 1)
        m_cur = col <= row
        m_prev = jnp.logical_and(col > row, has_prev)
        outs = []
        for h in range(ATTN_HEADS):
            g = h // grp
            qr = _rope(q[h], cc, ss).astype(BF16)
            kcr = _rope(kc[g], cc, ss).astype(BF16)
            kpr = _rope(kp[g], ccp, ssp).astype(BF16)
            s_c = jnp.where(m_cur, lax.dot_general(qr, kcr, nt, preferred_element_type=F32) * scale, -jnp.inf)
            s_p = jnp.where(m_prev, lax.dot_general(qr, kpr, nt, preferred_element_type=F32) * scale, -jnp.inf)
            sink = sinks[h:h + 1, :]
            m = jnp.maximum(jnp.maximum(jnp.max(s_c, axis=-1, keepdims=True), jnp.max(s_p, axis=-1, keepdims=True)),
                            sink)
            p_c = jnp.exp(s_c - m)
            p_p = jnp.exp(s_p - m)
            den = (jnp.sum(p_c, axis=-1, keepdims=True) + jnp.sum(p_p, axis=-1, keepdims=True) + jnp.exp(sink - m))
            inv = 1.0 / den
            o = (lax.dot_general((p_c * inv).astype(BF16), vc[g].astype(BF16), nn, preferred_element_type=F32)
                 + lax.dot_general((p_p * inv).astype(BF16), vp[g].astype(BF16), nn, preferred_element_type=F32))
            outs.append(o)
        return tuple(outs)

    return fn


def _attn_specs(order):
    kvb = ATTN_WIDTH // (2 * ATTN_KV_WIDTH)
    return [
        pl.BlockSpec((WINDOW, ATTN_WIDTH), lambda i: (order(i), 0)),
        pl.BlockSpec((WINDOW, 2 * ATTN_KV_WIDTH), lambda i: (jnp.maximum(order(i) - 1, 0), kvb)),
        pl.BlockSpec((WINDOW, 2 * ATTN_KV_WIDTH), lambda i: (order(i), kvb)),
        pl.BlockSpec((WINDOW, HEAD_DIM), lambda i: (order(i), 0)),
        pl.BlockSpec((WINDOW, HEAD_DIM), lambda i: (order(i), 0)),
        pl.BlockSpec((WINDOW, HEAD_DIM), lambda i: (jnp.maximum(order(i) - 1, 0), 0)),
        pl.BlockSpec((WINDOW, HEAD_DIM), lambda i: (jnp.maximum(order(i) - 1, 0), 0)),
        pl.BlockSpec((ATTN_HEADS, HEAD_DIM), lambda i: (0, 0)),
    ]


def _attn_args(q_ref, kvp_ref, kvc_ref, cc, ss, ccp, ssp, sinks):
    d = HEAD_DIM
    q = [q_ref[:, h * d:(h + 1) * d].astype(F32) for h in range(ATTN_HEADS)]
    kp = [kvp_ref[:, g * d:(g + 1) * d].astype(F32) for g in range(ATTN_KV_HEADS)]
    vp = [kvp_ref[:, ATTN_KV_WIDTH + g * d:ATTN_KV_WIDTH + (g + 1) * d].astype(F32) for g in range(ATTN_KV_HEADS)]
    kc = [kvc_ref[:, g * d:(g + 1) * d].astype(F32) for g in range(ATTN_KV_HEADS)]
    vc = [kvc_ref[:, ATTN_KV_WIDTH + g * d:ATTN_KV_WIDTH + (g + 1) * d].astype(F32) for g in range(ATTN_KV_HEADS)]
    return q + kp + kc + vp + vc + [cc[...], ss[...], ccp[...], ssp[...], sinks[...]]


def _attn_fwd(z, cc, ss, sinks_b, tag):
    S = z.shape[0]
    nb = S // WINDOW

    def body(q_ref, kvp_ref, kvc_ref, cc_r, ss_r, ccp_r, ssp_r, sink_r, o_ref):
        n = pl.program_id(0)
        outs = _attn_block_fn(n > 0)(*_attn_args(q_ref, kvp_ref, kvc_ref, cc_r, ss_r, ccp_r, ssp_r, sink_r))
        for h in range(ATTN_HEADS):
            o_ref[:, h * HEAD_DIM:(h + 1) * HEAD_DIM] = outs[h].astype(o_ref.dtype)

    return pl.pallas_call(
        body, name=f"{tag}_attn", grid=(nb,), in_specs=_attn_specs(lambda i: i),
        out_specs=pl.BlockSpec((WINDOW, ATTN_WIDTH), lambda i: (i, 0)),
        out_shape=jax.ShapeDtypeStruct((S, ATTN_WIDTH), BF16),
        compiler_params=_params(("parallel",)))(z, z, z, cc, ss, cc, ss, sinks_b)


def _attn_bwd(z, cc, ss, sinks_b, d_out, tag):
    S = z.shape[0]
    nb = S // WINDOW
    d = HEAD_DIM
    rev = lambda i: nb - 1 - i

    def body(q_ref, kvp_ref, kvc_ref, cc_r, ss_r, ccp_r, ssp_r, sink_r, do_ref, dq_ref, dkv_ref, dsink_ref, carry):
        i = pl.program_id(0)
        n = nb - 1 - i

        @pl.when(i == 0)
        def _():
            carry[...] = jnp.zeros_like(carry)
            dsink_ref[...] = jnp.zeros_like(dsink_ref)

        args = _attn_args(q_ref, kvp_ref, kvc_ref, cc_r, ss_r, ccp_r, ssp_r, sink_r)
        _, vjp = jax.vjp(_attn_block_fn(n > 0), *args)
        g = vjp(tuple(do_ref[:, h * d:(h + 1) * d].astype(F32) for h in range(ATTN_HEADS)))
        for h in range(ATTN_HEADS):
            dq_ref[:, h * d:(h + 1) * d] = g[h].astype(dq_ref.dtype)
        for gi in range(ATTN_KV_HEADS):
            ks = slice(gi * d, (gi + 1) * d)
            vs = slice(ATTN_KV_WIDTH + gi * d, ATTN_KV_WIDTH + (gi + 1) * d)
            dkv_ref[:, ks] = (g[10 + gi] + carry[:, ks]).astype(dkv_ref.dtype)
            dkv_ref[:, vs] = (g[14 + gi] + carry[:, vs]).astype(dkv_ref.dtype)
            carry[:, ks] = g[8 + gi]
            carry[:, vs] = g[12 + gi]
        ds = g[20]
        dsink_ref[...] += jnp.broadcast_to(jnp.sum(ds, axis=-1, keepdims=True), ds.shape)

    return pl.pallas_call(
        body, name=f"{tag}_attn_bwd", grid=(nb,),
        in_specs=_attn_specs(rev) + [pl.BlockSpec((WINDOW, ATTN_WIDTH), lambda i: (rev(i), 0))],
        out_specs=[pl.BlockSpec((WINDOW, ATTN_WIDTH), lambda i: (rev(i), 0)),
                   pl.BlockSpec((WINDOW, 2 * ATTN_KV_WIDTH), lambda i: (rev(i), 0)),
                   pl.BlockSpec((ATTN_HEADS, HEAD_DIM), lambda i: (0, 0))],
        out_shape=[jax.ShapeDtypeStruct((S, ATTN_WIDTH), BF16), jax.ShapeDtypeStruct((S, 2 * ATTN_KV_WIDTH), BF16),
                   jax.ShapeDtypeStruct((ATTN_HEADS, HEAD_DIM), F32)],
        scratch_shapes=[pltpu.VMEM((WINDOW, 2 * ATTN_KV_WIDTH), F32)],
        compiler_params=_params(("arbitrary",)))(z, z, z, cc, ss, cc, ss, sinks_b, d_out)


def _rope_tables(seq):
    half = HEAD_DIM // 2
    inv_freq = ROPE_THETA ** (-jnp.arange(half, dtype=F32) / half)
    ang = jnp.arange(seq, dtype=F32)[:, None] * inv_freq[None, :]
    cos, sin = jnp.cos(ang), jnp.sin(ang)
    return jnp.concatenate([cos, cos], axis=1), jnp.concatenate([-sin, sin], axis=1)


CONV_COLS = 128


def _conv_pre(u, w_ref, S):
    row = lax.broadcasted_iota(jnp.int32, u.shape, 0)
    shifted = [u] + [jnp.where(row >= s, pltpu.roll(u, s, 0), 0.0) for s in range(1, DN_CONV)]
    y = shifted[0] * w_ref[DN_CONV - 1:DN_CONV, :]
    for s in range(1, DN_CONV):
        y = y + shifted[s] * w_ref[DN_CONV - 1 - s:DN_CONV - s, :]
    return y, shifted, row


def _conv_fwd(z, conv_w, tag):
    S = z.shape[0]
    ncol = 3 * DN_WIDTH // CONV_COLS

    def body(u_ref, w_ref, o_ref):
        y, _, _ = _conv_pre(u_ref[...], w_ref, S)
        o_ref[...] = y * jax.nn.sigmoid(y)

    return pl.pallas_call(
        body, name=f"{tag}_conv", grid=(ncol,),
        in_specs=[pl.BlockSpec((S, CONV_COLS), lambda j: (0, Z_DQKV // CONV_COLS + j)),
                  pl.BlockSpec((DN_CONV, CONV_COLS), lambda j: (0, j))],
        out_specs=pl.BlockSpec((S, CONV_COLS), lambda j: (0, j)),
        out_shape=jax.ShapeDtypeStruct((S, 3 * DN_WIDTH), F32),
        compiler_params=_params(("parallel",)))(z, conv_w)


def _conv_bwd(z, conv_w, d_out, tag):
    S = z.shape[0]
    ncol = 3 * DN_WIDTH // CONV_COLS

    def body(u_ref, w_ref, do_ref, du_ref, dw_ref):
        y, shifted, row = _conv_pre(u_ref[...], w_ref, S)
        sg = jax.nn.sigmoid(y)
        d_y = do_ref[...] * (sg * (1.0 + y * (1.0 - sg)))
        d_u = d_y * w_ref[DN_CONV - 1:DN_CONV, :]
        dw_ref[DN_CONV - 1:DN_CONV, :] = jnp.sum(d_y * shifted[0], axis=0, keepdims=True)
        for s in range(1, DN_CONV):
            back = jnp.where(row < S - s, pltpu.roll(d_y, S - s, 0), 0.0)
            d_u = d_u + back * w_ref[DN_CONV - 1 - s:DN_CONV - s, :]
            dw_ref[DN_CONV - 1 - s:DN_CONV - s, :] = jnp.sum(d_y * shifted[s], axis=0, keepdims=True)
        du_ref[...] = d_u.astype(du_ref.dtype)

    return pl.pallas_call(
        body, name=f"{tag}_conv_bwd", grid=(ncol,),
        in_specs=[pl.BlockSpec((S, CONV_COLS), lambda j: (0, Z_DQKV // CONV_COLS + j)),
                  pl.BlockSpec((DN_CONV, CONV_COLS), lambda j: (0, j)),
                  pl.BlockSpec((S, CONV_COLS), lambda j: (0, j))],
        out_specs=[pl.BlockSpec((S, CONV_COLS), lambda j: (0, j)), pl.BlockSpec((DN_CONV, CONV_COLS), lambda j: (0, j))],
        out_shape=[jax.ShapeDtypeStruct((S, 3 * DN_WIDTH), BF16), jax.ShapeDtypeStruct((DN_CONV, 3 * DN_WIDTH), F32)],
        compiler_params=_params(("parallel",)))(z, conv_w, d_out)


_NN = (((1,), (0,)), ((), ()))
_NT = (((1,), (1,)), ((), ()))
_TN = (((0,), (0,)), ((), ()))


def _dot3(a, b, dims):
    a_hi, b_hi = a.astype(BF16), b.astype(BF16)
    a_lo, b_lo = (a - a_hi.astype(F32)).astype(BF16), (b - b_hi.astype(F32)).astype(BF16)
    mm = lambda p, q: lax.dot_general(p, q, dims, preferred_element_type=F32)
    return mm(a_hi, b_hi) + (mm(a_hi, b_lo) + mm(a_lo, b_hi))


@functools.partial(jax.custom_vjp, nondiff_argnums=(2,))
def _dot_vjp(a, b, dims):
    return _dot3(a, b, dims)


def _dot_vjp_fwd(a, b, dims):
    return _dot3(a, b, dims), (a, b)


_BATCH = ((0,), (0,))
_BNN, _BNT, _BTN = (((2,), (1,)), _BATCH), (((2,), (2,)), _BATCH), (((1,), (1,)), _BATCH)


def _dot_vjp_bwd(dims, res, g):
    a, b = res
    nn, nt, tn = (_NN, _NT, _TN) if dims in (_NN, _NT, _TN) else (_BNN, _BNT, _BTN)
    if dims == nn:
        return _dot3(g, b, nt), _dot3(a, g, tn)
    if dims == nt:
        return _dot3(g, b, nn), _dot3(g, a, tn)
    return _dot3(b, g, nt), _dot3(a, g, nn)


_dot_vjp.defvjp(_dot_vjp_fwd, _dot_vjp_bwd)


def _dot(a, b, dims=_NN):
    return _dot_vjp(a, b, dims)


def _dot1(a, b, dims=_NN):
    return lax.dot_general(a.astype(BF16), b.astype(BF16), dims, preferred_element_type=F32)


def _nilpotent_inverse(m):
    H, C, _ = m.shape
    eye = (lax.broadcasted_iota(jnp.int32, (H, C, C), 1) == lax.broadcasted_iota(jnp.int32, (H, C, C), 2)).astype(F32)
    inv = eye + m
    for _ in range(5):
        m = _dot3(m, m, _BNN)
        inv = inv + _dot3(m, inv, _BNN)
    return inv


@jax.custom_vjp
def _unit_lower_solve(m, ru, rw):
    inv = _nilpotent_inverse(m)
    return _dot3(inv, ru, _BNN), _dot3(inv, rw, _BNN)


def _unit_lower_solve_fwd(m, ru, rw):
    inv = _nilpotent_inverse(m)
    xu, xw = _dot3(inv, ru, _BNN), _dot3(inv, rw, _BNN)
    return (xu, xw), (inv, xu, xw)


def _unit_lower_solve_bwd(res, g):
    inv, xu, xw = res
    dru, drw = _dot3(inv, g[0], _BTN), _dot3(inv, g[1], _BTN)
    return _dot3(dru, xu, _BNT) + _dot3(drw, xw, _BNT), dru, drw


_unit_lower_solve.defvjp(_unit_lower_solve_fwd, _unit_lower_solve_bwd)


def _dn_chunk_fn(state, q, k, v, zg, ba, a_log, dtb, norm_w):
    H, C, dk = DN_HEADS, DN_CHUNK, DN_HEAD_DIM
    head = lax.broadcasted_iota(jnp.int32, (H, C, dk), 0)
    rowl = lax.broadcasted_iota(jnp.int32, (H, C, dk), 1)
    lane = lax.broadcasted_iota(jnp.int32, (H, C, dk), 2)
    row = lax.broadcasted_iota(jnp.int32, (H, C, C), 1)
    col = lax.broadcasted_iota(jnp.int32, (H, C, C), 2)
    ba3 = jnp.broadcast_to(ba[None], (H, C, dk))
    bcol = jnp.sum(jnp.where(lane == head, ba3, 0.0), axis=-1, keepdims=True)
    acol = jnp.sum(jnp.where(lane == H + head, ba3, 0.0), axis=-1, keepdims=True)
    qn = q * lax.rsqrt(jnp.sum(q * q, axis=-1, keepdims=True) + NORM_EPS) * (dk ** -0.5)
    kn = k * lax.rsqrt(jnp.sum(k * k, axis=-1, keepdims=True) + NORM_EPS)
    beta = jax.nn.sigmoid(bcol)
    sp_in = acol + dtb
    softplus = jnp.maximum(sp_in, 0.0) + jnp.log(1.0 + jnp.exp(-jnp.abs(sp_in)))
    gt = -jnp.exp(a_log) * softplus
    gc = _dot((row >= col).astype(F32), gt, _BNN)
    gcol = jnp.mean(gc, axis=-1, keepdims=True)
    grow = _dot(jnp.full((H, C, dk), 1.0 / dk, F32), gc, _BNT)
    decay = jnp.exp(jnp.where(row >= col, gcol - grow, -jnp.inf))
    kb = kn * beta
    m = -jnp.where(row > col, _dot(kb, kn, _BNT) * decay, 0.0)
    u, w = _unit_lower_solve(m, v * beta, kb * jnp.exp(gc))
    attn = jnp.where(row >= col, _dot(qn, kn, _BNT) * decay, 0.0)
    q_dec = qn * jnp.exp(gc)
    gl = jnp.sum(jnp.where(rowl == C - 1, gc, 0.0), axis=1, keepdims=True)
    k_dec = kn * jnp.exp(gl - gc)
    v_new = u - _dot(w, state, _BNN)
    o = _dot(q_dec, state, _BNN) + _dot(attn, v_new, _BNN)
    state_new = state * jnp.exp(gl) + _dot(k_dec, v_new, _BTN)
    y = o * lax.rsqrt(jnp.mean(o * o, axis=-1, keepdims=True) + NORM_EPS) * norm_w
    y = y * (zg * jax.nn.sigmoid(zg))
    return state_new, y


def _dn_specs(order):
    C = DN_CHUNK
    return [pl.BlockSpec((C, 3 * DN_WIDTH), lambda i: (order(i), 0)),
            pl.BlockSpec((C, DN_WIDTH), lambda i: (order(i), Z_DZ // DN_WIDTH)),
            pl.BlockSpec((C, 128), lambda i: (order(i), Z_DBA // 128)),
            pl.BlockSpec((DN_HEADS, 1, DN_HEAD_DIM), lambda i: (0, 0, 0)),
            pl.BlockSpec((DN_HEADS, 1, DN_HEAD_DIM), lambda i: (0, 0, 0)),
            pl.BlockSpec((1, 1, DN_HEAD_DIM), lambda i: (0, 0, 0))]


def _dn_heads(ref, base=0):
    d = DN_HEAD_DIM
    return jnp.stack([ref[:, base + h * d:base + (h + 1) * d].astype(F32) for h in range(DN_HEADS)], axis=0)


def _dn_args(qkv_ref, zg_ref, ba_ref, alog_ref, dtb_ref, nw_ref):
    return [_dn_heads(qkv_ref), _dn_heads(qkv_ref, DN_WIDTH), _dn_heads(qkv_ref, 2 * DN_WIDTH), _dn_heads(zg_ref),
            ba_ref[...].astype(F32), alog_ref[...], dtb_ref[...], nw_ref[...]]


def _dn_fwd(qkv, z, a_log, dtb, norm_w, tag):
    S = qkv.shape[0]
    nchunk = S // DN_CHUNK
    d = DN_HEAD_DIM

    def body(qkv_ref, zg_ref, ba_ref, alog_ref, dtb_ref, nw_ref, y_ref, st_ref, state):
        @pl.when(pl.program_id(0) == 0)
        def _():
            state[...] = jnp.zeros_like(state)

        st_ref[...] = state[...]
        new, y = _dn_chunk_fn(state[...], *_dn_args(qkv_ref, zg_ref, ba_ref, alog_ref, dtb_ref, nw_ref))
        state[...] = new
        for h in range(DN_HEADS):
            y_ref[:, h * d:(h + 1) * d] = y[h].astype(y_ref.dtype)

    return pl.pallas_call(
        body, name=f"{tag}_deltanet", grid=(nchunk,), in_specs=_dn_specs(lambda i: i),
        out_specs=[pl.BlockSpec((DN_CHUNK, DN_WIDTH), lambda i: (i, 0)),
                   pl.BlockSpec((None, DN_HEADS, d, d), lambda i: (i, 0, 0, 0))],
        out_shape=[jax.ShapeDtypeStruct((S, DN_WIDTH), BF16), jax.ShapeDtypeStruct((nchunk, DN_HEADS, d, d), F32)],
        scratch_shapes=[pltpu.VMEM((DN_HEADS, d, d), F32)],
        compiler_params=_params(("arbitrary",)))(qkv, z, z, a_log, dtb, norm_w)


def _dn_bwd(qkv, z, a_log, dtb, norm_w, states, d_y, tag):
    S = qkv.shape[0]
    nchunk = S // DN_CHUNK
    d = DN_HEAD_DIM
    rev = lambda i: nchunk - 1 - i

    def body(qkv_ref, zg_ref, ba_ref, alog_ref, dtb_ref, nw_ref, st_ref, dy_ref,
             dqkv_ref, dzg_ref, dba_ref, dalog_ref, ddtb_ref, dnw_ref, d_state):
        @pl.when(pl.program_id(0) == 0)
        def _():
            d_state[...] = jnp.zeros_like(d_state)
            dalog_ref[...] = jnp.zeros_like(dalog_ref)
            ddtb_ref[...] = jnp.zeros_like(ddtb_ref)
            dnw_ref[...] = jnp.zeros_like(dnw_ref)

        args = [st_ref[...]] + _dn_args(qkv_ref, zg_ref, ba_ref, alog_ref, dtb_ref, nw_ref)
        _, vjp = jax.vjp(_dn_chunk_fn, *args)
        g = vjp((d_state[...], _dn_heads(dy_ref)))
        d_state[...] = g[0]
        for h in range(DN_HEADS):
            for n, base in enumerate((0, DN_WIDTH, 2 * DN_WIDTH)):
                dqkv_ref[:, base + h * d:base + (h + 1) * d] = g[1 + n][h]
            dzg_ref[:, h * d:(h + 1) * d] = g[4][h].astype(dzg_ref.dtype)
        dba_ref[...] = g[5].astype(dba_ref.dtype)
        lane_sum = lambda t: jnp.broadcast_to(jnp.sum(t, axis=-1, keepdims=True), t.shape)
        dalog_ref[...] += lane_sum(g[6])
        ddtb_ref[...] += lane_sum(g[7])
        dnw_ref[...] += g[8]

    hspec = pl.BlockSpec((DN_HEADS, 1, d), lambda i: (0, 0, 0))
    return pl.pallas_call(
        body, name=f"{tag}_deltanet_bwd", grid=(nchunk,),
        in_specs=_dn_specs(rev) + [pl.BlockSpec((None, DN_HEADS, d, d), lambda i: (rev(i), 0, 0, 0)),
                                   pl.BlockSpec((DN_CHUNK, DN_WIDTH), lambda i: (rev(i), 0))],
        out_specs=[pl.BlockSpec((DN_CHUNK, 3 * DN_WIDTH), lambda i: (rev(i), 0)),
                   pl.BlockSpec((DN_CHUNK, DN_WIDTH), lambda i: (rev(i), 0)),
                   pl.BlockSpec((DN_CHUNK, 128), lambda i: (rev(i), 0)),
                   hspec, hspec, pl.BlockSpec((1, 1, d), lambda i: (0, 0, 0))],
        out_shape=[jax.ShapeDtypeStruct((S, 3 * DN_WIDTH), F32), jax.ShapeDtypeStruct((S, DN_WIDTH), BF16),
                   jax.ShapeDtypeStruct((S, 128), BF16), jax.ShapeDtypeStruct((DN_HEADS, 1, d), F32),
                   jax.ShapeDtypeStruct((DN_HEADS, 1, d), F32), jax.ShapeDtypeStruct((1, 1, d), F32)],
        scratch_shapes=[pltpu.VMEM((DN_HEADS, d, d), F32)],
        compiler_params=_params(("arbitrary",)))(qkv, z, z, a_log, dtb, norm_w, states, d_y)


def _whole_fwd(fn, ins, outs, *, name):
    n = len(ins)

    def body(*refs):
        res = fn(*[r[...] for r in refs[:n]])
        for o_ref, o in zip(refs[n:], res):
            o_ref[...] = o

    return pl.pallas_call(body, name=name, out_shape=[jax.ShapeDtypeStruct(s, F32) for s in outs],
                          compiler_params=_params())(*ins)


def _whole_bwd(fn, ins, cts, n_grads, *, name, lane_sum=()):
    n, nt = len(ins), len(cts)

    def body(*refs):
        _, vjp = jax.vjp(fn, *[r[...] for r in refs[:n]])
        grads = vjp(tuple(r[...] for r in refs[n:n + nt]))
        for k in range(n_grads):
            g = grads[k]
            if k in lane_sum:
                g = jnp.broadcast_to(jnp.sum(g, axis=-1, keepdims=True), g.shape)
            refs[n + nt + k][...] = g

    return pl.pallas_call(body, name=name, out_shape=[jax.ShapeDtypeStruct(a.shape, F32) for a in ins[:n_grads]],
                          compiler_params=_params())(*ins, *cts)


S5_CHUNK = 256


def _s5_param_fn(a_re, a_im, ldt, bt_re, bt_im, expand):
    dt = jnp.exp(ldt)
    er = jnp.exp(a_re * dt)
    ab_re, ab_im = er * jnp.cos(a_im * dt), er * jnp.sin(a_im * dt)
    den = a_re * a_re + a_im * a_im
    co_re = ((ab_re - 1.0) * a_re + ab_im * a_im) / den
    co_im = (ab_im * a_re - (ab_re - 1.0) * a_im) / den
    cr, ci = _dot(expand, co_re), _dot(expand, co_im)
    return ab_re, ab_im, cr * bt_re - ci * bt_im, cr * bt_im + ci * bt_re


def _s5_scan(b_re, b_im, a_re, a_im, row, T, reverse):
    x_re, x_im, p_re, p_im = b_re, b_im, a_re, a_im
    d = 1
    while d < T:
        if reverse:
            s_re = jnp.where(row < T - d, pltpu.roll(x_re, T - d, 0), 0.0)
            s_im = jnp.where(row < T - d, pltpu.roll(x_im, T - d, 0), 0.0)
        else:
            s_re = jnp.where(row >= d, pltpu.roll(x_re, d, 0), 0.0)
            s_im = jnp.where(row >= d, pltpu.roll(x_im, d, 0), 0.0)
        x_re, x_im = x_re + p_re * s_re - p_im * s_im, x_im + p_re * s_im + p_im * s_re
        p_re, p_im = p_re * p_re - p_im * p_im, 2.0 * p_re * p_im
        d *= 2
    return x_re, x_im


def _s5_states(u, bre_ref, bim_ref, a_re, a_im, c_re, c_im, row, T):
    bu_re = _dot(u, bre_ref[...]) + jnp.where(row == 0, a_re * c_re - a_im * c_im, 0.0)
    bu_im = _dot(u, bim_ref[...]) + jnp.where(row == 0, a_re * c_im + a_im * c_re, 0.0)
    return _s5_scan(bu_re, bu_im, a_re, a_im, row, T, False)


def _s5_in_specs(order, T):
    full = lambda shape: pl.BlockSpec(shape, lambda i: (0,) * len(shape), pipeline_mode=pl.Buffered(1))
    return [pl.BlockSpec((T, S5_WIDTH), lambda i: (order(i), Z_SU // S5_WIDTH)),
            full((S5_WIDTH, S5_LANES)), full((S5_WIDTH, S5_LANES)), full((S5_WIDTH, S5_LANES)),
            full((S5_WIDTH, S5_LANES)), full((1, S5_LANES)), full((1, S5_LANES)), full((1, S5_WIDTH))]


def _s5_fwd(z, bre, bim, cre, cim, ab_re, ab_im, dskip, tag):
    S = z.shape[0]
    T = _pick(S, (S5_CHUNK, 128))
    nch = S // T

    def body(u_ref, bre_ref, bim_ref, cre_ref, cim_ref, are_ref, aim_ref, d_ref, y_ref, xre_ref, xim_ref, c_re, c_im):
        @pl.when(pl.program_id(0) == 0)
        def _():
            c_re[...] = jnp.zeros_like(c_re)
            c_im[...] = jnp.zeros_like(c_im)

        u = u_ref[...]
        row = lax.broadcasted_iota(jnp.int32, (T, S5_LANES), 0)
        x_re, x_im = _s5_states(u, bre_ref, bim_ref, are_ref[...], aim_ref[...], c_re[...], c_im[...], row, T)
        c_re[...] = jnp.sum(jnp.where(row == T - 1, x_re, 0.0), axis=0, keepdims=True)
        c_im[...] = jnp.sum(jnp.where(row == T - 1, x_im, 0.0), axis=0, keepdims=True)
        xre_ref[...] = x_re
        xim_ref[...] = x_im
        y_ref[...] = _dot(x_re, cre_ref[...], _NT) - _dot(x_im, cim_ref[...], _NT) + d_ref[...] * u

    return pl.pallas_call(
        body, name=f"{tag}_s5", grid=(nch,), in_specs=_s5_in_specs(lambda i: i, T),
        out_specs=[pl.BlockSpec((T, S5_WIDTH), lambda i: (i, 0)),
                   pl.BlockSpec((T, S5_LANES), lambda i: (i, 0)), pl.BlockSpec((T, S5_LANES), lambda i: (i, 0))],
        out_shape=[jax.ShapeDtypeStruct((S, S5_WIDTH), F32), jax.ShapeDtypeStruct((S, S5_LANES), F32),
                   jax.ShapeDtypeStruct((S, S5_LANES), F32)],
        scratch_shapes=[pltpu.VMEM((1, S5_LANES), F32), pltpu.VMEM((1, S5_LANES), F32)],
        compiler_params=_params(("arbitrary",)))(z, bre, bim, cre, cim, ab_re, ab_im, dskip)


S5_BWD_CHUNK = 128


def _s5_bwd(z, bre, bim, cre, cim, ab_re, ab_im, dskip, xre, xim, d_y, tag):
    S = z.shape[0]
    T = _pick(S, (S5_BWD_CHUNK, 64))
    nch = S // T
    rev = lambda i: nch - 1 - i
    above = lambda i: (jnp.maximum(rev(i) * (T // 8) - 1, 0), 0)
    full = lambda shape: pl.BlockSpec(shape, lambda i: (0,) * len(shape))

    def body(u_ref, bre_ref, bim_ref, cre_ref, cim_ref, are_ref, aim_ref, d_ref, xre_ref, xim_ref, pre_ref, pim_ref,
             dy_ref, du_ref, dbre_ref, dbim_ref, dcre_ref, dcim_ref, dare_ref, daim_ref, dd_ref, g_re, g_im):
        @pl.when(pl.program_id(0) == 0)
        def _():
            g_re[...] = jnp.zeros_like(g_re)
            g_im[...] = jnp.zeros_like(g_im)
            for r in (dbre_ref, dbim_ref, dcre_ref, dcim_ref, dare_ref, daim_ref, dd_ref):
                r[...] = jnp.zeros_like(r)

        u = u_ref[...]
        dy = dy_ref[...].astype(F32)
        a_re, a_im = are_ref[...], aim_ref[...]
        row = lax.broadcasted_iota(jnp.int32, (T, S5_LANES), 0)
        x_re, x_im = xre_ref[...], xim_ref[...]
        dcre_ref[...] += _dot1(dy, x_re, _TN)
        dcim_ref[...] -= _dot1(dy, x_im, _TN)
        has_before = (pl.program_id(0) < nch - 1).astype(F32)
        row8 = lax.broadcasted_iota(jnp.int32, (8, S5_LANES), 0)
        before = lambda ref: has_before * jnp.sum(jnp.where(row8 == 7, ref[...], 0.0), axis=0, keepdims=True)
        xp_re = jnp.where(row >= 1, pltpu.roll(x_re, 1, 0), 0.0) + jnp.where(row == 0, before(pre_ref), 0.0)
        xp_im = jnp.where(row >= 1, pltpu.roll(x_im, 1, 0), 0.0) + jnp.where(row == 0, before(pim_ref), 0.0)
        last = row == T - 1
        gd_re = _dot(dy, cre_ref[...]) + jnp.where(last, a_re * g_re[...] + a_im * g_im[...], 0.0)
        gd_im = -_dot(dy, cim_ref[...]) + jnp.where(last, a_re * g_im[...] - a_im * g_re[...], 0.0)
        t_re, t_im = _s5_scan(gd_re, gd_im, a_re, -a_im, row, T, True)
        g_re[...] = jnp.sum(jnp.where(row == 0, t_re, 0.0), axis=0, keepdims=True)
        g_im[...] = jnp.sum(jnp.where(row == 0, t_im, 0.0), axis=0, keepdims=True)
        du_ref[...] = (_dot1(t_re, bre_ref[...], _NT) + _dot1(t_im, bim_ref[...], _NT) + dy * d_ref[...]).astype(du_ref.dtype)
        dbre_ref[...] += _dot1(u, t_re, _TN)
        dbim_ref[...] += _dot1(u, t_im, _TN)
        dare_ref[...] += jnp.sum(t_re * xp_re + t_im * xp_im, axis=0, keepdims=True)
        daim_ref[...] += jnp.sum(t_im * xp_re - t_re * xp_im, axis=0, keepdims=True)
        dd_ref[...] += jnp.sum(dy * u, axis=0, keepdims=True)

    return pl.pallas_call(
        body, name=f"{tag}_s5_bwd", grid=(nch,),
        in_specs=_s5_in_specs(rev, T) + [pl.BlockSpec((T, S5_LANES), lambda i: (rev(i), 0)),
                                         pl.BlockSpec((T, S5_LANES), lambda i: (rev(i), 0)),
                                         pl.BlockSpec((8, S5_LANES), above), pl.BlockSpec((8, S5_LANES), above),
                                         pl.BlockSpec((T, S5_WIDTH), lambda i: (rev(i), 0))],
        out_specs=[pl.BlockSpec((T, S5_WIDTH), lambda i: (rev(i), 0))] + [full((S5_WIDTH, S5_LANES))] * 4
        + [full((1, S5_LANES))] * 2 + [full((1, S5_WIDTH))],
        out_shape=[jax.ShapeDtypeStruct((S, S5_WIDTH), BF16)] + [jax.ShapeDtypeStruct((S5_WIDTH, S5_LANES), F32)] * 4
        + [jax.ShapeDtypeStruct((1, S5_LANES), F32)] * 2 + [jax.ShapeDtypeStruct((1, S5_WIDTH), F32)],
        scratch_shapes=[pltpu.VMEM((1, S5_LANES), F32), pltpu.VMEM((1, S5_LANES), F32)],
        compiler_params=_params(("arbitrary",)))(z, bre, bim, cre, cim, ab_re, ab_im, dskip, xre, xim, xre, xim, d_y)


def _s5_glu_fn(y, glu_w, glu_b):
    g = 0.5 * y * (1.0 + jnp.tanh(math.sqrt(2.0 / math.pi) * (y + 0.044715 * (y * y * y))))
    lin = lax.dot_general(g.astype(BF16), glu_w.astype(BF16), (((1,), (0,)), ((), ())), preferred_element_type=F32)
    return (g * jax.nn.sigmoid(lin + glu_b),)


def _block_diag(m):
    G, H, P = S5_GROUPS, S5_GROUP_CH, S5_STATE
    eye = jnp.eye(G, dtype=m.dtype)
    return (m.reshape(G, H, 1, P) * eye[:, None, :, None]).reshape(G * H, G * P)


def _block_diag_take(m):
    G, H, P = S5_GROUPS, S5_GROUP_CH, S5_STATE
    eye = jnp.eye(G, dtype=m.dtype)
    return jnp.sum(m.reshape(G, H, G, P) * eye[:, None, :, None], axis=2).reshape(G * H, P)


def _loss_head(y, target, tag):
    S, D = y.shape
    tm = _pick(S, (256, 128, 64))

    def body(y_ref, t_ref, dy_ref, loss_ref):
        @pl.when(pl.program_id(0) == 0)
        def _():
            loss_ref[...] = jnp.zeros_like(loss_ref)

        err = y_ref[...] - t_ref[...]
        dy_ref[...] = err * (1.0 / D)
        part = 0.5 * jnp.sum(jnp.mean(err * err, axis=-1, keepdims=True), axis=0, keepdims=True)
        loss_ref[...] += jnp.broadcast_to(part, loss_ref.shape)

    return pl.pallas_call(
        body, name=f"{tag}_loss", grid=(S // tm,),
        in_specs=[pl.BlockSpec((tm, D), lambda i: (i, 0)), pl.BlockSpec((tm, D), lambda i: (i, 0))],
        out_specs=[pl.BlockSpec((tm, D), lambda i: (i, 0)), pl.BlockSpec((8, 128), lambda i: (0, 0))],
        out_shape=[jax.ShapeDtypeStruct((S, D), F32), jax.ShapeDtypeStruct((8, 128), F32)],
        compiler_params=_params(("arbitrary",)))(y, target)


def _adamw(w, g, m, v, name, g_block=None):
    shape = w.shape
    cols = shape[-1]
    rows = int(np.prod(shape[:-1]))
    tr = _pick(rows, [t for t in (512, 256, 128, 64, 32, 16, 8) if t * cols <= 256 * 1024] or [8])
    c1 = 1.0 - ADAM_B1 ** ADAM_STEP
    c2 = 1.0 - ADAM_B2 ** ADAM_STEP
    n_out = 3 if g_block is None else 4

    def body(w_ref, g_ref, m_ref, v_ref, d_ref, mo_ref, vo_ref, *go_ref):
        gg = g_ref[...]
        mn = ADAM_B1 * m_ref[...] + (1.0 - ADAM_B1) * gg
        vn = ADAM_B2 * v_ref[...] + (1.0 - ADAM_B2) * (gg * gg)
        d_ref[...] = -ADAM_LR * ((mn / c1) / (jnp.sqrt(vn / c2) + ADAM_EPS) + ADAM_WD * w_ref[...])
        mo_ref[...] = mn
        vo_ref[...] = vn
        for r in go_ref:
            r[...] = gg

    spec = pl.BlockSpec((tr, cols), lambda i: (i, 0))
    g_spec = spec if g_block is None else pl.BlockSpec((tr, cols), lambda i: (i, g_block))
    outs = pl.pallas_call(
        body, name=name, grid=(rows // tr,), in_specs=[spec, g_spec, spec, spec], out_specs=[spec] * n_out,
        out_shape=[jax.ShapeDtypeStruct((rows, cols), F32)] * n_out,
        compiler_params=_params(("parallel",)))(w.reshape(rows, cols), g.reshape(rows, g.shape[-1]),
                                                m.reshape(rows, cols), v.reshape(rows, cols))
    return [o.reshape(shape) for o in outs]


def _sum_slots(slots, name):
    n, R, C = slots.shape
    tr = _pick(R, [t for t in (1024, 512, 256, 128, 64, 32, 16, 8) if n * t * C * 4 <= (2 << 20)] or [8])

    def body(s_ref, o_ref):
        acc = s_ref[0].astype(F32)
        for k in range(1, n):
            acc = acc + s_ref[k].astype(F32)
        o_ref[...] = acc

    return pl.pallas_call(
        body, name=name, grid=(R // tr,), in_specs=[pl.BlockSpec((n, tr, C), lambda i: (0, i, 0))],
        out_specs=pl.BlockSpec((tr, C), lambda i: (i, 0)),
        out_shape=jax.ShapeDtypeStruct((R, C), F32), compiler_params=_params(("parallel",)))(slots)


def _row_tile(R, C, itemsize, target=1 << 20):
    return _pick(R, [t for t in (2048, 1024, 512, 256, 128, 64, 32, 16) if t * C * itemsize <= target] or [16])


def _add_sibling(part, got, name):
    n, _, R2, C = part.shape
    tr = _row_tile(R2, C, 2)

    def body(p_ref, q_ref, o_ref):
        o_ref[...] = (p_ref[...].astype(F32) + q_ref[...].astype(F32)).astype(o_ref.dtype)

    spec = pl.BlockSpec((None, tr, C), lambda k, i: (k, i, 0))
    return pl.pallas_call(
        body, name=name, grid=(n, R2 // tr),
        in_specs=[pl.BlockSpec((None, None, tr, C), lambda k, i: (k, lax.axis_index("c"), i, 0)), spec], out_specs=spec,
        out_shape=jax.ShapeDtypeStruct((n, R2, C), BF16), compiler_params=_params(("parallel", "parallel")))(part, got)


def _sum_chips(chip_part, others, name):
    _, R2, C = chip_part.shape
    tr = _row_tile(R2, C, 4)

    def body(a_ref, b_ref, o_ref):
        acc = a_ref[...].astype(F32)
        for j in range(N_SHARD - 1):
            acc = acc + b_ref[j].astype(F32)
        o_ref[...] = acc

    return pl.pallas_call(
        body, name=name, grid=(R2 // tr,),
        in_specs=[pl.BlockSpec((None, tr, C), lambda i: (2 * lax.axis_index("x") + lax.axis_index("y"), i, 0)),
                  pl.BlockSpec((N_SHARD - 1, tr, C), lambda i: (0, i, 0))],
        out_specs=pl.BlockSpec((None, tr, C), lambda i: (lax.axis_index("c"), i, 0)),
        out_shape=jax.ShapeDtypeStruct((2, R2, C), F32), compiler_params=_params(("parallel",)))(chip_part, others)


_ANY = pl.BlockSpec(memory_space=pl.ANY)


def _place():
    return lax.axis_index("x"), lax.axis_index("y"), lax.axis_index("c")


def _other_chips(x, y):
    return [(1 - x, y), (x, 1 - y), (1 - x, 1 - y)]


def _remote(src, dst, send_sems, recv_sems, n, to):
    return pltpu.make_async_remote_copy(src_ref=src, dst_ref=dst, send_sem=send_sems.at[n], recv_sem=recv_sems.at[n],
                                        device_id=to, device_id_type=MESH_ID)


_JOB_SEMS = {"gather_ici": 3, "gather_d2d": 4, "scatter": 3, "swap": 4}


def _job_out_shape(job):
    src = job["ins"][0]
    if job["kind"] == "gather_ici":
        return jax.ShapeDtypeStruct((N_SHARD,) + src.shape, src.dtype)
    if job["kind"] == "gather_d2d":
        return jax.ShapeDtypeStruct(src.shape, src.dtype)
    if job["kind"] == "swap":
        return jax.ShapeDtypeStruct((N_SHARD,) + src.shape[2:], src.dtype)
    return jax.ShapeDtypeStruct((N_SHARD - 1,) + src.shape[1:], src.dtype)


def _job_copies(kind, ins, out, send_sems, recv_sems, base):
    x, y, c = _place()
    k = 2 * x + y
    sibling = (x, y, 1 - c)
    sends, recvs = [], []
    if kind == "swap":
        for s in range(N_SHARD):
            sends.append(_remote(ins[0].at[s, 1 - c], out.at[s], send_sems, recv_sems, base + s, sibling))
            recvs.append(_remote(out.at[s], out.at[s], send_sems, recv_sems, base + s, sibling))
        return sends, recvs
    for j, (cx, cy) in enumerate(_other_chips(x, y)):
        kj = 2 * cx + cy
        if kind == "gather_ici":
            sends.append(_remote(ins[0].at[c], out.at[k, c], send_sems, recv_sems, base + j, (cx, cy, c)))
            recvs.append(_remote(out.at[kj, c], out.at[kj, c], send_sems, recv_sems, base + j, sibling))
        elif kind == "gather_d2d":
            sends.append(_remote(out.at[kj, c], out.at[kj, c], send_sems, recv_sems, base + 1 + j, sibling))
            recvs.append(_remote(out.at[kj, 1 - c], out.at[kj, 1 - c], send_sems, recv_sems, base + 1 + j, sibling))
        else:
            sends.append(_remote(ins[0].at[kj], out.at[j], send_sems, recv_sems, base + j, (cx, cy, c)))
            recvs.append(_remote(out.at[j], out.at[j], send_sems, recv_sems, base + j, sibling))
    if kind == "gather_d2d":
        sends.append(_remote(ins[1], out.at[k], send_sems, recv_sems, base, sibling))
        recvs.append(_remote(out.at[k], out.at[k], send_sems, recv_sems, base, sibling))
    return sends, recvs


def _gather_shards(shards, name):
    n = len(shards)
    per = 7

    def body(*refs):
        ins, outs = refs[:n], refs[n:2 * n]
        send_sems, recv_sems = refs[2 * n:]
        x, y, c = _place()
        k = 2 * x + y
        sibling = (x, y, 1 - c)
        chips = _other_chips(x, y)
        started = []
        for a in range(n):
            cp = _remote(ins[a], outs[a].at[k], send_sems, recv_sems, per * a, sibling)
            cp.start()
            started.append(cp)
            for j, (cx, cy) in enumerate(chips):
                cp = _remote(ins[a].at[c], outs[a].at[k, c], send_sems, recv_sems, per * a + 1 + j, (cx, cy, c))
                cp.start()
                started.append(cp)
        for a in range(n):
            for j, (cx, cy) in enumerate(chips):
                landed = outs[a].at[2 * cx + cy, c]
                _remote(landed, landed, send_sems, recv_sems, per * a + 1 + j, sibling).wait_recv()
                cp = _remote(landed, landed, send_sems, recv_sems, per * a + 4 + j, sibling)
                cp.start()
                started.append(cp)
        for a in range(n):
            own = outs[a].at[k]
            _remote(own, own, send_sems, recv_sems, per * a, sibling).wait_recv()
            for j, (cx, cy) in enumerate(chips):
                passed = outs[a].at[2 * cx + cy, 1 - c]
                _remote(passed, passed, send_sems, recv_sems, per * a + 4 + j, sibling).wait_recv()
        for cp in started:
            cp.wait_send()

    return pl.pallas_call(
        body, name=name, in_specs=[_ANY] * n, out_specs=[_ANY] * n,
        out_shape=[jax.ShapeDtypeStruct((N_SHARD,) + s.shape, s.dtype) for s in shards],
        scratch_shapes=[pltpu.SemaphoreType.DMA((per * n,)), pltpu.SemaphoreType.DMA((per * n,))],
        )(*shards)


def _swap_other_half(parts, name):
    n = len(parts)

    def body(*refs):
        ins, outs = refs[:n], refs[n:2 * n]
        send_sems, recv_sems = refs[2 * n:]
        x, y, c = _place()
        started = []
        for a in range(n):
            for k in range(N_SHARD):
                cp = _remote(ins[a].at[k, 1 - c], outs[a].at[k], send_sems, recv_sems, N_SHARD * a + k, (x, y, 1 - c))
                cp.start()
                started.append(cp)
        for cp in started:
            cp.wait()

    return pl.pallas_call(
        body, name=name, in_specs=[_ANY] * n, out_specs=[_ANY] * n,
        out_shape=[jax.ShapeDtypeStruct((N_SHARD,) + p.shape[2:], p.dtype) for p in parts],
        scratch_shapes=[pltpu.SemaphoreType.DMA((N_SHARD * n,)), pltpu.SemaphoreType.DMA((N_SHARD * n,))],
        )(*parts)


def _scatter_to_chips(parts, name):
    n = len(parts)
    per = N_SHARD - 1

    def body(*refs):
        ins, outs = refs[:n], refs[n:2 * n]
        send_sems, recv_sems = refs[2 * n:]
        x, y, c = _place()
        chips = _other_chips(x, y)
        started = []
        for a in range(n):
            for j, (cx, cy) in enumerate(chips):
                cp = _remote(ins[a].at[2 * cx + cy], outs[a].at[j], send_sems, recv_sems, per * a + j, (cx, cy, c))
                cp.start()
                started.append(cp)
        for cp in started:
            cp.wait()

    return pl.pallas_call(
        body, name=name, in_specs=[_ANY] * n, out_specs=[_ANY] * n,
        out_shape=[jax.ShapeDtypeStruct((per,) + p.shape[1:], p.dtype) for p in parts],
        scratch_shapes=[pltpu.SemaphoreType.DMA((per * n,)), pltpu.SemaphoreType.DMA((per * n,))],
        )(*parts)


def _join_halves(bufs, name):
    n = len(bufs)

    def body(*refs):
        outs = refs[n:2 * n]
        send_sems, recv_sems = refs[2 * n:]
        x, y, c = _place()
        started = []
        for a in range(n):
            cp = _remote(outs[a].at[c], outs[a].at[c], send_sems, recv_sems, a, (x, y, 1 - c))
            cp.start()
            started.append(cp)
        for a in range(n):
            arrives = outs[a].at[1 - c]
            _remote(arrives, arrives, send_sems, recv_sems, a, (x, y, 1 - c)).wait_recv()
        for cp in started:
            cp.wait_send()

    return pl.pallas_call(
        body, name=name, in_specs=[_ANY] * n, out_specs=[_ANY] * n,
        out_shape=[jax.ShapeDtypeStruct(b.shape, b.dtype) for b in bufs],
        input_output_aliases={a: a for a in range(n)},
        scratch_shapes=[pltpu.SemaphoreType.DMA((n,)), pltpu.SemaphoreType.DMA((n,))],
        )(*bufs)


def _gather_all_devices(vec, name):
    def body(in_ref, out_ref, send_sems, recv_sems, local_sem):
        x, y, c = _place()
        me = 4 * x + 2 * y + c
        own = pltpu.make_async_copy(in_ref, out_ref.at[me], local_sem)
        own.start()
        sends = []
        for r in range(1, 8):
            fx, fy, fc = (r >> 2) & 1, (r >> 1) & 1, r & 1
            to = (x ^ fx, y ^ fy, c ^ fc)
            sends.append(pltpu.make_async_remote_copy(
                src_ref=in_ref, dst_ref=out_ref.at[me], send_sem=send_sems.at[r - 1], recv_sem=recv_sems.at[r - 1],
                device_id=to, device_id_type=MESH_ID))
            sends[-1].start()
        for r in range(1, 8):
            fx, fy, fc = (r >> 2) & 1, (r >> 1) & 1, r & 1
            slot = out_ref.at[4 * (x ^ fx) + 2 * (y ^ fy) + (c ^ fc)]
            pltpu.make_async_remote_copy(src_ref=slot, dst_ref=slot, send_sem=send_sems.at[r - 1],
                                         recv_sem=recv_sems.at[r - 1], device_id=(x, y, c),
                                         device_id_type=MESH_ID).wait_recv()
        for cp in sends:
            cp.wait_send()
        own.wait()

    return pl.pallas_call(
        body, name=name, in_specs=[_ANY], out_specs=_ANY, out_shape=jax.ShapeDtypeStruct((8,) + vec.shape, vec.dtype),
        scratch_shapes=[pltpu.SemaphoreType.DMA((7,)), pltpu.SemaphoreType.DMA((7,)), pltpu.SemaphoreType.DMA(())],
        )(vec)


def _permute_w_in(w):
    cut = Z_SU
    return jnp.concatenate([w[:, :cut], w[:, cut + 2 * DN_HEADS:], w[:, cut:cut + 2 * DN_HEADS],
                            jnp.zeros((w.shape[0], Z_WIDTH - IN_WIDTH), w.dtype)], axis=1)


def _unpermute_w_in(wp):
    return jnp.concatenate([wp[:, :Z_SU], wp[:, Z_DBA:Z_DBA + 2 * DN_HEADS], wp[:, Z_SU:Z_DBA]], axis=1)


def _s5_inputs(sp):
    ab_re, ab_im, bb_re, bb_im = _whole_fwd(
        _s5_param_fn, sp["s5_pins"], [(S5_GROUPS, S5_STATE)] * 2 + [(S5_WIDTH, S5_STATE)] * 2, name="s5_params")
    return (_block_diag(bb_re), _block_diag(bb_im), _block_diag(sp["c_re"]), _block_diag(sp["c_im"]),
            ab_re.reshape(1, S5_LANES), ab_im.reshape(1, S5_LANES), sp["dskip"])


def _mixer_fwd(x, sp, weights, rope, mm):
    D = x.shape[1]
    (h,) = _rows_fwd(_prenorm_fn, [(x, D, 0)], [sp["mix_norm_pre"]], [(D, D, BF16)], name="mix_prenorm")
    w_in = weights("win")
    z = mm("win", h, w_in, out_dtype=F32, name="mix_in")
    w_out, glu_w = weights("wout")
    y_attn = _attn_fwd(z, rope[0], rope[1], sp["sinks"], "mix")
    qkv = _conv_fwd(z, sp["conv_w"], "mix")
    y_dn, states = _dn_fwd(qkv, z, sp["dn_a_log"], sp["dn_dt_bias"], sp["dn_norm_w"], "mix")
    s5_in = _s5_inputs(sp)
    y_lin, *s5_states = _s5_fwd(z, *s5_in, "mix")
    (y_s5,) = _rows_fwd(_s5_glu_fn, [(y_lin, S5_WIDTH, 0)], [glu_w, sp["glu_b"]], [(S5_WIDTH, S5_WIDTH, BF16)],
                        name="mix_s5_glu")
    cat = jnp.concatenate([y_attn, y_dn, y_s5], axis=1)
    mixed = mm("wout", cat, w_out, out_dtype=F32, name="mix_out")
    (x_new,) = _rows_fwd(_residual_fn(1.0), [(x, D, 0), (mixed, D, 0)], [sp["mix_norm_post"]], [(D, D, F32)],
                         name="mix_residual")
    return x_new, (h, z, qkv, states, s5_in, y_lin, s5_states, cat, mixed, w_in, w_out, glu_w)


def _mixer_bwd(dx_new, x, sp, rope, saved, mm, sink):
    h, z, qkv, states, s5_in, y_lin, s5_states, cat, mixed, w_in, w_out, glu_w = saved
    D = x.shape[1]
    G, H, P = S5_GROUPS, S5_GROUP_CH, S5_STATE
    d_mixed, d_g_post = _rows_bwd(_postnorm_fn(1.0), [(mixed, D, 0)], [sp["mix_norm_post"]], [(dx_new, D, 0)],
                                  [(0, D, BF16)], name="mix_postnorm_bwd")
    d_cat = mm(d_mixed, w_out, tb=True, out_dtype=BF16, name="mix_out_dx")
    sink("wout", mm(cat, d_mixed, ta=True, out_dtype=BF16, name="mix_out_dw"))
    d_attn, d_dn, d_s5 = d_cat[:, :ATTN_WIDTH], d_cat[:, ATTN_WIDTH:ATTN_WIDTH + DN_WIDTH], d_cat[:, ATTN_WIDTH + DN_WIDTH:]
    d_ylin, d_glu_w, d_glu_b = _rows_bwd(_s5_glu_fn, [(y_lin, S5_WIDTH, 0)], [glu_w, sp["glu_b"]],
                                         [(d_s5, S5_WIDTH, 0)], [(0, S5_WIDTH, F32)], name="mix_s5_glu_bwd")
    sink("glu", d_glu_w.astype(BF16))
    d_us5, d_bre, d_bim, d_cre, d_cim, d_are, d_aim, d_dskip = _s5_bwd(z, *s5_in, *s5_states, d_ylin, "mix")
    cts = [d_are.reshape(G, P), d_aim.reshape(G, P), _block_diag_take(d_bre), _block_diag_take(d_bim)]
    d_a_re, d_a_im, d_ldt, d_bt_re, d_bt_im = _whole_bwd(_s5_param_fn, sp["s5_pins"], cts, 5, name="s5_params_bwd",
                                                         lane_sum=(2,))
    from_t = lambda m: m.reshape(G, H, P).transpose(0, 2, 1)
    d_qkv, d_zg, d_ba, d_alog, d_dtb, d_nw = _dn_bwd(qkv, z, sp["dn_a_log"], sp["dn_dt_bias"], sp["dn_norm_w"], states,
                                                     d_dn, "mix")
    d_uconv, d_conv_w = _conv_bwd(z, sp["conv_w"], d_qkv, "mix")
    d_q, d_kv, d_sinks = _attn_bwd(z, rope[0], rope[1], sp["sinks"], d_attn, "mix")
    d_z = jnp.concatenate([d_q, d_kv, d_uconv, d_zg, d_us5, d_ba], axis=1)
    sink("win", mm(h, d_z, ta=True, out_dtype=BF16, name="mix_in_dw"))
    d_h = mm(d_z, w_in, tb=True, out_dtype=BF16, name="mix_in_dx")
    dx, d_g_pre = _rows_bwd(_prenorm_fn, [(x, D, 0)], [sp["mix_norm_pre"]], [(d_h, D, 0)], [(0, D, F32)],
                            name="mix_prenorm_bwd", add_to_first=dx_new)
    small = {
        "mix_norm_pre": d_g_pre[0], "mix_norm_post": d_g_post[0], "attn_sinks": d_sinks[:, 0], "dn_conv_w": d_conv_w,
        "dn_a_log": d_alog[:, 0, 0], "dn_dt_bias": d_dtb[:, 0, 0], "dn_norm_w": d_nw[0, 0],
        "s5_a_re": d_a_re, "s5_a_im": d_a_im, "s5_log_dt": d_ldt[:, 0], "s5_b_re": from_t(d_bt_re),
        "s5_b_im": from_t(d_bt_im), "s5_c_re": _block_diag_take(d_cre).reshape(G, H, P),
        "s5_c_im": _block_diag_take(d_cim).reshape(G, H, P), "s5_d": d_dskip[0], "s5_glu_b": d_glu_b[0],
    }
    return dx, small


BIG_PIECES = ("gu1", "wd1", "win", "wout", "glu", "gu2", "wd2")


def _halves(m):
    return m.reshape(m.shape[:-2] + (2, m.shape[-2] // 2, m.shape[-1]))


def _whole(m):
    return m.reshape(m.shape[:-3] + (2 * m.shape[-2], m.shape[-1]))


class _WeightGather:
    def __init__(self, shards):
        self.shards = shards
        pieces = list(shards[0])
        flat = [s for p in pieces for s in shards[0][p]]
        got = iter(_gather_shards(flat, "gather_weights"))
        self.ready = {(0, p): [next(got) for _ in shards[0][p]] for p in pieces}
        self.landing = None

    def weights(self, l, piece):
        return self.ready[(l, piece)]

    def mm(self, l, piece, a, b, **kw):
        jobs, done = [], None
        if self.landing is not None:
            done = self.landing
            jobs += [{"kind": "gather_d2d", "ins": [buf, s]} for buf, s in zip(done[2], self.shards[done[0]][done[1]])]
        n_done = len(jobs)
        if l + 1 < len(self.shards):
            jobs += [{"kind": "gather_ici", "ins": [s]} for s in self.shards[l + 1][piece]]
        if not jobs:
            return _mm(a, b, **kw)
        out, bufs = _mm(a, b, jobs=jobs, **kw)
        if done is not None:
            self.ready[done[:2]] = bufs[:n_done]
        self.landing = (l + 1, piece, bufs[n_done:]) if l + 1 < len(self.shards) else None
        return out


class _GradReduce:
    def __init__(self):
        self.fresh, self.waiting, self.landed = [], [], {}

    def add(self, key, grad):
        self.fresh.append((key, _halves(grad)))

    def _summed(self, swapped):
        self.waiting += [(key, _add_sibling(part, got, "reduce_siblings_add"))
                         for (key, part), got in zip(self.fresh, swapped)]
        self.fresh = []

    def mm(self, a, b, **kw):
        k = a.shape[0] if kw.get("ta") else a.shape[1] // (2 if kw.get("a_swiglu") else 1)
        room = 2.0 * a.size * (b.size // k) / (2 if kw.get("a_swiglu") else 1) / MXU_FLOPS_PER_US * 1.15
        sent, kept = [], []
        for key, cp in self.waiting:
            cost = (N_SHARD - 1) * (cp.size // N_SHARD) * cp.dtype.itemsize / ICI_BYTES_PER_US
            if cost <= room:
                sent.append((key, cp))
                room -= cost
            else:
                kept.append((key, cp))
        if not (self.fresh or sent):
            return _mm(a, b, **kw)
        jobs = ([{"kind": "swap", "ins": [part]} for _, part in self.fresh]
                + [{"kind": "scatter", "ins": [cp]} for _, cp in sent])
        out, bufs = _mm(a, b, jobs=jobs, **kw)
        self.waiting = kept
        for (key, cp), o in zip(sent, bufs[len(self.fresh):]):
            self.landed[key] = (cp, o)
        self._summed(bufs[:len(self.fresh)])
        return out

    def finish(self):
        if self.fresh:
            self._summed(_swap_other_half([part for _, part in self.fresh], "reduce_siblings"))
        if self.waiting:
            others = _scatter_to_chips([cp for _, cp in self.waiting], "reduce_chips")
            for (key, cp), o in zip(self.waiting, others):
                self.landed[key] = (cp, o)
        keys = list(self.landed)
        joined = _join_halves([_sum_chips(*self.landed[k], "reduce_chips_sum") for k in keys], "reduce_join")
        return {k: _whole(m) for k, m in zip(keys, joined)}


def _pack_small(arrs, extra=()):
    flat = jnp.concatenate([a.reshape(-1) for a in arrs] + list(extra))
    n = flat.shape[0]
    quantum = 8 * SMALL_LANES
    padded = -(-n // quantum) * quantum
    return jnp.concatenate([flat, jnp.zeros((padded - n,), flat.dtype)]).reshape(padded // SMALL_LANES, SMALL_LANES)


def _unpack_small(flat2d, shapes):
    flat = flat2d.reshape(-1)
    out, o = [], 0
    for s in shapes:
        n = int(np.prod(s))
        out.append(flat[o:o + n].reshape(s))
        o += n
    return out, flat[o:]


def _step(a):
    x, target = a["x"][0], a["loss_target"][0]
    S, D = x.shape
    L, _, Fs = a["ff1_w_gate"].shape
    px, py, pc = _place()
    chip = 2 * px + py

    def rows_of_chips(g):
        g = _whole(g)
        return g.reshape(N_SHARD * g.shape[1], g.shape[2])

    shards = []
    for l in range(L):
        half = lambda m: _halves(m.astype(BF16))
        shards.append({
            "gu1": [half(jnp.concatenate([a["ff1_w_gate"][l], a["ff1_w_up"][l]], axis=1))], "wd1": [half(a["ff1_w_down"][l])],
            "win": [half(a["w_in"][l])], "wout": [half(a["w_out"][l]), half(a["s5_glu_w"][l])],
            "gu2": [half(jnp.concatenate([a["ff2_w_gate"][l], a["ff2_w_up"][l]], axis=1))], "wd2": [half(a["ff2_w_down"][l])]})
    gather = _WeightGather(shards)

    def ffn_weights(l, f):
        def get(piece):
            (g,) = gather.weights(l, piece + f)
            return _whole(g) if piece == "gu" else rows_of_chips(g)
        return get

    def mixer_weights(l):
        def get(piece):
            got = gather.weights(l, piece)
            if piece == "win":
                return _permute_w_in(_whole(got[0]).transpose(1, 0, 2).reshape(D, IN_WIDTH))
            return rows_of_chips(got[0]), rows_of_chips(got[1])
        return get

    conv_local = a["dn_conv_w"].reshape(-1)
    conv_rows = -(-conv_local.shape[0] // (16 * FLAT_LANES)) * 16
    conv_pad = jnp.concatenate([conv_local, jnp.zeros((conv_rows * FLAT_LANES - conv_local.shape[0],), F32)])
    (conv_all,) = _gather_shards([conv_pad.reshape(2, conv_rows // 2, FLAT_LANES)], "gather_conv")
    conv_all = conv_all.reshape(N_SHARD, -1)[:, :conv_local.shape[0]].reshape(N_SHARD, L, DN_CONV, -1)
    conv_full = conv_all.transpose(1, 2, 0, 3).reshape(L, DN_CONV, 3 * DN_WIDTH)

    expand = jnp.repeat(jnp.eye(S5_GROUPS, dtype=F32), S5_GROUP_CH, axis=0)
    lanes = lambda v, n=128: jnp.broadcast_to(v[:, None], (v.shape[0], n))
    to_t = lambda m: m.transpose(0, 2, 1).reshape(S5_WIDTH, S5_STATE)

    def small_params(l):
        sp = {k: a[k][l][None] for k in ("ff1_norm_pre", "ff1_norm_post", "mix_norm_pre", "mix_norm_post",
                                         "ff2_norm_pre", "ff2_norm_post")}
        sp["sinks"] = lanes(a["attn_sinks"][l])
        sp["conv_w"] = conv_full[l]
        sp["dn_a_log"] = lanes(a["dn_a_log"][l])[:, None, :]
        sp["dn_dt_bias"] = lanes(a["dn_dt_bias"][l])[:, None, :]
        sp["dn_norm_w"] = a["dn_norm_w"][l][None, None]
        sp["s5_pins"] = [a["s5_a_re"][l], a["s5_a_im"][l], lanes(a["s5_log_dt"][l], S5_STATE),
                         to_t(a["s5_b_re"][l]), to_t(a["s5_b_im"][l]), expand]
        sp["c_re"] = a["s5_c_re"][l].reshape(S5_WIDTH, S5_STATE)
        sp["c_im"] = a["s5_c_im"][l].reshape(S5_WIDTH, S5_STATE)
        sp["dskip"] = a["s5_d"][l][None]
        sp["glu_b"] = a["s5_glu_b"][l][None]
        return sp

    rope = _rope_tables(S)
    sps = [small_params(l) for l in range(L)]

    saved = []
    for l in range(L):
        sp = sps[l]
        mm_of = lambda f: (lambda piece, p, q, **kw: gather.mm(l, piece + f, p, q, **kw))
        x1, s1 = _ffn_fwd(x, sp["ff1_norm_pre"], sp["ff1_norm_post"], ffn_weights(l, "1"), "ff1", mm_of("1"))
        x2, s2 = _mixer_fwd(x1, sp, mixer_weights(l), rope, mm_of(""))
        x3, s3 = _ffn_fwd(x2, sp["ff2_norm_pre"], sp["ff2_norm_post"], ffn_weights(l, "2"), "ff2", mm_of("2"))
        saved.append((x, s1, x1, s2, x2, s3))
        x = x3
    dx, loss_part = _loss_head(x, target, "head")

    small_grads = [None] * L
    shard_major = lambda m: m.reshape(N_SHARD, m.shape[0] // N_SHARD, m.shape[1])
    red = _GradReduce()
    for l in reversed(range(L)):
        sp = sps[l]
        x0, s1, x1, s2, x2, s3 = saved[l]

        def ffn_sink(f):
            return lambda piece, g: red.add((l, piece + f), g if piece == "gu" else shard_major(g))

        def mixer_sink(piece, g):
            if piece == "win":
                g = _unpermute_w_in(g).reshape(D, N_SHARD, IN_WIDTH // N_SHARD).transpose(1, 0, 2)
            red.add((l, piece), g if piece == "win" else shard_major(g))

        dx, g_pre2, g_post2 = _ffn_bwd(dx, x2, sp["ff2_norm_pre"], sp["ff2_norm_post"], s3, "ff2", red.mm, ffn_sink("2"))
        dx, sg = _mixer_bwd(dx, x1, sp, rope, s2, red.mm, mixer_sink)
        dx, g_pre1, g_post1 = _ffn_bwd(dx, x0, sp["ff1_norm_pre"], sp["ff1_norm_post"], s1, "ff1", red.mm, ffn_sink("1"))
        sg.update({"ff1_norm_pre": g_pre1[0], "ff1_norm_post": g_post1[0], "ff2_norm_pre": g_pre2[0],
                   "ff2_norm_post": g_post2[0]})
        small_grads[l] = sg
    grad_x = dx[None]
    reduced = red.finish()

    grads = {}
    layers = lambda k: jnp.stack([reduced[(l, k)] for l in range(L)])
    gate_up = {}
    for f, (gu, wd) in (("ff1", ("gu1", "wd1")), ("ff2", ("gu2", "wd2"))):
        gus = layers(gu)
        gate_up[f + "_w_gate"], gate_up[f + "_w_up"] = (gus, 0), (gus, 1)
        grads[f + "_w_down"] = layers(wd)
    grads["w_in"], grads["w_out"], grads["s5_glu_w"] = layers("win"), layers("wout"), layers("glu")

    small_local = [jnp.stack([small_grads[l][n] for l in range(L)]) for n in SMALL]
    vec = _pack_small(small_local, extra=(loss_part[0, :1],))
    total = _sum_slots(_gather_all_devices(vec, "gather_small"), "sum_small")
    small_total, rest = _unpack_small(total, [g.shape for g in small_local])
    loss = rest[0]
    for n, g in zip(SMALL, small_total):
        grads[n] = g
    cw = 3 * DN_WIDTH // N_SHARD
    grads["dn_conv_w"] = lax.dynamic_slice_in_dim(grads["dn_conv_w"], chip * cw, cw, axis=2)

    delta, new_m, new_v = {}, {}, {}
    for n in BIG:
        if n in gate_up:
            delta[n], new_m[n], new_v[n], grads[n] = _adamw(a[n], gate_up[n][0], a["m_" + n], a["v_" + n], "adamw_" + n,
                                                            g_block=gate_up[n][1])
        else:
            delta[n], new_m[n], new_v[n] = _adamw(a[n], grads[n], a["m_" + n], a["v_" + n], "adamw_" + n)
    shapes = [a[n].shape for n in SMALL]
    packed = [_pack_small([src[n] for n in SMALL]) for src in
              (a, grads, {n: a["m_" + n] for n in SMALL}, {n: a["v_" + n] for n in SMALL})]
    for dst, res in zip((delta, new_m, new_v), _adamw(*packed, "adamw_small")):
        for n, val in zip(SMALL, _unpack_small(res, shapes)[0]):
            dst[n] = val
    return (loss, grad_x, *[grads[n] for n in WEIGHTS], *[delta[n] for n in WEIGHTS], *[new_m[n] for n in WEIGHTS],
            *[new_v[n] for n in WEIGHTS])


def kernel(x, ff1_norm_pre, ff1_w_gate, ff1_w_up, ff1_w_down, ff1_norm_post, mix_norm_pre, w_in, attn_sinks, dn_conv_w, dn_a_log, dn_dt_bias, dn_norm_w, s5_a_re, s5_a_im, s5_log_dt, s5_b_re, s5_b_im, s5_c_re, s5_c_im, s5_d, s5_glu_w, s5_glu_b, w_out, mix_norm_post, ff2_norm_pre, ff2_w_gate, ff2_w_up, ff2_w_down, ff2_norm_post, loss_target, m_ff1_norm_pre, m_ff1_w_gate, m_ff1_w_up, m_ff1_w_down, m_ff1_norm_post, m_mix_norm_pre, m_w_in, m_attn_sinks, m_dn_conv_w, m_dn_a_log, m_dn_dt_bias, m_dn_norm_w, m_s5_a_re, m_s5_a_im, m_s5_log_dt, m_s5_b_re, m_s5_b_im, m_s5_c_re, m_s5_c_im, m_s5_d, m_s5_glu_w, m_s5_glu_b, m_w_out, m_mix_norm_post, m_ff2_norm_pre, m_ff2_w_gate, m_ff2_w_up, m_ff2_w_down, m_ff2_norm_post, v_ff1_norm_pre, v_ff1_w_gate, v_ff1_w_up, v_ff1_w_down, v_ff1_norm_post, v_mix_norm_pre, v_w_in, v_attn_sinks, v_dn_conv_w, v_dn_a_log, v_dn_dt_bias, v_dn_norm_w, v_s5_a_re, v_s5_a_im, v_s5_log_dt, v_s5_b_re, v_s5_b_im, v_s5_c_re, v_s5_c_im, v_s5_d, v_s5_glu_w, v_s5_glu_b, v_w_out, v_mix_norm_post, v_ff2_norm_pre, v_ff2_w_gate, v_ff2_w_up, v_ff2_w_down, v_ff2_norm_post):
    return _step(dict(locals()))
```
